```python
import math
import jax
import jax.numpy as jnp
from jax import lax
import numpy as np

D_MODEL = 1024
BATCH = 8
SEQ = 2048
DEPTH = 2

GRID_W = 64
CTX_LEN = 256
N_BRANCH = 4
BRANCH_W = 256
NORM_EPS = 1e-6
ROPE_BASE = 10000.0
Q_BLOCK = 128
ADA_CHUNKS = 6
ADA_INIT = 0.3

MLA_HEADS = 4
MLA_NOPE = 64
MLA_ROPE = 32
MLA_QK = MLA_NOPE + MLA_ROPE
MLA_V = 64
Q_LORA = 256
KV_LORA = 128

FNET_GROUPS = 4
FNET_GW = BRANCH_W // FNET_GROUPS

S5_GROUP_CH = 16
S5_GROUPS = BRANCH_W // S5_GROUP_CH
S5_STATE = 64
S5_DT_MIN = 0.001
S5_DT_MAX = 0.1

RET_HEADS = 4
RET_HD = BRANCH_W // RET_HEADS
RET_CHUNK = 128

D_FF = 4 * D_MODEL

STATE_SIZES = (KV_LORA, MLA_ROPE, BRANCH_W, BRANCH_W, BRANCH_W)
MAIN_SIZES = (Q_LORA, BRANCH_W, BRANCH_W, BRANCH_W, N_BRANCH * D_MODEL)
STATE_COLS = sum(STATE_SIZES)
IN_COLS = STATE_COLS + sum(MAIN_SIZES)

kernel_name = 'hybrid_mla_fnet_s5_retention_dit_block'


def split_cols(z, sizes):
    return jnp.split(z, np.cumsum(sizes)[:-1].tolist(), axis=-1)


def rms_norm(x, w):
    xf = x.astype(jnp.float32)
    y = xf * lax.rsqrt(jnp.mean(xf * xf, axis=-1, keepdims=True) + NORM_EPS)
    return (y * w.astype(jnp.float32)).astype(x.dtype)


def ada_rms(x, w, shift, scale):
    return rms_norm(x, w) * (1.0 + scale) + shift


def rotary(x, pos):
    half = x.shape[-1] // 2
    freqs = ROPE_BASE ** (-jnp.arange(half, dtype=jnp.float32) / half)
    ang = pos.astype(jnp.float32)[:, None] * freqs
    cos = jnp.cos(ang)[:, None, :]
    sin = jnp.sin(ang)[:, None, :]
    xf = x.astype(jnp.float32)
    x1, x2 = xf[..., :half], xf[..., half:]
    return jnp.concatenate([x1 * cos - x2 * sin, x1 * sin + x2 * cos], axis=-1).astype(x.dtype)


def axial_rotary(x, rows, cols):
    h = x.shape[-1] // 2
    return jnp.concatenate([rotary(x[..., :h], rows), rotary(x[..., h:], cols)], axis=-1)


def mla_keys(kv_c, k_r, kv_norm, w_ukv, qk_k, rows, cols):
    B, L, _ = kv_c.shape
    kv = (rms_norm(kv_c, kv_norm) @ w_ukv).reshape(B, L, MLA_HEADS, MLA_NOPE + MLA_V)
    k_nope, v = kv[..., :MLA_NOPE], kv[..., MLA_NOPE:]
    k_r = jnp.broadcast_to(k_r[:, :, None, :], (B, L, MLA_HEADS, MLA_ROPE))
    k = rms_norm(jnp.concatenate([k_nope, k_r], axis=-1), qk_k)
    if rows is not None:
        k = jnp.concatenate([k[..., :MLA_NOPE], axial_rotary(k[..., MLA_NOPE:], rows, cols)], axis=-1)
    return k, v


def mla_queries(q_c, q_norm, w_uq, qk_q, rows, cols):
    B, L, _ = q_c.shape
    q = (rms_norm(q_c, q_norm) @ w_uq).reshape(B, L, MLA_HEADS, MLA_QK)
    q = rms_norm(q, qk_q)
    if rows is None:
        return q
    return jnp.concatenate([q[..., :MLA_NOPE], axial_rotary(q[..., MLA_NOPE:], rows, cols)], axis=-1)


def block_softmax_attention(q, k, v):
    B, Lq, H, dk = q.shape
    dv = v.shape[-1]
    nb = Lq // Q_BLOCK
    scale = dk ** -0.5
    qb = q.reshape(B, nb, Q_BLOCK, H, dk).swapaxes(0, 1)

    def attend(qi):
        s = jnp.einsum('bqhd,bkhd->bhqk', qi, k).astype(jnp.float32) * scale
        p = jax.nn.softmax(s, axis=-1).astype(v.dtype)
        return jnp.einsum('bhqk,bkhd->bqhd', p, v)

    o = lax.map(attend, qb)
    return o.swapaxes(0, 1).reshape(B, Lq, H * dv)


def fourier_mix(u):
    B, L, _ = u.shape
    ug = u.astype(jnp.float32).reshape(B, L, FNET_GROUPS, FNET_GW)
    y = jnp.fft.fft2(ug, axes=(1, 3), norm='ortho').real
    return y.reshape(B, L, BRANCH_W).astype(u.dtype)


def s5_discretise(lam_re, lam_im, log_step, b_re, b_im):
    lam = lax.complex(lam_re.astype(jnp.float32), lam_im.astype(jnp.float32))
    step = jnp.exp(log_step.astype(jnp.float32))[:, None]
    lam_bar = jnp.exp(lam * step)
    b = lax.complex(b_re.astype(jnp.float32), b_im.astype(jnp.float32))
    b_bar = ((lam_bar - 1.0) / lam)[..., None] * b
    return lam_bar, b_bar


def linear_recurrence_op(left, right):
    a_l, b_l = left
    a_r, b_r = right
    return a_l * a_r, a_r * b_l + b_r


def s5_scan(u, lam_bar, b_bar, x0, reverse):
    B, L, _ = u.shape
    ug = u.astype(jnp.float32).reshape(B, L, S5_GROUPS, S5_GROUP_CH).astype(jnp.complex64)
    bu = jnp.einsum('blgh,gph->blgp', ug, b_bar)
    if reverse:
        bu = jnp.flip(bu, axis=1)
    if x0 is not None:
        bu = bu.at[:, 0].add(lam_bar * x0)
    a = jnp.broadcast_to(lam_bar, bu.shape)
    _, xs = lax.associative_scan(linear_recurrence_op, (a, bu), axis=1)
    return jnp.flip(xs, axis=1) if reverse else xs


def s5_readout(u, xs_f, xs_b, c_f, c_b, d, w_glu):
    B, L, _ = u.shape
    y = (jnp.einsum('blgp,ghp->blgh', xs_f, c_f).real
         + jnp.einsum('blgp,ghp->blgh', xs_b, c_b).real)
    y = y.reshape(B, L, BRANCH_W).astype(u.dtype) + d * u
    y = jax.nn.gelu(y)
    val, gate = jnp.split(y @ w_glu, 2, axis=-1)
    return val * jax.nn.sigmoid(gate)


def retention_heads(t, pos):
    B, L, _ = t.shape
    t = t.reshape(B, L, RET_HEADS, RET_HD)
    if pos is not None:
        t = rotary(t, pos)
    return t.transpose(0, 2, 1, 3)


def retention_chunkwise(q, k, v, log_g, s0):
    B, H, L, dk = q.shape
    n = L // RET_CHUNK
    idx = jnp.arange(RET_CHUNK, dtype=jnp.float32)
    diff = idx[:, None] - idx[None, :]
    intra = jnp.where(diff >= 0, jnp.exp(log_g[:, None, None] * jnp.maximum(diff, 0.0)), 0.0)
    q_dec = jnp.exp(log_g[:, None] * (idx + 1.0))
    k_dec = jnp.exp(log_g[:, None] * (RET_CHUNK - 1.0 - idx))
    chunk_dec = jnp.exp(log_g * RET_CHUNK)

    def blocks(t):
        return t.astype(jnp.float32).reshape(B, H, n, RET_CHUNK, t.shape[-1]).transpose(2, 0, 1, 3, 4)

    def step(s, qkv):
        qi, ki, vi = qkv
        att = jnp.einsum('bhqd,bhkd->bhqk', qi, ki) * intra
        o = (jnp.einsum('bhqk,bhkv->bhqv', att, vi)
             + jnp.einsum('bhqd,bhdv->bhqv', qi * q_dec[..., None], s))
        s = s * chunk_dec[:, None, None] + jnp.einsum('bhkd,bhkv->bhdv', ki * k_dec[..., None], vi)
        return s, o

    s, o = lax.scan(step, s0, (blocks(q), blocks(k), blocks(v)))
    return o.transpose(1, 2, 0, 3, 4).reshape(B, H, L, -1), s


def retention_final_state(k, v, log_g):
    L = k.shape[2]
    w = jnp.exp(log_g[:, None] * (L - 1.0 - jnp.arange(L, dtype=jnp.float32)))
    return jnp.einsum('bhld,bhlv,hl->bhdv', k.astype(jnp.float32), v.astype(jnp.float32), w)


def retention_bidir(q, k, v, log_g_f, log_g_b, s0_f, s0_b):
    o_f, s_f = retention_chunkwise(q, k, v, log_g_f, s0_f)
    o_b, s_b = retention_chunkwise(jnp.flip(q, 2), jnp.flip(k, 2), jnp.flip(v, 2), log_g_b, s0_b)
    return o_f + jnp.flip(o_b, 2), s_f, s_b


def retention_output(o, g, gn_w):
    B, H, L, dv = o.shape
    of = o.astype(jnp.float32).transpose(0, 2, 1, 3)
    mu = jnp.mean(of, axis=-1, keepdims=True)
    var = jnp.mean(jnp.square(of - mu), axis=-1, keepdims=True)
    y = ((of - mu) * lax.rsqrt(var + NORM_EPS)).reshape(B, L, H * dv) * gn_w.astype(jnp.float32)
    return (jax.nn.silu(g.astype(jnp.float32)) * y).astype(g.dtype)


def merge_branches(o_a, o_b, o_c, o_d, gates, w_branch, w_out):
    B, L, _ = gates.shape
    o = jnp.stack([o_a.astype(gates.dtype), o_b.astype(gates.dtype),
                   o_c.astype(gates.dtype), o_d.astype(gates.dtype)], axis=2)
    y = jnp.einsum('blnw,nwd->blnd', o, w_branch)
    g = jax.nn.sigmoid(gates.reshape(B, L, N_BRANCH, D_MODEL).astype(jnp.float32)).astype(y.dtype)
    return jnp.sum(g * y, axis=2) @ w_out


def sq_relu_mlp(h, w1, w2):
    return jnp.square(jax.nn.relu(h @ w1)) @ w2


def setup_inputs(seed: int = 0) -> dict:
    key = jax.random.key(seed)
    ks = jax.random.split(key, 30)
    f32 = jnp.float32

    def nrm(i, shape, scale):
        return jax.random.normal(ks[i], shape, f32) * scale

    def gain(i, shape):
        return 1.0 + 0.01 * jax.random.normal(ks[i], shape, f32)

    Ld = DEPTH
    n_idx = jnp.arange(S5_STATE, dtype=f32)
    gamma0 = 1.0 - 2.0 ** (-5.0 - np.arange(RET_HEADS))
    logit0 = jnp.asarray(np.log(gamma0 / (1.0 - gamma0)), dtype=f32)
    s5_shape = (Ld, 2, S5_GROUPS, S5_STATE)
    return {
        'x': nrm(0, (BATCH, SEQ, D_MODEL), 1.0),
        'c': nrm(1, (BATCH, D_MODEL), 1.0),
        'ctx': nrm(2, (BATCH, CTX_LEN, D_MODEL), 1.0),
        'c_ctx': nrm(3, (D_MODEL,), 1.0),
        'ada_w': nrm(4, (Ld, D_MODEL, ADA_CHUNKS * D_MODEL), ADA_INIT * D_MODEL ** -0.5),
        'ada_b': nrm(5, (Ld, ADA_CHUNKS * D_MODEL), 0.01),
        'norm_mix_w': gain(6, (Ld, D_MODEL)),
        'norm_ffn_w': gain(7, (Ld, D_MODEL)),
        'w_in': nrm(8, (Ld, D_MODEL, IN_COLS), D_MODEL ** -0.5),
        'mla_q_norm': gain(9, (Ld, Q_LORA)),
        'mla_w_uq': nrm(10, (Ld, Q_LORA, MLA_HEADS * MLA_QK), Q_LORA ** -0.5),
        'mla_kv_norm': gain(11, (Ld, KV_LORA)),
        'mla_w_ukv': nrm(12, (Ld, KV_LORA, MLA_HEADS * (MLA_NOPE + MLA_V)), KV_LORA ** -0.5),
        'mla_qk_norm_q': gain(13, (Ld, MLA_QK)),
        'mla_qk_norm_k': gain(14, (Ld, MLA_QK)),
        's5_lam_re': -0.5 + nrm(15, s5_shape, 0.01),
        's5_lam_im': math.pi * n_idx + nrm(16, s5_shape, 0.01),
        's5_log_step': jax.random.uniform(ks[17], (Ld, 2, S5_GROUPS), f32,
                                          math.log(S5_DT_MIN), math.log(S5_DT_MAX)),
        's5_b_re': nrm(18, (Ld, 2, S5_GROUPS, S5_STATE, S5_GROUP_CH), (2 * S5_GROUP_CH) ** -0.5),
        's5_b_im': nrm(19, (Ld, 2, S5_GROUPS, S5_STATE, S5_GROUP_CH), (2 * S5_GROUP_CH) ** -0.5),
        's5_c_re': nrm(20, (Ld, 2, S5_GROUPS, S5_GROUP_CH, S5_STATE), (2 * S5_STATE) ** -0.5),
        's5_c_im': nrm(21, (Ld, 2, S5_GROUPS, S5_GROUP_CH, S5_STATE), (2 * S5_STATE) ** -0.5),
        's5_d': nrm(22, (Ld, BRANCH_W), 1.0),
        's5_w_glu': nrm(23, (Ld, BRANCH_W, 2 * BRANCH_W), BRANCH_W ** -0.5),
        'ret_decay_logit': logit0 + nrm(24, (Ld, 2, RET_HEADS), 0.01),
        'ret_gn_w': gain(25, (Ld, BRANCH_W)),
        'w_branch': nrm(26, (Ld, N_BRANCH, BRANCH_W, D_MODEL), BRANCH_W ** -0.5),
        'w_out': nrm(27, (Ld, D_MODEL, D_MODEL), D_MODEL ** -0.5),
        'ffn_w1': nrm(28, (Ld, D_MODEL, D_FF), D_MODEL ** -0.5),
        'ffn_w2': nrm(29, (Ld, D_FF, D_MODEL), D_FF ** -0.5),
    }


def reference(x, c, ctx, c_ctx, ada_w, ada_b, norm_mix_w, norm_ffn_w, w_in,
              mla_q_norm, mla_w_uq, mla_kv_norm, mla_w_ukv, mla_qk_norm_q, mla_qk_norm_k,
              s5_lam_re, s5_lam_im, s5_log_step, s5_b_re, s5_b_im, s5_c_re, s5_c_im,
              s5_d, s5_w_glu, ret_decay_logit, ret_gn_w, w_branch, w_out, ffn_w1, ffn_w2):
    f32 = jnp.float32
    B, L, _ = x.shape
    ROWS = L // GRID_W
    rows = jnp.repeat(jnp.arange(ROWS, dtype=f32), GRID_W)
    cols = jnp.tile(jnp.arange(GRID_W, dtype=f32), ROWS)
    pos = jnp.arange(L, dtype=f32)
    ret_scale = RET_HD ** -0.5
    zero_ret = jnp.zeros((B, RET_HEADS, RET_HD, RET_HD), f32)
    full_sizes = STATE_SIZES + MAIN_SIZES

    xl, xc = x, ctx
    for l in range(DEPTH):
        last = l == DEPTH - 1
        mod_l = (jax.nn.silu(c) @ ada_w[l] + ada_b[l])[:, None, :]
        mod_c = (jax.nn.silu(c_ctx) @ ada_w[l] + ada_b[l])[None, None, :]
        sh1, sc1, g1, sh2, sc2, g2 = jnp.split(mod_l, ADA_CHUNKS, axis=-1)
        csh1, csc1, cg1, csh2, csc2, cg2 = jnp.split(mod_c, ADA_CHUNKS, axis=-1)

        lam_f, bbar_f = s5_discretise(s5_lam_re[l, 0], s5_lam_im[l, 0], s5_log_step[l, 0],
                                      s5_b_re[l, 0], s5_b_im[l, 0])
        lam_b, bbar_b = s5_discretise(s5_lam_re[l, 1], s5_lam_im[l, 1], s5_log_step[l, 1],
                                      s5_b_re[l, 1], s5_b_im[l, 1])
        cmat_f = lax.complex(s5_c_re[l, 0].astype(f32), s5_c_im[l, 0].astype(f32))
        cmat_b = lax.complex(s5_c_re[l, 1].astype(f32), s5_c_im[l, 1].astype(f32))
        log_g_f = jax.nn.log_sigmoid(ret_decay_logit[l, 0].astype(f32))
        log_g_b = jax.nn.log_sigmoid(ret_decay_logit[l, 1].astype(f32))

        hc = ada_rms(xc, norm_mix_w[l], csh1, csc1)
        if last:
            zc = split_cols(hc @ w_in[l][:, :STATE_COLS], STATE_SIZES)
        else:
            zc = split_cols(hc @ w_in[l], full_sizes)
        k_ctx, v_ctx = mla_keys(zc[0], zc[1], mla_kv_norm[l], mla_w_ukv[l], mla_qk_norm_k[l], None, None)
        xs_cf = s5_scan(zc[2], lam_f, bbar_f, None, False)
        xs_cb = s5_scan(zc[2], lam_b, bbar_b, None, True)
        rk_c = retention_heads(zc[3], None) * ret_scale
        rv_c = retention_heads(zc[4], None)
        if last:
            s_f = retention_final_state(rk_c, rv_c, log_g_f)
            s_b = retention_final_state(jnp.flip(rk_c, 2), jnp.flip(rv_c, 2), log_g_b)
        else:
            rq_c = retention_heads(zc[7], None)
            o_ret_c, s_f, s_b = retention_bidir(rq_c, rk_c, rv_c, log_g_f, log_g_b, zero_ret, zero_ret)

        hl = ada_rms(xl, norm_mix_w[l], sh1, sc1)
        zl = split_cols(hl @ w_in[l], full_sizes)
        k_l, v_l = mla_keys(zl[0], zl[1], mla_kv_norm[l], mla_w_ukv[l], mla_qk_norm_k[l], rows, cols)
        q_l = mla_queries(zl[5], mla_q_norm[l], mla_w_uq[l], mla_qk_norm_q[l], rows, cols)
        o_a = block_softmax_attention(q_l, jnp.concatenate([k_ctx, k_l], axis=1),
                                      jnp.concatenate([v_ctx, v_l], axis=1))
        o_b = fourier_mix(zl[6])
        xs_f = s5_scan(zl[2], lam_f, bbar_f, xs_cf[:, -1], False)
        xs_b = s5_scan(zl[2], lam_b, bbar_b, xs_cb[:, 0], True)
        o_c = s5_readout(zl[2], xs_f, xs_b, cmat_f, cmat_b, s5_d[l], s5_w_glu[l])
        rq = retention_heads(zl[7], pos)
        rk = retention_heads(zl[3], pos) * ret_scale
        rv = retention_heads(zl[4], None)
        o_ret, _, _ = retention_bidir(rq, rk, rv, log_g_f, log_g_b, s_f, s_b)
        o_d = retention_output(o_ret, zl[8], ret_gn_w[l])
        xl_new = xl + g1 * merge_branches(o_a, o_b, o_c, o_d, zl[9], w_branch[l], w_out[l])
        xl_new = xl_new + g2 * sq_relu_mlp(ada_rms(xl_new, norm_ffn_w[l], sh2, sc2), ffn_w1[l], ffn_w2[l])

        if not last:
            q_cx = mla_queries(zc[5], mla_q_norm[l], mla_w_uq[l], mla_qk_norm_q[l], None, None)
            oc_a = block_softmax_attention(q_cx, k_ctx, v_ctx)
            oc_b = fourier_mix(zc[6])
            oc_c = s5_readout(zc[2], xs_cf, xs_cb, cmat_f, cmat_b, s5_d[l], s5_w_glu[l])
            oc_d = retention_output(o_ret_c, zc[8], ret_gn_w[l])
            xc = xc + cg1 * merge_branches(oc_a, oc_b, oc_c, oc_d, zc[9], w_branch[l], w_out[l])
            xc = xc + cg2 * sq_relu_mlp(ada_rms(xc, norm_ffn_w[l], csh2, csc2), ffn_w1[l], ffn_w2[l])
        xl = xl_new
    return xl
```

```python
import functools
import math

import numpy as np
import jax
import jax.numpy as jnp
from jax import lax
from jax.experimental import pallas as pl
from jax.experimental.pallas import tpu as pltpu

F32 = jnp.float32
BF16 = jnp.bfloat16

D_MODEL = 1024
BATCH = 8
SEQ = 2048
DEPTH = 2
GRID_W = 64
CTX_LEN = 256
LCAT = CTX_LEN + SEQ
N_BRANCH = 4
BRANCH_W = 256
NORM_EPS = 1e-6
ROPE_BASE = 10000.0
ADA_CHUNKS = 6

MLA_HEADS = 4
MLA_NOPE = 64
MLA_ROPE = 32
MLA_QK = MLA_NOPE + MLA_ROPE
MLA_V = 64
Q_LORA = 256
KV_LORA = 128
HEAD_PAD = 128

FNET_GROUPS = 4
FNET_GW = BRANCH_W // FNET_GROUPS

S5_GROUP_CH = 16
S5_GROUPS = BRANCH_W // S5_GROUP_CH
S5_STATE = 64
S5_LANES = S5_GROUPS * S5_STATE

RET_HEADS = 4
RET_HD = BRANCH_W // RET_HEADS
RET_CHUNK = 128

D_FF = 4 * D_MODEL

_O_KV, _O_KR, _O_S5, _O_RK, _O_RV = 0, 128, 160, 416, 672
_O_Q, _O_FN, _O_RQ, _O_RG, _O_GATE = 928, 1184, 1440, 1696, 1952
IN_COLS = _O_GATE + N_BRANCH * D_MODEL
MAIN_COLS = 2048

TM = 256
N_TILES = LCAT // TM
S5_TC = 128
S5_ROWS = S5_TC * BATCH
S5_STEPS = LCAT // S5_TC
S5_CTX_STEPS = CTX_LEN // S5_TC
RET_NCHUNK = LCAT // RET_CHUNK
RET_CTX_CHUNKS = CTX_LEN // RET_CHUNK

VMEM_LIMIT = 56 * 1024 * 1024


def _cparams(n_grid):
    return pltpu.CompilerParams(dimension_semantics=("arbitrary",) * n_grid,
                                vmem_limit_bytes=VMEM_LIMIT)


def _dot(a, b):
    return jnp.dot(a, b, preferred_element_type=F32)


def _sigmoid(x):
    return 0.5 * (jnp.tanh(0.5 * x) + 1.0)


def _gelu_tanh(y):
    return 0.5 * y * (1.0 + jnp.tanh(math.sqrt(2.0 / math.pi) * (y + 0.044715 * (y * y * y))))


def _rms(x, w):
    return x * lax.rsqrt(jnp.mean(x * x, axis=-1, keepdims=True) + NORM_EPS) * w


def _full(shape):
    n = len(shape)
    return pl.BlockSpec(shape, lambda *_: (0,) * n)


def _mla_rope_perm():
    r = np.arange(MLA_ROPE)
    first = (r % 16) < 8
    return np.where(first, r + 8, r - 8), np.where(first, -1.0, 1.0)


@functools.lru_cache(maxsize=None)
def _mla_tables():
    pos = np.arange(SEQ)
    rows, cols = pos // GRID_W, pos % GRID_W
    freqs = ROPE_BASE ** (-np.arange(8, dtype=np.float64) / 8)
    r = np.arange(MLA_ROPE)
    _, sign = _mla_rope_perm()
    p = np.where((r // 16 == 0)[None, :], rows[:, None], cols[:, None]).astype(np.float64)
    ang = p * freqs[(r % 16) % 8][None, :]
    cosf = np.zeros((LCAT, HEAD_PAD))
    sinf = np.zeros((LCAT, HEAD_PAD))
    cosf[:, :MLA_QK] = 1.0
    cosf[CTX_LEN:, MLA_NOPE:MLA_QK] = np.cos(ang)
    sinf[CTX_LEN:, MLA_NOPE:MLA_QK] = np.sin(ang) * sign[None, :]
    return cosf.astype(np.float32), sinf.astype(np.float32)


def _ret_perm():
    d = np.arange(RET_HD)
    first = d < RET_HD // 2
    return np.where(first, d + RET_HD // 2, d - RET_HD // 2), np.where(first, -1.0, 1.0)


@functools.lru_cache(maxsize=None)
def _ret_tables():
    half = RET_HD // 2
    pos = np.arange(SEQ, dtype=np.float64)
    freqs = ROPE_BASE ** (-np.arange(half, dtype=np.float64) / half)
    d = np.arange(RET_HD)
    perm, sign = _ret_perm()
    ang = pos[:, None] * freqs[d % half][None, :]
    cosr = np.ones((LCAT, RET_HD))
    sinr = np.zeros((LCAT, RET_HD))
    cosr[CTX_LEN:] = np.cos(ang)
    sinr[CTX_LEN:] = np.sin(ang) * sign[None, :]
    cosr = np.tile(cosr, (1, RET_HEADS))
    sinr = np.tile(sinr, (1, RET_HEADS))
    pm = np.zeros((BRANCH_W, BRANCH_W))
    for h in range(RET_HEADS):
        pm[h * RET_HD + perm, h * RET_HD + d] = 1.0
    return cosr.astype(np.float32), sinr.astype(np.float32), pm.astype(np.float32)


def _dft(n, scale):
    k = np.arange(n)
    kt = (k[:, None] * k[None, :]) % n
    ang = 2.0 * np.pi * kt / n
    return np.cos(ang) * scale, np.sin(ang) * scale


@functools.lru_cache(maxsize=None)
def _fnet_tables():
    cw, sw = _dft(FNET_GW, 1.0)
    t = np.zeros((BRANCH_W, 2 * BRANCH_W))
    for g in range(FNET_GROUPS):
        s = slice(g * FNET_GW, (g + 1) * FNET_GW)
        t[s, s] = cw
        t[s, BRANCH_W + g * FNET_GW:BRANCH_W + (g + 1) * FNET_GW] = -sw
    cl, sl = _dft(SEQ, 1.0 / math.sqrt(SEQ * FNET_GW))
    clc, slc = _dft(CTX_LEN, 1.0 / math.sqrt(CTX_LEN * FNET_GW))
    return tuple(a.astype(np.float32) for a in (t, cl, sl, clc, slc))


def _bf16_const(a):
    return jnp.asarray(a).astype(BF16)


@functools.lru_cache(maxsize=None)
def _head_avg():
    p = np.zeros((BRANCH_W, BRANCH_W))
    for h in range(RET_HEADS):
        p[h * RET_HD:(h + 1) * RET_HD, h * RET_HD:(h + 1) * RET_HD] = 1.0 / RET_HD
    return p.astype(np.float32)


ADA_TN = 1536


def _ada_kernel(c_ref, w_ref, b_ref, o_ref):
    c = c_ref[...]
    s = (c * _sigmoid(c)).astype(BF16)
    o_ref[0] = _dot(s, w_ref[0].astype(BF16)) + b_ref[0]


def _ada_mod(c16, ada_w, ada_b):
    n = ADA_CHUNKS * D_MODEL
    return pl.pallas_call(
        _ada_kernel,
        grid=(DEPTH, n // ADA_TN),
        in_specs=[pl.BlockSpec((16, D_MODEL), lambda l, j: (0, 0)),
                  pl.BlockSpec((1, D_MODEL, ADA_TN), lambda l, j: (l, 0, j)),
                  pl.BlockSpec((1, 1, ADA_TN), lambda l, j: (l, 0, j))],
        out_specs=pl.BlockSpec((1, 16, ADA_TN), lambda l, j: (l, 0, j)),
        out_shape=jax.ShapeDtypeStruct((DEPTH, 16, n), F32),
        compiler_params=_cparams(2),
        name="ada_mod",
    )(c16, ada_w, ada_b.reshape(DEPTH, 1, n))


def _head_norm_rot(xf, xp, a, b, scale):
    outs = []
    for h in range(MLA_HEADS):
        f = xf[:, h * HEAD_PAD:(h + 1) * HEAD_PAD]
        p = xp[:, h * HEAD_PAD:(h + 1) * HEAD_PAD]
        n = lax.rsqrt(jnp.sum(f * f, axis=-1, keepdims=True) * (1.0 / MLA_QK) + NORM_EPS) * scale
        outs.append(n * (f * a + p * b))
    return jnp.concatenate(outs, axis=-1)


def _inproj_kernel(x_ref, modl_ref, modc_ref, nw_ref, w_ref, kvw_ref, wkv_ref, qnw_ref, wq_ref,
                   cosf_ref, sinf_ref, wk_ref, wkp_ref, wqq_ref, wqp_ref, pm_ref, cosr_ref, sinr_ref,
                   t_ref,
                   q_out, k_out, v_out, u_out, uc_out, us_out, rq_out, rk_out, rv_out, rg_out):
    is_ctx = pl.program_id(1) == 0
    mod = jnp.where(is_ctx, modc_ref[0], modl_ref[0])
    sh, sc = mod[:, 0:D_MODEL], mod[:, D_MODEL:2 * D_MODEL]
    h = (_rms(x_ref[0], nw_ref[...]) * (1.0 + sc) + sh).astype(BF16)
    z = _dot(h, w_ref[...])

    cosf, sinf = cosf_ref[...], sinf_ref[...]
    kvn = _rms(z[:, 0:128], kvw_ref[...]).astype(BF16)
    lhs = jnp.concatenate([kvn, z[:, 128:256].astype(BF16)], axis=-1)
    kv = _dot(lhs, wkv_ref[...])
    k = _head_norm_rot(kv[:, 0:512], kv[:, 512:1024],
                       cosf * wk_ref[...], sinf * wkp_ref[...], 1.0)
    k_out[0] = k.astype(BF16)
    v_out[0] = kv[:, 1024:1280].astype(BF16)
    qn = _rms(z[:, 256:512], qnw_ref[...]).astype(BF16)
    qq = _dot(qn, wq_ref[...])
    q = _head_norm_rot(qq[:, 0:512], qq[:, 512:1024],
                       cosf * wqq_ref[...], sinf * wqp_ref[...], MLA_QK ** -0.5)
    q_out[0] = q.astype(BF16)
    u_out[...] = z[:, 512:768].astype(BF16)
    ucs = _dot(z[:, 768:1024].astype(BF16), t_ref[...])
    uc_out[...] = ucs[:, 0:BRANCH_W].astype(BF16)
    us_out[...] = ucs[:, BRANCH_W:2 * BRANCH_W].astype(BF16)
    cosr, sinr = cosr_ref[...], sinr_ref[...]
    rq = z[:, 1024:1280]
    rk = z[:, 1280:1536]
    rq = rq * cosr + _dot(rq.astype(BF16), pm_ref[...]) * sinr
    rk = rk * cosr + _dot(rk.astype(BF16), pm_ref[...]) * sinr
    rq_out[0] = rq.astype(BF16)
    rk_out[0] = (rk * (RET_HD ** -0.5)).astype(BF16)
    rv_out[0] = z[:, 1536:1792].astype(BF16)
    rg_out[0] = z[:, 1792:2048].astype(BF16)


def _in_proj(xcat, modl, modc, lw):
    tok = lambda w: pl.BlockSpec((1, TM, w), lambda b, j: (b, j, 0))
    tmaj = pl.BlockSpec((TM, BRANCH_W), lambda b, j: (j, b))
    tab = lambda w: pl.BlockSpec((TM, w), lambda b, j: (j, 0))
    cosf, sinf = _mla_tables()
    cosr, sinr, pm = _ret_tables()
    t = _bf16_const(_fnet_tables()[0])
    bshape = lambda w: jax.ShapeDtypeStruct((BATCH, LCAT, w), BF16)
    tshape = jax.ShapeDtypeStruct((LCAT, BATCH * BRANCH_W), BF16)
    return pl.pallas_call(
        _inproj_kernel,
        grid=(BATCH, N_TILES),
        in_specs=[tok(D_MODEL),
                  pl.BlockSpec((1, 1, ADA_CHUNKS * D_MODEL), lambda b, j: (b, 0, 0)),
                  _full((1, 1, ADA_CHUNKS * D_MODEL)),
                  _full((1, D_MODEL)),
                  _full((D_MODEL, MAIN_COLS)),
                  _full((1, KV_LORA)),
                  _full((256, 1280)),
                  _full((1, Q_LORA)),
                  _full((Q_LORA, 1024)),
                  tab(HEAD_PAD), tab(HEAD_PAD),
                  _full((1, HEAD_PAD)), _full((1, HEAD_PAD)), _full((1, HEAD_PAD)), _full((1, HEAD_PAD)),
                  _full((BRANCH_W, BRANCH_W)),
                  tab(BRANCH_W), tab(BRANCH_W),
                  _full((BRANCH_W, 2 * BRANCH_W))],
        out_specs=[tok(512), tok(512), tok(256), tmaj, tmaj, tmaj, tok(256), tok(256), tok(256), tok(256)],
        out_shape=[bshape(512), bshape(512), bshape(256), tshape, tshape, tshape,
                   bshape(256), bshape(256), bshape(256), bshape(256)],
        compiler_params=_cparams(2),
        name="in_proj",
    )(xcat, modl, modc, lw["norm_mix"], lw["w_main"], lw["kv_norm"], lw["wkv"], lw["q_norm"], lw["wq"],
      jnp.asarray(cosf), jnp.asarray(sinf), lw["wk"], lw["wkp"], lw["wqq"], lw["wqp"],
      jnp.asarray(pm, dtype=BF16), jnp.asarray(cosr), jnp.asarray(sinr), t)


def _attn_body(q_ref, k_ref, v_ref, o_ref, nk):
    q = q_ref[0]
    v = v_ref[0, 0:nk, :]
    lane = lax.broadcasted_iota(jnp.int32, (1, BRANCH_W), 1)
    acc = jnp.zeros((TM, BRANCH_W), F32)
    for h in range(MLA_HEADS):
        qh = q[:, h * HEAD_PAD:(h + 1) * HEAD_PAD]
        kh = k_ref[0, 0:nk, h * HEAD_PAD:(h + 1) * HEAD_PAD]
        s = lax.dot_general(qh, kh, (((1,), (1,)), ((), ())), preferred_element_type=F32)
        p = jnp.exp(s - jnp.max(s, axis=-1, keepdims=True))
        inv = 1.0 / jnp.sum(p, axis=-1, keepdims=True)
        oh = _dot(p.astype(BF16), v)
        acc = acc + jnp.where(lane // MLA_V == h, oh * inv, 0.0)
    o_ref[0] = acc.astype(BF16)


def _attn_kernel(q_ref, k_ref, v_ref, o_ref, *, off):
    if off:
        _attn_body(q_ref, k_ref, v_ref, o_ref, LCAT)
    else:
        j = pl.program_id(1)
        pl.when(j == 0)(lambda: _attn_body(q_ref, k_ref, v_ref, o_ref, CTX_LEN))
        pl.when(j > 0)(lambda: _attn_body(q_ref, k_ref, v_ref, o_ref, LCAT))


def _attention(q, k, v, off):
    nt = N_TILES - off
    return pl.pallas_call(
        functools.partial(_attn_kernel, off=off),
        grid=(BATCH, nt),
        in_specs=[pl.BlockSpec((1, TM, 512), lambda b, j: (b, j + off, 0)),
                  pl.BlockSpec((1, LCAT, 512), lambda b, j: (b, 0, 0)),
                  pl.BlockSpec((1, LCAT, 256), lambda b, j: (b, 0, 0))],
        out_specs=pl.BlockSpec((1, TM, BRANCH_W), lambda b, j: (b, j, 0)),
        out_shape=jax.ShapeDtypeStruct((BATCH, nt * TM, BRANCH_W), BF16),
        compiler_params=_cparams(2),
        name="mla_attention",
    )(q, k, v)


def _fnet_kernel(uc_ref, us_ref, cl_ref, sl_ref, clc_ref, slc_ref, o_ref):
    o_ref[0:CTX_LEN, :] = (_dot(clc_ref[...], uc_ref[0:CTX_LEN, :])
                           + _dot(slc_ref[...], us_ref[0:CTX_LEN, :])).astype(BF16)
    o_ref[CTX_LEN:LCAT, :] = (_dot(cl_ref[...], uc_ref[CTX_LEN:LCAT, :])
                              + _dot(sl_ref[...], us_ref[CTX_LEN:LCAT, :])).astype(BF16)


def _fnet(uc, us):
    cl, sl, clc, slc = (_bf16_const(a) for a in _fnet_tables()[1:])
    col = pl.BlockSpec((LCAT, BRANCH_W), lambda b: (0, b))
    return pl.pallas_call(
        _fnet_kernel,
        grid=(BATCH,),
        in_specs=[col, col, _full((SEQ, SEQ)), _full((SEQ, SEQ)),
                  _full((CTX_LEN, CTX_LEN)), _full((CTX_LEN, CTX_LEN))],
        out_specs=col,
        out_shape=jax.ShapeDtypeStruct((LCAT, BATCH * BRANCH_W), BF16),
        compiler_params=_cparams(1),
        name="fnet_dft",
    )(uc, us, cl, sl, clc, slc)


def _s5_param_kernel(lr_ref, li_ref, ls_ref, bre_ref, bim_ref, lam_out, b_out):
    lr, li = lr_ref[0], li_ref[0]
    step = jnp.exp(ls_ref[0])
    mag = jnp.exp(lr * step)
    lbr = mag * jnp.cos(li * step)
    lbi = mag * jnp.sin(li * step)
    den = 1.0 / (lr * lr + li * li)
    cr = ((lbr - 1.0) * lr + lbi * li) * den
    ci = (lbi * lr - (lbr - 1.0) * li) * den
    lam_out[0, 0] = jnp.broadcast_to(lbr, (BATCH, S5_LANES))
    lam_out[0, 1] = jnp.broadcast_to(lbi, (BATCH, S5_LANES))
    bre, bim = bre_ref[0], bim_ref[0]
    b_out[0, :, 0:S5_LANES] = (cr * bre - ci * bim).astype(BF16)
    b_out[0, :, S5_LANES:2 * S5_LANES] = (cr * bim + ci * bre).astype(BF16)


def _s5_params(lam_re, lam_im, log_step, bre_blk, bim_blk):
    vec = pl.BlockSpec((1, 1, S5_LANES), lambda d: (d, 0, 0))
    blk = pl.BlockSpec((1, BRANCH_W, S5_LANES), lambda d: (d, 0, 0))
    return pl.pallas_call(
        _s5_param_kernel,
        grid=(2,),
        in_specs=[vec, vec, vec, blk, blk],
        out_specs=[pl.BlockSpec((1, 2, BATCH, S5_LANES), lambda d: (d, 0, 0, 0)),
                   pl.BlockSpec((1, BRANCH_W, 2 * S5_LANES), lambda d: (d, 0, 0))],
        out_shape=[jax.ShapeDtypeStruct((2, 2, BATCH, S5_LANES), F32),
                   jax.ShapeDtypeStruct((2, BRANCH_W, 2 * S5_LANES), BF16)],
        compiler_params=_cparams(1),
        name="s5_discretise",
    )(lam_re, lam_im, log_step, bre_blk, bim_blk)


def _s5_bwd_block(i):
    return jnp.where(i < S5_CTX_STEPS, S5_CTX_STEPS - 1 - i, S5_STEPS + S5_CTX_STEPS - 1 - i)


def _s5_kernel(uf_ref, ub_ref, lam_ref, b_ref, c_ref, yf_ref, yb_ref, xf_scr, xb_scr, st_scr):
    @pl.when(pl.program_id(0) == 0)
    def _():
        st_scr[...] = jnp.zeros_like(st_scr)

    xf_scr[...] = _dot(uf_ref[...], b_ref[0])
    xb_scr[...] = _dot(ub_ref[...], b_ref[1])
    re, im = pl.ds(0, S5_LANES), pl.ds(S5_LANES, S5_LANES)

    def body(t, carry):
        fr, fi, br, bi = carry
        rf = pl.ds(pl.multiple_of(t * BATCH, BATCH), BATCH)
        rb = pl.ds(pl.multiple_of((S5_TC - 1 - t) * BATCH, BATCH), BATCH)
        afr, afi = lam_ref[0, 0], lam_ref[0, 1]
        abr, abi = lam_ref[1, 0], lam_ref[1, 1]
        nfr = afr * fr - afi * fi + xf_scr[rf, re]
        nfi = afr * fi + afi * fr + xf_scr[rf, im]
        nbr = abr * br - abi * bi + xb_scr[rb, re]
        nbi = abr * bi + abi * br + xb_scr[rb, im]
        xf_scr[rf, re] = nfr
        xf_scr[rf, im] = nfi
        xb_scr[rb, re] = nbr
        xb_scr[rb, im] = nbi
        return nfr, nfi, nbr, nbi

    carry = lax.fori_loop(0, S5_TC, body, (st_scr[0], st_scr[1], st_scr[2], st_scr[3]))
    for n in range(4):
        st_scr[n] = carry[n]
    yf_ref[...] = _dot(xf_scr[...].astype(BF16), c_ref[0]).astype(BF16)
    yb_ref[...] = _dot(xb_scr[...].astype(BF16), c_ref[1]).astype(BF16)


def _s5(u_t, lam, bblk, cblk):
    rows = LCAT * BATCH
    fwd = pl.BlockSpec((S5_ROWS, BRANCH_W), lambda i: (i, 0))
    bwd = pl.BlockSpec((S5_ROWS, BRANCH_W), lambda i: (_s5_bwd_block(i), 0))
    u2 = u_t.reshape(rows, BRANCH_W)
    yf, yb = pl.pallas_call(
        _s5_kernel,
        grid=(S5_STEPS,),
        in_specs=[fwd, bwd, _full((2, 2, BATCH, S5_LANES)),
                  _full((2, BRANCH_W, 2 * S5_LANES)), _full((2, 2 * S5_LANES, BRANCH_W))],
        out_specs=[fwd, bwd],
        out_shape=[jax.ShapeDtypeStruct((rows, BRANCH_W), BF16)] * 2,
        scratch_shapes=[pltpu.VMEM((S5_ROWS, 2 * S5_LANES), F32),
                        pltpu.VMEM((S5_ROWS, 2 * S5_LANES), F32),
                        pltpu.VMEM((4, BATCH, S5_LANES), F32)],
        compiler_params=_cparams(1),
        name="s5_scan",
    )(u2, u2, lam, bblk, cblk)
    return yf.reshape(LCAT, BATCH * BRANCH_W), yb.reshape(LCAT, BATCH * BRANCH_W)


def _log_sigmoid(x):
    return jnp.minimum(x, 0.0) - jnp.log(1.0 + jnp.exp(-jnp.abs(x)))


def _ret_kernel(q_ref, k_ref, v_ref, lgl_ref, lgh_ref, o_ref, dec_scr, sb_scr, sf_scr, sbc_scr):
    c_len = RET_CHUNK
    lgl = _log_sigmoid(lgl_ref[...])
    lgf, lgb = lgl[0], lgl[1]
    ti = lax.broadcasted_iota(jnp.int32, (c_len, BRANCH_W), 0).astype(F32)
    qdf = jnp.exp(lgf * (ti + 1.0))
    kdf = jnp.exp(lgf * (c_len - 1.0 - ti))
    qdb = jnp.exp(lgb * (c_len - ti))
    kdb = jnp.exp(lgb * ti)
    cdf = jnp.exp(lgf * float(c_len))
    cdb = jnp.exp(lgb * float(c_len))
    lane = lax.broadcasted_iota(jnp.int32, (1, BRANCH_W), 1)
    rr = lax.broadcasted_iota(jnp.int32, (BRANCH_W, BRANCH_W), 0) // RET_HD
    cc = lax.broadcasted_iota(jnp.int32, (BRANCH_W, BRANCH_W), 1) // RET_HD
    same_head = rr == cc

    diff = (lax.broadcasted_iota(jnp.int32, (c_len, c_len), 0)
            - lax.broadcasted_iota(jnp.int32, (c_len, c_len), 1)).astype(F32)
    for h in range(RET_HEADS):
        gf = _log_sigmoid(lgh_ref[0, h])
        gb = _log_sigmoid(lgh_ref[1, h])
        dec_scr[h] = (jnp.where(diff >= 0, jnp.exp(gf * jnp.maximum(diff, 0.0)), 0.0)
                      + jnp.where(diff <= 0, jnp.exp(gb * jnp.maximum(-diff, 0.0)), 0.0))

    def chunk(ref, c):
        return ref[0, pl.ds(pl.multiple_of(c * c_len, c_len), c_len), :]

    def kv_outer(kd, v):
        s = lax.dot_general(kd.astype(BF16), v, (((0,), (0,)), ((), ())), preferred_element_type=F32)
        return jnp.where(same_head, s, 0.0)

    sbc_scr[...] = jnp.zeros_like(sbc_scr)

    def bwd(i, _):
        c = jnp.where(i < RET_CTX_CHUNKS, RET_CTX_CHUNKS - 1 - i, RET_NCHUNK + RET_CTX_CHUNKS - 1 - i)
        sb_scr[c] = sbc_scr[...].astype(BF16)
        k = chunk(k_ref, c).astype(F32)
        sbc_scr[...] = sbc_scr[...] * cdb + kv_outer(k * kdb, chunk(v_ref, c))
        return 0

    lax.fori_loop(0, RET_NCHUNK, bwd, 0)

    sf_scr[...] = jnp.zeros_like(sf_scr)

    def fwd(c, _):
        qb, kb, vb = chunk(q_ref, c), chunk(k_ref, c), chunk(v_ref, c)
        q = qb.astype(F32)
        o = (_dot((q * qdf).astype(BF16), sf_scr[...].astype(BF16))
             + _dot((q * qdb).astype(BF16), sb_scr[c]))
        for h in range(RET_HEADS):
            hm = lane // RET_HD == h
            att = lax.dot_general(jnp.where(hm, qb, jnp.zeros_like(qb)), kb,
                                  (((1,), (1,)), ((), ())), preferred_element_type=F32)
            oh = _dot((att * dec_scr[h]).astype(BF16), vb)
            o = o + jnp.where(hm, oh, 0.0)
        o_ref[0, pl.ds(pl.multiple_of(c * c_len, c_len), c_len), :] = o.astype(BF16)
        sf_scr[...] = sf_scr[...] * cdf + kv_outer(kb.astype(F32) * kdf, vb)
        return 0

    lax.fori_loop(0, RET_NCHUNK, fwd, 0)


def _retention(rq, rk, rv, lgl, lgh):
    tok = pl.BlockSpec((1, LCAT, BRANCH_W), lambda b: (b, 0, 0))
    return pl.pallas_call(
        _ret_kernel,
        grid=(BATCH,),
        in_specs=[tok, tok, tok, _full((2, 1, BRANCH_W)), _full((2, RET_HEADS, 1, 128))],
        out_specs=tok,
        out_shape=jax.ShapeDtypeStruct((BATCH, LCAT, BRANCH_W), BF16),
        scratch_shapes=[pltpu.VMEM((RET_HEADS, RET_CHUNK, RET_CHUNK), F32),
                        pltpu.VMEM((RET_NCHUNK, BRANCH_W, BRANCH_W), BF16),
                        pltpu.VMEM((BRANCH_W, BRANCH_W), F32),
                        pltpu.VMEM((BRANCH_W, BRANCH_W), F32)],
        compiler_params=_cparams(1),
        name="retention",
    )(rq, rk, rv, lgl, lgh)


def _select_mod(modl_ref, modc_ref, off):
    is_ctx = (pl.program_id(1) + off) == 0
    return jnp.where(is_ctx, modc_ref[0], modl_ref[0])


def _merge_kernel(x_ref, modl_ref, modc_ref, nw_ref, wg_ref, oa_ref, ob_ref, yf_ref, yb_ref, u_ref,
                  d_ref, wglu_ref, oret_ref, rg_ref, gnw_ref, pavg_ref, wb_ref, wout_ref, o_ref, *, off):
    mod = _select_mod(modl_ref, modc_ref, off)
    sh, sc, gate_res = (mod[:, 0:D_MODEL], mod[:, D_MODEL:2 * D_MODEL], mod[:, 2 * D_MODEL:3 * D_MODEL])
    x = x_ref[0]
    h = (_rms(x, nw_ref[...]) * (1.0 + sc) + sh).astype(BF16)
    y = yf_ref[...].astype(F32) + yb_ref[...].astype(F32) + d_ref[...] * u_ref[...].astype(F32)
    vg = _dot(_gelu_tanh(y).astype(BF16), wglu_ref[...])
    oc = vg[:, 0:BRANCH_W] * _sigmoid(vg[:, BRANCH_W:2 * BRANCH_W])
    o = oret_ref[0]
    dl = o.astype(F32) - _dot(o, pavg_ref[...])
    var = _dot((dl * dl).astype(BF16), pavg_ref[...])
    g = rg_ref[0].astype(F32)
    od = g * _sigmoid(g) * (dl * lax.rsqrt(var + NORM_EPS) * gnw_ref[...])
    branches = (oa_ref[0], ob_ref[...], oc.astype(BF16), od.astype(BF16))
    acc = jnp.zeros((TM, D_MODEL), F32)
    for n in range(N_BRANCH):
        gate = _dot(h, wg_ref[:, n * D_MODEL:(n + 1) * D_MODEL])
        acc = acc + _sigmoid(gate) * _dot(branches[n], wb_ref[n])
    o_ref[0] = x + gate_res * _dot(acc.astype(BF16), wout_ref[...])


def _merge(xcat, modl, modc, lw, oa, ob, yf, yb, u_t, oret, rg, off):
    nt = N_TILES - off
    tok = lambda w: pl.BlockSpec((1, TM, w), lambda b, j: (b, j + off, 0))
    tmaj = pl.BlockSpec((TM, BRANCH_W), lambda b, j: (j + off, b))
    return pl.pallas_call(
        functools.partial(_merge_kernel, off=off),
        grid=(BATCH, nt),
        in_specs=[tok(D_MODEL),
                  pl.BlockSpec((1, 1, ADA_CHUNKS * D_MODEL), lambda b, j: (b, 0, 0)),
                  _full((1, 1, ADA_CHUNKS * D_MODEL)),
                  _full((1, D_MODEL)),
                  _full((D_MODEL, N_BRANCH * D_MODEL)),
                  pl.BlockSpec((1, TM, BRANCH_W), lambda b, j: (b, j, 0)),
                  tmaj, tmaj, tmaj, tmaj,
                  _full((1, BRANCH_W)),
                  _full((BRANCH_W, 2 * BRANCH_W)),
                  tok(BRANCH_W), tok(BRANCH_W),
                  _full((1, BRANCH_W)),
                  _full((BRANCH_W, BRANCH_W)),
                  _full((N_BRANCH, BRANCH_W, D_MODEL)),
                  _full((D_MODEL, D_MODEL))],
        out_specs=pl.BlockSpec((1, TM, D_MODEL), lambda b, j: (b, j, 0)),
        out_shape=jax.ShapeDtypeStruct((BATCH, nt * TM, D_MODEL), F32),
        compiler_params=_cparams(2),
        name="merge",
    )(xcat, modl, modc, lw["norm_mix"], lw["w_gate"], oa, ob, yf, yb, u_t, lw["s5_d"], lw["w_glu"],
      oret, rg, lw["gn_w"], jnp.asarray(_head_avg(), dtype=BF16), lw["w_branch"], lw["w_out"])


def _ffn_kernel(x_ref, modl_ref, modc_ref, nw_ref, w1_ref, w2_ref, o_ref, *, off):
    mod = _select_mod(modl_ref, modc_ref, off)
    sh, sc, gate_res = (mod[:, 3 * D_MODEL:4 * D_MODEL], mod[:, 4 * D_MODEL:5 * D_MODEL],
                        mod[:, 5 * D_MODEL:6 * D_MODEL])
    x = x_ref[0]
    h = (_rms(x, nw_ref[...]) * (1.0 + sc) + sh).astype(BF16)
    a = jnp.maximum(_dot(h, w1_ref[...]), 0.0)
    o_ref[0] = x + gate_res * _dot((a * a).astype(BF16), w2_ref[...])


def _ffn(xm, modl, modc, lw, off):
    nt = N_TILES - off
    tok = pl.BlockSpec((1, TM, D_MODEL), lambda b, j: (b, j, 0))
    return pl.pallas_call(
        functools.partial(_ffn_kernel, off=off),
        grid=(BATCH, nt),
        in_specs=[tok,
                  pl.BlockSpec((1, 1, ADA_CHUNKS * D_MODEL), lambda b, j: (b, 0, 0)),
                  _full((1, 1, ADA_CHUNKS * D_MODEL)),
                  _full((1, D_MODEL)),
                  _full((D_MODEL, D_FF)),
                  _full((D_FF, D_MODEL))],
        out_specs=tok,
        out_shape=jax.ShapeDtypeStruct((BATCH, nt * TM, D_MODEL), F32),
        compiler_params=_cparams(2),
        name="ffn",
    )(xm, modl, modc, lw["norm_ffn"], lw["ffn_w1"], lw["ffn_w2"])


def _layer_weights(l, p):
    w = p["w_in"][l]
    col = lambda o, n: w[:, o:o + n]
    w_main = jnp.concatenate(
        [col(_O_KV, 128), col(_O_KR, 32), jnp.zeros((D_MODEL, 96), F32), col(_O_Q, 256), col(_O_S5, 256),
         col(_O_FN, 256), col(_O_RQ, 256), col(_O_RK, 256), col(_O_RV, 256), col(_O_RG, 256)],
        axis=1).astype(BF16)

    perm, _ = _mla_rope_perm()
    wu = p["mla_w_ukv"][l].reshape(KV_LORA, MLA_HEADS, MLA_NOPE + MLA_V)
    kf_top = jnp.concatenate([wu[:, :, :MLA_NOPE], jnp.zeros((KV_LORA, MLA_HEADS, 64), F32)], -1)
    top = jnp.concatenate([kf_top.reshape(KV_LORA, 512), jnp.zeros((KV_LORA, 512), F32),
                           wu[:, :, MLA_NOPE:].reshape(KV_LORA, 256)], axis=1)
    place = np.zeros((128, 1280), np.float32)
    for h in range(MLA_HEADS):
        for r in range(MLA_ROPE):
            place[r, h * HEAD_PAD + MLA_NOPE + r] = 1.0
            place[perm[r], 512 + h * HEAD_PAD + MLA_NOPE + r] = 1.0
    wkv = jnp.concatenate([top, jnp.asarray(place)], axis=0).astype(BF16)

    wuq = p["mla_w_uq"][l].reshape(Q_LORA, MLA_HEADS, MLA_QK)
    qf = jnp.concatenate([wuq, jnp.zeros((Q_LORA, MLA_HEADS, 32), F32)], -1).reshape(Q_LORA, 512)
    qp = jnp.concatenate([jnp.zeros((Q_LORA, MLA_HEADS, MLA_NOPE), F32),
                          wuq[:, :, MLA_NOPE:][:, :, perm],
                          jnp.zeros((Q_LORA, MLA_HEADS, 32), F32)], -1).reshape(Q_LORA, 512)
    wq = jnp.concatenate([qf, qp], axis=1).astype(BF16)

    def head_w(v):
        wf = jnp.concatenate([v, jnp.zeros((32,), F32)])[None, :]
        wp = jnp.concatenate([jnp.zeros((MLA_NOPE,), F32), v[MLA_NOPE:][perm], jnp.zeros((32,), F32)])[None, :]
        return wf, wp

    wk, wkp = head_w(p["mla_qk_norm_k"][l])
    wqq, wqp = head_w(p["mla_qk_norm_q"][l])

    eye = jnp.eye(S5_GROUPS, dtype=F32)
    bre = jnp.einsum("dgph,gG->dghGp", p["s5_b_re"][l], eye).reshape(2, BRANCH_W, S5_LANES)
    bim = jnp.einsum("dgph,gG->dghGp", p["s5_b_im"][l], eye).reshape(2, BRANCH_W, S5_LANES)
    cre = jnp.einsum("dghp,gG->dgpGh", p["s5_c_re"][l], eye).reshape(2, S5_LANES, BRANCH_W)
    cim = jnp.einsum("dghp,gG->dgpGh", p["s5_c_im"][l], eye).reshape(2, S5_LANES, BRANCH_W)
    cblk = jnp.concatenate([cre, -cim], axis=1).astype(BF16)
    lam_re = p["s5_lam_re"][l].reshape(2, 1, S5_LANES)
    lam_im = p["s5_lam_im"][l].reshape(2, 1, S5_LANES)
    log_step = jnp.repeat(p["s5_log_step"][l], S5_STATE, axis=-1).reshape(2, 1, S5_LANES)

    logit = p["ret_decay_logit"][l]
    lgl = jnp.repeat(logit, RET_HD, axis=-1).reshape(2, 1, BRANCH_W)
    lgh = jnp.broadcast_to(logit[:, :, None, None], (2, RET_HEADS, 1, 128))

    return dict(
        w_main=w_main, w_gate=w[:, _O_GATE:].astype(BF16),
        norm_mix=p["norm_mix_w"][l][None, :], norm_ffn=p["norm_ffn_w"][l][None, :],
        kv_norm=p["mla_kv_norm"][l][None, :], q_norm=p["mla_q_norm"][l][None, :],
        wkv=wkv, wq=wq, wk=wk, wkp=wkp, wqq=wqq, wqp=wqp,
        s5=(lam_re, lam_im, log_step, bre, bim), cblk=cblk,
        s5_d=p["s5_d"][l][None, :], w_glu=p["s5_w_glu"][l].astype(BF16),
        lgl=lgl, lgh=lgh, gn_w=p["ret_gn_w"][l][None, :],
        w_branch=p["w_branch"][l].astype(BF16), w_out=p["w_out"][l].astype(BF16),
        ffn_w1=p["ffn_w1"][l].astype(BF16), ffn_w2=p["ffn_w2"][l].astype(BF16))


def kernel(x, c, ctx, c_ctx, ada_w, ada_b, norm_mix_w, norm_ffn_w, w_in, mla_q_norm, mla_w_uq, mla_kv_norm,
           mla_w_ukv, mla_qk_norm_q, mla_qk_norm_k, s5_lam_re, s5_lam_im, s5_log_step, s5_b_re, s5_b_im,
           s5_c_re, s5_c_im, s5_d, s5_w_glu, ret_decay_logit, ret_gn_w, w_branch, w_out, ffn_w1, ffn_w2):
    p = dict(norm_mix_w=norm_mix_w, norm_ffn_w=norm_ffn_w, w_in=w_in, mla_q_norm=mla_q_norm,
             mla_w_uq=mla_w_uq, mla_kv_norm=mla_kv_norm, mla_w_ukv=mla_w_ukv, mla_qk_norm_q=mla_qk_norm_q,
             mla_qk_norm_k=mla_qk_norm_k, s5_lam_re=s5_lam_re, s5_lam_im=s5_lam_im, s5_log_step=s5_log_step,
             s5_b_re=s5_b_re, s5_b_im=s5_b_im, s5_c_re=s5_c_re, s5_c_im=s5_c_im, s5_d=s5_d,
             s5_w_glu=s5_w_glu, ret_decay_logit=ret_decay_logit, ret_gn_w=ret_gn_w, w_branch=w_branch,
             w_out=w_out, ffn_w1=ffn_w1, ffn_w2=ffn_w2)
    c16 = jnp.concatenate([c, c_ctx[None, :], jnp.zeros((16 - BATCH - 1, D_MODEL), F32)], axis=0)
    mod = _ada_mod(c16, ada_w, ada_b)
    xs = jnp.concatenate([ctx, x], axis=1)
    for l in range(DEPTH):
        off = 1 if l == DEPTH - 1 else 0
        lw = _layer_weights(l, p)
        modl = mod[l, 0:BATCH][:, None, :]
        modc = mod[l, BATCH:BATCH + 1][:, None, :]
        q, k, v, u_t, uc, us, rq, rk, rv, rg = _in_proj(xs, modl, modc, lw)
        oa = _attention(q, k, v, off)
        ob = _fnet(uc, us)
        lam, bblk = _s5_params(*lw["s5"])
        yf, yb = _s5(u_t, lam, bblk, lw["cblk"])
        oret = _retention(rq, rk, rv, lw["lgl"], lw["lgh"])
        xm = _merge(xs, modl, modc, lw, oa, ob, yf, yb, u_t, oret, rg, off)
        xs = _ffn(xm, modl, modc, lw, off)
    return xs
```

```python
import functools
import math

import numpy as np
import jax
import jax.numpy as jnp
from jax import lax
from jax.experimental import pallas as pl
from jax.experimental.pallas import tpu as pltpu

F32 = jnp.float32
BF16 = jnp.bfloat16

D_MODEL = 1024
BATCH = 8
SEQ = 2048
DEPTH = 2
GRID_W = 64
CTX_LEN = 256
LCAT = CTX_LEN + SEQ
N_BRANCH = 4
BRANCH_W = 256
NORM_EPS = 1e-6
ROPE_BASE = 10000.0
ADA_CHUNKS = 6

MLA_HEADS = 4
MLA_NOPE = 64
MLA_ROPE = 32
MLA_QK = MLA_NOPE + MLA_ROPE
MLA_V = 64
Q_LORA = 256
KV_LORA = 128
HEAD_PAD = 128
HEADS_W = MLA_HEADS * HEAD_PAD

FNET_GROUPS = 4
FNET_GW = BRANCH_W // FNET_GROUPS

S5_GROUP_CH = 16
S5_GROUPS = BRANCH_W // S5_GROUP_CH
S5_STATE = 64
S5_LANES = S5_GROUPS * S5_STATE

RET_HEADS = 4
RET_HD = BRANCH_W // RET_HEADS
RET_CHUNK = 256
RET_NCHUNK = LCAT // RET_CHUNK
RET_CTX_CHUNKS = CTX_LEN // RET_CHUNK
RET_SPB = 2

D_FF = 4 * D_MODEL

_O_KV, _O_KR, _O_S5, _O_RK, _O_RV = 0, 128, 160, 416, 672
_O_Q, _O_FN, _O_RQ, _O_RG, _O_GATE = 928, 1184, 1440, 1696, 1952
MAIN_COLS = 2048

TT = 32
TR = TT * BATCH
CTX_TILES = CTX_LEN // TT
ALL_TILES = LCAT // TT
TQ = 256
S5_TC = 128
S5_ROWS = S5_TC * BATCH
S5_STEPS = LCAT // S5_TC
S5_CTX_STEPS = CTX_LEN // S5_TC

VMEM_LIMIT = 56 * 1024 * 1024


def _cparams(n_grid):
    return pltpu.CompilerParams(dimension_semantics=("arbitrary",) * n_grid,
                                vmem_limit_bytes=VMEM_LIMIT)


def _dot(a, b):
    return jnp.dot(a, b, preferred_element_type=F32)


def _dot_nt(a, b):
    return lax.dot_general(a, b, (((1,), (1,)), ((), ())), preferred_element_type=F32)


def _sigmoid(x):
    return 0.5 * (jnp.tanh(0.5 * x) + 1.0)


def _gelu_tanh(y):
    return 0.5 * y * (1.0 + jnp.tanh(math.sqrt(2.0 / math.pi) * (y + 0.044715 * (y * y * y))))


def _rms(x, w):
    return x * lax.rsqrt(jnp.mean(x * x, axis=-1, keepdims=True) + NORM_EPS) * w


def _full(shape):
    n = len(shape)
    return pl.BlockSpec(shape, lambda *_: (0,) * n)


def _layer(l, shape):
    n = len(shape)
    return pl.BlockSpec((1,) + tuple(shape), lambda *_: (l,) + (0,) * n)


def _mla_rope_perm():
    r = np.arange(MLA_ROPE)
    first = (r % 16) < 8
    return np.where(first, r + 8, r - 8), np.where(first, -1.0, 1.0)


@functools.lru_cache(maxsize=None)
def _mla_tables():
    pos = np.arange(SEQ)
    rows, cols = pos // GRID_W, pos % GRID_W
    freqs = ROPE_BASE ** (-np.arange(8, dtype=np.float64) / 8)
    r = np.arange(MLA_ROPE)
    _, sign = _mla_rope_perm()
    p = np.where((r // 16 == 0)[None, :], rows[:, None], cols[:, None]).astype(np.float64)
    ang = p * freqs[(r % 16) % 8][None, :]
    cosf = np.zeros((LCAT, HEAD_PAD))
    sinf = np.zeros((LCAT, HEAD_PAD))
    cosf[:, :MLA_QK] = 1.0
    cosf[CTX_LEN:, MLA_NOPE:MLA_QK] = np.cos(ang)
    sinf[CTX_LEN:, MLA_NOPE:MLA_QK] = np.sin(ang) * sign[None, :]
    return cosf.astype(np.float32), sinf.astype(np.float32)


def _ret_perm():
    d = np.arange(RET_HD)
    first = d < RET_HD // 2
    return np.where(first, d + RET_HD // 2, d - RET_HD // 2), np.where(first, -1.0, 1.0)


@functools.lru_cache(maxsize=None)
def _ret_tables():
    half = RET_HD // 2
    pos = np.arange(SEQ, dtype=np.float64)
    freqs = ROPE_BASE ** (-np.arange(half, dtype=np.float64) / half)
    d = np.arange(RET_HD)
    perm, sign = _ret_perm()
    ang = pos[:, None] * freqs[d % half][None, :]
    cosr = np.ones((LCAT, RET_HD))
    sinr = np.zeros((LCAT, RET_HD))
    cosr[CTX_LEN:] = np.cos(ang)
    sinr[CTX_LEN:] = np.sin(ang) * sign[None, :]
    cosr = np.tile(cosr, (1, RET_HEADS))
    sinr = np.tile(sinr, (1, RET_HEADS))
    pm = np.zeros((BRANCH_W, BRANCH_W))
    for h in range(RET_HEADS):
        pm[h * RET_HD + perm, h * RET_HD + d] = 1.0
    return cosr.astype(np.float32), sinr.astype(np.float32), pm.astype(np.float32)


def _dft(n, scale):
    k = np.arange(n)
    kt = (k[:, None] * k[None, :]) % n
    ang = 2.0 * np.pi * kt / n
    return np.cos(ang) * scale, np.sin(ang) * scale


@functools.lru_cache(maxsize=None)
def _fnet_tables():
    cw, sw = _dft(FNET_GW, 1.0)
    t = np.zeros((BRANCH_W, 2 * BRANCH_W))
    for g in range(FNET_GROUPS):
        s = slice(g * FNET_GW, (g + 1) * FNET_GW)
        t[s, s] = cw
        t[s, BRANCH_W + g * FNET_GW:BRANCH_W + (g + 1) * FNET_GW] = -sw
    cl, sl = _dft(SEQ, 1.0 / math.sqrt(SEQ * FNET_GW))
    clc, slc = _dft(CTX_LEN, 1.0 / math.sqrt(CTX_LEN * FNET_GW))
    return tuple(a.astype(np.float32) for a in (t, cl, sl, clc, slc))


def _bf16_const(a):
    return jnp.asarray(a).astype(BF16)


@functools.lru_cache(maxsize=None)
def _head_avg():
    p = np.zeros((BRANCH_W, BRANCH_W))
    for h in range(RET_HEADS):
        p[h * RET_HD:(h + 1) * RET_HD, h * RET_HD:(h + 1) * RET_HD] = 1.0 / RET_HD
    return p.astype(np.float32)


@functools.lru_cache(maxsize=None)
def _tile_perm():
    p = np.zeros((TR, TR), np.float32)
    for b in range(BATCH):
        for t in range(TT):
            p[t * BATCH + b, b * TT + t] = 1.0
    return p


@functools.lru_cache(maxsize=None)
def _kv_place():
    perm, _ = _mla_rope_perm()
    place = np.zeros((128, 3 * HEADS_W), np.float32)
    for h in range(MLA_HEADS):
        for r in range(MLA_ROPE):
            place[r, h * HEAD_PAD + MLA_NOPE + r] = 1.0
            place[perm[r], HEADS_W + h * HEAD_PAD + MLA_NOPE + r] = 1.0
    return place


@functools.lru_cache(maxsize=None)
def _v_ones():
    v = np.zeros((1, HEADS_W), np.float32)
    v[0, np.arange(MLA_HEADS) * HEAD_PAD + MLA_V] = 1.0
    return v


ADA_TN = 1536


def _ada_kernel(c_ref, w_ref, b_ref, o_ref):
    c = c_ref[...]
    s = (c * _sigmoid(c)).astype(BF16)
    o_ref[0] = _dot(s, w_ref[0].astype(BF16)) + b_ref[0]


def _ada_mod(c16, ada_w, ada_b):
    n = ADA_CHUNKS * D_MODEL
    return pl.pallas_call(
        _ada_kernel,
        grid=(DEPTH, n // ADA_TN),
        in_specs=[pl.BlockSpec((16, D_MODEL), lambda l, j: (0, 0)),
                  pl.BlockSpec((1, D_MODEL, ADA_TN), lambda l, j: (l, 0, j)),
                  pl.BlockSpec((1, 1, ADA_TN), lambda l, j: (l, 0, j))],
        out_specs=pl.BlockSpec((1, 16, ADA_TN), lambda l, j: (l, 0, j)),
        out_shape=jax.ShapeDtypeStruct((DEPTH, 16, n), F32),
        compiler_params=_cparams(2),
        name="ada_mod",
    )(c16, ada_w, ada_b.reshape(DEPTH, 1, n))


def _mod_chunks(mod_ref, is_ctx, idxs):
    out = []
    for i in idxs:
        sl = slice(i * D_MODEL, (i + 1) * D_MODEL)
        m = mod_ref[0, 0:BATCH, sl]
        if is_ctx is not None:
            m = jnp.where(is_ctx, mod_ref[0, BATCH:BATCH + 1, sl], m)
        out.append(m[:, None, :])
    return out


def _tile_specs(l, skip_ctx):
    if l == 0:
        ctx_spec = pl.BlockSpec((BATCH, TT, D_MODEL), lambda j: (0, jnp.minimum(j, CTX_TILES - 1), 0))
        lat_spec = pl.BlockSpec((BATCH, TT, D_MODEL), lambda j: (0, jnp.maximum(j - CTX_TILES, 0), 0))
    elif skip_ctx:
        ctx_spec = pl.BlockSpec((BATCH, TT, D_MODEL), lambda j: (0, 0, 0))
        lat_spec = pl.BlockSpec((BATCH, TT, D_MODEL), lambda j: (0, j + CTX_TILES, 0))
    else:
        ctx_spec = pl.BlockSpec((BATCH, TT, D_MODEL), lambda j: (0, jnp.minimum(j, CTX_TILES - 1), 0))
        lat_spec = pl.BlockSpec((BATCH, TT, D_MODEL), lambda j: (0, jnp.maximum(j, CTX_TILES), 0))
    return ctx_spec, lat_spec


def _head_norm_rot(xf, xp, a, b, scale):
    outs = []
    for h in range(MLA_HEADS):
        f = xf[:, h * HEAD_PAD:(h + 1) * HEAD_PAD]
        p = xp[:, h * HEAD_PAD:(h + 1) * HEAD_PAD]
        n = lax.rsqrt(jnp.sum(f * f, axis=-1, keepdims=True) * (1.0 / MLA_QK) + NORM_EPS) * scale
        outs.append(n * (f * a + p * b))
    return jnp.concatenate(outs, axis=-1)


def _rows(table):
    w = table.shape[-1]
    return jnp.broadcast_to(table[None], (BATCH, TT, w)).reshape(TR, w)


def _tile3(x):
    return x.reshape(BATCH, TT, x.shape[-1])


def _inproj_kernel(xc_ref, xl_ref, mod_ref, nw_ref, w_ref, kvw_ref, wkv_ref, qnw_ref, wq_ref,
                   cosf_ref, sinf_ref, hw_ref, pm_ref, cosr_ref, sinr_ref, t_ref, ptb_ref, vone_ref,
                   q_out, k_out, v_out, ut_out, uc_out, us_out, rq_out, rk_out, rv_out, rg_out):
    is_ctx = pl.program_id(0) < CTX_TILES
    x3 = jnp.where(is_ctx, xc_ref[...], xl_ref[...])
    sh, sc = _mod_chunks(mod_ref, is_ctx, (0, 1))
    h = (_rms(x3, nw_ref[0]) * (1.0 + sc) + sh).reshape(TR, D_MODEL).astype(BF16)
    z = _dot(h, w_ref[0])

    cosf, sinf = _rows(cosf_ref[...]), _rows(sinf_ref[...])
    hw = hw_ref[0]
    kvn = _rms(z[:, 0:128], kvw_ref[0]).astype(BF16)
    lhs = jnp.concatenate([kvn, z[:, 128:256].astype(BF16)], axis=-1)
    kv = _dot(lhs, wkv_ref[0])
    k = _head_norm_rot(kv[:, 0:HEADS_W], kv[:, HEADS_W:2 * HEADS_W],
                       cosf * hw[0:1], sinf * hw[1:2], 1.0)
    k_out[...] = _tile3(k).astype(BF16)
    v_out[...] = _tile3(kv[:, 2 * HEADS_W:3 * HEADS_W] + vone_ref[...]).astype(BF16)
    qn = _rms(z[:, 256:512], qnw_ref[0]).astype(BF16)
    qq = _dot(qn, wq_ref[0])
    q = _head_norm_rot(qq[:, 0:HEADS_W], qq[:, HEADS_W:2 * HEADS_W],
                       cosf * hw[2:3], sinf * hw[3:4], MLA_QK ** -0.5)
    q_out[...] = _tile3(q).astype(BF16)
    ut_out[...] = _dot(ptb_ref[...], z[:, 512:768].astype(BF16)).astype(BF16)
    ucs = _dot(z[:, 768:1024].astype(BF16), t_ref[...])
    uc_out[...] = _tile3(ucs[:, 0:BRANCH_W]).astype(BF16)
    us_out[...] = _tile3(ucs[:, BRANCH_W:2 * BRANCH_W]).astype(BF16)
    cosr, sinr = _rows(cosr_ref[...]), _rows(sinr_ref[...])
    rq = z[:, 1024:1280]
    rk = z[:, 1280:1536]
    rq = rq * cosr + _dot(rq.astype(BF16), pm_ref[...]) * sinr
    rk = rk * cosr + _dot(rk.astype(BF16), pm_ref[...]) * sinr
    rq_out[...] = _tile3(rq).astype(BF16)
    rk_out[...] = _tile3(rk * (RET_HD ** -0.5)).astype(BF16)
    rv_out[...] = _tile3(z[:, 1536:1792]).astype(BF16)
    rg_out[...] = _tile3(z[:, 1792:2048]).astype(BF16)


def _in_proj(l, x_ctx, x_lat, mod, sw):
    ctx_spec, lat_spec = _tile_specs(l, False)
    tok = lambda w: pl.BlockSpec((BATCH, TT, w), lambda j: (0, j, 0))
    tab = lambda w: pl.BlockSpec((TT, w), lambda j: (j, 0))
    cosf, sinf = _mla_tables()
    cosr, sinr, pm = _ret_tables()
    bshape = lambda w: jax.ShapeDtypeStruct((BATCH, LCAT, w), BF16)
    return pl.pallas_call(
        _inproj_kernel,
        grid=(ALL_TILES,),
        in_specs=[ctx_spec, lat_spec,
                  _layer(l, (16, ADA_CHUNKS * D_MODEL)),
                  _layer(l, (1, D_MODEL)),
                  _layer(l, (D_MODEL, MAIN_COLS)),
                  _layer(l, (1, KV_LORA)),
                  _layer(l, (256, 3 * HEADS_W)),
                  _layer(l, (1, Q_LORA)),
                  _layer(l, (Q_LORA, 2 * HEADS_W)),
                  tab(HEAD_PAD), tab(HEAD_PAD),
                  _layer(l, (4, HEAD_PAD)),
                  _full((BRANCH_W, BRANCH_W)),
                  tab(BRANCH_W), tab(BRANCH_W),
                  _full((BRANCH_W, 2 * BRANCH_W)),
                  _full((TR, TR)),
                  _full((1, HEADS_W))],
        out_specs=[tok(HEADS_W), tok(HEADS_W), tok(HEADS_W),
                   pl.BlockSpec((TR, BRANCH_W), lambda j: (j, 0)),
                   tok(256), tok(256), tok(256), tok(256), tok(256), tok(256)],
        out_shape=[bshape(HEADS_W), bshape(HEADS_W), bshape(HEADS_W),
                   jax.ShapeDtypeStruct((LCAT * BATCH, BRANCH_W), BF16),
                   bshape(256), bshape(256), bshape(256), bshape(256), bshape(256), bshape(256)],
        compiler_params=_cparams(1),
        name="in_proj",
    )(x_ctx, x_lat, mod, sw["norm_mix"], sw["w_main"], sw["kv_norm"], sw["wkv"], sw["q_norm"], sw["wq"],
      jnp.asarray(cosf), jnp.asarray(sinf), sw["head_w"], jnp.asarray(pm, dtype=BF16),
      jnp.asarray(cosr), jnp.asarray(sinr), _bf16_const(_fnet_tables()[0]),
      jnp.asarray(_tile_perm(), dtype=BF16), jnp.asarray(_v_ones()))


def _attn_body(q_ref, k_ref, v_ref, o_ref, nk):
    q = q_ref[0]
    for h in range(MLA_HEADS):
        sl = slice(h * HEAD_PAD, (h + 1) * HEAD_PAD)
        s = _dot_nt(q[:, sl], k_ref[0, 0:nk, sl])
        p = jnp.exp((s - jnp.max(s, axis=-1, keepdims=True)).astype(BF16))
        oh = _dot(p, v_ref[0, 0:nk, sl])
        o_ref[0, :, sl] = (oh * (1.0 / oh[:, MLA_V:MLA_V + 1])).astype(BF16)


def _attn_kernel(q_ref, k_ref, v_ref, o_ref, *, off):
    if off:
        _attn_body(q_ref, k_ref, v_ref, o_ref, LCAT)
    else:
        j = pl.program_id(1)
        pl.when(j == 0)(lambda: _attn_body(q_ref, k_ref, v_ref, o_ref, CTX_LEN))
        pl.when(j > 0)(lambda: _attn_body(q_ref, k_ref, v_ref, o_ref, LCAT))


def _attention(q, k, v, off):
    nt = LCAT // TQ - off
    return pl.pallas_call(
        functools.partial(_attn_kernel, off=off),
        grid=(BATCH, nt),
        in_specs=[pl.BlockSpec((1, TQ, HEADS_W), lambda b, j: (b, j + off, 0)),
                  pl.BlockSpec((1, LCAT, HEADS_W), lambda b, j: (b, 0, 0)),
                  pl.BlockSpec((1, LCAT, HEADS_W), lambda b, j: (b, 0, 0))],
        out_specs=pl.BlockSpec((1, TQ, HEADS_W), lambda b, j: (b, j, 0)),
        out_shape=jax.ShapeDtypeStruct((BATCH, nt * TQ, HEADS_W), BF16),
        compiler_params=_cparams(2),
        name="mla_attention",
    )(q, k, v)


def _fnet_kernel(uc_ref, us_ref, cl_ref, sl_ref, clc_ref, slc_ref, o_ref):
    o_ref[0, 0:CTX_LEN, :] = (_dot(clc_ref[...], uc_ref[0, 0:CTX_LEN, :])
                              + _dot(slc_ref[...], us_ref[0, 0:CTX_LEN, :])).astype(BF16)
    o_ref[0, CTX_LEN:LCAT, :] = (_dot(cl_ref[...], uc_ref[0, CTX_LEN:LCAT, :])
                                 + _dot(sl_ref[...], us_ref[0, CTX_LEN:LCAT, :])).astype(BF16)


def _fnet(uc, us):
    cl, sl, clc, slc = (_bf16_const(a) for a in _fnet_tables()[1:])
    tok = pl.BlockSpec((1, LCAT, BRANCH_W), lambda b: (b, 0, 0))
    return pl.pallas_call(
        _fnet_kernel,
        grid=(BATCH,),
        in_specs=[tok, tok, _full((SEQ, SEQ)), _full((SEQ, SEQ)),
                  _full((CTX_LEN, CTX_LEN)), _full((CTX_LEN, CTX_LEN))],
        out_specs=tok,
        out_shape=jax.ShapeDtypeStruct((BATCH, LCAT, BRANCH_W), BF16),
        compiler_params=_cparams(1),
        name="fnet_dft",
    )(uc, us, cl, sl, clc, slc)


def _s5_param_kernel(lr_ref, li_ref, ls_ref, bre_ref, bim_ref, cre_ref, cim_ref, lam_out, b_out, c_out):
    lr, li = lr_ref[0, 0], li_ref[0, 0]
    step = jnp.exp(ls_ref[0, 0])
    mag = jnp.exp(lr * step)
    lbr = mag * jnp.cos(li * step)
    lbi = mag * jnp.sin(li * step)
    den = 1.0 / (lr * lr + li * li)
    cr = ((lbr - 1.0) * lr + lbi * li) * den
    ci = (lbi * lr - (lbr - 1.0) * li) * den
    lam_out[0, 0, 0] = jnp.broadcast_to(lbr, (BATCH, S5_LANES))
    lam_out[0, 0, 1] = jnp.broadcast_to(lbi, (BATCH, S5_LANES))
    grp_b = (lax.broadcasted_iota(jnp.int32, (BRANCH_W, S5_LANES), 0) // S5_GROUP_CH
             == lax.broadcasted_iota(jnp.int32, (BRANCH_W, S5_LANES), 1) // S5_STATE)
    wide = lambda r: jnp.concatenate([r[0, 0]] * (S5_LANES // 128), axis=-1)
    bre = jnp.where(grp_b, wide(bre_ref), 0.0)
    bim = jnp.where(grp_b, wide(bim_ref), 0.0)
    b_out[0, 0, :, 0:S5_LANES] = (cr * bre - ci * bim).astype(BF16)
    b_out[0, 0, :, S5_LANES:2 * S5_LANES] = (cr * bim + ci * bre).astype(BF16)
    grp_c = (lax.broadcasted_iota(jnp.int32, (S5_LANES, BRANCH_W), 0) // S5_STATE
             == lax.broadcasted_iota(jnp.int32, (S5_LANES, BRANCH_W), 1) // S5_GROUP_CH)
    tall = lambda r: jnp.concatenate([r[0, 0]] * S5_GROUPS, axis=0)
    c_out[0, 0, 0:S5_LANES, :] = jnp.where(grp_c, tall(cre_ref), 0.0).astype(BF16)
    c_out[0, 0, S5_LANES:2 * S5_LANES, :] = jnp.where(grp_c, -tall(cim_ref), 0.0).astype(BF16)


def _s5_params(sw):
    spec = lambda *s: pl.BlockSpec((1, 1) + s, lambda l, d: (l, d) + (0,) * len(s))
    return pl.pallas_call(
        _s5_param_kernel,
        grid=(DEPTH, 2),
        in_specs=[spec(1, S5_LANES), spec(1, S5_LANES), spec(1, S5_LANES),
                  spec(BRANCH_W, 128), spec(BRANCH_W, 128), spec(S5_STATE, BRANCH_W), spec(S5_STATE, BRANCH_W)],
        out_specs=[spec(2, BATCH, S5_LANES), spec(BRANCH_W, 2 * S5_LANES), spec(2 * S5_LANES, BRANCH_W)],
        out_shape=[jax.ShapeDtypeStruct((DEPTH, 2, 2, BATCH, S5_LANES), F32),
                   jax.ShapeDtypeStruct((DEPTH, 2, BRANCH_W, 2 * S5_LANES), BF16),
                   jax.ShapeDtypeStruct((DEPTH, 2, 2 * S5_LANES, BRANCH_W), BF16)],
        compiler_params=_cparams(2),
        name="s5_discretise",
    )(sw["s5_lam_re"], sw["s5_lam_im"], sw["s5_log_step"], sw["s5_bre"], sw["s5_bim"], sw["s5_cre"], sw["s5_cim"])


def _s5_bwd_block(i):
    return jnp.where(i < S5_CTX_STEPS, S5_CTX_STEPS - 1 - i, S5_STEPS + S5_CTX_STEPS - 1 - i)


def _s5_kernel(uf_ref, ub_ref, lam_ref, b_ref, c_ref, yf_ref, yb_ref, xf_scr, xb_scr, st_scr):
    @pl.when(pl.program_id(0) == 0)
    def _():
        st_scr[...] = jnp.zeros_like(st_scr)

    xf_scr[...] = _dot(uf_ref[...], b_ref[0, 0])
    xb_scr[...] = _dot(ub_ref[...], b_ref[0, 1])
    re, im = pl.ds(0, S5_LANES), pl.ds(S5_LANES, S5_LANES)

    def body(t, carry):
        fr, fi, br, bi = carry
        rf = pl.ds(pl.multiple_of(t * BATCH, BATCH), BATCH)
        rb = pl.ds(pl.multiple_of((S5_TC - 1 - t) * BATCH, BATCH), BATCH)
        afr, afi = lam_ref[0, 0, 0], lam_ref[0, 0, 1]
        abr, abi = lam_ref[0, 1, 0], lam_ref[0, 1, 1]
        nfr = afr * fr - afi * fi + xf_scr[rf, re]
        nfi = afr * fi + afi * fr + xf_scr[rf, im]
        nbr = abr * br - abi * bi + xb_scr[rb, re]
        nbi = abr * bi + abi * br + xb_scr[rb, im]
        xf_scr[rf, re] = nfr
        xf_scr[rf, im] = nfi
        xb_scr[rb, re] = nbr
        xb_scr[rb, im] = nbi
        return nfr, nfi, nbr, nbi

    carry = lax.fori_loop(0, S5_TC, body, (st_scr[0], st_scr[1], st_scr[2], st_scr[3]))
    for n in range(4):
        st_scr[n] = carry[n]
    yf_ref[...] = _dot(xf_scr[...].astype(BF16), c_ref[0, 0]).astype(BF16)
    yb_ref[...] = _dot(xb_scr[...].astype(BF16), c_ref[0, 1]).astype(BF16)


def _s5(l, u_t, lam, bblk, cblk):
    fwd = pl.BlockSpec((S5_ROWS, BRANCH_W), lambda i: (i, 0))
    bwd = pl.BlockSpec((S5_ROWS, BRANCH_W), lambda i: (_s5_bwd_block(i), 0))
    return pl.pallas_call(
        _s5_kernel,
        grid=(S5_STEPS,),
        in_specs=[fwd, bwd, _layer(l, (2, 2, BATCH, S5_LANES)),
                  _layer(l, (2, BRANCH_W, 2 * S5_LANES)), _layer(l, (2, 2 * S5_LANES, BRANCH_W))],
        out_specs=[fwd, bwd],
        out_shape=[jax.ShapeDtypeStruct((LCAT * BATCH, BRANCH_W), BF16)] * 2,
        scratch_shapes=[pltpu.VMEM((S5_ROWS, 2 * S5_LANES), F32),
                        pltpu.VMEM((S5_ROWS, 2 * S5_LANES), F32),
                        pltpu.VMEM((4, BATCH, S5_LANES), F32)],
        compiler_params=_cparams(1),
        name="s5_scan",
    )(u_t, u_t, lam, bblk, cblk)


def _log_sigmoid(x):
    return jnp.minimum(x, 0.0) - jnp.log(1.0 + jnp.exp(-jnp.abs(x)))


def _ret_kernel(q_ref, k_ref, v_ref, lgl_ref, lgh_ref, o_ref, dec_scr, sb_scr, sf_scr, sbc_scr):
    c_len = RET_CHUNK
    lgl = _log_sigmoid(lgl_ref[0])
    lgf, lgb = lgl[0], lgl[1]
    ti = lax.broadcasted_iota(jnp.int32, (c_len, BRANCH_W), 0).astype(F32)
    qdf = jnp.exp(lgf * (ti + 1.0))
    kdf = jnp.exp(lgf * (c_len - 1.0 - ti))
    qdb = jnp.exp(lgb * (c_len - ti))
    kdb = jnp.exp(lgb * ti)
    cdf = jnp.exp(lgf * float(c_len))
    cdb = jnp.exp(lgb * float(c_len))
    lane = lax.broadcasted_iota(jnp.int32, (1, BRANCH_W), 1)
    same_head = (lax.broadcasted_iota(jnp.int32, (BRANCH_W, BRANCH_W), 0) // RET_HD
                 == lax.broadcasted_iota(jnp.int32, (BRANCH_W, BRANCH_W), 1) // RET_HD)

    diff = (lax.broadcasted_iota(jnp.int32, (c_len, c_len), 0)
            - lax.broadcasted_iota(jnp.int32, (c_len, c_len), 1)).astype(F32)
    for h in range(RET_HEADS):
        gf = jnp.concatenate([_log_sigmoid(lgh_ref[0, 0, h])] * (c_len // 128), axis=-1)
        gb = jnp.concatenate([_log_sigmoid(lgh_ref[0, 1, h])] * (c_len // 128), axis=-1)
        dec_scr[h] = (jnp.where(diff >= 0, jnp.exp(gf * jnp.maximum(diff, 0.0)), 0.0)
                      + jnp.where(diff <= 0, jnp.exp(gb * jnp.maximum(-diff, 0.0)), 0.0))

    def chunk(ref, s, c):
        return ref[s, pl.ds(pl.multiple_of(c * c_len, c_len), c_len), :]

    def kv_outer(kd, v):
        s = lax.dot_general(kd.astype(BF16), v, (((0,), (0,)), ((), ())), preferred_element_type=F32)
        return jnp.where(same_head, s, 0.0)

    sbc_scr[...] = jnp.zeros_like(sbc_scr)

    def bwd(i, _):
        c = jnp.where(i < RET_CTX_CHUNKS, RET_CTX_CHUNKS - 1 - i, RET_NCHUNK + RET_CTX_CHUNKS - 1 - i)
        for s in range(RET_SPB):
            sb_scr[s, c] = sbc_scr[s].astype(BF16)
            k = chunk(k_ref, s, c).astype(F32)
            sbc_scr[s] = sbc_scr[s] * cdb + kv_outer(k * kdb, chunk(v_ref, s, c))
        return 0

    lax.fori_loop(0, RET_NCHUNK, bwd, 0)

    sf_scr[...] = jnp.zeros_like(sf_scr)

    def fwd(c, _):
        for s in range(RET_SPB):
            qb, kb, vb = chunk(q_ref, s, c), chunk(k_ref, s, c), chunk(v_ref, s, c)
            q = qb.astype(F32)
            o = (_dot((q * qdf).astype(BF16), sf_scr[s].astype(BF16))
                 + _dot((q * qdb).astype(BF16), sb_scr[s, c]))
            for h in range(RET_HEADS):
                hm = lane // RET_HD == h
                att = _dot_nt(jnp.where(hm, qb, jnp.zeros_like(qb)), kb)
                oh = _dot((att * dec_scr[h]).astype(BF16), vb)
                o = o + jnp.where(hm, oh, 0.0)
            o_ref[s, pl.ds(pl.multiple_of(c * c_len, c_len), c_len), :] = o.astype(BF16)
            sf_scr[s] = sf_scr[s] * cdf + kv_outer(kb.astype(F32) * kdf, vb)
        return 0

    lax.fori_loop(0, RET_NCHUNK, fwd, 0)


def _retention(l, rq, rk, rv, sw):
    tok = pl.BlockSpec((RET_SPB, LCAT, BRANCH_W), lambda b: (b, 0, 0))
    return pl.pallas_call(
        _ret_kernel,
        grid=(BATCH // RET_SPB,),
        in_specs=[tok, tok, tok, _layer(l, (2, 1, BRANCH_W)), _layer(l, (2, RET_HEADS, 1, 128))],
        out_specs=tok,
        out_shape=jax.ShapeDtypeStruct((BATCH, LCAT, BRANCH_W), BF16),
        scratch_shapes=[pltpu.VMEM((RET_HEADS, RET_CHUNK, RET_CHUNK), F32),
                        pltpu.VMEM((RET_SPB, RET_NCHUNK, BRANCH_W, BRANCH_W), BF16),
                        pltpu.VMEM((RET_SPB, BRANCH_W, BRANCH_W), F32),
                        pltpu.VMEM((RET_SPB, BRANCH_W, BRANCH_W), F32)],
        compiler_params=_cparams(1),
        name="retention",
    )(rq, rk, rv, sw["lgl"], sw["lgh"])


def _merge_kernel(xc_ref, xl_ref, mod_ref, nw_ref, wg_ref, oa_ref, ob_ref, yf_ref, yb_ref, ut_ref,
                  d_ref, wglu_ref, oret_ref, rg_ref, gnw_ref, pavg_ref, pbt_ref, wba_ref, wb_ref, wout_ref,
                  o_ref, *, skip_ctx):
    if skip_ctx:
        is_ctx = None
        x3 = xl_ref[...]
    else:
        is_ctx = pl.program_id(0) < CTX_TILES
        x3 = jnp.where(is_ctx, xc_ref[...], xl_ref[...])
    sh, sc, gate_res = _mod_chunks(mod_ref, is_ctx, (0, 1, 2))
    h = (_rms(x3, nw_ref[0]) * (1.0 + sc) + sh).reshape(TR, D_MODEL).astype(BF16)
    yt = yf_ref[...].astype(F32) + yb_ref[...].astype(F32) + d_ref[0] * ut_ref[...].astype(F32)
    y_hi = yt.astype(BF16)
    y_lo = (yt - y_hi.astype(F32)).astype(BF16)
    y = _dot(pbt_ref[...], y_hi) + _dot(pbt_ref[...], y_lo)
    vg = _dot(_gelu_tanh(y).astype(BF16), wglu_ref[0])
    oc = vg[:, 0:BRANCH_W] * _sigmoid(vg[:, BRANCH_W:2 * BRANCH_W])
    o = oret_ref[...].reshape(TR, BRANCH_W)
    dl = o.astype(F32) - _dot(o, pavg_ref[...])
    var = _dot((dl * dl).astype(BF16), pavg_ref[...])
    g = rg_ref[...].reshape(TR, BRANCH_W).astype(F32)
    od = g * _sigmoid(g) * (dl * lax.rsqrt(var + NORM_EPS) * gnw_ref[0])
    branches = (ob_ref[...].reshape(TR, BRANCH_W), oc.astype(BF16), od.astype(BF16))
    acc = _sigmoid(_dot(h, wg_ref[0, :, 0:D_MODEL])) * _dot(oa_ref[...].reshape(TR, HEADS_W), wba_ref[0])
    for n in range(1, N_BRANCH):
        gate = _dot(h, wg_ref[0, :, n * D_MODEL:(n + 1) * D_MODEL])
        acc = acc + _sigmoid(gate) * _dot(branches[n - 1], wb_ref[0, n - 1])
    m = _dot(acc.astype(BF16), wout_ref[0])
    o_ref[...] = x3 + gate_res * m.reshape(BATCH, TT, D_MODEL)


def _merge(l, x_ctx, x_lat, mod, sw, oa, ob, yf, yb, u_t, oret, rg, skip_ctx):
    off = CTX_TILES if skip_ctx else 0
    nt = ALL_TILES - off
    ctx_spec, lat_spec = _tile_specs(l, skip_ctx)
    tok = lambda w: pl.BlockSpec((BATCH, TT, w), lambda j: (0, j + off, 0))
    tmaj = pl.BlockSpec((TR, BRANCH_W), lambda j: (j + off, 0))
    return pl.pallas_call(
        functools.partial(_merge_kernel, skip_ctx=skip_ctx),
        grid=(nt,),
        in_specs=[ctx_spec, lat_spec,
                  _layer(l, (16, ADA_CHUNKS * D_MODEL)),
                  _layer(l, (1, D_MODEL)),
                  _layer(l, (D_MODEL, N_BRANCH * D_MODEL)),
                  pl.BlockSpec((BATCH, TT, HEADS_W), lambda j: (0, j, 0)),
                  tok(BRANCH_W), tmaj, tmaj, tmaj,
                  _layer(l, (1, BRANCH_W)),
                  _layer(l, (BRANCH_W, 2 * BRANCH_W)),
                  tok(BRANCH_W), tok(BRANCH_W),
                  _layer(l, (1, BRANCH_W)),
                  _full((BRANCH_W, BRANCH_W)),
                  _full((TR, TR)),
                  _layer(l, (HEADS_W, D_MODEL)),
                  _layer(l, (N_BRANCH - 1, BRANCH_W, D_MODEL)),
                  _layer(l, (D_MODEL, D_MODEL))],
        out_specs=pl.BlockSpec((BATCH, TT, D_MODEL), lambda j: (0, j, 0)),
        out_shape=jax.ShapeDtypeStruct((BATCH, nt * TT, D_MODEL), F32),
        compiler_params=_cparams(1),
        name="merge",
    )(x_ctx, x_lat, mod, sw["norm_mix"], sw["w_gate"], oa, ob, yf, yb, u_t, sw["s5_d"], sw["w_glu"],
      oret, rg, sw["gn_w"], jnp.asarray(_head_avg(), dtype=BF16), jnp.asarray(_tile_perm().T, dtype=BF16),
      sw["wb_a"], sw["wb_rest"], sw["w_out"])


def _ffn_kernel(x_ref, mod_ref, nw_ref, w1_ref, w2_ref, o_ref, *, skip_ctx):
    is_ctx = None if skip_ctx else pl.program_id(0) < CTX_TILES
    sh, sc, gate_res = _mod_chunks(mod_ref, is_ctx, (3, 4, 5))
    x3 = x_ref[...]
    h = (_rms(x3, nw_ref[0]) * (1.0 + sc) + sh).reshape(TR, D_MODEL).astype(BF16)
    a = jnp.maximum(_dot(h, w1_ref[0]), 0.0)
    f = _dot((a * a).astype(BF16), w2_ref[0])
    o_ref[...] = x3 + gate_res * f.reshape(BATCH, TT, D_MODEL)


def _ffn(l, xm, mod, sw, skip_ctx):
    nt = xm.shape[1] // TT
    tok = pl.BlockSpec((BATCH, TT, D_MODEL), lambda j: (0, j, 0))
    return pl.pallas_call(
        functools.partial(_ffn_kernel, skip_ctx=skip_ctx),
        grid=(nt,),
        in_specs=[tok,
                  _layer(l, (16, ADA_CHUNKS * D_MODEL)),
                  _layer(l, (1, D_MODEL)),
                  _layer(l, (D_MODEL, D_FF)),
                  _layer(l, (D_FF, D_MODEL))],
        out_specs=tok,
        out_shape=jax.ShapeDtypeStruct(xm.shape, F32),
        compiler_params=_cparams(1),
        name="ffn",
    )(xm, mod, sw["norm_ffn"], sw["ffn_w1"], sw["ffn_w2"])


def _stacked_weights(p):
    w = p["w_in"]
    col = lambda o, n: w[:, :, o:o + n]
    w_main = jnp.concatenate(
        [col(_O_KV, 128), col(_O_KR, 32), jnp.zeros((DEPTH, D_MODEL, 96), F32), col(_O_Q, 256),
         col(_O_S5, 256), col(_O_FN, 256), col(_O_RQ, 256), col(_O_RK, 256), col(_O_RV, 256),
         col(_O_RG, 256)], axis=2).astype(BF16)

    perm, _ = _mla_rope_perm()
    zeros = lambda *s: jnp.zeros((DEPTH,) + s, F32)
    wu = p["mla_w_ukv"].reshape(DEPTH, KV_LORA, MLA_HEADS, MLA_NOPE + MLA_V)
    pad_heads = lambda t: jnp.concatenate(
        [t, zeros(t.shape[1], MLA_HEADS, HEAD_PAD - t.shape[3])], -1).reshape(DEPTH, t.shape[1], HEADS_W)
    top = jnp.concatenate([pad_heads(wu[..., :MLA_NOPE]), zeros(KV_LORA, HEADS_W),
                           pad_heads(wu[..., MLA_NOPE:])], axis=2)
    place = jnp.broadcast_to(jnp.asarray(_kv_place())[None], (DEPTH, 128, 3 * HEADS_W))
    wkv = jnp.concatenate([top, place], axis=1).astype(BF16)

    wuq = p["mla_w_uq"].reshape(DEPTH, Q_LORA, MLA_HEADS, MLA_QK)
    qp = jnp.concatenate([zeros(Q_LORA, MLA_HEADS, MLA_NOPE), wuq[..., MLA_NOPE:][..., perm],
                          zeros(Q_LORA, MLA_HEADS, 32)], -1).reshape(DEPTH, Q_LORA, HEADS_W)
    wq = jnp.concatenate([pad_heads(wuq), qp], axis=2).astype(BF16)

    def head_w(v):
        wf = jnp.concatenate([v, zeros(32)], -1)[:, None, :]
        wp = jnp.concatenate([zeros(MLA_NOPE), v[:, MLA_NOPE:][:, perm], zeros(32)], -1)[:, None, :]
        return wf, wp

    head_ws = jnp.concatenate(head_w(p["mla_qk_norm_k"]) + head_w(p["mla_qk_norm_q"]), axis=1)

    def b_compact(b):
        t = b.transpose(0, 1, 2, 4, 3).reshape(DEPTH, 2, BRANCH_W, S5_STATE)
        return jnp.concatenate([t, t], axis=-1)

    c_compact = lambda c: c.transpose(0, 1, 4, 2, 3).reshape(DEPTH, 2, S5_STATE, BRANCH_W)
    vec = lambda a: a.reshape(DEPTH, 2, 1, S5_LANES)

    logit = p["ret_decay_logit"]
    wb = p["w_branch"]
    wb_a = jnp.concatenate([wb[:, 0].reshape(DEPTH, MLA_HEADS, MLA_V, D_MODEL),
                            zeros(MLA_HEADS, HEAD_PAD - MLA_V, D_MODEL)], axis=2).reshape(DEPTH, HEADS_W, D_MODEL)
    row = lambda a: a[:, None, :]
    return dict(
        w_main=w_main, w_gate=w[:, :, _O_GATE:].astype(BF16),
        norm_mix=row(p["norm_mix_w"]), norm_ffn=row(p["norm_ffn_w"]),
        kv_norm=row(p["mla_kv_norm"]), q_norm=row(p["mla_q_norm"]),
        wkv=wkv, wq=wq, head_w=head_ws,
        s5_lam_re=vec(p["s5_lam_re"]), s5_lam_im=vec(p["s5_lam_im"]),
        s5_log_step=vec(jnp.repeat(p["s5_log_step"], S5_STATE, axis=-1)),
        s5_bre=b_compact(p["s5_b_re"]), s5_bim=b_compact(p["s5_b_im"]),
        s5_cre=c_compact(p["s5_c_re"]), s5_cim=c_compact(p["s5_c_im"]),
        s5_d=row(p["s5_d"]), w_glu=p["s5_w_glu"].astype(BF16),
        lgl=jnp.repeat(logit, RET_HD, axis=-1).reshape(DEPTH, 2, 1, BRANCH_W),
        lgh=jnp.broadcast_to(logit[:, :, :, None, None], (DEPTH, 2, RET_HEADS, 1, 128)),
        gn_w=row(p["ret_gn_w"]),
        wb_a=wb_a.astype(BF16), wb_rest=wb[:, 1:].astype(BF16), w_out=p["w_out"].astype(BF16),
        ffn_w1=p["ffn_w1"].astype(BF16), ffn_w2=p["ffn_w2"].astype(BF16))


def kernel(x, c, ctx, c_ctx, ada_w, ada_b, norm_mix_w, norm_ffn_w, w_in, mla_q_norm, mla_w_uq, mla_kv_norm,
           mla_w_ukv, mla_qk_norm_q, mla_qk_norm_k, s5_lam_re, s5_lam_im, s5_log_step, s5_b_re, s5_b_im,
           s5_c_re, s5_c_im, s5_d, s5_w_glu, ret_decay_logit, ret_gn_w, w_branch, w_out, ffn_w1, ffn_w2):
    p = dict(norm_mix_w=norm_mix_w, norm_ffn_w=norm_ffn_w, w_in=w_in, mla_q_norm=mla_q_norm,
             mla_w_uq=mla_w_uq, mla_kv_norm=mla_kv_norm, mla_w_ukv=mla_w_ukv, mla_qk_norm_q=mla_qk_norm_q,
             mla_qk_norm_k=mla_qk_norm_k, s5_lam_re=s5_lam_re, s5_lam_im=s5_lam_im, s5_log_step=s5_log_step,
             s5_b_re=s5_b_re, s5_b_im=s5_b_im, s5_c_re=s5_c_re, s5_c_im=s5_c_im, s5_d=s5_d,
             s5_w_glu=s5_w_glu, ret_decay_logit=ret_decay_logit, ret_gn_w=ret_gn_w, w_branch=w_branch,
             w_out=w_out, ffn_w1=ffn_w1, ffn_w2=ffn_w2)
    sw = _stacked_weights(p)
    c16 = jnp.concatenate([c, c_ctx[None, :], jnp.zeros((16 - BATCH - 1, D_MODEL), F32)], axis=0)
    mod = _ada_mod(c16, ada_w, ada_b)
    lam, bblk, cblk = _s5_params(sw)
    x_ctx, x_lat = ctx, x
    for l in range(DEPTH):
        last = l == DEPTH - 1
        q, k, v, u_t, uc, us, rq, rk, rv, rg = _in_proj(l, x_ctx, x_lat, mod, sw)
        oa = _attention(q, k, v, 1 if last else 0)
        ob = _fnet(uc, us)
        yf, yb = _s5(l, u_t, lam, bblk, cblk)
        oret = _retention(l, rq, rk, rv, sw)
        xm = _merge(l, x_ctx, x_lat, mod, sw, oa, ob, yf, yb, u_t, oret, rg, last)
        x_ctx = x_lat = _ffn(l, xm, mod, sw, last)
    return x_lat
```

```python
import functools
import math

import numpy as np
import jax
import jax.numpy as jnp
from jax import lax
from jax.experimental import pallas as pl
from jax.experimental.pallas import tpu as pltpu

F32 = jnp.float32
BF16 = jnp.bfloat16

D_MODEL = 1024
BATCH = 8
SEQ = 2048
DEPTH = 2
GRID_W = 64
CTX_LEN = 256
LCAT = CTX_LEN + SEQ
N_BRANCH = 4
BRANCH_W = 256
NORM_EPS = 1e-6
ROPE_BASE = 10000.0
ADA_CHUNKS = 6

MLA_HEADS = 4
MLA_NOPE = 64
MLA_ROPE = 32
MLA_QK = MLA_NOPE + MLA_ROPE
MLA_V = 64
Q_LORA = 256
KV_LORA = 128
HEAD_PAD = 128
HEADS_W = MLA_HEADS * HEAD_PAD

FNET_GROUPS = 4
FNET_GW = BRANCH_W // FNET_GROUPS

S5_GROUP_CH = 16
S5_GROUPS = BRANCH_W // S5_GROUP_CH
S5_STATE = 64
S5_LANES = S5_GROUPS * S5_STATE

RET_HEADS = 4
RET_HD = BRANCH_W // RET_HEADS
RET_CHUNK = 256
RET_NCHUNK = LCAT // RET_CHUNK
RET_CTX_CHUNKS = CTX_LEN // RET_CHUNK
RET_SPB = 2

D_FF = 4 * D_MODEL

_O_KV, _O_KR, _O_S5, _O_RK, _O_RV = 0, 128, 160, 416, 672
_O_Q, _O_FN, _O_RQ, _O_RG, _O_GATE = 928, 1184, 1440, 1696, 1952
IN_COLS = _O_GATE + N_BRANCH * D_MODEL

TT = 64
TR = TT * BATCH
CTX_TILES = CTX_LEN // TT
ALL_TILES = LCAT // TT
TQ = 256
S5_TC = 128
S5_ROWS = S5_TC * BATCH
S5_STEPS = LCAT // S5_TC
S5_CTX_STEPS = CTX_LEN // S5_TC

VMEM_LIMIT = 56 * 1024 * 1024


def _cparams(n_grid):
    return pltpu.CompilerParams(dimension_semantics=("arbitrary",) * n_grid,
                                vmem_limit_bytes=VMEM_LIMIT)


def _dot(a, b):
    return jnp.dot(a, b, preferred_element_type=F32)


def _dot_nt(a, b):
    return lax.dot_general(a, b, (((1,), (1,)), ((), ())), preferred_element_type=F32)


def _sigmoid(x):
    return 0.5 * (jnp.tanh(0.5 * x) + 1.0)


def _gelu_tanh(y):
    return 0.5 * y * (1.0 + jnp.tanh(math.sqrt(2.0 / math.pi) * (y + 0.044715 * (y * y * y))))


def _rms(x, w):
    return x * lax.rsqrt(jnp.mean(x * x, axis=-1, keepdims=True) + NORM_EPS) * w


def _full(shape):
    n = len(shape)
    return pl.BlockSpec(shape, lambda *_: (0,) * n)


def _layer(l, shape, single=False):
    n = len(shape)
    mode = dict(pipeline_mode=pl.Buffered(1)) if single else {}
    return pl.BlockSpec((1,) + tuple(shape), lambda *_: (l,) + (0,) * n, **mode)


def _mla_rope_perm():
    r = np.arange(MLA_ROPE)
    first = (r % 16) < 8
    return np.where(first, r + 8, r - 8), np.where(first, -1.0, 1.0)


@functools.lru_cache(maxsize=None)
def _mla_tables():
    pos = np.arange(SEQ)
    rows, cols = pos // GRID_W, pos % GRID_W
    freqs = ROPE_BASE ** (-np.arange(8, dtype=np.float64) / 8)
    r = np.arange(MLA_ROPE)
    _, sign = _mla_rope_perm()
    p = np.where((r // 16 == 0)[None, :], rows[:, None], cols[:, None]).astype(np.float64)
    ang = p * freqs[(r % 16) % 8][None, :]
    cosf = np.zeros((LCAT, HEAD_PAD))
    sinf = np.zeros((LCAT, HEAD_PAD))
    cosf[:, :MLA_QK] = 1.0
    cosf[CTX_LEN:, MLA_NOPE:MLA_QK] = np.cos(ang)
    sinf[CTX_LEN:, MLA_NOPE:MLA_QK] = np.sin(ang) * sign[None, :]
    return cosf.astype(np.float32), sinf.astype(np.float32)


def _ret_perm():
    d = np.arange(RET_HD)
    first = d < RET_HD // 2
    return np.where(first, d + RET_HD // 2, d - RET_HD // 2), np.where(first, -1.0, 1.0)


@functools.lru_cache(maxsize=None)
def _ret_tables():
    half = RET_HD // 2
    pos = np.arange(SEQ, dtype=np.float64)
    freqs = ROPE_BASE ** (-np.arange(half, dtype=np.float64) / half)
    d = np.arange(RET_HD)
    perm, sign = _ret_perm()
    ang = pos[:, None] * freqs[d % half][None, :]
    cosr = np.ones((LCAT, RET_HD))
    sinr = np.zeros((LCAT, RET_HD))
    cosr[CTX_LEN:] = np.cos(ang)
    sinr[CTX_LEN:] = np.sin(ang) * sign[None, :]
    cosr = np.tile(cosr, (1, RET_HEADS))
    sinr = np.tile(sinr, (1, RET_HEADS))
    pm = np.zeros((BRANCH_W, BRANCH_W))
    for h in range(RET_HEADS):
        pm[h * RET_HD + perm, h * RET_HD + d] = 1.0
    return cosr.astype(np.float32), sinr.astype(np.float32), pm.astype(np.float32)


def _dft(n, scale):
    k = np.arange(n)
    kt = (k[:, None] * k[None, :]) % n
    ang = 2.0 * np.pi * kt / n
    return np.cos(ang) * scale, np.sin(ang) * scale


@functools.lru_cache(maxsize=None)
def _fnet_tables():
    cw, sw = _dft(FNET_GW, 1.0)
    t = np.zeros((BRANCH_W, 2 * BRANCH_W))
    for g in range(FNET_GROUPS):
        s = slice(g * FNET_GW, (g + 1) * FNET_GW)
        t[s, s] = cw
        t[s, BRANCH_W + g * FNET_GW:BRANCH_W + (g + 1) * FNET_GW] = -sw
    cl, sl = _dft(SEQ, 1.0 / math.sqrt(SEQ * FNET_GW))
    clc, slc = _dft(CTX_LEN, 1.0 / math.sqrt(CTX_LEN * FNET_GW))
    return tuple(a.astype(np.float32) for a in (t, cl, sl, clc, slc))


def _bf16_const(a):
    return jnp.asarray(a).astype(BF16)


@functools.lru_cache(maxsize=None)
def _head_avg():
    p = np.zeros((BRANCH_W, BRANCH_W))
    for h in range(RET_HEADS):
        p[h * RET_HD:(h + 1) * RET_HD, h * RET_HD:(h + 1) * RET_HD] = 1.0 / RET_HD
    return p.astype(np.float32)


@functools.lru_cache(maxsize=None)
def _tile_perm():
    p = np.zeros((TR, TR), np.float32)
    for b in range(BATCH):
        for t in range(TT):
            p[t * BATCH + b, b * TT + t] = 1.0
    return p


@functools.lru_cache(maxsize=None)
def _kv_place():
    perm, _ = _mla_rope_perm()
    place = np.zeros((128, 3 * HEADS_W), np.float32)
    for h in range(MLA_HEADS):
        for r in range(MLA_ROPE):
            place[r, h * HEAD_PAD + MLA_NOPE + r] = 1.0
            place[perm[r], HEADS_W + h * HEAD_PAD + MLA_NOPE + r] = 1.0
    return place


@functools.lru_cache(maxsize=None)
def _v_ones():
    v = np.zeros((1, HEADS_W), np.float32)
    v[0, np.arange(MLA_HEADS) * HEAD_PAD + MLA_V] = 1.0
    return v


ADA_TN = 1536


def _ada_kernel(c_ref, w_ref, b_ref, o_ref):
    c = c_ref[...]
    s = (c * _sigmoid(c)).astype(BF16)
    o_ref[0] = _dot(s, w_ref[0].astype(BF16)) + b_ref[0]


def _ada_mod(c16, ada_w, ada_b):
    n = ADA_CHUNKS * D_MODEL
    return pl.pallas_call(
        _ada_kernel,
        grid=(DEPTH, n // ADA_TN),
        in_specs=[pl.BlockSpec((16, D_MODEL), lambda l, j: (0, 0)),
                  pl.BlockSpec((1, D_MODEL, ADA_TN), lambda l, j: (l, 0, j)),
                  pl.BlockSpec((1, 1, ADA_TN), lambda l, j: (l, 0, j))],
        out_specs=pl.BlockSpec((1, 16, ADA_TN), lambda l, j: (l, 0, j)),
        out_shape=jax.ShapeDtypeStruct((DEPTH, 16, n), F32),
        compiler_params=_cparams(2),
        name="ada_mod",
    )(c16, ada_w, ada_b.reshape(DEPTH, 1, n))


def _mod_chunks(mod_ref, is_ctx, idxs):
    out = []
    for i in idxs:
        sl = slice(i * D_MODEL, (i + 1) * D_MODEL)
        m = mod_ref[0, 0:BATCH, sl]
        if is_ctx is not None:
            m = jnp.where(is_ctx, mod_ref[0, BATCH:BATCH + 1, sl], m)
        out.append(m[:, None, :])
    return out


def _tile_specs(l, skip_ctx):
    if l == 0:
        ctx_spec = pl.BlockSpec((BATCH, TT, D_MODEL), lambda j: (0, jnp.minimum(j, CTX_TILES - 1), 0))
        lat_spec = pl.BlockSpec((BATCH, TT, D_MODEL), lambda j: (0, jnp.maximum(j - CTX_TILES, 0), 0))
    elif skip_ctx:
        ctx_spec = pl.BlockSpec((BATCH, TT, D_MODEL), lambda j: (0, 0, 0))
        lat_spec = pl.BlockSpec((BATCH, TT, D_MODEL), lambda j: (0, j + CTX_TILES, 0))
    else:
        ctx_spec = pl.BlockSpec((BATCH, TT, D_MODEL), lambda j: (0, jnp.minimum(j, CTX_TILES - 1), 0))
        lat_spec = pl.BlockSpec((BATCH, TT, D_MODEL), lambda j: (0, jnp.maximum(j, CTX_TILES), 0))
    return ctx_spec, lat_spec


def _head_norm_rot(xf, xp, a, b, scale):
    outs = []
    for h in range(MLA_HEADS):
        f = xf[:, h * HEAD_PAD:(h + 1) * HEAD_PAD]
        p = xp[:, h * HEAD_PAD:(h + 1) * HEAD_PAD]
        n = lax.rsqrt(jnp.sum(f * f, axis=-1, keepdims=True) * (1.0 / MLA_QK) + NORM_EPS) * scale
        outs.append(n * (f * a + p * b))
    return jnp.concatenate(outs, axis=-1)


def _rows(table):
    w = table.shape[-1]
    return jnp.broadcast_to(table[None], (BATCH, TT, w)).reshape(TR, w)


def _tile3(x):
    return x.reshape(BATCH, TT, x.shape[-1])


def _inproj_kernel(xc_ref, xl_ref, mod_ref, nw_ref, w_ref, kvw_ref, wkv_ref, qnw_ref, wq_ref,
                   cosf_ref, sinf_ref, hw_ref, pm_ref, cosr_ref, sinr_ref, t_ref, ptb_ref, vone_ref,
                   q_out, k_out, v_out, ut_out, uc_out, us_out, rq_out, rk_out, rv_out, rg_out):
    is_ctx = pl.program_id(0) < CTX_TILES
    x3 = jnp.where(is_ctx, xc_ref[...], xl_ref[...])
    sh, sc = _mod_chunks(mod_ref, is_ctx, (0, 1))
    h = (_rms(x3, nw_ref[0]) * (1.0 + sc) + sh).reshape(TR, D_MODEL).astype(BF16)
    za = _dot_nt(h, w_ref[0, 0:256, :])
    z = _dot_nt(h, w_ref[0, _O_S5:_O_GATE, :])

    cosf, sinf = _rows(cosf_ref[...]), _rows(sinf_ref[...])
    hw = hw_ref[0]
    kvn = _rms(za[:, 0:128], kvw_ref[0]).astype(BF16)
    lhs = jnp.concatenate([kvn, za[:, 128:256].astype(BF16)], axis=-1)
    kv = _dot(lhs, wkv_ref[0])
    k = _head_norm_rot(kv[:, 0:HEADS_W], kv[:, HEADS_W:2 * HEADS_W],
                       cosf * hw[0:1], sinf * hw[1:2], 1.0)
    k_out[...] = _tile3(k).astype(BF16)
    v_out[...] = _tile3(kv[:, 2 * HEADS_W:3 * HEADS_W] + vone_ref[...]).astype(BF16)
    qn = _rms(z[:, 768:1024], qnw_ref[0]).astype(BF16)
    qq = _dot(qn, wq_ref[0])
    q = _head_norm_rot(qq[:, 0:HEADS_W], qq[:, HEADS_W:2 * HEADS_W],
                       cosf * hw[2:3], sinf * hw[3:4], MLA_QK ** -0.5)
    q_out[...] = _tile3(q).astype(BF16)
    ut_out[...] = _dot(ptb_ref[...], z[:, 0:256].astype(BF16)).astype(BF16)
    ucs = _dot(z[:, 1024:1280].astype(BF16), t_ref[...])
    uc_out[...] = _tile3(ucs[:, 0:BRANCH_W]).astype(BF16)
    us_out[...] = _tile3(ucs[:, BRANCH_W:2 * BRANCH_W]).astype(BF16)
    cosr, sinr = _rows(cosr_ref[...]), _rows(sinr_ref[...])
    rq = z[:, 1280:1536]
    rk = z[:, 256:512]
    rq = rq * cosr + _dot(rq.astype(BF16), pm_ref[...]) * sinr
    rk = rk * cosr + _dot(rk.astype(BF16), pm_ref[...]) * sinr
    rq_out[...] = _tile3(rq).astype(BF16)
    rk_out[...] = _tile3(rk * (RET_HD ** -0.5)).astype(BF16)
    rv_out[...] = _tile3(z[:, 512:768]).astype(BF16)
    rg_out[...] = _tile3(z[:, 1536:1792]).astype(BF16)


def _in_proj(l, x_ctx, x_lat, mod, sw):
    ctx_spec, lat_spec = _tile_specs(l, False)
    tok = lambda w: pl.BlockSpec((BATCH, TT, w), lambda j: (0, j, 0))
    tab = lambda w: pl.BlockSpec((TT, w), lambda j: (j, 0))
    cosf, sinf = _mla_tables()
    cosr, sinr, pm = _ret_tables()
    bshape = lambda w: jax.ShapeDtypeStruct((BATCH, LCAT, w), BF16)
    return pl.pallas_call(
        _inproj_kernel,
        grid=(ALL_TILES,),
        in_specs=[ctx_spec, lat_spec,
                  _layer(l, (16, ADA_CHUNKS * D_MODEL)),
                  _layer(l, (1, D_MODEL)),
                  _layer(l, (_O_GATE, D_MODEL)),
                  _layer(l, (1, KV_LORA)),
                  _layer(l, (256, 3 * HEADS_W)),
                  _layer(l, (1, Q_LORA)),
                  _layer(l, (Q_LORA, 2 * HEADS_W)),
                  tab(HEAD_PAD), tab(HEAD_PAD),
                  _layer(l, (4, HEAD_PAD)),
                  _full((BRANCH_W, BRANCH_W)),
                  tab(BRANCH_W), tab(BRANCH_W),
                  _full((BRANCH_W, 2 * BRANCH_W)),
                  _full((TR, TR)),
                  _full((1, HEADS_W))],
        out_specs=[tok(HEADS_W), tok(HEADS_W), tok(HEADS_W),
                   pl.BlockSpec((TR, BRANCH_W), lambda j: (j, 0)),
                   tok(256), tok(256), tok(256), tok(256), tok(256), tok(256)],
        out_shape=[bshape(HEADS_W), bshape(HEADS_W), bshape(HEADS_W),
                   jax.ShapeDtypeStruct((LCAT * BATCH, BRANCH_W), BF16),
                   bshape(256), bshape(256), bshape(256), bshape(256), bshape(256), bshape(256)],
        compiler_params=_cparams(1),
        name="in_proj",
    )(x_ctx, x_lat, mod, sw["norm_mix"], sw["w_in_t"], sw["kv_norm"], sw["wkv"], sw["q_norm"], sw["wq"],
      jnp.asarray(cosf), jnp.asarray(sinf), sw["head_w"], jnp.asarray(pm, dtype=BF16),
      jnp.asarray(cosr), jnp.asarray(sinr), _bf16_const(_fnet_tables()[0]),
      jnp.asarray(_tile_perm(), dtype=BF16), jnp.asarray(_v_ones()))


def _attn_body(q_ref, k_ref, v_ref, o_ref, nk):
    q = q_ref[0]
    for h in range(MLA_HEADS):
        sl = slice(h * HEAD_PAD, (h + 1) * HEAD_PAD)
        s = _dot_nt(q[:, sl], k_ref[0, 0:nk, sl])
        p = jnp.exp((s - jnp.max(s, axis=-1, keepdims=True)).astype(BF16))
        oh = _dot(p, v_ref[0, 0:nk, sl])
        o_ref[0, :, sl] = (oh * (1.0 / oh[:, MLA_V:MLA_V + 1])).astype(BF16)


def _attn_kernel(q_ref, k_ref, v_ref, o_ref, *, off):
    if off:
        _attn_body(q_ref, k_ref, v_ref, o_ref, LCAT)
    else:
        j = pl.program_id(1)
        pl.when(j == 0)(lambda: _attn_body(q_ref, k_ref, v_ref, o_ref, CTX_LEN))
        pl.when(j > 0)(lambda: _attn_body(q_ref, k_ref, v_ref, o_ref, LCAT))


def _attention(q, k, v, off):
    nt = LCAT // TQ - off
    return pl.pallas_call(
        functools.partial(_attn_kernel, off=off),
        grid=(BATCH, nt),
        in_specs=[pl.BlockSpec((1, TQ, HEADS_W), lambda b, j: (b, j + off, 0)),
                  pl.BlockSpec((1, LCAT, HEADS_W), lambda b, j: (b, 0, 0)),
                  pl.BlockSpec((1, LCAT, HEADS_W), lambda b, j: (b, 0, 0))],
        out_specs=pl.BlockSpec((1, TQ, HEADS_W), lambda b, j: (b, j, 0)),
        out_shape=jax.ShapeDtypeStruct((BATCH, nt * TQ, HEADS_W), BF16),
        compiler_params=_cparams(2),
        name="mla_attention",
    )(q, k, v)


def _fnet_kernel(uc_ref, us_ref, cl_ref, sl_ref, clc_ref, slc_ref, o_ref):
    o_ref[0, 0:CTX_LEN, :] = (_dot(clc_ref[...], uc_ref[0, 0:CTX_LEN, :])
                              + _dot(slc_ref[...], us_ref[0, 0:CTX_LEN, :])).astype(BF16)
    o_ref[0, CTX_LEN:LCAT, :] = (_dot(cl_ref[...], uc_ref[0, CTX_LEN:LCAT, :])
                                 + _dot(sl_ref[...], us_ref[0, CTX_LEN:LCAT, :])).astype(BF16)


def _fnet(uc, us):
    cl, sl, clc, slc = (_bf16_const(a) for a in _fnet_tables()[1:])
    tok = pl.BlockSpec((1, LCAT, BRANCH_W), lambda b: (b, 0, 0))
    return pl.pallas_call(
        _fnet_kernel,
        grid=(BATCH,),
        in_specs=[tok, tok, _full((SEQ, SEQ)), _full((SEQ, SEQ)),
                  _full((CTX_LEN, CTX_LEN)), _full((CTX_LEN, CTX_LEN))],
        out_specs=tok,
        out_shape=jax.ShapeDtypeStruct((BATCH, LCAT, BRANCH_W), BF16),
        compiler_params=_cparams(1),
        name="fnet_dft",
    )(uc, us, cl, sl, clc, slc)


def _s5_param_kernel(lr_ref, li_ref, ls_ref, bre_ref, bim_ref, cre_ref, cim_ref, lam_out, b_out, c_out):
    lr, li = lr_ref[0, 0], li_ref[0, 0]
    step = jnp.exp(ls_ref[0, 0])
    mag = jnp.exp(lr * step)
    lbr = mag * jnp.cos(li * step)
    lbi = mag * jnp.sin(li * step)
    den = 1.0 / (lr * lr + li * li)
    cr = ((lbr - 1.0) * lr + lbi * li) * den
    ci = (lbi * lr - (lbr - 1.0) * li) * den
    lam_out[0, 0, 0] = jnp.broadcast_to(lbr, (BATCH, S5_LANES))
    lam_out[0, 0, 1] = jnp.broadcast_to(lbi, (BATCH, S5_LANES))
    grp_b = (lax.broadcasted_iota(jnp.int32, (BRANCH_W, S5_LANES), 0) // S5_GROUP_CH
             == lax.broadcasted_iota(jnp.int32, (BRANCH_W, S5_LANES), 1) // S5_STATE)
    wide = lambda r: jnp.concatenate([r[0, 0]] * (S5_LANES // 128), axis=-1)
    bre = jnp.where(grp_b, wide(bre_ref), 0.0)
    bim = jnp.where(grp_b, wide(bim_ref), 0.0)
    b_out[0, 0, :, 0:S5_LANES] = (cr * bre - ci * bim).astype(BF16)
    b_out[0, 0, :, S5_LANES:2 * S5_LANES] = (cr * bim + ci * bre).astype(BF16)
    grp_c = (lax.broadcasted_iota(jnp.int32, (S5_LANES, BRANCH_W), 0) // S5_STATE
             == lax.broadcasted_iota(jnp.int32, (S5_LANES, BRANCH_W), 1) // S5_GROUP_CH)
    tall = lambda r: jnp.concatenate([r[0, 0]] * S5_GROUPS, axis=0)
    c_out[0, 0, 0:S5_LANES, :] = jnp.where(grp_c, tall(cre_ref), 0.0).astype(BF16)
    c_out[0, 0, S5_LANES:2 * S5_LANES, :] = jnp.where(grp_c, -tall(cim_ref), 0.0).astype(BF16)


def _s5_params(sw):
    spec = lambda *s: pl.BlockSpec((1, 1) + s, lambda l, d: (l, d) + (0,) * len(s))
    return pl.pallas_call(
        _s5_param_kernel,
        grid=(DEPTH, 2),
        in_specs=[spec(1, S5_LANES), spec(1, S5_LANES), spec(1, S5_LANES),
                  spec(BRANCH_W, 128), spec(BRANCH_W, 128), spec(S5_STATE, BRANCH_W), spec(S5_STATE, BRANCH_W)],
        out_specs=[spec(2, BATCH, S5_LANES), spec(BRANCH_W, 2 * S5_LANES), spec(2 * S5_LANES, BRANCH_W)],
        out_shape=[jax.ShapeDtypeStruct((DEPTH, 2, 2, BATCH, S5_LANES), F32),
                   jax.ShapeDtypeStruct((DEPTH, 2, BRANCH_W, 2 * S5_LANES), BF16),
                   jax.ShapeDtypeStruct((DEPTH, 2, 2 * S5_LANES, BRANCH_W), BF16)],
        compiler_params=_cparams(2),
        name="s5_discretise",
    )(sw["s5_lam_re"], sw["s5_lam_im"], sw["s5_log_step"], sw["s5_bre"], sw["s5_bim"], sw["s5_cre"], sw["s5_cim"])


def _s5_bwd_block(i):
    return jnp.where(i < S5_CTX_STEPS, S5_CTX_STEPS - 1 - i, S5_STEPS + S5_CTX_STEPS - 1 - i)


def _s5_kernel(uf_ref, ub_ref, lam_ref, b_ref, c_ref, yf_ref, yb_ref, xf_scr, xb_scr, st_scr):
    @pl.when(pl.program_id(0) == 0)
    def _():
        st_scr[...] = jnp.zeros_like(st_scr)

    xf_scr[...] = _dot(uf_ref[...], b_ref[0, 0])
    xb_scr[...] = _dot(ub_ref[...], b_ref[0, 1])
    re, im = pl.ds(0, S5_LANES), pl.ds(S5_LANES, S5_LANES)

    def scan(x_scr, d, steps):
        xr, xi = st_scr[2 * d], st_scr[2 * d + 1]
        for t in steps:
            rows = pl.ds(t * BATCH, BATCH)
            ar, ai = lam_ref[0, d, 0], lam_ref[0, d, 1]
            xr, xi = (ar * xr - ai * xi + x_scr[rows, re], ar * xi + ai * xr + x_scr[rows, im])
            x_scr[rows, re] = xr
            x_scr[rows, im] = xi
        st_scr[2 * d], st_scr[2 * d + 1] = xr, xi

    scan(xf_scr, 0, range(S5_TC))
    scan(xb_scr, 1, range(S5_TC - 1, -1, -1))
    yf_ref[...] = _dot(xf_scr[...].astype(BF16), c_ref[0, 0]).astype(BF16)
    yb_ref[...] = _dot(xb_scr[...].astype(BF16), c_ref[0, 1]).astype(BF16)


def _s5(l, u_t, lam, bblk, cblk):
    fwd = pl.BlockSpec((S5_ROWS, BRANCH_W), lambda i: (i, 0))
    bwd = pl.BlockSpec((S5_ROWS, BRANCH_W), lambda i: (_s5_bwd_block(i), 0))
    return pl.pallas_call(
        _s5_kernel,
        grid=(S5_STEPS,),
        in_specs=[fwd, bwd, _layer(l, (2, 2, BATCH, S5_LANES)),
                  _layer(l, (2, BRANCH_W, 2 * S5_LANES)), _layer(l, (2, 2 * S5_LANES, BRANCH_W))],
        out_specs=[fwd, bwd],
        out_shape=[jax.ShapeDtypeStruct((LCAT * BATCH, BRANCH_W), BF16)] * 2,
        scratch_shapes=[pltpu.VMEM((S5_ROWS, 2 * S5_LANES), F32),
                        pltpu.VMEM((S5_ROWS, 2 * S5_LANES), F32),
                        pltpu.VMEM((4, BATCH, S5_LANES), F32)],
        compiler_params=_cparams(1),
        name="s5_scan",
    )(u_t, u_t, lam, bblk, cblk)


def _log_sigmoid(x):
    return jnp.minimum(x, 0.0) - jnp.log(1.0 + jnp.exp(-jnp.abs(x)))


def _ret_kernel(q_ref, k_ref, v_ref, lgl_ref, lgh_ref, o_ref, dec_scr, sb_scr, sf_scr, sbc_scr):
    c_len = RET_CHUNK
    lgl = _log_sigmoid(lgl_ref[0])
    lgf, lgb = lgl[0], lgl[1]
    ti = lax.broadcasted_iota(jnp.int32, (c_len, BRANCH_W), 0).astype(F32)
    qdf = jnp.exp(lgf * (ti + 1.0))
    kdf = jnp.exp(lgf * (c_len - 1.0 - ti))
    qdb = jnp.exp(lgb * (c_len - ti))
    kdb = jnp.exp(lgb * ti)
    cdf = jnp.exp(lgf * float(c_len))
    cdb = jnp.exp(lgb * float(c_len))
    lane = lax.broadcasted_iota(jnp.int32, (1, BRANCH_W), 1)
    same_head = (lax.broadcasted_iota(jnp.int32, (BRANCH_W, BRANCH_W), 0) // RET_HD
                 == lax.broadcasted_iota(jnp.int32, (BRANCH_W, BRANCH_W), 1) // RET_HD)

    diff = (lax.broadcasted_iota(jnp.int32, (c_len, c_len), 0)
            - lax.broadcasted_iota(jnp.int32, (c_len, c_len), 1)).astype(F32)
    for h in range(RET_HEADS):
        gf = jnp.concatenate([_log_sigmoid(lgh_ref[0, 0, h])] * (c_len // 128), axis=-1)
        gb = jnp.concatenate([_log_sigmoid(lgh_ref[0, 1, h])] * (c_len // 128), axis=-1)
        dec_scr[h] = (jnp.where(diff >= 0, jnp.exp(gf * jnp.maximum(diff, 0.0)), 0.0)
                      + jnp.where(diff <= 0, jnp.exp(gb * jnp.maximum(-diff, 0.0)), 0.0))

    def chunk(ref, s, c):
        return ref[s, pl.ds(pl.multiple_of(c * c_len, c_len), c_len), :]

    def kv_outer(kd, v):
        s = lax.dot_general(kd.astype(BF16), v, (((0,), (0,)), ((), ())), preferred_element_type=F32)
        return jnp.where(same_head, s, 0.0)

    sbc_scr[...] = jnp.zeros_like(sbc_scr)

    def bwd(i, _):
        c = jnp.where(i < RET_CTX_CHUNKS, RET_CTX_CHUNKS - 1 - i, RET_NCHUNK + RET_CTX_CHUNKS - 1 - i)
        for s in range(RET_SPB):
            sb_scr[s, c] = sbc_scr[s].astype(BF16)
            k = chunk(k_ref, s, c).astype(F32)
            sbc_scr[s] = sbc_scr[s] * cdb + kv_outer(k * kdb, chunk(v_ref, s, c))
        return 0

    lax.fori_loop(0, RET_NCHUNK, bwd, 0)

    sf_scr[...] = jnp.zeros_like(sf_scr)

    def fwd(c, _):
        for s in range(RET_SPB):
            qb, kb, vb = chunk(q_ref, s, c), chunk(k_ref, s, c), chunk(v_ref, s, c)
            q = qb.astype(F32)
            o = (_dot((q * qdf).astype(BF16), sf_scr[s].astype(BF16))
                 + _dot((q * qdb).astype(BF16), sb_scr[s, c]))
            for h in range(RET_HEADS):
                hm = lane // RET_HD == h
                att = _dot_nt(jnp.where(hm, qb, jnp.zeros_like(qb)), kb)
                oh = _dot((att * dec_scr[h]).astype(BF16), vb)
                o = o + jnp.where(hm, oh, 0.0)
            o_ref[s, pl.ds(pl.multiple_of(c * c_len, c_len), c_len), :] = o.astype(BF16)
            sf_scr[s] = sf_scr[s] * cdf + kv_outer(kb.astype(F32) * kdf, vb)
        return 0

    lax.fori_loop(0, RET_NCHUNK, fwd, 0)


def _retention(l, rq, rk, rv, sw):
    tok = pl.BlockSpec((RET_SPB, LCAT, BRANCH_W), lambda b: (b, 0, 0))
    return pl.pallas_call(
        _ret_kernel,
        grid=(BATCH // RET_SPB,),
        in_specs=[tok, tok, tok, _layer(l, (2, 1, BRANCH_W)), _layer(l, (2, RET_HEADS, 1, 128))],
        out_specs=tok,
        out_shape=jax.ShapeDtypeStruct((BATCH, LCAT, BRANCH_W), BF16),
        scratch_shapes=[pltpu.VMEM((RET_HEADS, RET_CHUNK, RET_CHUNK), F32),
                        pltpu.VMEM((RET_SPB, RET_NCHUNK, BRANCH_W, BRANCH_W), BF16),
                        pltpu.VMEM((RET_SPB, BRANCH_W, BRANCH_W), F32),
                        pltpu.VMEM((RET_SPB, BRANCH_W, BRANCH_W), F32)],
        compiler_params=_cparams(1),
        name="retention",
    )(rq, rk, rv, sw["lgl"], sw["lgh"])


def _merge_kernel(xc_ref, xl_ref, mod_ref, nw_ref, wg_ref, oa_ref, ob_ref, yf_ref, yb_ref, ut_ref,
                  d_ref, wglu_ref, oret_ref, rg_ref, gnw_ref, pavg_ref, pbt_ref, wba_ref, wb_ref, wout_ref,
                  o_ref, *, skip_ctx):
    if skip_ctx:
        is_ctx = None
        x3 = xl_ref[...]
    else:
        is_ctx = pl.program_id(0) < CTX_TILES
        x3 = jnp.where(is_ctx, xc_ref[...], xl_ref[...])
    sh, sc, gate_res = _mod_chunks(mod_ref, is_ctx, (0, 1, 2))
    h = (_rms(x3, nw_ref[0]) * (1.0 + sc) + sh).reshape(TR, D_MODEL).astype(BF16)
    yt = yf_ref[...].astype(F32) + yb_ref[...].astype(F32) + d_ref[0] * ut_ref[...].astype(F32)
    y_hi = yt.astype(BF16)
    y_lo = (yt - y_hi.astype(F32)).astype(BF16)
    y = _dot(pbt_ref[...], y_hi) + _dot(pbt_ref[...], y_lo)
    vg = _dot(_gelu_tanh(y).astype(BF16), wglu_ref[0])
    oc = vg[:, 0:BRANCH_W] * _sigmoid(vg[:, BRANCH_W:2 * BRANCH_W])
    o = oret_ref[...].reshape(TR, BRANCH_W)
    dl = o.astype(F32) - _dot(o, pavg_ref[...])
    var = _dot((dl * dl).astype(BF16), pavg_ref[...])
    g = rg_ref[...].reshape(TR, BRANCH_W).astype(F32)
    od = g * _sigmoid(g) * (dl * lax.rsqrt(var + NORM_EPS) * gnw_ref[0])
    branches = (ob_ref[...].reshape(TR, BRANCH_W), oc.astype(BF16), od.astype(BF16))
    gate_rows = lambda n: wg_ref[0, _O_GATE + n * D_MODEL:_O_GATE + (n + 1) * D_MODEL, :]
    acc = _sigmoid(_dot_nt(h, gate_rows(0))) * _dot(oa_ref[...].reshape(TR, HEADS_W), wba_ref[0])
    for n in range(1, N_BRANCH):
        gate = _dot_nt(h, gate_rows(n))
        acc = acc + _sigmoid(gate) * _dot(branches[n - 1], wb_ref[0, n - 1])
    m = _dot(acc.astype(BF16), wout_ref[0])
    o_ref[...] = x3 + gate_res * m.reshape(BATCH, TT, D_MODEL)


def _merge(l, x_ctx, x_lat, mod, sw, oa, ob, yf, yb, u_t, oret, rg, skip_ctx):
    off = CTX_TILES if skip_ctx else 0
    nt = ALL_TILES - off
    ctx_spec, lat_spec = _tile_specs(l, skip_ctx)
    tok = lambda w: pl.BlockSpec((BATCH, TT, w), lambda j: (0, j + off, 0))
    tmaj = pl.BlockSpec((TR, BRANCH_W), lambda j: (j + off, 0))
    return pl.pallas_call(
        functools.partial(_merge_kernel, skip_ctx=skip_ctx),
        grid=(nt,),
        in_specs=[ctx_spec, lat_spec,
                  _layer(l, (16, ADA_CHUNKS * D_MODEL)),
                  _layer(l, (1, D_MODEL)),
                  _layer(l, (IN_COLS, D_MODEL), single=True),
                  pl.BlockSpec((BATCH, TT, HEADS_W), lambda j: (0, j, 0)),
                  tok(BRANCH_W), tmaj, tmaj, tmaj,
                  _layer(l, (1, BRANCH_W)),
                  _layer(l, (BRANCH_W, 2 * BRANCH_W)),
                  tok(BRANCH_W), tok(BRANCH_W),
                  _layer(l, (1, BRANCH_W)),
                  _full((BRANCH_W, BRANCH_W)),
                  _full((TR, TR)),
                  _layer(l, (HEADS_W, D_MODEL), single=True),
                  _layer(l, (N_BRANCH - 1, BRANCH_W, D_MODEL), single=True),
                  _layer(l, (D_MODEL, D_MODEL), single=True)],
        out_specs=pl.BlockSpec((BATCH, TT, D_MODEL), lambda j: (0, j, 0)),
        out_shape=jax.ShapeDtypeStruct((BATCH, nt * TT, D_MODEL), F32),
        compiler_params=_cparams(1),
        name="merge",
    )(x_ctx, x_lat, mod, sw["norm_mix"], sw["w_in_t"], oa, ob, yf, yb, u_t, sw["s5_d"], sw["w_glu"],
      oret, rg, sw["gn_w"], jnp.asarray(_head_avg(), dtype=BF16), jnp.asarray(_tile_perm().T, dtype=BF16),
      sw["wb_a"], sw["wb_rest"], sw["w_out"])


def _ffn_kernel(x_ref, mod_ref, nw_ref, w1_ref, w2_ref, o_ref, *, skip_ctx):
    is_ctx = None if skip_ctx else pl.program_id(0) < CTX_TILES
    sh, sc, gate_res = _mod_chunks(mod_ref, is_ctx, (3, 4, 5))
    x3 = x_ref[...]
    h = (_rms(x3, nw_ref[0]) * (1.0 + sc) + sh).reshape(TR, D_MODEL).astype(BF16)
    a = jnp.maximum(_dot(h, w1_ref[0]), 0.0)
    f = _dot((a * a).astype(BF16), w2_ref[0])
    o_ref[...] = x3 + gate_res * f.reshape(BATCH, TT, D_MODEL)


def _ffn(l, xm, mod, sw, skip_ctx):
    nt = xm.shape[1] // TT
    tok = pl.BlockSpec((BATCH, TT, D_MODEL), lambda j: (0, j, 0))
    return pl.pallas_call(
        functools.partial(_ffn_kernel, skip_ctx=skip_ctx),
        grid=(nt,),
        in_specs=[tok,
                  _layer(l, (16, ADA_CHUNKS * D_MODEL)),
                  _layer(l, (1, D_MODEL)),
                  _layer(l, (D_MODEL, D_FF), single=True),
                  _layer(l, (D_FF, D_MODEL), single=True)],
        out_specs=tok,
        out_shape=jax.ShapeDtypeStruct(xm.shape, F32),
        compiler_params=_cparams(1),
        name="ffn",
    )(xm, mod, sw["norm_ffn"], sw["ffn_w1"], sw["ffn_w2"])


def _stacked_weights(p):
    w_in_t = jnp.swapaxes(p["w_in"], 1, 2).astype(BF16)
    perm, _ = _mla_rope_perm()
    zeros = lambda *s: jnp.zeros((DEPTH,) + s, F32)
    wu = p["mla_w_ukv"].reshape(DEPTH, KV_LORA, MLA_HEADS, MLA_NOPE + MLA_V)
    pad_heads = lambda t: jnp.concatenate(
        [t, zeros(t.shape[1], MLA_HEADS, HEAD_PAD - t.shape[3])], -1).reshape(DEPTH, t.shape[1], HEADS_W)
    top = jnp.concatenate([pad_heads(wu[..., :MLA_NOPE]), zeros(KV_LORA, HEADS_W),
                           pad_heads(wu[..., MLA_NOPE:])], axis=2)
    place = jnp.broadcast_to(jnp.asarray(_kv_place())[None], (DEPTH, 128, 3 * HEADS_W))
    wkv = jnp.concatenate([top, place], axis=1).astype(BF16)

    wuq = p["mla_w_uq"].reshape(DEPTH, Q_LORA, MLA_HEADS, MLA_QK)
    qp = jnp.concatenate([zeros(Q_LORA, MLA_HEADS, MLA_NOPE), wuq[..., MLA_NOPE:][..., perm],
                          zeros(Q_LORA, MLA_HEADS, 32)], -1).reshape(DEPTH, Q_LORA, HEADS_W)
    wq = jnp.concatenate([pad_heads(wuq), qp], axis=2).astype(BF16)

    def head_w(v):
        wf = jnp.concatenate([v, zeros(32)], -1)[:, None, :]
        wp = jnp.concatenate([zeros(MLA_NOPE), v[:, MLA_NOPE:][:, perm], zeros(32)], -1)[:, None, :]
        return wf, wp

    head_ws = jnp.concatenate(head_w(p["mla_qk_norm_k"]) + head_w(p["mla_qk_norm_q"]), axis=1)

    def b_compact(b):
        t = b.transpose(0, 1, 2, 4, 3).reshape(DEPTH, 2, BRANCH_W, S5_STATE)
        return jnp.concatenate([t, t], axis=-1)

    c_compact = lambda c: c.transpose(0, 1, 4, 2, 3).reshape(DEPTH, 2, S5_STATE, BRANCH_W)
    vec = lambda a: a.reshape(DEPTH, 2, 1, S5_LANES)

    logit = p["ret_decay_logit"]
    wb = p["w_branch"]
    wb_a = jnp.concatenate([wb[:, 0].reshape(DEPTH, MLA_HEADS, MLA_V, D_MODEL),
                            zeros(MLA_HEADS, HEAD_PAD - MLA_V, D_MODEL)], axis=2).reshape(DEPTH, HEADS_W, D_MODEL)
    row = lambda a: a[:, None, :]
    return dict(
        w_in_t=w_in_t,
        norm_mix=row(p["norm_mix_w"]), norm_ffn=row(p["norm_ffn_w"]),
        kv_norm=row(p["mla_kv_norm"]), q_norm=row(p["mla_q_norm"]),
        wkv=wkv, wq=wq, head_w=head_ws,
        s5_lam_re=vec(p["s5_lam_re"]), s5_lam_im=vec(p["s5_lam_im"]),
        s5_log_step=vec(jnp.repeat(p["s5_log_step"], S5_STATE, axis=-1)),
        s5_bre=b_compact(p["s5_b_re"]), s5_bim=b_compact(p["s5_b_im"]),
        s5_cre=c_compact(p["s5_c_re"]), s5_cim=c_compact(p["s5_c_im"]),
        s5_d=row(p["s5_d"]), w_glu=p["s5_w_glu"].astype(BF16),
        lgl=jnp.repeat(logit, RET_HD, axis=-1).reshape(DEPTH, 2, 1, BRANCH_W),
        lgh=jnp.broadcast_to(logit[:, :, :, None, None], (DEPTH, 2, RET_HEADS, 1, 128)),
        gn_w=row(p["ret_gn_w"]),
        wb_a=wb_a.astype(BF16), wb_rest=wb[:, 1:].astype(BF16), w_out=p["w_out"].astype(BF16),
        ffn_w1=p["ffn_w1"].astype(BF16), ffn_w2=p["ffn_w2"].astype(BF16))


def kernel(x, c, ctx, c_ctx, ada_w, ada_b, norm_mix_w, norm_ffn_w, w_in, mla_q_norm, mla_w_uq, mla_kv_norm,
           mla_w_ukv, mla_qk_norm_q, mla_qk_norm_k, s5_lam_re, s5_lam_im, s5_log_step, s5_b_re, s5_b_im,
           s5_c_re, s5_c_im, s5_d, s5_w_glu, ret_decay_logit, ret_gn_w, w_branch, w_out, ffn_w1, ffn_w2):
    p = dict(norm_mix_w=norm_mix_w, norm_ffn_w=norm_ffn_w, w_in=w_in, mla_q_norm=mla_q_norm,
             mla_w_uq=mla_w_uq, mla_kv_norm=mla_kv_norm, mla_w_ukv=mla_w_ukv, mla_qk_norm_q=mla_qk_norm_q,
             mla_qk_norm_k=mla_qk_norm_k, s5_lam_re=s5_lam_re, s5_lam_im=s5_lam_im, s5_log_step=s5_log_step,
             s5_b_re=s5_b_re, s5_b_im=s5_b_im, s5_c_re=s5_c_re, s5_c_im=s5_c_im, s5_d=s5_d,
             s5_w_glu=s5_w_glu, ret_decay_logit=ret_decay_logit, ret_gn_w=ret_gn_w, w_branch=w_branch,
             w_out=w_out, ffn_w1=ffn_w1, ffn_w2=ffn_w2)
    sw = _stacked_weights(p)
    c16 = jnp.concatenate([c, c_ctx[None, :], jnp.zeros((16 - BATCH - 1, D_MODEL), F32)], axis=0)
    mod = _ada_mod(c16, ada_w, ada_b)
    lam, bblk, cblk = _s5_params(sw)
    x_ctx, x_lat = ctx, x
    for l in range(DEPTH):
        last = l == DEPTH - 1
        q, k, v, u_t, uc, us, rq, rk, rv, rg = _in_proj(l, x_ctx, x_lat, mod, sw)
        oa = _attention(q, k, v, 1 if last else 0)
        ob = _fnet(uc, us)
        yf, yb = _s5(l, u_t, lam, bblk, cblk)
        oret = _retention(l, rq, rk, rv, sw)
        xm = _merge(l, x_ctx, x_lat, mod, sw, oa, ob, yf, yb, u_t, oret, rg, last)
        x_ctx = x_lat = _ffn(l, xm, mod, sw, last)
    return x_lat
```

```python
import functools
import math

import numpy as np
import jax
import jax.numpy as jnp
from jax import lax
from jax.experimental import pallas as pl
from jax.experimental.pallas import tpu as pltpu

F32 = jnp.float32
BF16 = jnp.bfloat16

D_MODEL = 1024
BATCH = 8
SEQ = 2048
DEPTH = 2
GRID_W = 64
CTX_LEN = 256
LCAT = CTX_LEN + SEQ
N_BRANCH = 4
BRANCH_W = 256
NORM_EPS = 1e-6
ROPE_BASE = 10000.0
ADA_CHUNKS = 6

MLA_HEADS = 4
MLA_NOPE = 64
MLA_ROPE = 32
MLA_QK = MLA_NOPE + MLA_ROPE
MLA_V = 64
Q_LORA = 256
KV_LORA = 128
HEAD_PAD = 128
HEADS_W = MLA_HEADS * HEAD_PAD

FNET_GROUPS = 4
FNET_GW = BRANCH_W // FNET_GROUPS

S5_GROUP_CH = 16
S5_GROUPS = BRANCH_W // S5_GROUP_CH
S5_STATE = 64
S5_LANES = S5_GROUPS * S5_STATE

RET_HEADS = 4
RET_HD = BRANCH_W // RET_HEADS
RET_CHUNK = 256
RET_NCHUNK = LCAT // RET_CHUNK
RET_CTX_CHUNKS = CTX_LEN // RET_CHUNK
RET_SPB = 2

D_FF = 4 * D_MODEL

_O_KV, _O_KR, _O_S5, _O_RK, _O_RV = 0, 128, 160, 416, 672
_O_Q, _O_FN, _O_RQ, _O_RG, _O_GATE = 928, 1184, 1440, 1696, 1952
IN_COLS = _O_GATE + N_BRANCH * D_MODEL

TT = 64
TR = TT * BATCH
CTX_TILES = CTX_LEN // TT
ALL_TILES = LCAT // TT
TQ = 256
ATTN_QB = 4
ATTN_AHEAD = 1
S5_TC = 128
S5_ROWS = S5_TC * BATCH
S5_STEPS = LCAT // S5_TC
S5_CTX_STEPS = CTX_LEN // S5_TC

VMEM_LIMIT = 56 * 1024 * 1024


def _cparams(n_grid):
    return pltpu.CompilerParams(dimension_semantics=("arbitrary",) * n_grid,
                                vmem_limit_bytes=VMEM_LIMIT)


def _dot(a, b):
    return jnp.dot(a, b, preferred_element_type=F32)


def _dot_nt(a, b):
    return lax.dot_general(a, b, (((1,), (1,)), ((), ())), preferred_element_type=F32)


def _sigmoid(x):
    return 0.5 * (jnp.tanh(0.5 * x) + 1.0)


def _gelu_tanh(y):
    return 0.5 * y * (1.0 + jnp.tanh(math.sqrt(2.0 / math.pi) * (y + 0.044715 * (y * y * y))))


def _rms(x, w):
    return x * lax.rsqrt(jnp.mean(x * x, axis=-1, keepdims=True) + NORM_EPS) * w


def _full(shape):
    n = len(shape)
    return pl.BlockSpec(shape, lambda *_: (0,) * n)


def _layer(l, shape, single=False):
    n = len(shape)
    mode = dict(pipeline_mode=pl.Buffered(1)) if single else {}
    return pl.BlockSpec((1,) + tuple(shape), lambda *_: (l,) + (0,) * n, **mode)


def _mla_rope_perm():
    r = np.arange(MLA_ROPE)
    first = (r % 16) < 8
    return np.where(first, r + 8, r - 8), np.where(first, -1.0, 1.0)


@functools.lru_cache(maxsize=None)
def _mla_tables():
    pos = np.arange(SEQ)
    rows, cols = pos // GRID_W, pos % GRID_W
    freqs = ROPE_BASE ** (-np.arange(8, dtype=np.float64) / 8)
    r = np.arange(MLA_ROPE)
    _, sign = _mla_rope_perm()
    p = np.where((r // 16 == 0)[None, :], rows[:, None], cols[:, None]).astype(np.float64)
    ang = p * freqs[(r % 16) % 8][None, :]
    cosf = np.zeros((LCAT, HEAD_PAD))
    sinf = np.zeros((LCAT, HEAD_PAD))
    cosf[:, :MLA_QK] = 1.0
    cosf[CTX_LEN:, MLA_NOPE:MLA_QK] = np.cos(ang)
    sinf[CTX_LEN:, MLA_NOPE:MLA_QK] = np.sin(ang) * sign[None, :]
    return cosf.astype(np.float32), sinf.astype(np.float32)


def _ret_perm():
    d = np.arange(RET_HD)
    first = d < RET_HD // 2
    return np.where(first, d + RET_HD // 2, d - RET_HD // 2), np.where(first, -1.0, 1.0)


@functools.lru_cache(maxsize=None)
def _ret_tables():
    half = RET_HD // 2
    pos = np.arange(SEQ, dtype=np.float64)
    freqs = ROPE_BASE ** (-np.arange(half, dtype=np.float64) / half)
    d = np.arange(RET_HD)
    perm, sign = _ret_perm()
    ang = pos[:, None] * freqs[d % half][None, :]
    cosr = np.ones((LCAT, RET_HD))
    sinr = np.zeros((LCAT, RET_HD))
    cosr[CTX_LEN:] = np.cos(ang)
    sinr[CTX_LEN:] = np.sin(ang) * sign[None, :]
    cosr = np.tile(cosr, (1, RET_HEADS))
    sinr = np.tile(sinr, (1, RET_HEADS))
    pm = np.zeros((BRANCH_W, BRANCH_W))
    for h in range(RET_HEADS):
        pm[h * RET_HD + perm, h * RET_HD + d] = 1.0
    return cosr.astype(np.float32), sinr.astype(np.float32), pm.astype(np.float32)


def _dft(n, scale):
    k = np.arange(n)
    kt = (k[:, None] * k[None, :]) % n
    ang = 2.0 * np.pi * kt / n
    return np.cos(ang) * scale, np.sin(ang) * scale


@functools.lru_cache(maxsize=None)
def _fnet_tables():
    cw, sw = _dft(FNET_GW, 1.0)
    t = np.zeros((BRANCH_W, 2 * BRANCH_W))
    for g in range(FNET_GROUPS):
        s = slice(g * FNET_GW, (g + 1) * FNET_GW)
        t[s, s] = cw
        t[s, BRANCH_W + g * FNET_GW:BRANCH_W + (g + 1) * FNET_GW] = -sw
    cl, sl = _dft(SEQ, 1.0 / math.sqrt(SEQ * FNET_GW))
    clc, slc = _dft(CTX_LEN, 1.0 / math.sqrt(CTX_LEN * FNET_GW))
    return tuple(a.astype(np.float32) for a in (t, cl, sl, clc, slc))


def _bf16_const(a):
    return jnp.asarray(a).astype(BF16)


@functools.lru_cache(maxsize=None)
def _head_avg():
    p = np.zeros((BRANCH_W, BRANCH_W))
    for h in range(RET_HEADS):
        p[h * RET_HD:(h + 1) * RET_HD, h * RET_HD:(h + 1) * RET_HD] = 1.0 / RET_HD
    return p.astype(np.float32)


@functools.lru_cache(maxsize=None)
def _tile_perm():
    p = np.zeros((TR, TR), np.float32)
    for b in range(BATCH):
        for t in range(TT):
            p[t * BATCH + b, b * TT + t] = 1.0
    return p


@functools.lru_cache(maxsize=None)
def _kv_place():
    perm, _ = _mla_rope_perm()
    place = np.zeros((128, 3 * HEADS_W), np.float32)
    for h in range(MLA_HEADS):
        for r in range(MLA_ROPE):
            place[r, h * HEAD_PAD + MLA_NOPE + r] = 1.0
            place[perm[r], HEADS_W + h * HEAD_PAD + MLA_NOPE + r] = 1.0
    return place


@functools.lru_cache(maxsize=None)
def _v_ones():
    v = np.zeros((1, HEADS_W), np.float32)
    v[0, np.arange(MLA_HEADS) * HEAD_PAD + MLA_V] = 1.0
    return v


ADA_TN = 1536


def _ada_kernel(c_ref, w_ref, b_ref, o_ref):
    c = c_ref[...]
    s = (c * _sigmoid(c)).astype(BF16)
    o_ref[0] = _dot(s, w_ref[0].astype(BF16)) + b_ref[0]


def _ada_mod(c16, ada_w, ada_b):
    n = ADA_CHUNKS * D_MODEL
    return pl.pallas_call(
        _ada_kernel,
        grid=(DEPTH, n // ADA_TN),
        in_specs=[pl.BlockSpec((16, D_MODEL), lambda l, j: (0, 0)),
                  pl.BlockSpec((1, D_MODEL, ADA_TN), lambda l, j: (l, 0, j)),
                  pl.BlockSpec((1, 1, ADA_TN), lambda l, j: (l, 0, j))],
        out_specs=pl.BlockSpec((1, 16, ADA_TN), lambda l, j: (l, 0, j)),
        out_shape=jax.ShapeDtypeStruct((DEPTH, 16, n), F32),
        compiler_params=_cparams(2),
        name="ada_mod",
    )(c16, ada_w, ada_b.reshape(DEPTH, 1, n))


def _mod_chunks(mod_ref, is_ctx, idxs):
    out = []
    for i in idxs:
        sl = slice(i * D_MODEL, (i + 1) * D_MODEL)
        m = mod_ref[0, 0:BATCH, sl]
        if is_ctx is not None:
            m = jnp.where(is_ctx, mod_ref[0, BATCH:BATCH + 1, sl], m)
        out.append(m[:, None, :])
    return out


def _tile_specs(separate, skip_ctx, width=D_MODEL):
    base = 0 if separate else CTX_TILES
    shape = (BATCH, TT, width)
    if skip_ctx:
        return (pl.BlockSpec(shape, lambda j: (0, 0, 0)),
                pl.BlockSpec(shape, lambda j: (0, j + base, 0)))
    return (pl.BlockSpec(shape, lambda j: (0, jnp.minimum(j, CTX_TILES - 1), 0)),
            pl.BlockSpec(shape, lambda j: (0, jnp.maximum(j - CTX_TILES, 0) + base, 0)))


def _head_norm_rot(xf, xp, a, b, scale):
    outs = []
    for h in range(MLA_HEADS):
        f = xf[:, h * HEAD_PAD:(h + 1) * HEAD_PAD]
        p = xp[:, h * HEAD_PAD:(h + 1) * HEAD_PAD]
        n = lax.rsqrt(jnp.sum(f * f, axis=-1, keepdims=True) * (1.0 / MLA_QK) + NORM_EPS) * scale
        outs.append(n * (f * a + p * b))
    return jnp.concatenate(outs, axis=-1)


def _rows(table):
    w = table.shape[-1]
    return jnp.broadcast_to(table[None], (BATCH, TT, w)).reshape(TR, w)


def _tile3(x):
    return x.reshape(BATCH, TT, x.shape[-1])


def _inproj_kernel(xc_ref, xl_ref, mod_ref, nw_ref, w_ref, kvw_ref, wkv_ref, qnw_ref, wq_ref,
                   cosf_ref, sinf_ref, hw_ref, pm_ref, cosr_ref, sinr_ref, t_ref, ptb_ref, vone_ref,
                   q_out, k_out, v_out, ut_out, uc_out, us_out, rq_out, rk_out, rv_out, rg_out):
    is_ctx = pl.program_id(0) < CTX_TILES
    x3 = jnp.where(is_ctx, xc_ref[...], xl_ref[...])
    sh, sc = _mod_chunks(mod_ref, is_ctx, (0, 1))
    h = (_rms(x3, nw_ref[0]) * (1.0 + sc) + sh).reshape(TR, D_MODEL).astype(BF16)
    za = _dot_nt(h, w_ref[0, 0:256, :])
    z = _dot_nt(h, w_ref[0, _O_S5:_O_GATE, :])

    cosf, sinf = _rows(cosf_ref[...]), _rows(sinf_ref[...])
    hw = hw_ref[0]
    kvn = _rms(za[:, 0:128], kvw_ref[0]).astype(BF16)
    lhs = jnp.concatenate([kvn, za[:, 128:256].astype(BF16)], axis=-1)
    kv = _dot(lhs, wkv_ref[0])
    k = _head_norm_rot(kv[:, 0:HEADS_W], kv[:, HEADS_W:2 * HEADS_W],
                       cosf * hw[0:1], sinf * hw[1:2], 1.0)
    k_out[...] = _tile3(k).astype(BF16)
    v_out[...] = _tile3(kv[:, 2 * HEADS_W:3 * HEADS_W] + vone_ref[...]).astype(BF16)
    qn = _rms(z[:, 768:1024], qnw_ref[0]).astype(BF16)
    qq = _dot(qn, wq_ref[0])
    q = _head_norm_rot(qq[:, 0:HEADS_W], qq[:, HEADS_W:2 * HEADS_W],
                       cosf * hw[2:3], sinf * hw[3:4], MLA_QK ** -0.5)
    q_out[...] = _tile3(q).astype(BF16)
    ut_out[...] = _dot(ptb_ref[...], z[:, 0:256].astype(BF16)).astype(BF16)
    ucs = _dot(z[:, 1024:1280].astype(BF16), t_ref[...])
    uc_out[...] = _tile3(ucs[:, 0:BRANCH_W]).astype(BF16)
    us_out[...] = _tile3(ucs[:, BRANCH_W:2 * BRANCH_W]).astype(BF16)
    cosr, sinr = _rows(cosr_ref[...]), _rows(sinr_ref[...])
    rq = z[:, 1280:1536]
    rk = z[:, 256:512]
    rq = rq * cosr + _dot(rq.astype(BF16), pm_ref[...]) * sinr
    rk = rk * cosr + _dot(rk.astype(BF16), pm_ref[...]) * sinr
    rq_out[...] = _tile3(rq).astype(BF16)
    rk_out[...] = _tile3(rk * (RET_HD ** -0.5)).astype(BF16)
    rv_out[...] = _tile3(z[:, 512:768]).astype(BF16)
    rg_out[...] = _tile3(z[:, 1536:1792]).astype(BF16)


def _in_proj(l, x_ctx, x_lat, mod, sw):
    ctx_spec, lat_spec = _tile_specs(l == 0, False)
    tok = lambda w: pl.BlockSpec((BATCH, TT, w), lambda j: (0, j, 0))
    tab = lambda w: pl.BlockSpec((TT, w), lambda j: (j, 0))
    cosf, sinf = _mla_tables()
    cosr, sinr, pm = _ret_tables()
    bshape = lambda w: jax.ShapeDtypeStruct((BATCH, LCAT, w), BF16)
    return pl.pallas_call(
        _inproj_kernel,
        grid=(ALL_TILES,),
        in_specs=[ctx_spec, lat_spec,
                  _layer(l, (16, ADA_CHUNKS * D_MODEL)),
                  _layer(l, (1, D_MODEL)),
                  _layer(l, (_O_GATE, D_MODEL)),
                  _layer(l, (1, KV_LORA)),
                  _layer(l, (256, 3 * HEADS_W)),
                  _layer(l, (1, Q_LORA)),
                  _layer(l, (Q_LORA, 2 * HEADS_W)),
                  tab(HEAD_PAD), tab(HEAD_PAD),
                  _layer(l, (4, HEAD_PAD)),
                  _full((BRANCH_W, BRANCH_W)),
                  tab(BRANCH_W), tab(BRANCH_W),
                  _full((BRANCH_W, 2 * BRANCH_W)),
                  _full((TR, TR)),
                  _full((1, HEADS_W))],
        out_specs=[tok(HEADS_W), tok(HEADS_W), tok(HEADS_W),
                   pl.BlockSpec((TR, BRANCH_W), lambda j: (j, 0)),
                   tok(256), tok(256), tok(256), tok(256), tok(256), tok(256)],
        out_shape=[bshape(HEADS_W), bshape(HEADS_W), bshape(HEADS_W),
                   jax.ShapeDtypeStruct((LCAT * BATCH, BRANCH_W), BF16),
                   bshape(256), bshape(256), bshape(256), bshape(256), bshape(256), bshape(256)],
        compiler_params=_cparams(1),
        name="in_proj",
    )(x_ctx, x_lat, mod, sw["norm_mix"], sw["w_in_t"], sw["kv_norm"], sw["wkv"], sw["q_norm"], sw["wq"],
      jnp.asarray(cosf), jnp.asarray(sinf), sw["head_w"], jnp.asarray(pm, dtype=BF16),
      jnp.asarray(cosr), jnp.asarray(sinr), _bf16_const(_fnet_tables()[0]),
      jnp.asarray(_tile_perm(), dtype=BF16), jnp.asarray(_v_ones()))


def _attn_kernel(*refs):
    q_refs, (k_ref, v_ref, o_ref) = refs[:-3], refs[-3:]
    units = [(r, slice(h * HEAD_PAD, (h + 1) * HEAD_PAD))
             for r in range(len(q_refs)) for h in range(MLA_HEADS)]
    scores = lambda u: _dot_nt(q_refs[u[0]][0, :, u[1]], k_ref[0, :, u[1]])
    pending = [scores(u) for u in units[:ATTN_AHEAD]]
    for i, (r, sl) in enumerate(units):
        if i + ATTN_AHEAD < len(units):
            pending.append(scores(units[i + ATTN_AHEAD]))
        s = pending.pop(0)
        p = jnp.exp((s - jnp.max(s, axis=-1, keepdims=True)).astype(BF16))
        oh = _dot(p, v_ref[0, :, sl])
        o_ref[0, r * TQ:(r + 1) * TQ, sl] = (oh * (1.0 / oh[:, MLA_V:MLA_V + 1])).astype(BF16)


def _attention_latent(q, k, v):
    first = CTX_LEN // TQ
    qspec = lambda r: pl.BlockSpec((1, TQ, HEADS_W), lambda b, j: (b, first + ATTN_QB * j + r, 0))
    kv = pl.BlockSpec((1, LCAT, HEADS_W), lambda b, j: (b, 0, 0))
    return pl.pallas_call(
        _attn_kernel,
        grid=(BATCH, SEQ // (ATTN_QB * TQ)),
        in_specs=[qspec(r) for r in range(ATTN_QB)] + [kv, kv],
        out_specs=pl.BlockSpec((1, ATTN_QB * TQ, HEADS_W), lambda b, j: (b, j, 0)),
        out_shape=jax.ShapeDtypeStruct((BATCH, SEQ, HEADS_W), BF16),
        compiler_params=_cparams(2),
        name="mla_attention",
    )(*([q] * ATTN_QB), k, v)


def _attention_context(q, k, v):
    blk = pl.BlockSpec((1, CTX_LEN, HEADS_W), lambda b: (b, 0, 0))
    return pl.pallas_call(
        _attn_kernel,
        grid=(BATCH,),
        in_specs=[blk, blk, blk],
        out_specs=blk,
        out_shape=jax.ShapeDtypeStruct((BATCH, CTX_LEN, HEADS_W), BF16),
        compiler_params=_cparams(1),
        name="mla_attention_ctx",
    )(q, k, v)


def _fnet_kernel(uc_ref, us_ref, cl_ref, sl_ref, clc_ref, slc_ref, o_ref):
    o_ref[0, 0:CTX_LEN, :] = (_dot(clc_ref[...], uc_ref[0, 0:CTX_LEN, :])
                              + _dot(slc_ref[...], us_ref[0, 0:CTX_LEN, :])).astype(BF16)
    o_ref[0, CTX_LEN:LCAT, :] = (_dot(cl_ref[...], uc_ref[0, CTX_LEN:LCAT, :])
                                 + _dot(sl_ref[...], us_ref[0, CTX_LEN:LCAT, :])).astype(BF16)


def _fnet(uc, us):
    cl, sl, clc, slc = (_bf16_const(a) for a in _fnet_tables()[1:])
    tok = pl.BlockSpec((1, LCAT, BRANCH_W), lambda b: (b, 0, 0))
    return pl.pallas_call(
        _fnet_kernel,
        grid=(BATCH,),
        in_specs=[tok, tok, _full((SEQ, SEQ)), _full((SEQ, SEQ)),
                  _full((CTX_LEN, CTX_LEN)), _full((CTX_LEN, CTX_LEN))],
        out_specs=tok,
        out_shape=jax.ShapeDtypeStruct((BATCH, LCAT, BRANCH_W), BF16),
        compiler_params=_cparams(1),
        name="fnet_dft",
    )(uc, us, cl, sl, clc, slc)


def _s5_param_kernel(lr_ref, li_ref, ls_ref, bre_ref, bim_ref, cre_ref, cim_ref, lam_out, b_out, c_out):
    lr, li = lr_ref[0, 0], li_ref[0, 0]
    step = jnp.exp(ls_ref[0, 0])
    mag = jnp.exp(lr * step)
    lbr = mag * jnp.cos(li * step)
    lbi = mag * jnp.sin(li * step)
    den = 1.0 / (lr * lr + li * li)
    cr = ((lbr - 1.0) * lr + lbi * li) * den
    ci = (lbi * lr - (lbr - 1.0) * li) * den
    lam_out[0, 0, 0] = jnp.broadcast_to(lbr, (BATCH, S5_LANES))
    lam_out[0, 0, 1] = jnp.broadcast_to(lbi, (BATCH, S5_LANES))
    grp_b = (lax.broadcasted_iota(jnp.int32, (BRANCH_W, S5_LANES), 0) // S5_GROUP_CH
             == lax.broadcasted_iota(jnp.int32, (BRANCH_W, S5_LANES), 1) // S5_STATE)
    wide = lambda r: jnp.concatenate([r[0, 0]] * (S5_LANES // 128), axis=-1)
    bre = jnp.where(grp_b, wide(bre_ref), 0.0)
    bim = jnp.where(grp_b, wide(bim_ref), 0.0)
    b_out[0, 0, :, 0:S5_LANES] = (cr * bre - ci * bim).astype(BF16)
    b_out[0, 0, :, S5_LANES:2 * S5_LANES] = (cr * bim + ci * bre).astype(BF16)
    grp_c = (lax.broadcasted_iota(jnp.int32, (S5_LANES, BRANCH_W), 0) // S5_STATE
             == lax.broadcasted_iota(jnp.int32, (S5_LANES, BRANCH_W), 1) // S5_GROUP_CH)
    tall = lambda r: jnp.concatenate([r[0, 0]] * S5_GROUPS, axis=0)
    c_out[0, 0, 0:S5_LANES, :] = jnp.where(grp_c, tall(cre_ref), 0.0).astype(BF16)
    c_out[0, 0, S5_LANES:2 * S5_LANES, :] = jnp.where(grp_c, -tall(cim_ref), 0.0).astype(BF16)


def _s5_params(sw):
    spec = lambda *s: pl.BlockSpec((1, 1) + s, lambda l, d: (l, d) + (0,) * len(s))
    return pl.pallas_call(
        _s5_param_kernel,
        grid=(DEPTH, 2),
        in_specs=[spec(1, S5_LANES), spec(1, S5_LANES), spec(1, S5_LANES),
                  spec(BRANCH_W, 128), spec(BRANCH_W, 128), spec(S5_STATE, BRANCH_W), spec(S5_STATE, BRANCH_W)],
        out_specs=[spec(2, BATCH, S5_LANES), spec(BRANCH_W, 2 * S5_LANES), spec(2 * S5_LANES, BRANCH_W)],
        out_shape=[jax.ShapeDtypeStruct((DEPTH, 2, 2, BATCH, S5_LANES), F32),
                   jax.ShapeDtypeStruct((DEPTH, 2, BRANCH_W, 2 * S5_LANES), BF16),
                   jax.ShapeDtypeStruct((DEPTH, 2, 2 * S5_LANES, BRANCH_W), BF16)],
        compiler_params=_cparams(2),
        name="s5_discretise",
    )(sw["s5_lam_re"], sw["s5_lam_im"], sw["s5_log_step"], sw["s5_bre"], sw["s5_bim"], sw["s5_cre"], sw["s5_cim"])


def _s5_bwd_block(i):
    return jnp.where(i < S5_CTX_STEPS, S5_CTX_STEPS - 1 - i, S5_STEPS + S5_CTX_STEPS - 1 - i)


def _s5_kernel(uf_ref, ub_ref, lam_ref, b_ref, c_ref, yf_ref, yb_ref, xf_scr, xb_scr, st_scr):
    @pl.when(pl.program_id(0) == 0)
    def _():
        st_scr[...] = jnp.zeros_like(st_scr)

    xf_scr[...] = _dot(uf_ref[...], b_ref[0, 0])
    xb_scr[...] = _dot(ub_ref[...], b_ref[0, 1])
    re, im = pl.ds(0, S5_LANES), pl.ds(S5_LANES, S5_LANES)

    def scan(x_scr, d, steps):
        xr, xi = st_scr[2 * d], st_scr[2 * d + 1]
        for t in steps:
            rows = pl.ds(t * BATCH, BATCH)
            ar, ai = lam_ref[0, d, 0], lam_ref[0, d, 1]
            xr, xi = (ar * xr - ai * xi + x_scr[rows, re], ar * xi + ai * xr + x_scr[rows, im])
            x_scr[rows, re] = xr
            x_scr[rows, im] = xi
        st_scr[2 * d], st_scr[2 * d + 1] = xr, xi

    scan(xf_scr, 0, range(S5_TC))
    scan(xb_scr, 1, range(S5_TC - 1, -1, -1))
    yf_ref[...] = _dot(xf_scr[...].astype(BF16), c_ref[0, 0]).astype(BF16)
    yb_ref[...] = _dot(xb_scr[...].astype(BF16), c_ref[0, 1]).astype(BF16)


def _s5(l, u_t, lam, bblk, cblk):
    fwd = pl.BlockSpec((S5_ROWS, BRANCH_W), lambda i: (i, 0))
    bwd = pl.BlockSpec((S5_ROWS, BRANCH_W), lambda i: (_s5_bwd_block(i), 0))
    return pl.pallas_call(
        _s5_kernel,
        grid=(S5_STEPS,),
        in_specs=[fwd, bwd, _layer(l, (2, 2, BATCH, S5_LANES)),
                  _layer(l, (2, BRANCH_W, 2 * S5_LANES)), _layer(l, (2, 2 * S5_LANES, BRANCH_W))],
        out_specs=[fwd, bwd],
        out_shape=[jax.ShapeDtypeStruct((LCAT * BATCH, BRANCH_W), BF16)] * 2,
        scratch_shapes=[pltpu.VMEM((S5_ROWS, 2 * S5_LANES), F32),
                        pltpu.VMEM((S5_ROWS, 2 * S5_LANES), F32),
                        pltpu.VMEM((4, BATCH, S5_LANES), F32)],
        compiler_params=_cparams(1),
        name="s5_scan",
    )(u_t, u_t, lam, bblk, cblk)


def _log_sigmoid(x):
    return jnp.minimum(x, 0.0) - jnp.log(1.0 + jnp.exp(-jnp.abs(x)))


def _ret_kernel(q_ref, k_ref, v_ref, lgl_ref, lgh_ref, o_ref, dec_scr, sb_scr, sf_scr, sbc_scr):
    c_len = RET_CHUNK
    lgl = _log_sigmoid(lgl_ref[0])
    lgf, lgb = lgl[0], lgl[1]
    ti = lax.broadcasted_iota(jnp.int32, (c_len, BRANCH_W), 0).astype(F32)
    qdf = jnp.exp(lgf * (ti + 1.0))
    kdf = jnp.exp(lgf * (c_len - 1.0 - ti))
    qdb = jnp.exp(lgb * (c_len - ti))
    kdb = jnp.exp(lgb * ti)
    cdf = jnp.exp(lgf * float(c_len))
    cdb = jnp.exp(lgb * float(c_len))
    lane = lax.broadcasted_iota(jnp.int32, (1, BRANCH_W), 1)
    same_head = (lax.broadcasted_iota(jnp.int32, (BRANCH_W, BRANCH_W), 0) // RET_HD
                 == lax.broadcasted_iota(jnp.int32, (BRANCH_W, BRANCH_W), 1) // RET_HD)

    diff = (lax.broadcasted_iota(jnp.int32, (c_len, c_len), 0)
            - lax.broadcasted_iota(jnp.int32, (c_len, c_len), 1)).astype(F32)
    for h in range(RET_HEADS):
        gf = jnp.concatenate([_log_sigmoid(lgh_ref[0, 0, h])] * (c_len // 128), axis=-1)
        gb = jnp.concatenate([_log_sigmoid(lgh_ref[0, 1, h])] * (c_len // 128), axis=-1)
        dec_scr[h] = (jnp.where(diff >= 0, jnp.exp(gf * jnp.maximum(diff, 0.0)), 0.0)
                      + jnp.where(diff <= 0, jnp.exp(gb * jnp.maximum(-diff, 0.0)), 0.0))

    def chunk(ref, s, c):
        return ref[s, pl.ds(pl.multiple_of(c * c_len, c_len), c_len), :]

    def kv_outer(kd, v):
        s = lax.dot_general(kd.astype(BF16), v, (((0,), (0,)), ((), ())), preferred_element_type=F32)
        return jnp.where(same_head, s, 0.0)

    sbc_scr[...] = jnp.zeros_like(sbc_scr)

    def bwd(i, _):
        c = jnp.where(i < RET_CTX_CHUNKS, RET_CTX_CHUNKS - 1 - i, RET_NCHUNK + RET_CTX_CHUNKS - 1 - i)
        for s in range(RET_SPB):
            sb_scr[s, c] = sbc_scr[s].astype(BF16)
            k = chunk(k_ref, s, c).astype(F32)
            sbc_scr[s] = sbc_scr[s] * cdb + kv_outer(k * kdb, chunk(v_ref, s, c))
        return 0

    lax.fori_loop(0, RET_NCHUNK, bwd, 0)

    sf_scr[...] = jnp.zeros_like(sf_scr)

    def fwd(c, _):
        for s in range(RET_SPB):
            qb, kb, vb = chunk(q_ref, s, c), chunk(k_ref, s, c), chunk(v_ref, s, c)
            q = qb.astype(F32)
            o = (_dot((q * qdf).astype(BF16), sf_scr[s].astype(BF16))
                 + _dot((q * qdb).astype(BF16), sb_scr[s, c]))
            for h in range(RET_HEADS):
                hm = lane // RET_HD == h
                att = _dot_nt(jnp.where(hm, qb, jnp.zeros_like(qb)), kb)
                oh = _dot((att * dec_scr[h]).astype(BF16), vb)
                o = o + jnp.where(hm, oh, 0.0)
            o_ref[s, pl.ds(pl.multiple_of(c * c_len, c_len), c_len), :] = o.astype(BF16)
            sf_scr[s] = sf_scr[s] * cdf + kv_outer(kb.astype(F32) * kdf, vb)
        return 0

    lax.fori_loop(0, RET_NCHUNK, fwd, 0)


def _retention(l, rq, rk, rv, sw):
    tok = pl.BlockSpec((RET_SPB, LCAT, BRANCH_W), lambda b: (b, 0, 0))
    return pl.pallas_call(
        _ret_kernel,
        grid=(BATCH // RET_SPB,),
        in_specs=[tok, tok, tok, _layer(l, (2, 1, BRANCH_W)), _layer(l, (2, RET_HEADS, 1, 128))],
        out_specs=tok,
        out_shape=jax.ShapeDtypeStruct((BATCH, LCAT, BRANCH_W), BF16),
        scratch_shapes=[pltpu.VMEM((RET_HEADS, RET_CHUNK, RET_CHUNK), F32),
                        pltpu.VMEM((RET_SPB, RET_NCHUNK, BRANCH_W, BRANCH_W), BF16),
                        pltpu.VMEM((RET_SPB, BRANCH_W, BRANCH_W), F32),
                        pltpu.VMEM((RET_SPB, BRANCH_W, BRANCH_W), F32)],
        compiler_params=_cparams(1),
        name="retention",
    )(rq, rk, rv, sw["lgl"], sw["lgh"])


def _merge_kernel(xc_ref, xl_ref, mod_ref, nw_ref, wg_ref, oac_ref, oal_ref, ob_ref, yf_ref, yb_ref, ut_ref,
                  d_ref, wglu_ref, oret_ref, rg_ref, gnw_ref, pavg_ref, pbt_ref, wba_ref, wb_ref, wout_ref,
                  o_ref, *, skip_ctx):
    if skip_ctx:
        is_ctx = None
        x3 = xl_ref[...]
        oa = oal_ref[...]
    else:
        is_ctx = pl.program_id(0) < CTX_TILES
        x3 = jnp.where(is_ctx, xc_ref[...], xl_ref[...])
        oa = jnp.where(is_ctx, oac_ref[...], oal_ref[...])
    sh, sc, gate_res = _mod_chunks(mod_ref, is_ctx, (0, 1, 2))
    h = (_rms(x3, nw_ref[0]) * (1.0 + sc) + sh).reshape(TR, D_MODEL).astype(BF16)
    gate = lambda n: _dot_nt(h, wg_ref[0, _O_GATE + n * D_MODEL:_O_GATE + (n + 1) * D_MODEL, :])
    g0 = gate(0)
    g1 = gate(1)
    acc = _sigmoid(g0) * _dot(oa.reshape(TR, HEADS_W), wba_ref[0])
    g2 = gate(2)
    acc = acc + _sigmoid(g1) * _dot(ob_ref[...].reshape(TR, BRANCH_W), wb_ref[0, 0])
    yt = yf_ref[...].astype(F32) + yb_ref[...].astype(F32) + d_ref[0] * ut_ref[...].astype(F32)
    y_hi = yt.astype(BF16)
    y_lo = (yt - y_hi.astype(F32)).astype(BF16)
    y = _dot(pbt_ref[...], y_hi) + _dot(pbt_ref[...], y_lo)
    vg = _dot(_gelu_tanh(y).astype(BF16), wglu_ref[0])
    oc = vg[:, 0:BRANCH_W] * _sigmoid(vg[:, BRANCH_W:2 * BRANCH_W])
    g3 = gate(3)
    acc = acc + _sigmoid(g2) * _dot(oc.astype(BF16), wb_ref[0, 1])
    o = oret_ref[...].reshape(TR, BRANCH_W)
    dl = o.astype(F32) - _dot(o, pavg_ref[...])
    var = _dot((dl * dl).astype(BF16), pavg_ref[...])
    g = rg_ref[...].reshape(TR, BRANCH_W).astype(F32)
    od = g * _sigmoid(g) * (dl * lax.rsqrt(var + NORM_EPS) * gnw_ref[0])
    acc = acc + _sigmoid(g3) * _dot(od.astype(BF16), wb_ref[0, 2])
    m = _dot(acc.astype(BF16), wout_ref[0])
    o_ref[...] = x3 + gate_res * m.reshape(BATCH, TT, D_MODEL)


def _merge(l, x_ctx, x_lat, mod, sw, oa_ctx, oa_lat, ob, yf, yb, u_t, oret, rg, skip_ctx):
    off = CTX_TILES if skip_ctx else 0
    nt = ALL_TILES - off
    ctx_spec, lat_spec = _tile_specs(l == 0, skip_ctx)
    oac_spec, oal_spec = _tile_specs(True, skip_ctx, HEADS_W)
    tok = lambda w: pl.BlockSpec((BATCH, TT, w), lambda j: (0, j + off, 0))
    tmaj = pl.BlockSpec((TR, BRANCH_W), lambda j: (j + off, 0))
    return pl.pallas_call(
        functools.partial(_merge_kernel, skip_ctx=skip_ctx),
        grid=(nt,),
        in_specs=[ctx_spec, lat_spec,
                  _layer(l, (16, ADA_CHUNKS * D_MODEL)),
                  _layer(l, (1, D_MODEL)),
                  _layer(l, (IN_COLS, D_MODEL), single=True),
                  oac_spec, oal_spec,
                  tok(BRANCH_W), tmaj, tmaj, tmaj,
                  _layer(l, (1, BRANCH_W)),
                  _layer(l, (BRANCH_W, 2 * BRANCH_W)),
                  tok(BRANCH_W), tok(BRANCH_W),
                  _layer(l, (1, BRANCH_W)),
                  _full((BRANCH_W, BRANCH_W)),
                  _full((TR, TR)),
                  _layer(l, (HEADS_W, D_MODEL), single=True),
                  _layer(l, (N_BRANCH - 1, BRANCH_W, D_MODEL), single=True),
                  _layer(l, (D_MODEL, D_MODEL), single=True)],
        out_specs=pl.BlockSpec((BATCH, TT, D_MODEL), lambda j: (0, j, 0)),
        out_shape=jax.ShapeDtypeStruct((BATCH, nt * TT, D_MODEL), F32),
        compiler_params=_cparams(1),
        name="merge",
    )(x_ctx, x_lat, mod, sw["norm_mix"], sw["w_in_t"], oa_ctx, oa_lat, ob, yf, yb, u_t, sw["s5_d"], sw["w_glu"],
      oret, rg, sw["gn_w"], jnp.asarray(_head_avg(), dtype=BF16), jnp.asarray(_tile_perm().T, dtype=BF16),
      sw["wb_a"], sw["wb_rest"], sw["w_out"])


def _ffn_kernel(x_ref, mod_ref, nw_ref, w1_ref, w2_ref, o_ref, *, skip_ctx):
    is_ctx = None if skip_ctx else pl.program_id(0) < CTX_TILES
    sh, sc, gate_res = _mod_chunks(mod_ref, is_ctx, (3, 4, 5))
    x3 = x_ref[...]
    h = (_rms(x3, nw_ref[0]) * (1.0 + sc) + sh).reshape(TR, D_MODEL).astype(BF16)
    a = jnp.maximum(_dot(h, w1_ref[0]), 0.0)
    f = _dot((a * a).astype(BF16), w2_ref[0])
    o_ref[...] = x3 + gate_res * f.reshape(BATCH, TT, D_MODEL)


def _ffn(l, xm, mod, sw, skip_ctx):
    nt = xm.shape[1] // TT
    tok = pl.BlockSpec((BATCH, TT, D_MODEL), lambda j: (0, j, 0))
    return pl.pallas_call(
        functools.partial(_ffn_kernel, skip_ctx=skip_ctx),
        grid=(nt,),
        in_specs=[tok,
                  _layer(l, (16, ADA_CHUNKS * D_MODEL)),
                  _layer(l, (1, D_MODEL)),
                  _layer(l, (D_MODEL, D_FF), single=True),
                  _layer(l, (D_FF, D_MODEL), single=True)],
        out_specs=tok,
        out_shape=jax.ShapeDtypeStruct(xm.shape, F32),
        compiler_params=_cparams(1),
        name="ffn",
    )(xm, mod, sw["norm_ffn"], sw["ffn_w1"], sw["ffn_w2"])


def _stacked_weights(p):
    w_in_t = jnp.swapaxes(p["w_in"], 1, 2).astype(BF16)
    perm, _ = _mla_rope_perm()
    zeros = lambda *s: jnp.zeros((DEPTH,) + s, F32)
    wu = p["mla_w_ukv"].reshape(DEPTH, KV_LORA, MLA_HEADS, MLA_NOPE + MLA_V)
    pad_heads = lambda t: jnp.concatenate(
        [t, zeros(t.shape[1], MLA_HEADS, HEAD_PAD - t.shape[3])], -1).reshape(DEPTH, t.shape[1], HEADS_W)
    top = jnp.concatenate([pad_heads(wu[..., :MLA_NOPE]), zeros(KV_LORA, HEADS_W),
                           pad_heads(wu[..., MLA_NOPE:])], axis=2)
    place = jnp.broadcast_to(jnp.asarray(_kv_place())[None], (DEPTH, 128, 3 * HEADS_W))
    wkv = jnp.concatenate([top, place], axis=1).astype(BF16)

    wuq = p["mla_w_uq"].reshape(DEPTH, Q_LORA, MLA_HEADS, MLA_QK)
    qp = jnp.concatenate([zeros(Q_LORA, MLA_HEADS, MLA_NOPE), wuq[..., MLA_NOPE:][..., perm],
                          zeros(Q_LORA, MLA_HEADS, 32)], -1).reshape(DEPTH, Q_LORA, HEADS_W)
    wq = jnp.concatenate([pad_heads(wuq), qp], axis=2).astype(BF16)

    def head_w(v):
        wf = jnp.concatenate([v, zeros(32)], -1)[:, None, :]
        wp = jnp.concatenate([zeros(MLA_NOPE), v[:, MLA_NOPE:][:, perm], zeros(32)], -1)[:, None, :]
        return wf, wp

    head_ws = jnp.concatenate(head_w(p["mla_qk_norm_k"]) + head_w(p["mla_qk_norm_q"]), axis=1)

    def b_compact(b):
        t = b.transpose(0, 1, 2, 4, 3).reshape(DEPTH, 2, BRANCH_W, S5_STATE)
        return jnp.concatenate([t, t], axis=-1)

    c_compact = lambda c: c.transpose(0, 1, 4, 2, 3).reshape(DEPTH, 2, S5_STATE, BRANCH_W)
    vec = lambda a: a.reshape(DEPTH, 2, 1, S5_LANES)

    logit = p["ret_decay_logit"]
    wb = p["w_branch"]
    wb_a = jnp.concatenate([wb[:, 0].reshape(DEPTH, MLA_HEADS, MLA_V, D_MODEL),
                            zeros(MLA_HEADS, HEAD_PAD - MLA_V, D_MODEL)], axis=2).reshape(DEPTH, HEADS_W, D_MODEL)
    row = lambda a: a[:, None, :]
    return dict(
        w_in_t=w_in_t,
        norm_mix=row(p["norm_mix_w"]), norm_ffn=row(p["norm_ffn_w"]),
        kv_norm=row(p["mla_kv_norm"]), q_norm=row(p["mla_q_norm"]),
        wkv=wkv, wq=wq, head_w=head_ws,
        s5_lam_re=vec(p["s5_lam_re"]), s5_lam_im=vec(p["s5_lam_im"]),
        s5_log_step=vec(jnp.repeat(p["s5_log_step"], S5_STATE, axis=-1)),
        s5_bre=b_compact(p["s5_b_re"]), s5_bim=b_compact(p["s5_b_im"]),
        s5_cre=c_compact(p["s5_c_re"]), s5_cim=c_compact(p["s5_c_im"]),
        s5_d=row(p["s5_d"]), w_glu=p["s5_w_glu"].astype(BF16),
        lgl=jnp.repeat(logit, RET_HD, axis=-1).reshape(DEPTH, 2, 1, BRANCH_W),
        lgh=jnp.broadcast_to(logit[:, :, :, None, None], (DEPTH, 2, RET_HEADS, 1, 128)),
        gn_w=row(p["ret_gn_w"]),
        wb_a=wb_a.astype(BF16), wb_rest=wb[:, 1:].astype(BF16), w_out=p["w_out"].astype(BF16),
        ffn_w1=p["ffn_w1"].astype(BF16), ffn_w2=p["ffn_w2"].astype(BF16))


def kernel(x, c, ctx, c_ctx, ada_w, ada_b, norm_mix_w, norm_ffn_w, w_in, mla_q_norm, mla_w_uq, mla_kv_norm,
           mla_w_ukv, mla_qk_norm_q, mla_qk_norm_k, s5_lam_re, s5_lam_im, s5_log_step, s5_b_re, s5_b_im,
           s5_c_re, s5_c_im, s5_d, s5_w_glu, ret_decay_logit, ret_gn_w, w_branch, w_out, ffn_w1, ffn_w2):
    p = dict(norm_mix_w=norm_mix_w, norm_ffn_w=norm_ffn_w, w_in=w_in, mla_q_norm=mla_q_norm,
             mla_w_uq=mla_w_uq, mla_kv_norm=mla_kv_norm, mla_w_ukv=mla_w_ukv, mla_qk_norm_q=mla_qk_norm_q,
             mla_qk_norm_k=mla_qk_norm_k, s5_lam_re=s5_lam_re, s5_lam_im=s5_lam_im, s5_log_step=s5_log_step,
             s5_b_re=s5_b_re, s5_b_im=s5_b_im, s5_c_re=s5_c_re, s5_c_im=s5_c_im, s5_d=s5_d,
             s5_w_glu=s5_w_glu, ret_decay_logit=ret_decay_logit, ret_gn_w=ret_gn_w, w_branch=w_branch,
             w_out=w_out, ffn_w1=ffn_w1, ffn_w2=ffn_w2)
    sw = _stacked_weights(p)
    c16 = jnp.concatenate([c, c_ctx[None, :], jnp.zeros((16 - BATCH - 1, D_MODEL), F32)], axis=0)
    mod = _ada_mod(c16, ada_w, ada_b)
    lam, bblk, cblk = _s5_params(sw)
    x_ctx, x_lat = ctx, x
    for l in range(DEPTH):
        last = l == DEPTH - 1
        q, k, v, u_t, uc, us, rq, rk, rv, rg = _in_proj(l, x_ctx, x_lat, mod, sw)
        oa_lat = _attention_latent(q, k, v)
        oa_ctx = oa_lat if last else _attention_context(q, k, v)
        ob = _fnet(uc, us)
        yf, yb = _s5(l, u_t, lam, bblk, cblk)
        oret = _retention(l, rq, rk, rv, sw)
        xm = _merge(l, x_ctx, x_lat, mod, sw, oa_ctx, oa_lat, ob, yf, yb, u_t, oret, rg, last)
        x_ctx = x_lat = _ffn(l, xm, mod, sw, last)
    return x_lat
```

```python
import functools
import math

import numpy as np
import jax
import jax.numpy as jnp
from jax import lax
from jax.experimental import pallas as pl
from jax.experimental.pallas import tpu as pltpu

F32 = jnp.float32
BF16 = jnp.bfloat16

D_MODEL = 1024
BATCH = 8
SEQ = 2048
DEPTH = 2
GRID_W = 64
CTX_LEN = 256
LCAT = CTX_LEN + SEQ
N_BRANCH = 4
BRANCH_W = 256
NORM_EPS = 1e-6
ROPE_BASE = 10000.0
ADA_CHUNKS = 6

MLA_HEADS = 4
MLA_NOPE = 64
MLA_ROPE = 32
MLA_QK = MLA_NOPE + MLA_ROPE
MLA_V = 64
Q_LORA = 256
KV_LORA = 128
HEAD_PAD = 128
HEADS_W = MLA_HEADS * HEAD_PAD

FNET_GROUPS = 4
FNET_GW = BRANCH_W // FNET_GROUPS

S5_GROUP_CH = 16
S5_GROUPS = BRANCH_W // S5_GROUP_CH
S5_STATE = 64
S5_LANES = S5_GROUPS * S5_STATE

RET_HEADS = 4
RET_HD = BRANCH_W // RET_HEADS
RET_CHUNK = 256
RET_NCHUNK = LCAT // RET_CHUNK
RET_CTX_CHUNKS = CTX_LEN // RET_CHUNK
RET_SPB = 2

D_FF = 4 * D_MODEL

_O_KV, _O_KR, _O_S5, _O_RK, _O_RV = 0, 128, 160, 416, 672
_O_Q, _O_FN, _O_RQ, _O_RG, _O_GATE = 928, 1184, 1440, 1696, 1952
IN_COLS = _O_GATE + N_BRANCH * D_MODEL

TT = 64
TR = TT * BATCH
CTX_TILES = CTX_LEN // TT
ALL_TILES = LCAT // TT
TQ = 256
ATTN_QB = 4
ATTN_AHEAD = 1
S5_TC = 128
S5_ROWS = S5_TC * BATCH
S5_STEPS = LCAT // S5_TC
S5_CTX_STEPS = CTX_LEN // S5_TC
S5_SUB = 4

VMEM_LIMIT = 56 * 1024 * 1024


def _cparams(n_grid):
    return pltpu.CompilerParams(dimension_semantics=("arbitrary",) * n_grid,
                                vmem_limit_bytes=VMEM_LIMIT)


def _dot(a, b):
    return jnp.dot(a, b, preferred_element_type=F32)


def _dot_nt(a, b):
    return lax.dot_general(a, b, (((1,), (1,)), ((), ())), preferred_element_type=F32)


def _sigmoid(x):
    return 0.5 * (jnp.tanh(0.5 * x) + 1.0)


def _gelu_tanh(y):
    return 0.5 * y * (1.0 + jnp.tanh(math.sqrt(2.0 / math.pi) * (y + 0.044715 * (y * y * y))))


def _rms(x, w):
    return x * lax.rsqrt(jnp.mean(x * x, axis=-1, keepdims=True) + NORM_EPS) * w


def _full(shape):
    n = len(shape)
    return pl.BlockSpec(shape, lambda *_: (0,) * n)


def _layer(l, shape, single=False):
    n = len(shape)
    mode = dict(pipeline_mode=pl.Buffered(1)) if single else {}
    return pl.BlockSpec((1,) + tuple(shape), lambda *_: (l,) + (0,) * n, **mode)


def _mla_rope_perm():
    r = np.arange(MLA_ROPE)
    first = (r % 16) < 8
    return np.where(first, r + 8, r - 8), np.where(first, -1.0, 1.0)


@functools.lru_cache(maxsize=None)
def _mla_tables():
    pos = np.arange(SEQ)
    rows, cols = pos // GRID_W, pos % GRID_W
    freqs = ROPE_BASE ** (-np.arange(8, dtype=np.float64) / 8)
    r = np.arange(MLA_ROPE)
    _, sign = _mla_rope_perm()
    p = np.where((r // 16 == 0)[None, :], rows[:, None], cols[:, None]).astype(np.float64)
    ang = p * freqs[(r % 16) % 8][None, :]
    cosf = np.zeros((LCAT, HEAD_PAD))
    sinf = np.zeros((LCAT, HEAD_PAD))
    cosf[:, :MLA_QK] = 1.0
    cosf[CTX_LEN:, MLA_NOPE:MLA_QK] = np.cos(ang)
    sinf[CTX_LEN:, MLA_NOPE:MLA_QK] = np.sin(ang) * sign[None, :]
    return cosf.astype(np.float32), sinf.astype(np.float32)


def _ret_perm():
    d = np.arange(RET_HD)
    first = d < RET_HD // 2
    return np.where(first, d + RET_HD // 2, d - RET_HD // 2), np.where(first, -1.0, 1.0)


@functools.lru_cache(maxsize=None)
def _ret_tables():
    half = RET_HD // 2
    pos = np.arange(SEQ, dtype=np.float64)
    freqs = ROPE_BASE ** (-np.arange(half, dtype=np.float64) / half)
    d = np.arange(RET_HD)
    perm, sign = _ret_perm()
    ang = pos[:, None] * freqs[d % half][None, :]
    cosr = np.ones((LCAT, RET_HD))
    sinr = np.zeros((LCAT, RET_HD))
    cosr[CTX_LEN:] = np.cos(ang)
    sinr[CTX_LEN:] = np.sin(ang) * sign[None, :]
    cosr = np.tile(cosr, (1, RET_HEADS))
    sinr = np.tile(sinr, (1, RET_HEADS))
    pm = np.zeros((BRANCH_W, BRANCH_W))
    for h in range(RET_HEADS):
        pm[h * RET_HD + perm, h * RET_HD + d] = 1.0
    return cosr.astype(np.float32), sinr.astype(np.float32), pm.astype(np.float32)


def _dft(n, scale):
    k = np.arange(n)
    kt = (k[:, None] * k[None, :]) % n
    ang = 2.0 * np.pi * kt / n
    return np.cos(ang) * scale, np.sin(ang) * scale


@functools.lru_cache(maxsize=None)
def _fnet_tables():
    cw, sw = _dft(FNET_GW, 1.0)
    t = np.zeros((BRANCH_W, 2 * BRANCH_W))
    for g in range(FNET_GROUPS):
        s = slice(g * FNET_GW, (g + 1) * FNET_GW)
        t[s, s] = cw
        t[s, BRANCH_W + g * FNET_GW:BRANCH_W + (g + 1) * FNET_GW] = -sw
    cl, sl = _dft(SEQ, 1.0 / math.sqrt(SEQ * FNET_GW))
    clc, slc = _dft(CTX_LEN, 1.0 / math.sqrt(CTX_LEN * FNET_GW))
    return tuple(a.astype(np.float32) for a in (t, cl, sl, clc, slc))


def _bf16_const(a):
    return jnp.asarray(a).astype(BF16)


@functools.lru_cache(maxsize=None)
def _head_avg():
    p = np.zeros((BRANCH_W, BRANCH_W))
    for h in range(RET_HEADS):
        p[h * RET_HD:(h + 1) * RET_HD, h * RET_HD:(h + 1) * RET_HD] = 1.0 / RET_HD
    return p.astype(np.float32)


@functools.lru_cache(maxsize=None)
def _tile_perm():
    p = np.zeros((TR, TR), np.float32)
    for b in range(BATCH):
        for t in range(TT):
            p[t * BATCH + b, b * TT + t] = 1.0
    return p


@functools.lru_cache(maxsize=None)
def _kv_place():
    perm, _ = _mla_rope_perm()
    place = np.zeros((128, 3 * HEADS_W), np.float32)
    for h in range(MLA_HEADS):
        for r in range(MLA_ROPE):
            place[r, h * HEAD_PAD + MLA_NOPE + r] = 1.0
            place[perm[r], HEADS_W + h * HEAD_PAD + MLA_NOPE + r] = 1.0
    return place


@functools.lru_cache(maxsize=None)
def _v_ones():
    v = np.zeros((1, HEADS_W), np.float32)
    v[0, np.arange(MLA_HEADS) * HEAD_PAD + MLA_V] = 1.0
    return v


ADA_TN = 1536


def _ada_kernel(c_ref, w_ref, b_ref, o_ref):
    c = c_ref[...]
    s = (c * _sigmoid(c)).astype(BF16)
    o_ref[0] = _dot(s, w_ref[0].astype(BF16)) + b_ref[0]


def _ada_mod(c16, ada_w, ada_b):
    n = ADA_CHUNKS * D_MODEL
    return pl.pallas_call(
        _ada_kernel,
        grid=(DEPTH, n // ADA_TN),
        in_specs=[pl.BlockSpec((16, D_MODEL), lambda l, j: (0, 0)),
                  pl.BlockSpec((1, D_MODEL, ADA_TN), lambda l, j: (l, 0, j)),
                  pl.BlockSpec((1, 1, ADA_TN), lambda l, j: (l, 0, j))],
        out_specs=pl.BlockSpec((1, 16, ADA_TN), lambda l, j: (l, 0, j)),
        out_shape=jax.ShapeDtypeStruct((DEPTH, 16, n), F32),
        compiler_params=_cparams(2),
        name="ada_mod",
    )(c16, ada_w, ada_b.reshape(DEPTH, 1, n))


def _mod_chunks(mod_ref, is_ctx, idxs):
    out = []
    for i in idxs:
        sl = slice(i * D_MODEL, (i + 1) * D_MODEL)
        m = mod_ref[0, 0:BATCH, sl]
        if is_ctx is not None:
            m = jnp.where(is_ctx, mod_ref[0, BATCH:BATCH + 1, sl], m)
        out.append(m[:, None, :])
    return out


def _tile_specs(separate, skip_ctx, width=D_MODEL):
    base = 0 if separate else CTX_TILES
    shape = (BATCH, TT, width)
    if skip_ctx:
        return (pl.BlockSpec(shape, lambda j: (0, 0, 0)),
                pl.BlockSpec(shape, lambda j: (0, j + base, 0)))
    return (pl.BlockSpec(shape, lambda j: (0, jnp.minimum(j, CTX_TILES - 1), 0)),
            pl.BlockSpec(shape, lambda j: (0, jnp.maximum(j - CTX_TILES, 0) + base, 0)))


def _head_norm_rot(xf, xp, a, b, scale):
    outs = []
    for h in range(MLA_HEADS):
        f = xf[:, h * HEAD_PAD:(h + 1) * HEAD_PAD]
        p = xp[:, h * HEAD_PAD:(h + 1) * HEAD_PAD]
        n = lax.rsqrt(jnp.sum(f * f, axis=-1, keepdims=True) * (1.0 / MLA_QK) + NORM_EPS) * scale
        outs.append(n * (f * a + p * b))
    return jnp.concatenate(outs, axis=-1)


def _rows(table):
    w = table.shape[-1]
    return jnp.broadcast_to(table[None], (BATCH, TT, w)).reshape(TR, w)


def _tile3(x):
    return x.reshape(BATCH, TT, x.shape[-1])


def _inproj_kernel(xc_ref, xl_ref, mod_ref, nw_ref, w_ref, kvw_ref, wkv_ref, qnw_ref, wq_ref,
                   cosf_ref, sinf_ref, hw_ref, pm_ref, cosr_ref, sinr_ref, t_ref, ptb_ref, vone_ref,
                   q_out, k_out, v_out, ut_out, uc_out, us_out, rq_out, rk_out, rv_out, rg_out):
    is_ctx = pl.program_id(0) < CTX_TILES
    x3 = jnp.where(is_ctx, xc_ref[...], xl_ref[...])
    sh, sc = _mod_chunks(mod_ref, is_ctx, (0, 1))
    h = (_rms(x3, nw_ref[0]) * (1.0 + sc) + sh).reshape(TR, D_MODEL).astype(BF16)
    za = _dot_nt(h, w_ref[0, 0:256, :])
    z = _dot_nt(h, w_ref[0, _O_S5:_O_GATE, :])

    cosf, sinf = _rows(cosf_ref[...]), _rows(sinf_ref[...])
    hw = hw_ref[0]
    kvn = _rms(za[:, 0:128], kvw_ref[0]).astype(BF16)
    lhs = jnp.concatenate([kvn, za[:, 128:256].astype(BF16)], axis=-1)
    kv = _dot(lhs, wkv_ref[0])
    k = _head_norm_rot(kv[:, 0:HEADS_W], kv[:, HEADS_W:2 * HEADS_W],
                       cosf * hw[0:1], sinf * hw[1:2], 1.0)
    k_out[...] = _tile3(k).astype(BF16)
    v_out[...] = _tile3(kv[:, 2 * HEADS_W:3 * HEADS_W] + vone_ref[...]).astype(BF16)
    qn = _rms(z[:, 768:1024], qnw_ref[0]).astype(BF16)
    qq = _dot(qn, wq_ref[0])
    q = _head_norm_rot(qq[:, 0:HEADS_W], qq[:, HEADS_W:2 * HEADS_W],
                       cosf * hw[2:3], sinf * hw[3:4], MLA_QK ** -0.5)
    q_out[...] = _tile3(q).astype(BF16)
    ut_out[...] = _dot(ptb_ref[...], z[:, 0:256].astype(BF16)).astype(BF16)
    ucs = _dot(z[:, 1024:1280].astype(BF16), t_ref[...])
    uc_out[...] = _tile3(ucs[:, 0:BRANCH_W]).astype(BF16)
    us_out[...] = _tile3(ucs[:, BRANCH_W:2 * BRANCH_W]).astype(BF16)
    cosr, sinr = _rows(cosr_ref[...]), _rows(sinr_ref[...])
    rq = z[:, 1280:1536]
    rk = z[:, 256:512]
    rq = rq * cosr + _dot(rq.astype(BF16), pm_ref[...]) * sinr
    rk = rk * cosr + _dot(rk.astype(BF16), pm_ref[...]) * sinr
    rq_out[...] = _tile3(rq).astype(BF16)
    rk_out[...] = _tile3(rk * (RET_HD ** -0.5)).astype(BF16)
    rv_out[...] = _tile3(z[:, 512:768]).astype(BF16)
    rg_out[...] = _tile3(z[:, 1536:1792]).astype(BF16)


def _in_proj(l, x_ctx, x_lat, mod, sw):
    ctx_spec, lat_spec = _tile_specs(l == 0, False)
    tok = lambda w: pl.BlockSpec((BATCH, TT, w), lambda j: (0, j, 0))
    tab = lambda w: pl.BlockSpec((TT, w), lambda j: (j, 0))
    cosf, sinf = _mla_tables()
    cosr, sinr, pm = _ret_tables()
    bshape = lambda w: jax.ShapeDtypeStruct((BATCH, LCAT, w), BF16)
    return pl.pallas_call(
        _inproj_kernel,
        grid=(ALL_TILES,),
        in_specs=[ctx_spec, lat_spec,
                  _layer(l, (16, ADA_CHUNKS * D_MODEL)),
                  _layer(l, (1, D_MODEL)),
                  _layer(l, (_O_GATE, D_MODEL)),
                  _layer(l, (1, KV_LORA)),
                  _layer(l, (256, 3 * HEADS_W)),
                  _layer(l, (1, Q_LORA)),
                  _layer(l, (Q_LORA, 2 * HEADS_W)),
                  tab(HEAD_PAD), tab(HEAD_PAD),
                  _layer(l, (4, HEAD_PAD)),
                  _full((BRANCH_W, BRANCH_W)),
                  tab(BRANCH_W), tab(BRANCH_W),
                  _full((BRANCH_W, 2 * BRANCH_W)),
                  _full((TR, TR)),
                  _full((1, HEADS_W))],
        out_specs=[tok(HEADS_W), tok(HEADS_W), tok(HEADS_W),
                   pl.BlockSpec((TR, BRANCH_W), lambda j: (j, 0)),
                   tok(256), tok(256), tok(256), tok(256), tok(256), tok(256)],
        out_shape=[bshape(HEADS_W), bshape(HEADS_W), bshape(HEADS_W),
                   jax.ShapeDtypeStruct((LCAT * BATCH, BRANCH_W), BF16),
                   bshape(256), bshape(256), bshape(256), bshape(256), bshape(256), bshape(256)],
        compiler_params=_cparams(1),
        name="in_proj",
    )(x_ctx, x_lat, mod, sw["norm_mix"], sw["w_in_t"], sw["kv_norm"], sw["wkv"], sw["q_norm"], sw["wq"],
      jnp.asarray(cosf), jnp.asarray(sinf), sw["head_w"], jnp.asarray(pm, dtype=BF16),
      jnp.asarray(cosr), jnp.asarray(sinr), _bf16_const(_fnet_tables()[0]),
      jnp.asarray(_tile_perm(), dtype=BF16), jnp.asarray(_v_ones()))


def _attn_kernel(*refs):
    q_refs, (k_ref, v_ref, o_ref) = refs[:-3], refs[-3:]
    units = [(r, h, slice(h * HEAD_PAD, (h + 1) * HEAD_PAD))
             for r in range(len(q_refs)) for h in range(MLA_HEADS)]
    scores = lambda u: _dot_nt(q_refs[u[0]][0, :, u[2]], k_ref[0, :, u[2]])
    lane = lax.broadcasted_iota(jnp.int32, (1, HEAD_PAD), 1)
    pending = [scores(u) for u in units[:ATTN_AHEAD]]
    for i, (r, h, sl) in enumerate(units):
        if i + ATTN_AHEAD < len(units):
            pending.append(scores(units[i + ATTN_AHEAD]))
        s = pending.pop(0)
        p = jnp.exp((s - jnp.max(s, axis=-1, keepdims=True)).astype(BF16))
        oh = _dot(p, v_ref[0, :, sl])
        oh = oh * (1.0 / oh[:, MLA_V:MLA_V + 1])
        if h % 2 == 0:
            even = oh
        else:
            pair = jnp.where(lane < MLA_V, even, pltpu.roll(oh, MLA_V, axis=1))
            o_ref[0, r * TQ:(r + 1) * TQ, (h // 2) * HEAD_PAD:(h // 2 + 1) * HEAD_PAD] = pair.astype(BF16)


def _attention_latent(q, k, v):
    first = CTX_LEN // TQ
    qspec = lambda r: pl.BlockSpec((1, TQ, HEADS_W), lambda b, j: (b, first + ATTN_QB * j + r, 0))
    kv = pl.BlockSpec((1, LCAT, HEADS_W), lambda b, j: (b, 0, 0))
    return pl.pallas_call(
        _attn_kernel,
        grid=(BATCH, SEQ // (ATTN_QB * TQ)),
        in_specs=[qspec(r) for r in range(ATTN_QB)] + [kv, kv],
        out_specs=pl.BlockSpec((1, ATTN_QB * TQ, BRANCH_W), lambda b, j: (b, j, 0)),
        out_shape=jax.ShapeDtypeStruct((BATCH, SEQ, BRANCH_W), BF16),
        compiler_params=_cparams(2),
        name="mla_attention",
    )(*([q] * ATTN_QB), k, v)


def _attention_context(q, k, v):
    blk = pl.BlockSpec((1, CTX_LEN, HEADS_W), lambda b: (b, 0, 0))
    return pl.pallas_call(
        _attn_kernel,
        grid=(BATCH,),
        in_specs=[blk, blk, blk],
        out_specs=pl.BlockSpec((1, CTX_LEN, BRANCH_W), lambda b: (b, 0, 0)),
        out_shape=jax.ShapeDtypeStruct((BATCH, CTX_LEN, BRANCH_W), BF16),
        compiler_params=_cparams(1),
        name="mla_attention_ctx",
    )(q, k, v)


def _fnet_kernel(uc_ref, us_ref, cl_ref, sl_ref, clc_ref, slc_ref, o_ref):
    o_ref[0, 0:CTX_LEN, :] = (_dot(clc_ref[...], uc_ref[0, 0:CTX_LEN, :])
                              + _dot(slc_ref[...], us_ref[0, 0:CTX_LEN, :])).astype(BF16)
    o_ref[0, CTX_LEN:LCAT, :] = (_dot(cl_ref[...], uc_ref[0, CTX_LEN:LCAT, :])
                                 + _dot(sl_ref[...], us_ref[0, CTX_LEN:LCAT, :])).astype(BF16)


def _fnet(uc, us):
    cl, sl, clc, slc = (_bf16_const(a) for a in _fnet_tables()[1:])
    tok = pl.BlockSpec((1, LCAT, BRANCH_W), lambda b: (b, 0, 0))
    return pl.pallas_call(
        _fnet_kernel,
        grid=(BATCH,),
        in_specs=[tok, tok, _full((SEQ, SEQ)), _full((SEQ, SEQ)),
                  _full((CTX_LEN, CTX_LEN)), _full((CTX_LEN, CTX_LEN))],
        out_specs=tok,
        out_shape=jax.ShapeDtypeStruct((BATCH, LCAT, BRANCH_W), BF16),
        compiler_params=_cparams(1),
        name="fnet_dft",
    )(uc, us, cl, sl, clc, slc)


def _s5_param_kernel(lr_ref, li_ref, ls_ref, bre_ref, bim_ref, cre_ref, cim_ref, lam_out, b_out, c_out):
    lr, li = lr_ref[0, 0], li_ref[0, 0]
    step = jnp.exp(ls_ref[0, 0])
    mag = jnp.exp(lr * step)
    lbr = mag * jnp.cos(li * step)
    lbi = mag * jnp.sin(li * step)
    den = 1.0 / (lr * lr + li * li)
    cr = ((lbr - 1.0) * lr + lbi * li) * den
    ci = (lbi * lr - (lbr - 1.0) * li) * den
    lam_out[0, 0, 0] = jnp.broadcast_to(lbr, (BATCH, S5_LANES))
    lam_out[0, 0, 1] = jnp.broadcast_to(lbi, (BATCH, S5_LANES))
    grp_b = (lax.broadcasted_iota(jnp.int32, (BRANCH_W, S5_LANES), 0) // S5_GROUP_CH
             == lax.broadcasted_iota(jnp.int32, (BRANCH_W, S5_LANES), 1) // S5_STATE)
    wide = lambda r: jnp.concatenate([r[0, 0]] * (S5_LANES // 128), axis=-1)
    bre = jnp.where(grp_b, wide(bre_ref), 0.0)
    bim = jnp.where(grp_b, wide(bim_ref), 0.0)
    b_out[0, 0, :, 0:S5_LANES] = (cr * bre - ci * bim).astype(BF16)
    b_out[0, 0, :, S5_LANES:2 * S5_LANES] = (cr * bim + ci * bre).astype(BF16)
    grp_c = (lax.broadcasted_iota(jnp.int32, (S5_LANES, BRANCH_W), 0) // S5_STATE
             == lax.broadcasted_iota(jnp.int32, (S5_LANES, BRANCH_W), 1) // S5_GROUP_CH)
    tall = lambda r: jnp.concatenate([r[0, 0]] * S5_GROUPS, axis=0)
    c_out[0, 0, 0:S5_LANES, :] = jnp.where(grp_c, tall(cre_ref), 0.0).astype(BF16)
    c_out[0, 0, S5_LANES:2 * S5_LANES, :] = jnp.where(grp_c, -tall(cim_ref), 0.0).astype(BF16)


def _s5_params(sw):
    spec = lambda *s: pl.BlockSpec((1, 1) + s, lambda l, d: (l, d) + (0,) * len(s))
    return pl.pallas_call(
        _s5_param_kernel,
        grid=(DEPTH, 2),
        in_specs=[spec(1, S5_LANES), spec(1, S5_LANES), spec(1, S5_LANES),
                  spec(BRANCH_W, 128), spec(BRANCH_W, 128), spec(S5_STATE, BRANCH_W), spec(S5_STATE, BRANCH_W)],
        out_specs=[spec(2, BATCH, S5_LANES), spec(BRANCH_W, 2 * S5_LANES), spec(2 * S5_LANES, BRANCH_W)],
        out_shape=[jax.ShapeDtypeStruct((DEPTH, 2, 2, BATCH, S5_LANES), F32),
                   jax.ShapeDtypeStruct((DEPTH, 2, BRANCH_W, 2 * S5_LANES), BF16),
                   jax.ShapeDtypeStruct((DEPTH, 2, 2 * S5_LANES, BRANCH_W), BF16)],
        compiler_params=_cparams(2),
        name="s5_discretise",
    )(sw["s5_lam_re"], sw["s5_lam_im"], sw["s5_log_step"], sw["s5_bre"], sw["s5_bim"], sw["s5_cre"], sw["s5_cim"])


def _s5_bwd_block(i):
    return jnp.where(i < S5_CTX_STEPS, S5_CTX_STEPS - 1 - i, S5_STEPS + S5_CTX_STEPS - 1 - i)


def _s5_kernel(uf_ref, ub_ref, lam_ref, b_ref, c_ref, yf_ref, yb_ref, xf_scr, xb_scr, st_scr):
    @pl.when(pl.program_id(0) == 0)
    def _():
        st_scr[...] = jnp.zeros_like(st_scr)

    sub_t = S5_TC // S5_SUB
    sub_rows = lambda k: pl.ds(k * sub_t * BATCH, sub_t * BATCH)
    order = (range(S5_SUB), range(S5_SUB - 1, -1, -1))
    dirs = ((uf_ref, xf_scr, yf_ref), (ub_ref, xb_scr, yb_ref))
    for d, (u_ref, x_scr, _) in enumerate(dirs):
        for k in order[d]:
            x_scr[sub_rows(k), :] = _dot(u_ref[sub_rows(k), :], b_ref[0, d])
    re, im = pl.ds(0, S5_LANES), pl.ds(S5_LANES, S5_LANES)
    for d, (_, x_scr, y_ref) in enumerate(dirs):
        xr, xi = st_scr[2 * d], st_scr[2 * d + 1]
        for k in order[d]:
            steps = range(k * sub_t, (k + 1) * sub_t)
            for t in (steps if d == 0 else reversed(steps)):
                rows = pl.ds(t * BATCH, BATCH)
                ar, ai = lam_ref[0, d, 0], lam_ref[0, d, 1]
                xr, xi = (ar * xr - ai * xi + x_scr[rows, re], ar * xi + ai * xr + x_scr[rows, im])
                x_scr[rows, re] = xr
                x_scr[rows, im] = xi
            y_ref[sub_rows(k), :] = _dot(x_scr[sub_rows(k), :].astype(BF16), c_ref[0, d]).astype(BF16)
        st_scr[2 * d], st_scr[2 * d + 1] = xr, xi


def _s5(l, u_t, lam, bblk, cblk):
    fwd = pl.BlockSpec((S5_ROWS, BRANCH_W), lambda i: (i, 0))
    bwd = pl.BlockSpec((S5_ROWS, BRANCH_W), lambda i: (_s5_bwd_block(i), 0))
    return pl.pallas_call(
        _s5_kernel,
        grid=(S5_STEPS,),
        in_specs=[fwd, bwd, _layer(l, (2, 2, BATCH, S5_LANES)),
                  _layer(l, (2, BRANCH_W, 2 * S5_LANES)), _layer(l, (2, 2 * S5_LANES, BRANCH_W))],
        out_specs=[fwd, bwd],
        out_shape=[jax.ShapeDtypeStruct((LCAT * BATCH, BRANCH_W), BF16)] * 2,
        scratch_shapes=[pltpu.VMEM((S5_ROWS, 2 * S5_LANES), F32),
                        pltpu.VMEM((S5_ROWS, 2 * S5_LANES), F32),
                        pltpu.VMEM((4, BATCH, S5_LANES), F32)],
        compiler_params=_cparams(1),
        name="s5_scan",
    )(u_t, u_t, lam, bblk, cblk)


def _log_sigmoid(x):
    return jnp.minimum(x, 0.0) - jnp.log(1.0 + jnp.exp(-jnp.abs(x)))


def _ret_kernel(q_ref, k_ref, v_ref, lgl_ref, lgh_ref, o_ref, dec_scr, sb_scr, sf_scr, sbc_scr):
    c_len = RET_CHUNK
    lgl = _log_sigmoid(lgl_ref[0])
    lgf, lgb = lgl[0], lgl[1]
    ti = lax.broadcasted_iota(jnp.int32, (c_len, BRANCH_W), 0).astype(F32)
    qdf = jnp.exp(lgf * (ti + 1.0))
    kdf = jnp.exp(lgf * (c_len - 1.0 - ti))
    qdb = jnp.exp(lgb * (c_len - ti))
    kdb = jnp.exp(lgb * ti)
    cdf = jnp.exp(lgf * float(c_len))
    cdb = jnp.exp(lgb * float(c_len))
    lane = lax.broadcasted_iota(jnp.int32, (1, BRANCH_W), 1)
    same_head = (lax.broadcasted_iota(jnp.int32, (BRANCH_W, BRANCH_W), 0) // RET_HD
                 == lax.broadcasted_iota(jnp.int32, (BRANCH_W, BRANCH_W), 1) // RET_HD)

    diff = (lax.broadcasted_iota(jnp.int32, (c_len, c_len), 0)
            - lax.broadcasted_iota(jnp.int32, (c_len, c_len), 1)).astype(F32)
    for h in range(RET_HEADS):
        gf = jnp.concatenate([_log_sigmoid(lgh_ref[0, 0, h])] * (c_len // 128), axis=-1)
        gb = jnp.concatenate([_log_sigmoid(lgh_ref[0, 1, h])] * (c_len // 128), axis=-1)
        dec_scr[h] = (jnp.where(diff >= 0, jnp.exp(gf * jnp.maximum(diff, 0.0)), 0.0)
                      + jnp.where(diff <= 0, jnp.exp(gb * jnp.maximum(-diff, 0.0)), 0.0))

    def chunk(ref, s, c):
        return ref[s, pl.ds(pl.multiple_of(c * c_len, c_len), c_len), :]

    def kv_outer(kd, v):
        s = lax.dot_general(kd.astype(BF16), v, (((0,), (0,)), ((), ())), preferred_element_type=F32)
        return jnp.where(same_head, s, 0.0)

    sbc_scr[...] = jnp.zeros_like(sbc_scr)

    def bwd(i, _):
        c = jnp.where(i < RET_CTX_CHUNKS, RET_CTX_CHUNKS - 1 - i, RET_NCHUNK + RET_CTX_CHUNKS - 1 - i)
        for s in range(RET_SPB):
            sb_scr[s, c] = sbc_scr[s].astype(BF16)
            k = chunk(k_ref, s, c).astype(F32)
            sbc_scr[s] = sbc_scr[s] * cdb + kv_outer(k * kdb, chunk(v_ref, s, c))
        return 0

    lax.fori_loop(0, RET_NCHUNK, bwd, 0)

    sf_scr[...] = jnp.zeros_like(sf_scr)

    def fwd(c, _):
        for s in range(RET_SPB):
            qb, kb, vb = chunk(q_ref, s, c), chunk(k_ref, s, c), chunk(v_ref, s, c)
            q = qb.astype(F32)
            o = (_dot((q * qdf).astype(BF16), sf_scr[s].astype(BF16))
                 + _dot((q * qdb).astype(BF16), sb_scr[s, c]))
            for h in range(RET_HEADS):
                hm = lane // RET_HD == h
                att = _dot_nt(jnp.where(hm, qb, jnp.zeros_like(qb)), kb)
                oh = _dot((att * dec_scr[h]).astype(BF16), vb)
                o = o + jnp.where(hm, oh, 0.0)
            o_ref[s, pl.ds(pl.multiple_of(c * c_len, c_len), c_len), :] = o.astype(BF16)
            sf_scr[s] = sf_scr[s] * cdf + kv_outer(kb.astype(F32) * kdf, vb)
        return 0

    lax.fori_loop(0, RET_NCHUNK, fwd, 0)


def _retention(l, rq, rk, rv, sw):
    tok = pl.BlockSpec((RET_SPB, LCAT, BRANCH_W), lambda b: (b, 0, 0))
    return pl.pallas_call(
        _ret_kernel,
        grid=(BATCH // RET_SPB,),
        in_specs=[tok, tok, tok, _layer(l, (2, 1, BRANCH_W)), _layer(l, (2, RET_HEADS, 1, 128))],
        out_specs=tok,
        out_shape=jax.ShapeDtypeStruct((BATCH, LCAT, BRANCH_W), BF16),
        scratch_shapes=[pltpu.VMEM((RET_HEADS, RET_CHUNK, RET_CHUNK), F32),
                        pltpu.VMEM((RET_SPB, RET_NCHUNK, BRANCH_W, BRANCH_W), BF16),
                        pltpu.VMEM((RET_SPB, BRANCH_W, BRANCH_W), F32),
                        pltpu.VMEM((RET_SPB, BRANCH_W, BRANCH_W), F32)],
        compiler_params=_cparams(1),
        name="retention",
    )(rq, rk, rv, sw["lgl"], sw["lgh"])


def _merge_kernel(xc_ref, xl_ref, mod_ref, nw_ref, wg_ref, oac_ref, oal_ref, ob_ref, yf_ref, yb_ref, ut_ref,
                  d_ref, wglu_ref, oret_ref, rg_ref, gnw_ref, pavg_ref, pbt_ref, wb_ref, wout_ref,
                  o_ref, *, skip_ctx):
    if skip_ctx:
        is_ctx = None
        x3 = xl_ref[...]
        oa = oal_ref[...]
    else:
        is_ctx = pl.program_id(0) < CTX_TILES
        x3 = jnp.where(is_ctx, xc_ref[...], xl_ref[...])
        oa = jnp.where(is_ctx, oac_ref[...], oal_ref[...])
    sh, sc, gate_res = _mod_chunks(mod_ref, is_ctx, (0, 1, 2))
    h = (_rms(x3, nw_ref[0]) * (1.0 + sc) + sh).reshape(TR, D_MODEL).astype(BF16)
    yt = yf_ref[...].astype(F32) + yb_ref[...].astype(F32) + d_ref[0] * ut_ref[...].astype(F32)
    y = _dot(pbt_ref[...], yt.astype(BF16))
    vg = _dot(_gelu_tanh(y).astype(BF16), wglu_ref[0])
    oc = vg[:, 0:BRANCH_W] * _sigmoid(vg[:, BRANCH_W:2 * BRANCH_W])
    o = oret_ref[...].reshape(TR, BRANCH_W)
    dl = o.astype(F32) - _dot(o, pavg_ref[...])
    var = _dot((dl * dl).astype(BF16), pavg_ref[...])
    g = rg_ref[...].reshape(TR, BRANCH_W).astype(F32)
    od = g * _sigmoid(g) * (dl * lax.rsqrt(var + NORM_EPS) * gnw_ref[0])
    branches = (oa.reshape(TR, BRANCH_W), ob_ref[...].reshape(TR, BRANCH_W), oc.astype(BF16), od.astype(BF16))
    acc = None
    for n, branch in enumerate(branches):
        gate = _dot_nt(h, wg_ref[0, _O_GATE + n * D_MODEL:_O_GATE + (n + 1) * D_MODEL, :])
        term = _sigmoid(gate) * _dot(branch, wb_ref[0, n])
        acc = term if acc is None else acc + term
    m = _dot(acc.astype(BF16), wout_ref[0])
    o_ref[...] = x3 + gate_res * m.reshape(BATCH, TT, D_MODEL)


def _merge(l, x_ctx, x_lat, mod, sw, oa_ctx, oa_lat, ob, yf, yb, u_t, oret, rg, skip_ctx):
    off = CTX_TILES if skip_ctx else 0
    nt = ALL_TILES - off
    ctx_spec, lat_spec = _tile_specs(l == 0, skip_ctx)
    oac_spec, oal_spec = _tile_specs(True, skip_ctx, BRANCH_W)
    tok = lambda w: pl.BlockSpec((BATCH, TT, w), lambda j: (0, j + off, 0))
    tmaj = pl.BlockSpec((TR, BRANCH_W), lambda j: (j + off, 0))
    return pl.pallas_call(
        functools.partial(_merge_kernel, skip_ctx=skip_ctx),
        grid=(nt,),
        in_specs=[ctx_spec, lat_spec,
                  _layer(l, (16, ADA_CHUNKS * D_MODEL)),
                  _layer(l, (1, D_MODEL)),
                  _layer(l, (IN_COLS, D_MODEL), single=True),
                  oac_spec, oal_spec,
                  tok(BRANCH_W), tmaj, tmaj, tmaj,
                  _layer(l, (1, BRANCH_W)),
                  _layer(l, (BRANCH_W, 2 * BRANCH_W)),
                  tok(BRANCH_W), tok(BRANCH_W),
                  _layer(l, (1, BRANCH_W)),
                  _full((BRANCH_W, BRANCH_W)),
                  _full((TR, TR)),
                  _layer(l, (N_BRANCH, BRANCH_W, D_MODEL), single=True),
                  _layer(l, (D_MODEL, D_MODEL), single=True)],
        out_specs=pl.BlockSpec((BATCH, TT, D_MODEL), lambda j: (0, j, 0)),
        out_shape=jax.ShapeDtypeStruct((BATCH, nt * TT, D_MODEL), F32),
        compiler_params=_cparams(1),
        name="merge",
    )(x_ctx, x_lat, mod, sw["norm_mix"], sw["w_in_t"], oa_ctx, oa_lat, ob, yf, yb, u_t, sw["s5_d"], sw["w_glu"],
      oret, rg, sw["gn_w"], jnp.asarray(_head_avg(), dtype=BF16), jnp.asarray(_tile_perm().T, dtype=BF16),
      sw["w_branch"], sw["w_out"])


def _ffn_kernel(x_ref, mod_ref, nw_ref, w1_ref, w2_ref, o_ref, *, skip_ctx):
    is_ctx = None if skip_ctx else pl.program_id(0) < CTX_TILES
    sh, sc, gate_res = _mod_chunks(mod_ref, is_ctx, (3, 4, 5))
    x3 = x_ref[...]
    h = (_rms(x3, nw_ref[0]) * (1.0 + sc) + sh).reshape(TR, D_MODEL).astype(BF16)
    a = jnp.maximum(_dot(h, w1_ref[0]), 0.0)
    f = _dot((a * a).astype(BF16), w2_ref[0])
    o_ref[...] = x3 + gate_res * f.reshape(BATCH, TT, D_MODEL)


def _ffn(l, xm, mod, sw, skip_ctx):
    nt = xm.shape[1] // TT
    tok = pl.BlockSpec((BATCH, TT, D_MODEL), lambda j: (0, j, 0))
    return pl.pallas_call(
        functools.partial(_ffn_kernel, skip_ctx=skip_ctx),
        grid=(nt,),
        in_specs=[tok,
                  _layer(l, (16, ADA_CHUNKS * D_MODEL)),
                  _layer(l, (1, D_MODEL)),
                  _layer(l, (D_MODEL, D_FF), single=True),
                  _layer(l, (D_FF, D_MODEL), single=True)],
        out_specs=tok,
        out_shape=jax.ShapeDtypeStruct(xm.shape, F32),
        compiler_params=_cparams(1),
        name="ffn",
    )(xm, mod, sw["norm_ffn"], sw["ffn_w1"], sw["ffn_w2"])


def _stacked_weights(p):
    w_in_t = jnp.swapaxes(p["w_in"], 1, 2).astype(BF16)
    perm, _ = _mla_rope_perm()
    zeros = lambda *s: jnp.zeros((DEPTH,) + s, F32)
    wu = p["mla_w_ukv"].reshape(DEPTH, KV_LORA, MLA_HEADS, MLA_NOPE + MLA_V)
    pad_heads = lambda t: jnp.concatenate(
        [t, zeros(t.shape[1], MLA_HEADS, HEAD_PAD - t.shape[3])], -1).reshape(DEPTH, t.shape[1], HEADS_W)
    top = jnp.concatenate([pad_heads(wu[..., :MLA_NOPE]), zeros(KV_LORA, HEADS_W),
                           pad_heads(wu[..., MLA_NOPE:])], axis=2)
    place = jnp.broadcast_to(jnp.asarray(_kv_place())[None], (DEPTH, 128, 3 * HEADS_W))
    wkv = jnp.concatenate([top, place], axis=1).astype(BF16)

    wuq = p["mla_w_uq"].reshape(DEPTH, Q_LORA, MLA_HEADS, MLA_QK)
    qp = jnp.concatenate([zeros(Q_LORA, MLA_HEADS, MLA_NOPE), wuq[..., MLA_NOPE:][..., perm],
                          zeros(Q_LORA, MLA_HEADS, 32)], -1).reshape(DEPTH, Q_LORA, HEADS_W)
    wq = jnp.concatenate([pad_heads(wuq), qp], axis=2).astype(BF16)

    def head_w(v):
        wf = jnp.concatenate([v, zeros(32)], -1)[:, None, :]
        wp = jnp.concatenate([zeros(MLA_NOPE), v[:, MLA_NOPE:][:, perm], zeros(32)], -1)[:, None, :]
        return wf, wp

    head_ws = jnp.concatenate(head_w(p["mla_qk_norm_k"]) + head_w(p["mla_qk_norm_q"]), axis=1)

    def b_compact(b):
        t = b.transpose(0, 1, 2, 4, 3).reshape(DEPTH, 2, BRANCH_W, S5_STATE)
        return jnp.concatenate([t, t], axis=-1)

    c_compact = lambda c: c.transpose(0, 1, 4, 2, 3).reshape(DEPTH, 2, S5_STATE, BRANCH_W)
    vec = lambda a: a.reshape(DEPTH, 2, 1, S5_LANES)

    logit = p["ret_decay_logit"]
    row = lambda a: a[:, None, :]
    return dict(
        w_in_t=w_in_t,
        norm_mix=row(p["norm_mix_w"]), norm_ffn=row(p["norm_ffn_w"]),
        kv_norm=row(p["mla_kv_norm"]), q_norm=row(p["mla_q_norm"]),
        wkv=wkv, wq=wq, head_w=head_ws,
        s5_lam_re=vec(p["s5_lam_re"]), s5_lam_im=vec(p["s5_lam_im"]),
        s5_log_step=vec(jnp.repeat(p["s5_log_step"], S5_STATE, axis=-1)),
        s5_bre=b_compact(p["s5_b_re"]), s5_bim=b_compact(p["s5_b_im"]),
        s5_cre=c_compact(p["s5_c_re"]), s5_cim=c_compact(p["s5_c_im"]),
        s5_d=row(p["s5_d"]), w_glu=p["s5_w_glu"].astype(BF16),
        lgl=jnp.repeat(logit, RET_HD, axis=-1).reshape(DEPTH, 2, 1, BRANCH_W),
        lgh=jnp.broadcast_to(logit[:, :, :, None, None], (DEPTH, 2, RET_HEADS, 1, 128)),
        gn_w=row(p["ret_gn_w"]),
        w_branch=p["w_branch"].astype(BF16), w_out=p["w_out"].astype(BF16),
        ffn_w1=p["ffn_w1"].astype(BF16), ffn_w2=p["ffn_w2"].astype(BF16))


def kernel(x, c, ctx, c_ctx, ada_w, ada_b, norm_mix_w, norm_ffn_w, w_in, mla_q_norm, mla_w_uq, mla_kv_norm,
           mla_w_ukv, mla_qk_norm_q, mla_qk_norm_k, s5_lam_re, s5_lam_im, s5_log_step, s5_b_re, s5_b_im,
           s5_c_re, s5_c_im, s5_d, s5_w_glu, ret_decay_logit, ret_gn_w, w_branch, w_out, ffn_w1, ffn_w2):
    p = dict(norm_mix_w=norm_mix_w, norm_ffn_w=norm_ffn_w, w_in=w_in, mla_q_norm=mla_q_norm,
             mla_w_uq=mla_w_uq, mla_kv_norm=mla_kv_norm, mla_w_ukv=mla_w_ukv, mla_qk_norm_q=mla_qk_norm_q,
             mla_qk_norm_k=mla_qk_norm_k, s5_lam_re=s5_lam_re, s5_lam_im=s5_lam_im, s5_log_step=s5_log_step,
             s5_b_re=s5_b_re, s5_b_im=s5_b_im, s5_c_re=s5_c_re, s5_c_im=s5_c_im, s5_d=s5_d,
             s5_w_glu=s5_w_glu, ret_decay_logit=ret_decay_logit, ret_gn_w=ret_gn_w, w_branch=w_branch,
             w_out=w_out, ffn_w1=ffn_w1, ffn_w2=ffn_w2)
    sw = _stacked_weights(p)
    c16 = jnp.concatenate([c, c_ctx[None, :], jnp.zeros((16 - BATCH - 1, D_MODEL), F32)], axis=0)
    mod = _ada_mod(c16, ada_w, ada_b)
    lam, bblk, cblk = _s5_params(sw)
    x_ctx, x_lat = ctx, x
    for l in range(DEPTH):
        last = l == DEPTH - 1
        q, k, v, u_t, uc, us, rq, rk, rv, rg = _in_proj(l, x_ctx, x_lat, mod, sw)
        oa_lat = _attention_latent(q, k, v)
        oa_ctx = oa_lat if last else _attention_context(q, k, v)
        ob = _fnet(uc, us)
        yf, yb = _s5(l, u_t, lam, bblk, cblk)
        oret = _retention(l, rq, rk, rv, sw)
        xm = _merge(l, x_ctx, x_lat, mod, sw, oa_ctx, oa_lat, ob, yf, yb, u_t, oret, rg, last)
        x_ctx = x_lat = _ffn(l, xm, mod, sw, last)
    return x_lat
```

```python
import functools
import math

import numpy as np
import jax
import jax.numpy as jnp
from jax import lax
from jax.experimental import pallas as pl
from jax.experimental.pallas import tpu as pltpu

F32 = jnp.float32
BF16 = jnp.bfloat16

D_MODEL = 1024
BATCH = 8
SEQ = 2048
DEPTH = 2
GRID_W = 64
CTX_LEN = 256
LCAT = CTX_LEN + SEQ
N_BRANCH = 4
BRANCH_W = 256
NORM_EPS = 1e-6
ROPE_BASE = 10000.0
ADA_CHUNKS = 6

MLA_HEADS = 4
MLA_NOPE = 64
MLA_ROPE = 32
MLA_QK = MLA_NOPE + MLA_ROPE
MLA_V = 64
Q_LORA = 256
KV_LORA = 128
HEAD_PAD = 128
HEADS_W = MLA_HEADS * HEAD_PAD

FNET_GROUPS = 4
FNET_GW = BRANCH_W // FNET_GROUPS

S5_GROUP_CH = 16
S5_GROUPS = BRANCH_W // S5_GROUP_CH
S5_STATE = 64
S5_LANES = S5_GROUPS * S5_STATE

RET_HEADS = 4
RET_HD = BRANCH_W // RET_HEADS
RET_CHUNK = 256
RET_NCHUNK = LCAT // RET_CHUNK
RET_CTX_CHUNKS = CTX_LEN // RET_CHUNK
RET_SPB = 4

D_FF = 4 * D_MODEL

_O_KV, _O_KR, _O_S5, _O_RK, _O_RV = 0, 128, 160, 416, 672
_O_Q, _O_FN, _O_RQ, _O_RG, _O_GATE = 928, 1184, 1440, 1696, 1952
IN_COLS = _O_GATE + N_BRANCH * D_MODEL

TT = 64
TR = TT * BATCH
FFN_TT = 128
FFN_PIECE = 2048
CTX_TILES = CTX_LEN // TT
ALL_TILES = LCAT // TT
TQ = 256
ATTN_QB = 4
ATTN_AHEAD = 1
S5_TC = 128
S5_ROWS = S5_TC * BATCH
S5_STEPS = LCAT // S5_TC
S5_CTX_STEPS = CTX_LEN // S5_TC
S5_SUB = 1

VMEM_LIMIT = 56 * 1024 * 1024


def _cparams(n_grid):
    return pltpu.CompilerParams(dimension_semantics=("arbitrary",) * n_grid,
                                vmem_limit_bytes=VMEM_LIMIT)


def _dot(a, b):
    return jnp.dot(a, b, preferred_element_type=F32)


def _dot_nt(a, b):
    return lax.dot_general(a, b, (((1,), (1,)), ((), ())), preferred_element_type=F32)


def _sigmoid(x):
    return 0.5 * (jnp.tanh(0.5 * x) + 1.0)


def _gelu_tanh(y):
    return 0.5 * y * (1.0 + jnp.tanh(math.sqrt(2.0 / math.pi) * (y + 0.044715 * (y * y * y))))


def _rms(x, w):
    return x * lax.rsqrt(jnp.mean(x * x, axis=-1, keepdims=True) + NORM_EPS) * w


def _full(shape):
    n = len(shape)
    return pl.BlockSpec(shape, lambda *_: (0,) * n)


def _layer(l, shape, single=False):
    n = len(shape)
    mode = dict(pipeline_mode=pl.Buffered(1)) if single else {}
    return pl.BlockSpec((1,) + tuple(shape), lambda *_: (l,) + (0,) * n, **mode)


def _mla_rope_perm():
    r = np.arange(MLA_ROPE)
    first = (r % 16) < 8
    return np.where(first, r + 8, r - 8), np.where(first, -1.0, 1.0)


@functools.lru_cache(maxsize=None)
def _mla_tables():
    pos = np.arange(SEQ)
    rows, cols = pos // GRID_W, pos % GRID_W
    freqs = ROPE_BASE ** (-np.arange(8, dtype=np.float64) / 8)
    r = np.arange(MLA_ROPE)
    _, sign = _mla_rope_perm()
    p = np.where((r // 16 == 0)[None, :], rows[:, None], cols[:, None]).astype(np.float64)
    ang = p * freqs[(r % 16) % 8][None, :]
    cosf = np.zeros((LCAT, HEAD_PAD))
    sinf = np.zeros((LCAT, HEAD_PAD))
    cosf[:, :MLA_QK] = 1.0
    cosf[CTX_LEN:, MLA_NOPE:MLA_QK] = np.cos(ang)
    sinf[CTX_LEN:, MLA_NOPE:MLA_QK] = np.sin(ang) * sign[None, :]
    return cosf.astype(np.float32), sinf.astype(np.float32)


def _ret_perm():
    d = np.arange(RET_HD)
    first = d < RET_HD // 2
    return np.where(first, d + RET_HD // 2, d - RET_HD // 2), np.where(first, -1.0, 1.0)


@functools.lru_cache(maxsize=None)
def _ret_tables():
    half = RET_HD // 2
    pos = np.arange(SEQ, dtype=np.float64)
    freqs = ROPE_BASE ** (-np.arange(half, dtype=np.float64) / half)
    d = np.arange(RET_HD)
    perm, sign = _ret_perm()
    ang = pos[:, None] * freqs[d % half][None, :]
    cosr = np.ones((LCAT, RET_HD))
    sinr = np.zeros((LCAT, RET_HD))
    cosr[CTX_LEN:] = np.cos(ang)
    sinr[CTX_LEN:] = np.sin(ang) * sign[None, :]
    cosr = np.tile(cosr, (1, RET_HEADS))
    sinr = np.tile(sinr, (1, RET_HEADS))
    pm = np.zeros((BRANCH_W, BRANCH_W))
    for h in range(RET_HEADS):
        pm[h * RET_HD + perm, h * RET_HD + d] = 1.0
    return cosr.astype(np.float32), sinr.astype(np.float32), pm.astype(np.float32)


def _dft(n, scale):
    k = np.arange(n)
    kt = (k[:, None] * k[None, :]) % n
    ang = 2.0 * np.pi * kt / n
    return np.cos(ang) * scale, np.sin(ang) * scale


@functools.lru_cache(maxsize=None)
def _fnet_tables():
    cw, sw = _dft(FNET_GW, 1.0)
    t = np.zeros((BRANCH_W, 2 * BRANCH_W))
    for g in range(FNET_GROUPS):
        s = slice(g * FNET_GW, (g + 1) * FNET_GW)
        t[s, s] = cw
        t[s, BRANCH_W + g * FNET_GW:BRANCH_W + (g + 1) * FNET_GW] = -sw
    cl, sl = _dft(SEQ, 1.0 / math.sqrt(SEQ * FNET_GW))
    clc, slc = _dft(CTX_LEN, 1.0 / math.sqrt(CTX_LEN * FNET_GW))
    return tuple(a.astype(np.float32) for a in (t, cl, sl, clc, slc))


def _bf16_const(a):
    return jnp.asarray(a).astype(BF16)


@functools.lru_cache(maxsize=None)
def _head_avg():
    p = np.zeros((BRANCH_W, BRANCH_W))
    for h in range(RET_HEADS):
        p[h * RET_HD:(h + 1) * RET_HD, h * RET_HD:(h + 1) * RET_HD] = 1.0 / RET_HD
    return p.astype(np.float32)


@functools.lru_cache(maxsize=None)
def _tile_perm():
    p = np.zeros((TR, TR), np.float32)
    for b in range(BATCH):
        for t in range(TT):
            p[t * BATCH + b, b * TT + t] = 1.0
    return p


@functools.lru_cache(maxsize=None)
def _kv_place():
    perm, _ = _mla_rope_perm()
    place = np.zeros((128, 3 * HEADS_W), np.float32)
    for h in range(MLA_HEADS):
        for r in range(MLA_ROPE):
            place[r, h * HEAD_PAD + MLA_NOPE + r] = 1.0
            place[perm[r], HEADS_W + h * HEAD_PAD + MLA_NOPE + r] = 1.0
    return place


@functools.lru_cache(maxsize=None)
def _v_ones():
    v = np.zeros((1, HEADS_W), np.float32)
    v[0, np.arange(MLA_HEADS) * HEAD_PAD + MLA_V] = 1.0
    return v


ADA_TN = 1536


def _ada_kernel(c_ref, w_ref, b_ref, o_ref):
    c = c_ref[...]
    s = (c * _sigmoid(c)).astype(BF16)
    o_ref[0] = _dot(s, w_ref[0].astype(BF16)) + b_ref[0]


def _ada_mod(c16, ada_w, ada_b):
    n = ADA_CHUNKS * D_MODEL
    return pl.pallas_call(
        _ada_kernel,
        grid=(DEPTH, n // ADA_TN),
        in_specs=[pl.BlockSpec((16, D_MODEL), lambda l, j: (0, 0)),
                  pl.BlockSpec((1, D_MODEL, ADA_TN), lambda l, j: (l, 0, j)),
                  pl.BlockSpec((1, 1, ADA_TN), lambda l, j: (l, 0, j))],
        out_specs=pl.BlockSpec((1, 16, ADA_TN), lambda l, j: (l, 0, j)),
        out_shape=jax.ShapeDtypeStruct((DEPTH, 16, n), F32),
        compiler_params=_cparams(2),
        name="ada_mod",
    )(c16, ada_w, ada_b.reshape(DEPTH, 1, n))


def _mod_chunks(mod_ref, is_ctx, idxs):
    out = []
    for i in idxs:
        sl = slice(i * D_MODEL, (i + 1) * D_MODEL)
        m = mod_ref[0, 0:BATCH, sl]
        if is_ctx is not None:
            m = jnp.where(is_ctx, mod_ref[0, BATCH:BATCH + 1, sl], m)
        out.append(m[:, None, :])
    return out


def _tile_specs(separate, skip_ctx, width=D_MODEL):
    base = 0 if separate else CTX_TILES
    shape = (BATCH, TT, width)
    if skip_ctx:
        return (pl.BlockSpec(shape, lambda j: (0, 0, 0)),
                pl.BlockSpec(shape, lambda j: (0, j + base, 0)))
    return (pl.BlockSpec(shape, lambda j: (0, jnp.minimum(j, CTX_TILES - 1), 0)),
            pl.BlockSpec(shape, lambda j: (0, jnp.maximum(j - CTX_TILES, 0) + base, 0)))


def _head_norm_rot(xf, xp, a, b, scale):
    outs = []
    for h in range(MLA_HEADS):
        f = xf[:, h * HEAD_PAD:(h + 1) * HEAD_PAD]
        p = xp[:, h * HEAD_PAD:(h + 1) * HEAD_PAD]
        n = lax.rsqrt(jnp.sum(f * f, axis=-1, keepdims=True) * (1.0 / MLA_QK) + NORM_EPS) * scale
        outs.append(n * (f * a + p * b))
    return jnp.concatenate(outs, axis=-1)


def _rows(table):
    w = table.shape[-1]
    return jnp.broadcast_to(table[None], (BATCH, TT, w)).reshape(TR, w)


def _tile3(x):
    return x.reshape(BATCH, TT, x.shape[-1])


def _inproj_kernel(xc_ref, xl_ref, mod_ref, nw_ref, w_ref, kvw_ref, wkv_ref, qnw_ref, wq_ref,
                   cosf_ref, sinf_ref, hw_ref, pm_ref, cosr_ref, sinr_ref, t_ref, ptb_ref, vone_ref,
                   q_out, k_out, v_out, ut_out, uc_out, us_out, rq_out, rk_out, rv_out, rg_out):
    is_ctx = pl.program_id(0) < CTX_TILES
    x3 = jnp.where(is_ctx, xc_ref[...], xl_ref[...])
    sh, sc = _mod_chunks(mod_ref, is_ctx, (0, 1))
    h = (_rms(x3, nw_ref[0]) * (1.0 + sc) + sh).reshape(TR, D_MODEL).astype(BF16)
    za = _dot_nt(h, w_ref[0, 0:256, :])
    z = _dot_nt(h, w_ref[0, _O_S5:_O_GATE, :])

    cosf, sinf = _rows(cosf_ref[...]), _rows(sinf_ref[...])
    hw = hw_ref[0]
    kvn = _rms(za[:, 0:128], kvw_ref[0]).astype(BF16)
    lhs = jnp.concatenate([kvn, za[:, 128:256].astype(BF16)], axis=-1)
    kv = _dot(lhs, wkv_ref[0])
    k = _head_norm_rot(kv[:, 0:HEADS_W], kv[:, HEADS_W:2 * HEADS_W],
                       cosf * hw[0:1], sinf * hw[1:2], 1.0)
    k_out[...] = _tile3(k).astype(BF16)
    v_out[...] = _tile3(kv[:, 2 * HEADS_W:3 * HEADS_W] + vone_ref[...]).astype(BF16)
    qn = _rms(z[:, 768:1024], qnw_ref[0]).astype(BF16)
    qq = _dot(qn, wq_ref[0])
    q = _head_norm_rot(qq[:, 0:HEADS_W], qq[:, HEADS_W:2 * HEADS_W],
                       cosf * hw[2:3], sinf * hw[3:4], MLA_QK ** -0.5)
    q_out[...] = _tile3(q).astype(BF16)
    cosr, sinr = _rows(cosr_ref[...]), _rows(sinr_ref[...])
    rq = z[:, 1280:1536]
    rk = z[:, 256:512]
    rq = rq * cosr + _dot(rq.astype(BF16), pm_ref[...]) * sinr
    rk = rk * cosr + _dot(rk.astype(BF16), pm_ref[...]) * sinr
    rq_out[...] = _tile3(rq).astype(BF16)
    rk_out[...] = _tile3(rk * (RET_HD ** -0.5)).astype(BF16)
    rv_out[...] = _tile3(z[:, 512:768]).astype(BF16)
    rg_out[...] = _tile3(z[:, 1536:1792]).astype(BF16)
    ucs = _dot(z[:, 1024:1280].astype(BF16), t_ref[...])
    uc_out[...] = _tile3(ucs[:, 0:BRANCH_W]).astype(BF16)
    us_out[...] = _tile3(ucs[:, BRANCH_W:2 * BRANCH_W]).astype(BF16)
    ut_out[...] = _dot(ptb_ref[...], z[:, 0:256].astype(BF16)).astype(BF16)


def _in_proj(l, x_ctx, x_lat, mod, sw):
    ctx_spec, lat_spec = _tile_specs(l == 0, False)
    tok = lambda w: pl.BlockSpec((BATCH, TT, w), lambda j: (0, j, 0))
    tab = lambda w: pl.BlockSpec((TT, w), lambda j: (j, 0))
    cosf, sinf = _mla_tables()
    cosr, sinr, pm = _ret_tables()
    bshape = lambda w: jax.ShapeDtypeStruct((BATCH, LCAT, w), BF16)
    return pl.pallas_call(
        _inproj_kernel,
        grid=(ALL_TILES,),
        in_specs=[ctx_spec, lat_spec,
                  _layer(l, (16, ADA_CHUNKS * D_MODEL)),
                  _layer(l, (1, D_MODEL)),
                  _layer(l, (_O_GATE, D_MODEL)),
                  _layer(l, (1, KV_LORA)),
                  _layer(l, (256, 3 * HEADS_W)),
                  _layer(l, (1, Q_LORA)),
                  _layer(l, (Q_LORA, 2 * HEADS_W)),
                  tab(HEAD_PAD), tab(HEAD_PAD),
                  _layer(l, (4, HEAD_PAD)),
                  _full((BRANCH_W, BRANCH_W)),
                  tab(BRANCH_W), tab(BRANCH_W),
                  _full((BRANCH_W, 2 * BRANCH_W)),
                  _full((TR, TR)),
                  _full((1, HEADS_W))],
        out_specs=[tok(HEADS_W), tok(HEADS_W), tok(HEADS_W),
                   pl.BlockSpec((TR, BRANCH_W), lambda j: (j, 0)),
                   tok(256), tok(256), tok(256), tok(256), tok(256), tok(256)],
        out_shape=[bshape(HEADS_W), bshape(HEADS_W), bshape(HEADS_W),
                   jax.ShapeDtypeStruct((LCAT * BATCH, BRANCH_W), BF16),
                   bshape(256), bshape(256), bshape(256), bshape(256), bshape(256), bshape(256)],
        compiler_params=_cparams(1),
        name="in_proj",
    )(x_ctx, x_lat, mod, sw["norm_mix"], sw["w_in_t"], sw["kv_norm"], sw["wkv"], sw["q_norm"], sw["wq"],
      jnp.asarray(cosf), jnp.asarray(sinf), sw["head_w"], jnp.asarray(pm, dtype=BF16),
      jnp.asarray(cosr), jnp.asarray(sinr), _bf16_const(_fnet_tables()[0]),
      jnp.asarray(_tile_perm(), dtype=BF16), jnp.asarray(_v_ones()))


def _attn_kernel(*refs):
    q_refs, (k_ref, v_ref, o_ref) = refs[:-3], refs[-3:]
    units = [(r, h, slice(h * HEAD_PAD, (h + 1) * HEAD_PAD))
             for r in range(len(q_refs)) for h in range(MLA_HEADS)]
    scores = lambda u: _dot_nt(q_refs[u[0]][0, :, u[2]], k_ref[0, :, u[2]])
    lane = lax.broadcasted_iota(jnp.int32, (1, HEAD_PAD), 1)
    pending = [scores(u) for u in units[:ATTN_AHEAD]]
    for i, (r, h, sl) in enumerate(units):
        if i + ATTN_AHEAD < len(units):
            pending.append(scores(units[i + ATTN_AHEAD]))
        s = pending.pop(0)
        p = jnp.exp((s - jnp.max(s, axis=-1, keepdims=True)).astype(BF16))
        oh = _dot(p, v_ref[0, :, sl])
        oh = oh * (1.0 / oh[:, MLA_V:MLA_V + 1])
        if h % 2 == 0:
            even = oh
        else:
            pair = jnp.where(lane < MLA_V, even, pltpu.roll(oh, MLA_V, axis=1))
            o_ref[0, r * TQ:(r + 1) * TQ, (h // 2) * HEAD_PAD:(h // 2 + 1) * HEAD_PAD] = pair.astype(BF16)


def _attention_latent(q, k, v):
    first = CTX_LEN // TQ
    qspec = lambda r: pl.BlockSpec((1, TQ, HEADS_W), lambda b, j: (b, first + ATTN_QB * j + r, 0))
    kv = pl.BlockSpec((1, LCAT, HEADS_W), lambda b, j: (b, 0, 0))
    return pl.pallas_call(
        _attn_kernel,
        grid=(BATCH, SEQ // (ATTN_QB * TQ)),
        in_specs=[qspec(r) for r in range(ATTN_QB)] + [kv, kv],
        out_specs=pl.BlockSpec((1, ATTN_QB * TQ, BRANCH_W), lambda b, j: (b, j, 0)),
        out_shape=jax.ShapeDtypeStruct((BATCH, SEQ, BRANCH_W), BF16),
        compiler_params=_cparams(2),
        name="mla_attention",
    )(*([q] * ATTN_QB), k, v)


def _attention_context(q, k, v):
    blk = pl.BlockSpec((1, CTX_LEN, HEADS_W), lambda b: (b, 0, 0))
    return pl.pallas_call(
        _attn_kernel,
        grid=(BATCH,),
        in_specs=[blk, blk, blk],
        out_specs=pl.BlockSpec((1, CTX_LEN, BRANCH_W), lambda b: (b, 0, 0)),
        out_shape=jax.ShapeDtypeStruct((BATCH, CTX_LEN, BRANCH_W), BF16),
        compiler_params=_cparams(1),
        name="mla_attention_ctx",
    )(q, k, v)


def _fnet_kernel(uc_ref, us_ref, cl_ref, sl_ref, clc_ref, slc_ref, o_ref):
    o_ref[0, 0:CTX_LEN, :] = (_dot(clc_ref[...], uc_ref[0, 0:CTX_LEN, :])
                              + _dot(slc_ref[...], us_ref[0, 0:CTX_LEN, :])).astype(BF16)
    o_ref[0, CTX_LEN:LCAT, :] = (_dot(cl_ref[...], uc_ref[0, CTX_LEN:LCAT, :])
                                 + _dot(sl_ref[...], us_ref[0, CTX_LEN:LCAT, :])).astype(BF16)


def _fnet(uc, us):
    cl, sl, clc, slc = (_bf16_const(a) for a in _fnet_tables()[1:])
    tok = pl.BlockSpec((1, LCAT, BRANCH_W), lambda b: (b, 0, 0))
    return pl.pallas_call(
        _fnet_kernel,
        grid=(BATCH,),
        in_specs=[tok, tok, _full((SEQ, SEQ)), _full((SEQ, SEQ)),
                  _full((CTX_LEN, CTX_LEN)), _full((CTX_LEN, CTX_LEN))],
        out_specs=tok,
        out_shape=jax.ShapeDtypeStruct((BATCH, LCAT, BRANCH_W), BF16),
        compiler_params=_cparams(1),
        name="fnet_dft",
    )(uc, us, cl, sl, clc, slc)


def _s5_param_kernel(lr_ref, li_ref, ls_ref, bre_ref, bim_ref, cre_ref, cim_ref, lam_out, b_out, c_out):
    lr, li = lr_ref[0, 0], li_ref[0, 0]
    step = jnp.exp(ls_ref[0, 0])
    mag = jnp.exp(lr * step)
    lbr = mag * jnp.cos(li * step)
    lbi = mag * jnp.sin(li * step)
    den = 1.0 / (lr * lr + li * li)
    cr = ((lbr - 1.0) * lr + lbi * li) * den
    ci = (lbi * lr - (lbr - 1.0) * li) * den
    lam_out[0, 0, 0] = jnp.broadcast_to(lbr, (BATCH, S5_LANES))
    lam_out[0, 0, 1] = jnp.broadcast_to(lbi, (BATCH, S5_LANES))
    grp_b = (lax.broadcasted_iota(jnp.int32, (BRANCH_W, S5_LANES), 0) // S5_GROUP_CH
             == lax.broadcasted_iota(jnp.int32, (BRANCH_W, S5_LANES), 1) // S5_STATE)
    wide = lambda r: jnp.concatenate([r[0, 0]] * (S5_LANES // 128), axis=-1)
    bre = jnp.where(grp_b, wide(bre_ref), 0.0)
    bim = jnp.where(grp_b, wide(bim_ref), 0.0)
    b_out[0, 0, :, 0:S5_LANES] = (cr * bre - ci * bim).astype(BF16)
    b_out[0, 0, :, S5_LANES:2 * S5_LANES] = (cr * bim + ci * bre).astype(BF16)
    grp_c = (lax.broadcasted_iota(jnp.int32, (S5_LANES, BRANCH_W), 0) // S5_STATE
             == lax.broadcasted_iota(jnp.int32, (S5_LANES, BRANCH_W), 1) // S5_GROUP_CH)
    tall = lambda r: jnp.concatenate([r[0, 0]] * S5_GROUPS, axis=0)
    c_out[0, 0, 0:S5_LANES, :] = jnp.where(grp_c, tall(cre_ref), 0.0).astype(BF16)
    c_out[0, 0, S5_LANES:2 * S5_LANES, :] = jnp.where(grp_c, -tall(cim_ref), 0.0).astype(BF16)


def _s5_params(sw):
    spec = lambda *s: pl.BlockSpec((1, 1) + s, lambda l, d: (l, d) + (0,) * len(s))
    return pl.pallas_call(
        _s5_param_kernel,
        grid=(DEPTH, 2),
        in_specs=[spec(1, S5_LANES), spec(1, S5_LANES), spec(1, S5_LANES),
                  spec(BRANCH_W, 128), spec(BRANCH_W, 128), spec(S5_STATE, BRANCH_W), spec(S5_STATE, BRANCH_W)],
        out_specs=[spec(2, BATCH, S5_LANES), spec(BRANCH_W, 2 * S5_LANES), spec(2 * S5_LANES, BRANCH_W)],
        out_shape=[jax.ShapeDtypeStruct((DEPTH, 2, 2, BATCH, S5_LANES), F32),
                   jax.ShapeDtypeStruct((DEPTH, 2, BRANCH_W, 2 * S5_LANES), BF16),
                   jax.ShapeDtypeStruct((DEPTH, 2, 2 * S5_LANES, BRANCH_W), BF16)],
        compiler_params=_cparams(2),
        name="s5_discretise",
    )(sw["s5_lam_re"], sw["s5_lam_im"], sw["s5_log_step"], sw["s5_bre"], sw["s5_bim"], sw["s5_cre"], sw["s5_cim"])


def _s5_bwd_block(i):
    return jnp.where(i < S5_CTX_STEPS, S5_CTX_STEPS - 1 - i, S5_STEPS + S5_CTX_STEPS - 1 - i)


def _s5_kernel(uf_ref, ub_ref, lam_ref, b_ref, c_ref, yf_ref, yb_ref, xf_scr, xb_scr, st_scr):
    @pl.when(pl.program_id(0) == 0)
    def _():
        st_scr[...] = jnp.zeros_like(st_scr)

    sub_t = S5_TC // S5_SUB
    sub_rows = lambda k: pl.ds(k * sub_t * BATCH, sub_t * BATCH)
    order = (range(S5_SUB), range(S5_SUB - 1, -1, -1))
    dirs = ((uf_ref, xf_scr, yf_ref), (ub_ref, xb_scr, yb_ref))
    for d, (u_ref, x_scr, _) in enumerate(dirs):
        for k in order[d]:
            x_scr[sub_rows(k), :] = _dot(u_ref[sub_rows(k), :], b_ref[0, d])
    re, im = pl.ds(0, S5_LANES), pl.ds(S5_LANES, S5_LANES)
    for d, (_, x_scr, y_ref) in enumerate(dirs):
        xr, xi = st_scr[2 * d], st_scr[2 * d + 1]
        ar, ai = lam_ref[0, d, 0], lam_ref[0, d, 1]
        for k in order[d]:
            steps = range(k * sub_t, (k + 1) * sub_t)
            for t in (steps if d == 0 else reversed(steps)):
                rows = pl.ds(t * BATCH, BATCH)
                xr, xi = (ar * xr - ai * xi + x_scr[rows, re], ar * xi + ai * xr + x_scr[rows, im])
                x_scr[rows, re] = xr
                x_scr[rows, im] = xi
            y_ref[sub_rows(k), :] = _dot(x_scr[sub_rows(k), :].astype(BF16), c_ref[0, d]).astype(BF16)
        st_scr[2 * d], st_scr[2 * d + 1] = xr, xi


def _s5(l, u_t, lam, bblk, cblk):
    fwd = pl.BlockSpec((S5_ROWS, BRANCH_W), lambda i: (i, 0))
    bwd = pl.BlockSpec((S5_ROWS, BRANCH_W), lambda i: (_s5_bwd_block(i), 0))
    return pl.pallas_call(
        _s5_kernel,
        grid=(S5_STEPS,),
        in_specs=[fwd, bwd, _layer(l, (2, 2, BATCH, S5_LANES)),
                  _layer(l, (2, BRANCH_W, 2 * S5_LANES)), _layer(l, (2, 2 * S5_LANES, BRANCH_W))],
        out_specs=[fwd, bwd],
        out_shape=[jax.ShapeDtypeStruct((LCAT * BATCH, BRANCH_W), BF16)] * 2,
        scratch_shapes=[pltpu.VMEM((S5_ROWS, 2 * S5_LANES), F32),
                        pltpu.VMEM((S5_ROWS, 2 * S5_LANES), F32),
                        pltpu.VMEM((4, BATCH, S5_LANES), F32)],
        compiler_params=_cparams(1),
        name="s5_scan",
    )(u_t, u_t, lam, bblk, cblk)


def _log_sigmoid(x):
    return jnp.minimum(x, 0.0) - jnp.log(1.0 + jnp.exp(-jnp.abs(x)))


def _ret_kernel(q_ref, k_ref, v_ref, lgl_ref, lgh_ref, o_ref, dec_scr, sb_scr, sf_scr, sbc_scr):
    c_len = RET_CHUNK
    lgl = _log_sigmoid(lgl_ref[0])
    lgf, lgb = lgl[0], lgl[1]
    ti = lax.broadcasted_iota(jnp.int32, (c_len, BRANCH_W), 0).astype(F32)
    qdf = jnp.exp(lgf * (ti + 1.0))
    kdf = jnp.exp(lgf * (c_len - 1.0 - ti))
    qdb = jnp.exp(lgb * (c_len - ti))
    kdb = jnp.exp(lgb * ti)
    cdf = jnp.exp(lgf * float(c_len))
    cdb = jnp.exp(lgb * float(c_len))
    lane = lax.broadcasted_iota(jnp.int32, (1, BRANCH_W), 1)
    same_head = (lax.broadcasted_iota(jnp.int32, (BRANCH_W, BRANCH_W), 0) // RET_HD
                 == lax.broadcasted_iota(jnp.int32, (BRANCH_W, BRANCH_W), 1) // RET_HD)

    diff = (lax.broadcasted_iota(jnp.int32, (c_len, c_len), 0)
            - lax.broadcasted_iota(jnp.int32, (c_len, c_len), 1)).astype(F32)
    for h in range(RET_HEADS):
        gf = jnp.concatenate([_log_sigmoid(lgh_ref[0, 0, h])] * (c_len // 128), axis=-1)
        gb = jnp.concatenate([_log_sigmoid(lgh_ref[0, 1, h])] * (c_len // 128), axis=-1)
        dec_scr[h] = (jnp.where(diff >= 0, jnp.exp(gf * jnp.maximum(diff, 0.0)), 0.0)
                      + jnp.where(diff <= 0, jnp.exp(gb * jnp.maximum(-diff, 0.0)), 0.0))

    def chunk(ref, s, c):
        return ref[s, pl.ds(pl.multiple_of(c * c_len, c_len), c_len), :]

    def kv_outer(kd, v):
        s = lax.dot_general(kd.astype(BF16), v, (((0,), (0,)), ((), ())), preferred_element_type=F32)
        return jnp.where(same_head, s, 0.0)

    sbc_scr[...] = jnp.zeros_like(sbc_scr)

    def bwd(i, _):
        c = jnp.where(i < RET_CTX_CHUNKS, RET_CTX_CHUNKS - 1 - i, RET_NCHUNK + RET_CTX_CHUNKS - 1 - i)
        for s in range(RET_SPB):
            sb_scr[s, c] = sbc_scr[s].astype(BF16)
            k = chunk(k_ref, s, c).astype(F32)
            sbc_scr[s] = sbc_scr[s] * cdb + kv_outer(k * kdb, chunk(v_ref, s, c))
        return 0

    lax.fori_loop(0, RET_NCHUNK, bwd, 0)

    sf_scr[...] = jnp.zeros_like(sf_scr)

    def fwd(c, _):
        for s in range(RET_SPB):
            qb, kb, vb = chunk(q_ref, s, c), chunk(k_ref, s, c), chunk(v_ref, s, c)
            q = qb.astype(F32)
            o = (_dot((q * qdf).astype(BF16), sf_scr[s].astype(BF16))
                 + _dot((q * qdb).astype(BF16), sb_scr[s, c]))
            for h in range(RET_HEADS):
                hm = lane // RET_HD == h
                att = _dot_nt(jnp.where(hm, qb, jnp.zeros_like(qb)), kb)
                oh = _dot((att * dec_scr[h]).astype(BF16), vb)
                o = o + jnp.where(hm, oh, 0.0)
            o_ref[s, pl.ds(pl.multiple_of(c * c_len, c_len), c_len), :] = o.astype(BF16)
            sf_scr[s] = sf_scr[s] * cdf + kv_outer(kb.astype(F32) * kdf, vb)
        return 0

    lax.fori_loop(0, RET_NCHUNK, fwd, 0)


def _retention(l, rq, rk, rv, sw):
    tok = pl.BlockSpec((RET_SPB, LCAT, BRANCH_W), lambda b: (b, 0, 0))
    return pl.pallas_call(
        _ret_kernel,
        grid=(BATCH // RET_SPB,),
        in_specs=[tok, tok, tok, _layer(l, (2, 1, BRANCH_W)), _layer(l, (2, RET_HEADS, 1, 128))],
        out_specs=tok,
        out_shape=jax.ShapeDtypeStruct((BATCH, LCAT, BRANCH_W), BF16),
        scratch_shapes=[pltpu.VMEM((RET_HEADS, RET_CHUNK, RET_CHUNK), F32),
                        pltpu.VMEM((RET_SPB, RET_NCHUNK, BRANCH_W, BRANCH_W), BF16),
                        pltpu.VMEM((RET_SPB, BRANCH_W, BRANCH_W), F32),
                        pltpu.VMEM((RET_SPB, BRANCH_W, BRANCH_W), F32)],
        compiler_params=_cparams(1),
        name="retention",
    )(rq, rk, rv, sw["lgl"], sw["lgh"])


def _merge_kernel(xc_ref, xl_ref, mod_ref, nw_ref, wg_ref, oac_ref, oal_ref, ob_ref, yf_ref, yb_ref, ut_ref,
                  d_ref, wglu_ref, oret_ref, rg_ref, gnw_ref, pavg_ref, pbt_ref, wb_ref, wout_ref,
                  o_ref, *, skip_ctx):
    if skip_ctx:
        is_ctx = None
        x3 = xl_ref[...]
        oa = oal_ref[...]
    else:
        is_ctx = pl.program_id(0) < CTX_TILES
        x3 = jnp.where(is_ctx, xc_ref[...], xl_ref[...])
        oa = jnp.where(is_ctx, oac_ref[...], oal_ref[...])
    sh, sc, gate_res = _mod_chunks(mod_ref, is_ctx, (0, 1, 2))
    h = (_rms(x3, nw_ref[0]) * (1.0 + sc) + sh).reshape(TR, D_MODEL).astype(BF16)
    yt = yf_ref[...].astype(F32) + yb_ref[...].astype(F32) + d_ref[0] * ut_ref[...].astype(F32)
    y = _dot(pbt_ref[...], yt.astype(BF16))
    vg = _dot(_gelu_tanh(y).astype(BF16), wglu_ref[0])
    oc = vg[:, 0:BRANCH_W] * _sigmoid(vg[:, BRANCH_W:2 * BRANCH_W])
    o = oret_ref[...].reshape(TR, BRANCH_W)
    dl = o.astype(F32) - _dot(o, pavg_ref[...])
    var = _dot((dl * dl).astype(BF16), pavg_ref[...])
    g = rg_ref[...].reshape(TR, BRANCH_W).astype(F32)
    od = g * _sigmoid(g) * (dl * lax.rsqrt(var + NORM_EPS) * gnw_ref[0])
    branches = (oa.reshape(TR, BRANCH_W), ob_ref[...].reshape(TR, BRANCH_W), oc.astype(BF16), od.astype(BF16))
    acc = None
    for n, branch in enumerate(branches):
        gate = _dot_nt(h, wg_ref[0, _O_GATE + n * D_MODEL:_O_GATE + (n + 1) * D_MODEL, :])
        term = _sigmoid(gate) * _dot(branch, wb_ref[0, n])
        acc = term if acc is None else acc + term
    m = _dot(acc.astype(BF16), wout_ref[0])
    o_ref[...] = x3 + gate_res * m.reshape(BATCH, TT, D_MODEL)


def _merge(l, x_ctx, x_lat, mod, sw, oa_ctx, oa_lat, ob, yf, yb, u_t, oret, rg, skip_ctx):
    off = CTX_TILES if skip_ctx else 0
    nt = ALL_TILES - off
    ctx_spec, lat_spec = _tile_specs(l == 0, skip_ctx)
    oac_spec, oal_spec = _tile_specs(True, skip_ctx, BRANCH_W)
    tok = lambda w: pl.BlockSpec((BATCH, TT, w), lambda j: (0, j + off, 0))
    tmaj = pl.BlockSpec((TR, BRANCH_W), lambda j: (j + off, 0))
    return pl.pallas_call(
        functools.partial(_merge_kernel, skip_ctx=skip_ctx),
        grid=(nt,),
        in_specs=[ctx_spec, lat_spec,
                  _layer(l, (16, ADA_CHUNKS * D_MODEL)),
                  _layer(l, (1, D_MODEL)),
                  _layer(l, (IN_COLS, D_MODEL), single=True),
                  oac_spec, oal_spec,
                  tok(BRANCH_W), tmaj, tmaj, tmaj,
                  _layer(l, (1, BRANCH_W)),
                  _layer(l, (BRANCH_W, 2 * BRANCH_W)),
                  tok(BRANCH_W), tok(BRANCH_W),
                  _layer(l, (1, BRANCH_W)),
                  _full((BRANCH_W, BRANCH_W)),
                  _full((TR, TR)),
                  _layer(l, (N_BRANCH, BRANCH_W, D_MODEL), single=True),
                  _layer(l, (D_MODEL, D_MODEL), single=True)],
        out_specs=pl.BlockSpec((BATCH, TT, D_MODEL), lambda j: (0, j, 0)),
        out_shape=jax.ShapeDtypeStruct((BATCH, nt * TT, D_MODEL), F32),
        compiler_params=_cparams(1),
        name="merge",
    )(x_ctx, x_lat, mod, sw["norm_mix"], sw["w_in_t"], oa_ctx, oa_lat, ob, yf, yb, u_t, sw["s5_d"], sw["w_glu"],
      oret, rg, sw["gn_w"], jnp.asarray(_head_avg(), dtype=BF16), jnp.asarray(_tile_perm().T, dtype=BF16),
      sw["w_branch"], sw["w_out"])


def _ffn_kernel(x_ref, mod_ref, nw_ref, w1_ref, w2_ref, o_ref, *, skip_ctx):
    is_ctx = None if skip_ctx else pl.program_id(0) < CTX_LEN // FFN_TT
    sh, sc, gate_res = _mod_chunks(mod_ref, is_ctx, (3, 4, 5))
    x3 = x_ref[...]
    h = (_rms(x3, nw_ref[0]) * (1.0 + sc) + sh).reshape(BATCH * FFN_TT, D_MODEL).astype(BF16)
    f = None
    for c in range(0, D_FF, FFN_PIECE):
        a = jnp.maximum(_dot(h, w1_ref[0, :, c:c + FFN_PIECE]), 0.0)
        part = _dot((a * a).astype(BF16), w2_ref[0, c:c + FFN_PIECE, :])
        f = part if f is None else f + part
    o_ref[...] = x3 + gate_res * f.reshape(BATCH, FFN_TT, D_MODEL)


def _ffn(l, xm, mod, sw, skip_ctx):
    nt = xm.shape[1] // FFN_TT
    tok = pl.BlockSpec((BATCH, FFN_TT, D_MODEL), lambda j: (0, j, 0))
    return pl.pallas_call(
        functools.partial(_ffn_kernel, skip_ctx=skip_ctx),
        grid=(nt,),
        in_specs=[tok,
                  _layer(l, (16, ADA_CHUNKS * D_MODEL)),
                  _layer(l, (1, D_MODEL)),
                  _layer(l, (D_MODEL, D_FF), single=True),
                  _layer(l, (D_FF, D_MODEL), single=True)],
        out_specs=tok,
        out_shape=jax.ShapeDtypeStruct(xm.shape, F32),
        compiler_params=_cparams(1),
        name="ffn",
    )(xm, mod, sw["norm_ffn"], sw["ffn_w1"], sw["ffn_w2"])


def _stacked_weights(p):
    w_in_t = jnp.swapaxes(p["w_in"], 1, 2).astype(BF16)
    perm, _ = _mla_rope_perm()
    zeros = lambda *s: jnp.zeros((DEPTH,) + s, F32)
    wu = p["mla_w_ukv"].reshape(DEPTH, KV_LORA, MLA_HEADS, MLA_NOPE + MLA_V)
    pad_heads = lambda t: jnp.concatenate(
        [t, zeros(t.shape[1], MLA_HEADS, HEAD_PAD - t.shape[3])], -1).reshape(DEPTH, t.shape[1], HEADS_W)
    top = jnp.concatenate([pad_heads(wu[..., :MLA_NOPE]), zeros(KV_LORA, HEADS_W),
                           pad_heads(wu[..., MLA_NOPE:])], axis=2)
    place = jnp.broadcast_to(jnp.asarray(_kv_place())[None], (DEPTH, 128, 3 * HEADS_W))
    wkv = jnp.concatenate([top, place], axis=1).astype(BF16)

    wuq = p["mla_w_uq"].reshape(DEPTH, Q_LORA, MLA_HEADS, MLA_QK)
    qp = jnp.concatenate([zeros(Q_LORA, MLA_HEADS, MLA_NOPE), wuq[..., MLA_NOPE:][..., perm],
                          zeros(Q_LORA, MLA_HEADS, 32)], -1).reshape(DEPTH, Q_LORA, HEADS_W)
    wq = jnp.concatenate([pad_heads(wuq), qp], axis=2).astype(BF16)

    def head_w(v):
        wf = jnp.concatenate([v, zeros(32)], -1)[:, None, :]
        wp = jnp.concatenate([zeros(MLA_NOPE), v[:, MLA_NOPE:][:, perm], zeros(32)], -1)[:, None, :]
        return wf, wp

    head_ws = jnp.concatenate(head_w(p["mla_qk_norm_k"]) + head_w(p["mla_qk_norm_q"]), axis=1)

    def b_compact(b):
        t = b.transpose(0, 1, 2, 4, 3).reshape(DEPTH, 2, BRANCH_W, S5_STATE)
        return jnp.concatenate([t, t], axis=-1)

    c_compact = lambda c: c.transpose(0, 1, 4, 2, 3).reshape(DEPTH, 2, S5_STATE, BRANCH_W)
    vec = lambda a: a.reshape(DEPTH, 2, 1, S5_LANES)

    logit = p["ret_decay_logit"]
    row = lambda a: a[:, None, :]
    return dict(
        w_in_t=w_in_t,
        norm_mix=row(p["norm_mix_w"]), norm_ffn=row(p["norm_ffn_w"]),
        kv_norm=row(p["mla_kv_norm"]), q_norm=row(p["mla_q_norm"]),
        wkv=wkv, wq=wq, head_w=head_ws,
        s5_lam_re=vec(p["s5_lam_re"]), s5_lam_im=vec(p["s5_lam_im"]),
        s5_log_step=vec(jnp.repeat(p["s5_log_step"], S5_STATE, axis=-1)),
        s5_bre=b_compact(p["s5_b_re"]), s5_bim=b_compact(p["s5_b_im"]),
        s5_cre=c_compact(p["s5_c_re"]), s5_cim=c_compact(p["s5_c_im"]),
        s5_d=row(p["s5_d"]), w_glu=p["s5_w_glu"].astype(BF16),
        lgl=jnp.repeat(logit, RET_HD, axis=-1).reshape(DEPTH, 2, 1, BRANCH_W),
        lgh=jnp.broadcast_to(logit[:, :, :, None, None], (DEPTH, 2, RET_HEADS, 1, 128)),
        gn_w=row(p["ret_gn_w"]),
        w_branch=p["w_branch"].astype(BF16), w_out=p["w_out"].astype(BF16),
        ffn_w1=p["ffn_w1"].astype(BF16), ffn_w2=p["ffn_w2"].astype(BF16))


def kernel(x, c, ctx, c_ctx, ada_w, ada_b, norm_mix_w, norm_ffn_w, w_in, mla_q_norm, mla_w_uq, mla_kv_norm,
           mla_w_ukv, mla_qk_norm_q, mla_qk_norm_k, s5_lam_re, s5_lam_im, s5_log_step, s5_b_re, s5_b_im,
           s5_c_re, s5_c_im, s5_d, s5_w_glu, ret_decay_logit, ret_gn_w, w_branch, w_out, ffn_w1, ffn_w2):
    p = dict(norm_mix_w=norm_mix_w, norm_ffn_w=norm_ffn_w, w_in=w_in, mla_q_norm=mla_q_norm,
             mla_w_uq=mla_w_uq, mla_kv_norm=mla_kv_norm, mla_w_ukv=mla_w_ukv, mla_qk_norm_q=mla_qk_norm_q,
             mla_qk_norm_k=mla_qk_norm_k, s5_lam_re=s5_lam_re, s5_lam_im=s5_lam_im, s5_log_step=s5_log_step,
             s5_b_re=s5_b_re, s5_b_im=s5_b_im, s5_c_re=s5_c_re, s5_c_im=s5_c_im, s5_d=s5_d,
             s5_w_glu=s5_w_glu, ret_decay_logit=ret_decay_logit, ret_gn_w=ret_gn_w, w_branch=w_branch,
             w_out=w_out, ffn_w1=ffn_w1, ffn_w2=ffn_w2)
    sw = _stacked_weights(p)
    c16 = jnp.concatenate([c, c_ctx[None, :], jnp.zeros((16 - BATCH - 1, D_MODEL), F32)], axis=0)
    mod = _ada_mod(c16, ada_w, ada_b)
    lam, bblk, cblk = _s5_params(sw)
    x_ctx, x_lat = ctx, x
    for l in range(DEPTH):
        last = l == DEPTH - 1
        q, k, v, u_t, uc, us, rq, rk, rv, rg = _in_proj(l, x_ctx, x_lat, mod, sw)
        oa_lat = _attention_latent(q, k, v)
        oa_ctx = oa_lat if last else _attention_context(q, k, v)
        ob = _fnet(uc, us)
        yf, yb = _s5(l, u_t, lam, bblk, cblk)
        oret = _retention(l, rq, rk, rv, sw)
        xm = _merge(l, x_ctx, x_lat, mod, sw, oa_ctx, oa_lat, ob, yf, yb, u_t, oret, rg, last)
        x_ctx = x_lat = _ffn(l, xm, mod, sw, last)
    return x_lat
```

```python
import functools
import math

import numpy as np
import jax
import jax.numpy as jnp
from jax import lax
from jax.experimental import pallas as pl
from jax.experimental.pallas import tpu as pltpu

F32 = jnp.float32
BF16 = jnp.bfloat16

D_MODEL = 1024
BATCH = 8
SEQ = 2048
DEPTH = 2
GRID_W = 64
CTX_LEN = 256
LCAT = CTX_LEN + SEQ
N_BRANCH = 4
BRANCH_W = 256
NORM_EPS = 1e-6
ROPE_BASE = 10000.0
ADA_CHUNKS = 6

MLA_HEADS = 4
MLA_NOPE = 64
MLA_ROPE = 32
MLA_QK = MLA_NOPE + MLA_ROPE
MLA_V = 64
Q_LORA = 256
KV_LORA = 128
HEAD_PAD = 128
HEADS_W = MLA_HEADS * HEAD_PAD

FNET_GROUPS = 4
FNET_GW = BRANCH_W // FNET_GROUPS

S5_GROUP_CH = 16
S5_GROUPS = BRANCH_W // S5_GROUP_CH
S5_STATE = 64
S5_LANES = S5_GROUPS * S5_STATE

RET_HEADS = 4
RET_HD = BRANCH_W // RET_HEADS
RET_CHUNK = 256
RET_NCHUNK = LCAT // RET_CHUNK
RET_CTX_CHUNKS = CTX_LEN // RET_CHUNK
RET_SPB = 4

D_FF = 4 * D_MODEL

_O_KV, _O_KR, _O_S5, _O_RK, _O_RV = 0, 128, 160, 416, 672
_O_Q, _O_FN, _O_RQ, _O_RG, _O_GATE = 928, 1184, 1440, 1696, 1952
IN_COLS = _O_GATE + N_BRANCH * D_MODEL

TT = 64
TR = TT * BATCH
FFN_TT = 128
FFN_PIECE = 2048
CTX_TILES = CTX_LEN // TT
ALL_TILES = LCAT // TT
TQ = 256
ATTN_QB = 4
ATTN_AHEAD = 1
S5_TC = 128
S5_ROWS = S5_TC * BATCH
S5_STEPS = LCAT // S5_TC
S5_CTX_STEPS = CTX_LEN // S5_TC
S5_SUB = 1

VMEM_LIMIT = 56 * 1024 * 1024


def _cparams(n_grid):
    return pltpu.CompilerParams(dimension_semantics=("arbitrary",) * n_grid,
                                vmem_limit_bytes=VMEM_LIMIT)


def _dot(a, b):
    return jnp.dot(a, b, preferred_element_type=F32)


def _dot_nt(a, b):
    return lax.dot_general(a, b, (((1,), (1,)), ((), ())), preferred_element_type=F32)


def _sigmoid(x):
    return 0.5 * (jnp.tanh(0.5 * x) + 1.0)


def _gelu_tanh(y):
    return 0.5 * y * (1.0 + jnp.tanh(math.sqrt(2.0 / math.pi) * (y + 0.044715 * (y * y * y))))


def _rms(x, w):
    return x * lax.rsqrt(jnp.mean(x * x, axis=-1, keepdims=True) + NORM_EPS) * w


def _full(shape):
    n = len(shape)
    return pl.BlockSpec(shape, lambda *_: (0,) * n)


def _layer(l, shape, single=False):
    n = len(shape)
    mode = dict(pipeline_mode=pl.Buffered(1)) if single else {}
    return pl.BlockSpec((1,) + tuple(shape), lambda *_: (l,) + (0,) * n, **mode)


def _mla_rope_perm():
    r = np.arange(MLA_ROPE)
    first = (r % 16) < 8
    return np.where(first, r + 8, r - 8), np.where(first, -1.0, 1.0)


@functools.lru_cache(maxsize=None)
def _mla_tables():
    pos = np.arange(SEQ)
    rows, cols = pos // GRID_W, pos % GRID_W
    freqs = ROPE_BASE ** (-np.arange(8, dtype=np.float64) / 8)
    r = np.arange(MLA_ROPE)
    _, sign = _mla_rope_perm()
    p = np.where((r // 16 == 0)[None, :], rows[:, None], cols[:, None]).astype(np.float64)
    ang = p * freqs[(r % 16) % 8][None, :]
    cosf = np.zeros((LCAT, HEAD_PAD))
    sinf = np.zeros((LCAT, HEAD_PAD))
    cosf[:, :MLA_QK] = 1.0
    cosf[CTX_LEN:, MLA_NOPE:MLA_QK] = np.cos(ang)
    sinf[CTX_LEN:, MLA_NOPE:MLA_QK] = np.sin(ang) * sign[None, :]
    return cosf.astype(np.float32), sinf.astype(np.float32)


def _ret_perm():
    d = np.arange(RET_HD)
    first = d < RET_HD // 2
    return np.where(first, d + RET_HD // 2, d - RET_HD // 2), np.where(first, -1.0, 1.0)


@functools.lru_cache(maxsize=None)
def _ret_tables():
    half = RET_HD // 2
    pos = np.arange(SEQ, dtype=np.float64)
    freqs = ROPE_BASE ** (-np.arange(half, dtype=np.float64) / half)
    d = np.arange(RET_HD)
    perm, sign = _ret_perm()
    ang = pos[:, None] * freqs[d % half][None, :]
    cosr = np.ones((LCAT, RET_HD))
    sinr = np.zeros((LCAT, RET_HD))
    cosr[CTX_LEN:] = np.cos(ang)
    sinr[CTX_LEN:] = np.sin(ang) * sign[None, :]
    cosr = np.tile(cosr, (1, RET_HEADS))
    sinr = np.tile(sinr, (1, RET_HEADS))
    pm = np.zeros((BRANCH_W, BRANCH_W))
    for h in range(RET_HEADS):
        pm[h * RET_HD + perm, h * RET_HD + d] = 1.0
    return cosr.astype(np.float32), sinr.astype(np.float32), pm.astype(np.float32)


def _dft(n, scale):
    k = np.arange(n)
    kt = (k[:, None] * k[None, :]) % n
    ang = 2.0 * np.pi * kt / n
    return np.cos(ang) * scale, np.sin(ang) * scale


@functools.lru_cache(maxsize=None)
def _fnet_tables():
    cw, sw = _dft(FNET_GW, 1.0)
    t = np.zeros((BRANCH_W, 2 * BRANCH_W))
    for g in range(FNET_GROUPS):
        s = slice(g * FNET_GW, (g + 1) * FNET_GW)
        t[s, s] = cw
        t[s, BRANCH_W + g * FNET_GW:BRANCH_W + (g + 1) * FNET_GW] = -sw
    cl, sl = _dft(SEQ, 1.0 / math.sqrt(SEQ * FNET_GW))
    clc, slc = _dft(CTX_LEN, 1.0 / math.sqrt(CTX_LEN * FNET_GW))
    return tuple(a.astype(np.float32) for a in (t, cl, sl, clc, slc))


def _bf16_const(a):
    return jnp.asarray(a).astype(BF16)


@functools.lru_cache(maxsize=None)
def _head_avg():
    p = np.zeros((BRANCH_W, BRANCH_W))
    for h in range(RET_HEADS):
        p[h * RET_HD:(h + 1) * RET_HD, h * RET_HD:(h + 1) * RET_HD] = 1.0 / RET_HD
    return p.astype(np.float32)


@functools.lru_cache(maxsize=None)
def _tile_perm():
    p = np.zeros((TR, TR), np.float32)
    for b in range(BATCH):
        for t in range(TT):
            p[t * BATCH + b, b * TT + t] = 1.0
    return p


@functools.lru_cache(maxsize=None)
def _kv_place():
    perm, _ = _mla_rope_perm()
    place = np.zeros((128, 3 * HEADS_W), np.float32)
    for h in range(MLA_HEADS):
        for r in range(MLA_ROPE):
            place[r, h * HEAD_PAD + MLA_NOPE + r] = 1.0
            place[perm[r], HEADS_W + h * HEAD_PAD + MLA_NOPE + r] = 1.0
    return place


@functools.lru_cache(maxsize=None)
def _v_ones():
    v = np.zeros((1, HEADS_W), np.float32)
    v[0, np.arange(MLA_HEADS) * HEAD_PAD + MLA_V] = 1.0
    return v


ADA_TN = 1536


def _ada_kernel(c_ref, w_ref, b_ref, o_ref):
    c = c_ref[...]
    s = (c * _sigmoid(c)).astype(BF16)
    o_ref[0] = _dot(s, w_ref[0].astype(BF16)) + b_ref[0]


def _ada_mod(c16, ada_w, ada_b):
    n = ADA_CHUNKS * D_MODEL
    return pl.pallas_call(
        _ada_kernel,
        grid=(DEPTH, n // ADA_TN),
        in_specs=[pl.BlockSpec((16, D_MODEL), lambda l, j: (0, 0)),
                  pl.BlockSpec((1, D_MODEL, ADA_TN), lambda l, j: (l, 0, j)),
                  pl.BlockSpec((1, 1, ADA_TN), lambda l, j: (l, 0, j))],
        out_specs=pl.BlockSpec((1, 16, ADA_TN), lambda l, j: (l, 0, j)),
        out_shape=jax.ShapeDtypeStruct((DEPTH, 16, n), F32),
        compiler_params=_cparams(2),
        name="ada_mod",
    )(c16, ada_w, ada_b.reshape(DEPTH, 1, n))


def _side_casts(jobs):
    specs_in, specs_out, shapes, args = [], [], [], []
    for a, l, n in jobs:
        _, r, c = a.shape
        blk = (1, r // n, c)
        specs_in.append(pl.BlockSpec(blk, lambda j, l=l, n=n: (l, jnp.minimum(j, n - 1), 0)))
        specs_out.append(pl.BlockSpec(blk, lambda j, n=n: (0, jnp.minimum(j, n - 1), 0)))
        shapes.append(jax.ShapeDtypeStruct((1, r, c), BF16))
        args.append(a)
    return specs_in, specs_out, shapes, args


def _with_side_casts(body, n_in, n_out, n_jobs):
    def wrapped(*refs):
        body(*refs[:n_in], *refs[n_in + n_jobs:n_in + n_jobs + n_out])
        for src, dst in zip(refs[n_in:n_in + n_jobs], refs[n_in + n_jobs + n_out:]):
            dst[...] = src[...].astype(BF16)
    return wrapped if n_jobs else body


def _mod_chunks(mod_ref, is_ctx, idxs):
    out = []
    for i in idxs:
        sl = slice(i * D_MODEL, (i + 1) * D_MODEL)
        m = mod_ref[0, 0:BATCH, sl]
        if is_ctx is not None:
            m = jnp.where(is_ctx, mod_ref[0, BATCH:BATCH + 1, sl], m)
        out.append(m[:, None, :])
    return out


def _tile_specs(separate, skip_ctx, width=D_MODEL):
    base = 0 if separate else CTX_TILES
    shape = (BATCH, TT, width)
    if skip_ctx:
        return (pl.BlockSpec(shape, lambda j: (0, 0, 0)),
                pl.BlockSpec(shape, lambda j: (0, j + base, 0)))
    return (pl.BlockSpec(shape, lambda j: (0, jnp.minimum(j, CTX_TILES - 1), 0)),
            pl.BlockSpec(shape, lambda j: (0, jnp.maximum(j - CTX_TILES, 0) + base, 0)))


def _head_norm_rot(xf, xp, a, b, scale):
    outs = []
    for h in range(MLA_HEADS):
        f = xf[:, h * HEAD_PAD:(h + 1) * HEAD_PAD]
        p = xp[:, h * HEAD_PAD:(h + 1) * HEAD_PAD]
        n = lax.rsqrt(jnp.sum(f * f, axis=-1, keepdims=True) * (1.0 / MLA_QK) + NORM_EPS) * scale
        outs.append(n * (f * a + p * b))
    return jnp.concatenate(outs, axis=-1)


def _rows(table):
    w = table.shape[-1]
    return jnp.broadcast_to(table[None], (BATCH, TT, w)).reshape(TR, w)


def _tile3(x):
    return x.reshape(BATCH, TT, x.shape[-1])


def _inproj_kernel(xc_ref, xl_ref, mod_ref, nw_ref, w_ref, kvw_ref, wkv_ref, qnw_ref, wq_ref,
                   cosf_ref, sinf_ref, hw_ref, pm_ref, cosr_ref, sinr_ref, t_ref, ptb_ref, vone_ref,
                   q_out, k_out, v_out, ut_out, uc_out, us_out, rq_out, rk_out, rv_out, rg_out):
    is_ctx = pl.program_id(0) < CTX_TILES
    x3 = jnp.where(is_ctx, xc_ref[...], xl_ref[...])
    sh, sc = _mod_chunks(mod_ref, is_ctx, (0, 1))
    h = (_rms(x3, nw_ref[0]) * (1.0 + sc) + sh).reshape(TR, D_MODEL).astype(BF16)
    za = _dot_nt(h, w_ref[0, 0:256, :])
    z = _dot_nt(h, w_ref[0, _O_S5:_O_GATE, :])

    cosf, sinf = _rows(cosf_ref[...]), _rows(sinf_ref[...])
    hw = hw_ref[0]
    kvn = _rms(za[:, 0:128], kvw_ref[0]).astype(BF16)
    lhs = jnp.concatenate([kvn, za[:, 128:256].astype(BF16)], axis=-1)
    kv = _dot(lhs, wkv_ref[0])
    k = _head_norm_rot(kv[:, 0:HEADS_W], kv[:, HEADS_W:2 * HEADS_W],
                       cosf * hw[0:1], sinf * hw[1:2], 1.0)
    k_out[...] = _tile3(k).astype(BF16)
    v_out[...] = _tile3(kv[:, 2 * HEADS_W:3 * HEADS_W] + vone_ref[...]).astype(BF16)
    qn = _rms(z[:, 768:1024], qnw_ref[0]).astype(BF16)
    qq = _dot(qn, wq_ref[0])
    q = _head_norm_rot(qq[:, 0:HEADS_W], qq[:, HEADS_W:2 * HEADS_W],
                       cosf * hw[2:3], sinf * hw[3:4], MLA_QK ** -0.5)
    q_out[...] = _tile3(q).astype(BF16)
    cosr, sinr = _rows(cosr_ref[...]), _rows(sinr_ref[...])
    rq = z[:, 1280:1536]
    rk = z[:, 256:512]
    rq = rq * cosr + _dot(rq.astype(BF16), pm_ref[...]) * sinr
    rk = rk * cosr + _dot(rk.astype(BF16), pm_ref[...]) * sinr
    rq_out[...] = _tile3(rq).astype(BF16)
    rk_out[...] = _tile3(rk * (RET_HD ** -0.5)).astype(BF16)
    rv_out[...] = _tile3(z[:, 512:768]).astype(BF16)
    rg_out[...] = _tile3(z[:, 1536:1792]).astype(BF16)
    ucs = _dot(z[:, 1024:1280].astype(BF16), t_ref[...])
    uc_out[...] = _tile3(ucs[:, 0:BRANCH_W]).astype(BF16)
    us_out[...] = _tile3(ucs[:, BRANCH_W:2 * BRANCH_W]).astype(BF16)
    ut_out[...] = _dot(ptb_ref[...], z[:, 0:256].astype(BF16)).astype(BF16)


def _in_proj(l, x_ctx, x_lat, mod, sw, w_in_t, jobs):
    ctx_spec, lat_spec = _tile_specs(l == 0, False)
    tok = lambda w: pl.BlockSpec((BATCH, TT, w), lambda j: (0, j, 0))
    tab = lambda w: pl.BlockSpec((TT, w), lambda j: (j, 0))
    cosf, sinf = _mla_tables()
    cosr, sinr, pm = _ret_tables()
    bshape = lambda w: jax.ShapeDtypeStruct((BATCH, LCAT, w), BF16)
    job_in, job_out, job_shapes, job_args = _side_casts(jobs)
    return pl.pallas_call(
        _with_side_casts(_inproj_kernel, 18, 10, len(jobs)),
        grid=(ALL_TILES,),
        in_specs=[ctx_spec, lat_spec,
                  _layer(l, (16, ADA_CHUNKS * D_MODEL)),
                  _layer(l, (1, D_MODEL)),
                  _layer(0, (_O_GATE, D_MODEL)),
                  _layer(l, (1, KV_LORA)),
                  _layer(l, (256, 3 * HEADS_W)),
                  _layer(l, (1, Q_LORA)),
                  _layer(l, (Q_LORA, 2 * HEADS_W)),
                  tab(HEAD_PAD), tab(HEAD_PAD),
                  _layer(l, (4, HEAD_PAD)),
                  _full((BRANCH_W, BRANCH_W)),
                  tab(BRANCH_W), tab(BRANCH_W),
                  _full((BRANCH_W, 2 * BRANCH_W)),
                  _full((TR, TR)),
                  _full((1, HEADS_W))] + job_in,
        out_specs=[tok(HEADS_W), tok(HEADS_W), tok(HEADS_W),
                   pl.BlockSpec((TR, BRANCH_W), lambda j: (j, 0)),
                   tok(256), tok(256), tok(256), tok(256), tok(256), tok(256)] + job_out,
        out_shape=[bshape(HEADS_W), bshape(HEADS_W), bshape(HEADS_W),
                   jax.ShapeDtypeStruct((LCAT * BATCH, BRANCH_W), BF16),
                   bshape(256), bshape(256), bshape(256), bshape(256), bshape(256), bshape(256)] + job_shapes,
        compiler_params=_cparams(1),
        name="in_proj",
    )(x_ctx, x_lat, mod, sw["norm_mix"], w_in_t, sw["kv_norm"], sw["wkv"], sw["q_norm"], sw["wq"],
      jnp.asarray(cosf), jnp.asarray(sinf), sw["head_w"], jnp.asarray(pm, dtype=BF16),
      jnp.asarray(cosr), jnp.asarray(sinr), _bf16_const(_fnet_tables()[0]),
      jnp.asarray(_tile_perm(), dtype=BF16), jnp.asarray(_v_ones()), *job_args)


def _attn_kernel(*refs):
    q_refs, (k_ref, v_ref, o_ref) = refs[:-3], refs[-3:]
    units = [(r, h, slice(h * HEAD_PAD, (h + 1) * HEAD_PAD))
             for r in range(len(q_refs)) for h in range(MLA_HEADS)]
    scores = lambda u: _dot_nt(q_refs[u[0]][0, :, u[2]], k_ref[0, :, u[2]])
    lane = lax.broadcasted_iota(jnp.int32, (1, HEAD_PAD), 1)
    pending = [scores(u) for u in units[:ATTN_AHEAD]]
    for i, (r, h, sl) in enumerate(units):
        if i + ATTN_AHEAD < len(units):
            pending.append(scores(units[i + ATTN_AHEAD]))
        s = pending.pop(0)
        p = jnp.exp((s - jnp.max(s, axis=-1, keepdims=True)).astype(BF16))
        oh = _dot(p, v_ref[0, :, sl])
        oh = oh * (1.0 / oh[:, MLA_V:MLA_V + 1])
        if h % 2 == 0:
            even = oh
        else:
            pair = jnp.where(lane < MLA_V, even, pltpu.roll(oh, MLA_V, axis=1))
            o_ref[0, r * TQ:(r + 1) * TQ, (h // 2) * HEAD_PAD:(h // 2 + 1) * HEAD_PAD] = pair.astype(BF16)


def _attention_latent(q, k, v):
    first = CTX_LEN // TQ
    qspec = lambda r: pl.BlockSpec((1, TQ, HEADS_W), lambda b, j: (b, first + ATTN_QB * j + r, 0))
    kv = pl.BlockSpec((1, LCAT, HEADS_W), lambda b, j: (b, 0, 0))
    return pl.pallas_call(
        _attn_kernel,
        grid=(BATCH, SEQ // (ATTN_QB * TQ)),
        in_specs=[qspec(r) for r in range(ATTN_QB)] + [kv, kv],
        out_specs=pl.BlockSpec((1, ATTN_QB * TQ, BRANCH_W), lambda b, j: (b, j, 0)),
        out_shape=jax.ShapeDtypeStruct((BATCH, SEQ, BRANCH_W), BF16),
        compiler_params=_cparams(2),
        name="mla_attention",
    )(*([q] * ATTN_QB), k, v)


def _attention_context(q, k, v):
    blk = pl.BlockSpec((1, CTX_LEN, HEADS_W), lambda b: (b, 0, 0))
    return pl.pallas_call(
        _attn_kernel,
        grid=(BATCH,),
        in_specs=[blk, blk, blk],
        out_specs=pl.BlockSpec((1, CTX_LEN, BRANCH_W), lambda b: (b, 0, 0)),
        out_shape=jax.ShapeDtypeStruct((BATCH, CTX_LEN, BRANCH_W), BF16),
        compiler_params=_cparams(1),
        name="mla_attention_ctx",
    )(q, k, v)


def _fnet_kernel(uc_ref, us_ref, cl_ref, sl_ref, clc_ref, slc_ref, o_ref):
    o_ref[0, 0:CTX_LEN, :] = (_dot(clc_ref[...], uc_ref[0, 0:CTX_LEN, :])
                              + _dot(slc_ref[...], us_ref[0, 0:CTX_LEN, :])).astype(BF16)
    o_ref[0, CTX_LEN:LCAT, :] = (_dot(cl_ref[...], uc_ref[0, CTX_LEN:LCAT, :])
                                 + _dot(sl_ref[...], us_ref[0, CTX_LEN:LCAT, :])).astype(BF16)


def _fnet(uc, us):
    cl, sl, clc, slc = (_bf16_const(a) for a in _fnet_tables()[1:])
    tok = pl.BlockSpec((1, LCAT, BRANCH_W), lambda b: (b, 0, 0))
    return pl.pallas_call(
        _fnet_kernel,
        grid=(BATCH,),
        in_specs=[tok, tok, _full((SEQ, SEQ)), _full((SEQ, SEQ)),
                  _full((CTX_LEN, CTX_LEN)), _full((CTX_LEN, CTX_LEN))],
        out_specs=tok,
        out_shape=jax.ShapeDtypeStruct((BATCH, LCAT, BRANCH_W), BF16),
        compiler_params=_cparams(1),
        name="fnet_dft",
    )(uc, us, cl, sl, clc, slc)


def _s5_param_kernel(lr_ref, li_ref, ls_ref, bre_ref, bim_ref, cre_ref, cim_ref, lam_out, b_out, c_out):
    lr, li = lr_ref[0, 0], li_ref[0, 0]
    step = jnp.exp(ls_ref[0, 0])
    mag = jnp.exp(lr * step)
    lbr = mag * jnp.cos(li * step)
    lbi = mag * jnp.sin(li * step)
    den = 1.0 / (lr * lr + li * li)
    cr = ((lbr - 1.0) * lr + lbi * li) * den
    ci = (lbi * lr - (lbr - 1.0) * li) * den
    lam_out[0, 0, 0] = jnp.broadcast_to(lbr, (BATCH, S5_LANES))
    lam_out[0, 0, 1] = jnp.broadcast_to(lbi, (BATCH, S5_LANES))
    grp_b = (lax.broadcasted_iota(jnp.int32, (BRANCH_W, S5_LANES), 0) // S5_GROUP_CH
             == lax.broadcasted_iota(jnp.int32, (BRANCH_W, S5_LANES), 1) // S5_STATE)
    wide = lambda r: jnp.concatenate([r[0, 0]] * (S5_LANES // 128), axis=-1)
    bre = jnp.where(grp_b, wide(bre_ref), 0.0)
    bim = jnp.where(grp_b, wide(bim_ref), 0.0)
    b_out[0, 0, :, 0:S5_LANES] = (cr * bre - ci * bim).astype(BF16)
    b_out[0, 0, :, S5_LANES:2 * S5_LANES] = (cr * bim + ci * bre).astype(BF16)
    grp_c = (lax.broadcasted_iota(jnp.int32, (S5_LANES, BRANCH_W), 0) // S5_STATE
             == lax.broadcasted_iota(jnp.int32, (S5_LANES, BRANCH_W), 1) // S5_GROUP_CH)
    tall = lambda r: jnp.concatenate([r[0, 0]] * S5_GROUPS, axis=0)
    c_out[0, 0, 0:S5_LANES, :] = jnp.where(grp_c, tall(cre_ref), 0.0).astype(BF16)
    c_out[0, 0, S5_LANES:2 * S5_LANES, :] = jnp.where(grp_c, -tall(cim_ref), 0.0).astype(BF16)


def _s5_params(sw):
    spec = lambda *s: pl.BlockSpec((1, 1) + s, lambda l, d: (l, d) + (0,) * len(s))
    return pl.pallas_call(
        _s5_param_kernel,
        grid=(DEPTH, 2),
        in_specs=[spec(1, S5_LANES), spec(1, S5_LANES), spec(1, S5_LANES),
                  spec(BRANCH_W, 128), spec(BRANCH_W, 128), spec(S5_STATE, BRANCH_W), spec(S5_STATE, BRANCH_W)],
        out_specs=[spec(2, BATCH, S5_LANES), spec(BRANCH_W, 2 * S5_LANES), spec(2 * S5_LANES, BRANCH_W)],
        out_shape=[jax.ShapeDtypeStruct((DEPTH, 2, 2, BATCH, S5_LANES), F32),
                   jax.ShapeDtypeStruct((DEPTH, 2, BRANCH_W, 2 * S5_LANES), BF16),
                   jax.ShapeDtypeStruct((DEPTH, 2, 2 * S5_LANES, BRANCH_W), BF16)],
        compiler_params=_cparams(2),
        name="s5_discretise",
    )(sw["s5_lam_re"], sw["s5_lam_im"], sw["s5_log_step"], sw["s5_bre"], sw["s5_bim"], sw["s5_cre"], sw["s5_cim"])


def _s5_bwd_block(i):
    return jnp.where(i < S5_CTX_STEPS, S5_CTX_STEPS - 1 - i, S5_STEPS + S5_CTX_STEPS - 1 - i)


def _s5_kernel(uf_ref, ub_ref, lam_ref, b_ref, c_ref, yf_ref, yb_ref, xf_scr, xb_scr, st_scr):
    @pl.when(pl.program_id(0) == 0)
    def _():
        st_scr[...] = jnp.zeros_like(st_scr)

    sub_t = S5_TC // S5_SUB
    sub_rows = lambda k: pl.ds(k * sub_t * BATCH, sub_t * BATCH)
    order = (range(S5_SUB), range(S5_SUB - 1, -1, -1))
    dirs = ((uf_ref, xf_scr, yf_ref), (ub_ref, xb_scr, yb_ref))
    for d, (u_ref, x_scr, _) in enumerate(dirs):
        for k in order[d]:
            x_scr[sub_rows(k), :] = _dot(u_ref[sub_rows(k), :], b_ref[0, d])
    re, im = pl.ds(0, S5_LANES), pl.ds(S5_LANES, S5_LANES)
    for d, (_, x_scr, y_ref) in enumerate(dirs):
        xr, xi = st_scr[2 * d], st_scr[2 * d + 1]
        ar, ai = lam_ref[0, d, 0], lam_ref[0, d, 1]
        for k in order[d]:
            steps = range(k * sub_t, (k + 1) * sub_t)
            for t in (steps if d == 0 else reversed(steps)):
                rows = pl.ds(t * BATCH, BATCH)
                xr, xi = (ar * xr - ai * xi + x_scr[rows, re], ar * xi + ai * xr + x_scr[rows, im])
                x_scr[rows, re] = xr
                x_scr[rows, im] = xi
            y_ref[sub_rows(k), :] = _dot(x_scr[sub_rows(k), :].astype(BF16), c_ref[0, d]).astype(BF16)
        st_scr[2 * d], st_scr[2 * d + 1] = xr, xi


def _s5(l, u_t, lam, bblk, cblk):
    fwd = pl.BlockSpec((S5_ROWS, BRANCH_W), lambda i: (i, 0))
    bwd = pl.BlockSpec((S5_ROWS, BRANCH_W), lambda i: (_s5_bwd_block(i), 0))
    return pl.pallas_call(
        _s5_kernel,
        grid=(S5_STEPS,),
        in_specs=[fwd, bwd, _layer(l, (2, 2, BATCH, S5_LANES)),
                  _layer(l, (2, BRANCH_W, 2 * S5_LANES)), _layer(l, (2, 2 * S5_LANES, BRANCH_W))],
        out_specs=[fwd, bwd],
        out_shape=[jax.ShapeDtypeStruct((LCAT * BATCH, BRANCH_W), BF16)] * 2,
        scratch_shapes=[pltpu.VMEM((S5_ROWS, 2 * S5_LANES), F32),
                        pltpu.VMEM((S5_ROWS, 2 * S5_LANES), F32),
                        pltpu.VMEM((4, BATCH, S5_LANES), F32)],
        compiler_params=_cparams(1),
        name="s5_scan",
    )(u_t, u_t, lam, bblk, cblk)


def _log_sigmoid(x):
    return jnp.minimum(x, 0.0) - jnp.log(1.0 + jnp.exp(-jnp.abs(x)))


def _ret_kernel(q_ref, k_ref, v_ref, lgl_ref, lgh_ref, o_ref, dec_scr, sb_scr, sf_scr, sbc_scr):
    c_len = RET_CHUNK
    lgl = _log_sigmoid(lgl_ref[0])
    lgf, lgb = lgl[0], lgl[1]
    ti = lax.broadcasted_iota(jnp.int32, (c_len, BRANCH_W), 0).astype(F32)
    qdf = jnp.exp(lgf * (ti + 1.0))
    kdf = jnp.exp(lgf * (c_len - 1.0 - ti))
    qdb = jnp.exp(lgb * (c_len - ti))
    kdb = jnp.exp(lgb * ti)
    cdf = jnp.exp(lgf * float(c_len))
    cdb = jnp.exp(lgb * float(c_len))
    lane = lax.broadcasted_iota(jnp.int32, (1, BRANCH_W), 1)
    same_head = (lax.broadcasted_iota(jnp.int32, (BRANCH_W, BRANCH_W), 0) // RET_HD
                 == lax.broadcasted_iota(jnp.int32, (BRANCH_W, BRANCH_W), 1) // RET_HD)

    diff = (lax.broadcasted_iota(jnp.int32, (c_len, c_len), 0)
            - lax.broadcasted_iota(jnp.int32, (c_len, c_len), 1)).astype(F32)
    for h in range(RET_HEADS):
        gf = jnp.concatenate([_log_sigmoid(lgh_ref[0, 0, h])] * (c_len // 128), axis=-1)
        gb = jnp.concatenate([_log_sigmoid(lgh_ref[0, 1, h])] * (c_len // 128), axis=-1)
        dec_scr[h] = (jnp.where(diff >= 0, jnp.exp(gf * jnp.maximum(diff, 0.0)), 0.0)
                      + jnp.where(diff <= 0, jnp.exp(gb * jnp.maximum(-diff, 0.0)), 0.0))

    def chunk(ref, s, c):
        return ref[s, pl.ds(pl.multiple_of(c * c_len, c_len), c_len), :]

    def kv_outer(kd, v):
        s = lax.dot_general(kd.astype(BF16), v, (((0,), (0,)), ((), ())), preferred_element_type=F32)
        return jnp.where(same_head, s, 0.0)

    sbc_scr[...] = jnp.zeros_like(sbc_scr)

    def bwd(i, _):
        c = jnp.where(i < RET_CTX_CHUNKS, RET_CTX_CHUNKS - 1 - i, RET_NCHUNK + RET_CTX_CHUNKS - 1 - i)
        for s in range(RET_SPB):
            sb_scr[s, c] = sbc_scr[s].astype(BF16)
            k = chunk(k_ref, s, c).astype(F32)
            sbc_scr[s] = sbc_scr[s] * cdb + kv_outer(k * kdb, chunk(v_ref, s, c))
        return 0

    lax.fori_loop(0, RET_NCHUNK, bwd, 0)

    sf_scr[...] = jnp.zeros_like(sf_scr)

    def fwd(c, _):
        for s in range(RET_SPB):
            qb, kb, vb = chunk(q_ref, s, c), chunk(k_ref, s, c), chunk(v_ref, s, c)
            q = qb.astype(F32)
            o = (_dot((q * qdf).astype(BF16), sf_scr[s].astype(BF16))
                 + _dot((q * qdb).astype(BF16), sb_scr[s, c]))
            for h in range(RET_HEADS):
                hm = lane // RET_HD == h
                att = _dot_nt(jnp.where(hm, qb, jnp.zeros_like(qb)), kb)
                oh = _dot((att * dec_scr[h]).astype(BF16), vb)
                o = o + jnp.where(hm, oh, 0.0)
            o_ref[s, pl.ds(pl.multiple_of(c * c_len, c_len), c_len), :] = o.astype(BF16)
            sf_scr[s] = sf_scr[s] * cdf + kv_outer(kb.astype(F32) * kdf, vb)
        return 0

    lax.fori_loop(0, RET_NCHUNK, fwd, 0)


def _retention(l, rq, rk, rv, sw):
    tok = pl.BlockSpec((RET_SPB, LCAT, BRANCH_W), lambda b: (b, 0, 0))
    return pl.pallas_call(
        _ret_kernel,
        grid=(BATCH // RET_SPB,),
        in_specs=[tok, tok, tok, _layer(l, (2, 1, BRANCH_W)), _layer(l, (2, RET_HEADS, 1, 128))],
        out_specs=tok,
        out_shape=jax.ShapeDtypeStruct((BATCH, LCAT, BRANCH_W), BF16),
        scratch_shapes=[pltpu.VMEM((RET_HEADS, RET_CHUNK, RET_CHUNK), F32),
                        pltpu.VMEM((RET_SPB, RET_NCHUNK, BRANCH_W, BRANCH_W), BF16),
                        pltpu.VMEM((RET_SPB, BRANCH_W, BRANCH_W), F32),
                        pltpu.VMEM((RET_SPB, BRANCH_W, BRANCH_W), F32)],
        compiler_params=_cparams(1),
        name="retention",
    )(rq, rk, rv, sw["lgl"], sw["lgh"])


def _merge_kernel(xc_ref, xl_ref, mod_ref, nw_ref, wg_ref, oac_ref, oal_ref, ob_ref, yf_ref, yb_ref, ut_ref,
                  d_ref, wglu_ref, oret_ref, rg_ref, gnw_ref, pavg_ref, pbt_ref, wb_ref, wout_ref,
                  o_ref, *, skip_ctx):
    if skip_ctx:
        is_ctx = None
        x3 = xl_ref[...]
        oa = oal_ref[...]
    else:
        is_ctx = pl.program_id(0) < CTX_TILES
        x3 = jnp.where(is_ctx, xc_ref[...], xl_ref[...])
        oa = jnp.where(is_ctx, oac_ref[...], oal_ref[...])
    sh, sc, gate_res = _mod_chunks(mod_ref, is_ctx, (0, 1, 2))
    h = (_rms(x3, nw_ref[0]) * (1.0 + sc) + sh).reshape(TR, D_MODEL).astype(BF16)
    yt = yf_ref[...].astype(F32) + yb_ref[...].astype(F32) + d_ref[0] * ut_ref[...].astype(F32)
    y = _dot(pbt_ref[...], yt.astype(BF16))
    vg = _dot(_gelu_tanh(y).astype(BF16), wglu_ref[0])
    oc = vg[:, 0:BRANCH_W] * _sigmoid(vg[:, BRANCH_W:2 * BRANCH_W])
    o = oret_ref[...].reshape(TR, BRANCH_W)
    dl = o.astype(F32) - _dot(o, pavg_ref[...])
    var = _dot((dl * dl).astype(BF16), pavg_ref[...])
    g = rg_ref[...].reshape(TR, BRANCH_W).astype(F32)
    od = g * _sigmoid(g) * (dl * lax.rsqrt(var + NORM_EPS) * gnw_ref[0])
    branches = (oa.reshape(TR, BRANCH_W), ob_ref[...].reshape(TR, BRANCH_W), oc.astype(BF16), od.astype(BF16))
    acc = None
    for n, branch in enumerate(branches):
        gate = _dot_nt(h, wg_ref[0, _O_GATE + n * D_MODEL:_O_GATE + (n + 1) * D_MODEL, :])
        term = _sigmoid(gate) * _dot(branch, wb_ref[0, n])
        acc = term if acc is None else acc + term
    m = _dot(acc.astype(BF16), wout_ref[0])
    o_ref[...] = x3 + gate_res * m.reshape(BATCH, TT, D_MODEL)


def _merge(l, x_ctx, x_lat, mod, sw, w_in_t, oa_ctx, oa_lat, ob, yf, yb, u_t, oret, rg, skip_ctx, jobs):
    off = CTX_TILES if skip_ctx else 0
    nt = ALL_TILES - off
    ctx_spec, lat_spec = _tile_specs(l == 0, skip_ctx)
    oac_spec, oal_spec = _tile_specs(True, skip_ctx, BRANCH_W)
    tok = lambda w: pl.BlockSpec((BATCH, TT, w), lambda j: (0, j + off, 0))
    tmaj = pl.BlockSpec((TR, BRANCH_W), lambda j: (j + off, 0))
    job_in, job_out, job_shapes, job_args = _side_casts(jobs)
    return pl.pallas_call(
        _with_side_casts(functools.partial(_merge_kernel, skip_ctx=skip_ctx), 20, 1, len(jobs)),
        grid=(nt,),
        in_specs=[ctx_spec, lat_spec,
                  _layer(l, (16, ADA_CHUNKS * D_MODEL)),
                  _layer(l, (1, D_MODEL)),
                  _layer(0, (IN_COLS, D_MODEL), single=True),
                  oac_spec, oal_spec,
                  tok(BRANCH_W), tmaj, tmaj, tmaj,
                  _layer(l, (1, BRANCH_W)),
                  _layer(l, (BRANCH_W, 2 * BRANCH_W)),
                  tok(BRANCH_W), tok(BRANCH_W),
                  _layer(l, (1, BRANCH_W)),
                  _full((BRANCH_W, BRANCH_W)),
                  _full((TR, TR)),
                  _layer(l, (N_BRANCH, BRANCH_W, D_MODEL), single=True),
                  _layer(l, (D_MODEL, D_MODEL), single=True)] + job_in,
        out_specs=[pl.BlockSpec((BATCH, TT, D_MODEL), lambda j: (0, j, 0))] + job_out,
        out_shape=[jax.ShapeDtypeStruct((BATCH, nt * TT, D_MODEL), F32)] + job_shapes,
        compiler_params=_cparams(1),
        name="merge",
    )(x_ctx, x_lat, mod, sw["norm_mix"], w_in_t, oa_ctx, oa_lat, ob, yf, yb, u_t, sw["s5_d"], sw["w_glu"],
      oret, rg, sw["gn_w"], jnp.asarray(_head_avg(), dtype=BF16), jnp.asarray(_tile_perm().T, dtype=BF16),
      sw["w_branch"], sw["w_out"], *job_args)


def _ffn_kernel(x_ref, mod_ref, nw_ref, w1_ref, w2_ref, o_ref, *, skip_ctx):
    is_ctx = None if skip_ctx else pl.program_id(0) < CTX_LEN // FFN_TT
    sh, sc, gate_res = _mod_chunks(mod_ref, is_ctx, (3, 4, 5))
    x3 = x_ref[...]
    h = (_rms(x3, nw_ref[0]) * (1.0 + sc) + sh).reshape(BATCH * FFN_TT, D_MODEL).astype(BF16)
    f = None
    for c in range(0, D_FF, FFN_PIECE):
        a = jnp.maximum(_dot(h, w1_ref[0, :, c:c + FFN_PIECE]), 0.0)
        part = _dot((a * a).astype(BF16), w2_ref[0, c:c + FFN_PIECE, :])
        f = part if f is None else f + part
    o_ref[...] = x3 + gate_res * f.reshape(BATCH, FFN_TT, D_MODEL)


def _ffn(l, xm, mod, sw, w1, w2, skip_ctx):
    nt = xm.shape[1] // FFN_TT
    tok = pl.BlockSpec((BATCH, FFN_TT, D_MODEL), lambda j: (0, j, 0))
    return pl.pallas_call(
        functools.partial(_ffn_kernel, skip_ctx=skip_ctx),
        grid=(nt,),
        in_specs=[tok,
                  _layer(l, (16, ADA_CHUNKS * D_MODEL)),
                  _layer(l, (1, D_MODEL)),
                  _layer(0, (D_MODEL, D_FF), single=True),
                  _layer(0, (D_FF, D_MODEL), single=True)],
        out_specs=tok,
        out_shape=jax.ShapeDtypeStruct(xm.shape, F32),
        compiler_params=_cparams(1),
        name="ffn",
    )(xm, mod, sw["norm_ffn"], w1, w2)


def _stacked_weights(p):
    perm, _ = _mla_rope_perm()
    zeros = lambda *s: jnp.zeros((DEPTH,) + s, F32)
    wu = p["mla_w_ukv"].reshape(DEPTH, KV_LORA, MLA_HEADS, MLA_NOPE + MLA_V)
    pad_heads = lambda t: jnp.concatenate(
        [t, zeros(t.shape[1], MLA_HEADS, HEAD_PAD - t.shape[3])], -1).reshape(DEPTH, t.shape[1], HEADS_W)
    top = jnp.concatenate([pad_heads(wu[..., :MLA_NOPE]), zeros(KV_LORA, HEADS_W),
                           pad_heads(wu[..., MLA_NOPE:])], axis=2)
    place = jnp.broadcast_to(jnp.asarray(_kv_place())[None], (DEPTH, 128, 3 * HEADS_W))
    wkv = jnp.concatenate([top, place], axis=1).astype(BF16)

    wuq = p["mla_w_uq"].reshape(DEPTH, Q_LORA, MLA_HEADS, MLA_QK)
    qp = jnp.concatenate([zeros(Q_LORA, MLA_HEADS, MLA_NOPE), wuq[..., MLA_NOPE:][..., perm],
                          zeros(Q_LORA, MLA_HEADS, 32)], -1).reshape(DEPTH, Q_LORA, HEADS_W)
    wq = jnp.concatenate([pad_heads(wuq), qp], axis=2).astype(BF16)

    def head_w(v):
        wf = jnp.concatenate([v, zeros(32)], -1)[:, None, :]
        wp = jnp.concatenate([zeros(MLA_NOPE), v[:, MLA_NOPE:][:, perm], zeros(32)], -1)[:, None, :]
        return wf, wp

    head_ws = jnp.concatenate(head_w(p["mla_qk_norm_k"]) + head_w(p["mla_qk_norm_q"]), axis=1)

    def b_compact(b):
        t = b.transpose(0, 1, 2, 4, 3).reshape(DEPTH, 2, BRANCH_W, S5_STATE)
        return jnp.concatenate([t, t], axis=-1)

    c_compact = lambda c: c.transpose(0, 1, 4, 2, 3).reshape(DEPTH, 2, S5_STATE, BRANCH_W)
    vec = lambda a: a.reshape(DEPTH, 2, 1, S5_LANES)

    logit = p["ret_decay_logit"]
    row = lambda a: a[:, None, :]
    return dict(
        norm_mix=row(p["norm_mix_w"]), norm_ffn=row(p["norm_ffn_w"]),
        kv_norm=row(p["mla_kv_norm"]), q_norm=row(p["mla_q_norm"]),
        wkv=wkv, wq=wq, head_w=head_ws,
        s5_lam_re=vec(p["s5_lam_re"]), s5_lam_im=vec(p["s5_lam_im"]),
        s5_log_step=vec(jnp.repeat(p["s5_log_step"], S5_STATE, axis=-1)),
        s5_bre=b_compact(p["s5_b_re"]), s5_bim=b_compact(p["s5_b_im"]),
        s5_cre=c_compact(p["s5_c_re"]), s5_cim=c_compact(p["s5_c_im"]),
        s5_d=row(p["s5_d"]), w_glu=p["s5_w_glu"].astype(BF16),
        lgl=jnp.repeat(logit, RET_HD, axis=-1).reshape(DEPTH, 2, 1, BRANCH_W),
        lgh=jnp.broadcast_to(logit[:, :, :, None, None], (DEPTH, 2, RET_HEADS, 1, 128)),
        gn_w=row(p["ret_gn_w"]),
        w_branch=p["w_branch"].astype(BF16), w_out=p["w_out"].astype(BF16))


def kernel(x, c, ctx, c_ctx, ada_w, ada_b, norm_mix_w, norm_ffn_w, w_in, mla_q_norm, mla_w_uq, mla_kv_norm,
           mla_w_ukv, mla_qk_norm_q, mla_qk_norm_k, s5_lam_re, s5_lam_im, s5_log_step, s5_b_re, s5_b_im,
           s5_c_re, s5_c_im, s5_d, s5_w_glu, ret_decay_logit, ret_gn_w, w_branch, w_out, ffn_w1, ffn_w2):
    p = dict(norm_mix_w=norm_mix_w, norm_ffn_w=norm_ffn_w, w_in=w_in, mla_q_norm=mla_q_norm,
             mla_w_uq=mla_w_uq, mla_kv_norm=mla_kv_norm, mla_w_ukv=mla_w_ukv, mla_qk_norm_q=mla_qk_norm_q,
             mla_qk_norm_k=mla_qk_norm_k, s5_lam_re=s5_lam_re, s5_lam_im=s5_lam_im, s5_log_step=s5_log_step,
             s5_b_re=s5_b_re, s5_b_im=s5_b_im, s5_c_re=s5_c_re, s5_c_im=s5_c_im, s5_d=s5_d,
             s5_w_glu=s5_w_glu, ret_decay_logit=ret_decay_logit, ret_gn_w=ret_gn_w, w_branch=w_branch,
             w_out=w_out, ffn_w1=ffn_w1, ffn_w2=ffn_w2)
    sw = _stacked_weights(p)
    c16 = jnp.concatenate([c, c_ctx[None, :], jnp.zeros((16 - BATCH - 1, D_MODEL), F32)], axis=0)
    mod = _ada_mod(c16, ada_w, ada_b)
    lam, bblk, cblk = _s5_params(sw)
    w_in_t32 = jnp.swapaxes(w_in, 1, 2)
    w_in_t = w_in_t32[0:1].astype(BF16)
    x_ctx, x_lat = ctx, x
    for l in range(DEPTH):
        last = l == DEPTH - 1
        q, k, v, u_t, uc, us, rq, rk, rv, rg, w1, w2 = _in_proj(
            l, x_ctx, x_lat, mod, sw, w_in_t, [(ffn_w1, l, 32), (ffn_w2, l, 32)])
        oa_lat = _attention_latent(q, k, v)
        oa_ctx = oa_lat if last else _attention_context(q, k, v)
        ob = _fnet(uc, us)
        yf, yb = _s5(l, u_t, lam, bblk, cblk)
        oret = _retention(l, rq, rk, rv, sw)
        xm, *w_in_next = _merge(l, x_ctx, x_lat, mod, sw, w_in_t, oa_ctx, oa_lat, ob, yf, yb, u_t, oret, rg, last,
                                [] if last else [(w_in_t32, l + 1, 18)])
        if not last:
            w_in_t, = w_in_next
        x_ctx = x_lat = _ffn(l, xm, mod, sw, w1, w2, last)
    return x_lat
```

```python
import functools
import math

import numpy as np
import jax
import jax.numpy as jnp
from jax import lax
from jax.experimental import pallas as pl
from jax.experimental.pallas import tpu as pltpu

F32 = jnp.float32
BF16 = jnp.bfloat16

D_MODEL = 1024
BATCH = 8
SEQ = 2048
DEPTH = 2
GRID_W = 64
CTX_LEN = 256
LCAT = CTX_LEN + SEQ
N_BRANCH = 4
BRANCH_W = 256
NORM_EPS = 1e-6
ROPE_BASE = 10000.0
ADA_CHUNKS = 6

MLA_HEADS = 4
MLA_NOPE = 64
MLA_ROPE = 32
MLA_QK = MLA_NOPE + MLA_ROPE
MLA_V = 64
Q_LORA = 256
KV_LORA = 128
HEAD_PAD = 128
HEADS_W = MLA_HEADS * HEAD_PAD

FNET_GROUPS = 4
FNET_GW = BRANCH_W // FNET_GROUPS

S5_GROUP_CH = 16
S5_GROUPS = BRANCH_W // S5_GROUP_CH
S5_STATE = 64
S5_LANES = S5_GROUPS * S5_STATE

RET_HEADS = 4
RET_HD = BRANCH_W // RET_HEADS
RET_CHUNK = 256
RET_NCHUNK = LCAT // RET_CHUNK
RET_CTX_CHUNKS = CTX_LEN // RET_CHUNK
RET_SPB = 4

D_FF = 4 * D_MODEL

_O_KV, _O_KR, _O_S5, _O_RK, _O_RV = 0, 128, 160, 416, 672
_O_Q, _O_FN, _O_RQ, _O_RG, _O_GATE = 928, 1184, 1440, 1696, 1952
IN_COLS = _O_GATE + N_BRANCH * D_MODEL

TT = 64
TR = TT * BATCH
FFN_TT = 128
FFN_PIECE = 2048
CTX_TILES = CTX_LEN // TT
ALL_TILES = LCAT // TT
TQ = 256
ATTN_QB = 4
ATTN_UNIT_QB = 2
ATTN_AHEAD = 1
S5_TC = 128
S5_ROWS = S5_TC * BATCH
S5_STEPS = LCAT // S5_TC
S5_CTX_STEPS = CTX_LEN // S5_TC
S5_SUB = 1

VMEM_LIMIT = 56 * 1024 * 1024


def _cparams(n_grid):
    return pltpu.CompilerParams(dimension_semantics=("arbitrary",) * n_grid,
                                vmem_limit_bytes=VMEM_LIMIT)


def _dot(a, b):
    return jnp.dot(a, b, preferred_element_type=F32)


def _dot_nt(a, b):
    return lax.dot_general(a, b, (((1,), (1,)), ((), ())), preferred_element_type=F32)


def _sigmoid(x):
    return 0.5 * (jnp.tanh(0.5 * x) + 1.0)


def _gelu_tanh(y):
    return 0.5 * y * (1.0 + jnp.tanh(math.sqrt(2.0 / math.pi) * (y + 0.044715 * (y * y * y))))


def _rms(x, w):
    return x * lax.rsqrt(jnp.mean(x * x, axis=-1, keepdims=True) + NORM_EPS) * w


def _full(shape):
    n = len(shape)
    return pl.BlockSpec(shape, lambda *_: (0,) * n)


def _layer(l, shape, single=False):
    n = len(shape)
    mode = dict(pipeline_mode=pl.Buffered(1)) if single else {}
    return pl.BlockSpec((1,) + tuple(shape), lambda *_: (l,) + (0,) * n, **mode)


def _mla_rope_perm():
    r = np.arange(MLA_ROPE)
    first = (r % 16) < 8
    return np.where(first, r + 8, r - 8), np.where(first, -1.0, 1.0)


@functools.lru_cache(maxsize=None)
def _mla_tables():
    pos = np.arange(SEQ)
    rows, cols = pos // GRID_W, pos % GRID_W
    freqs = ROPE_BASE ** (-np.arange(8, dtype=np.float64) / 8)
    r = np.arange(MLA_ROPE)
    _, sign = _mla_rope_perm()
    p = np.where((r // 16 == 0)[None, :], rows[:, None], cols[:, None]).astype(np.float64)
    ang = p * freqs[(r % 16) % 8][None, :]
    cosf = np.zeros((LCAT, HEAD_PAD))
    sinf = np.zeros((LCAT, HEAD_PAD))
    cosf[:, :MLA_QK] = 1.0
    cosf[CTX_LEN:, MLA_NOPE:MLA_QK] = np.cos(ang)
    sinf[CTX_LEN:, MLA_NOPE:MLA_QK] = np.sin(ang) * sign[None, :]
    return cosf.astype(np.float32), sinf.astype(np.float32)


def _ret_perm():
    d = np.arange(RET_HD)
    first = d < RET_HD // 2
    return np.where(first, d + RET_HD // 2, d - RET_HD // 2), np.where(first, -1.0, 1.0)


@functools.lru_cache(maxsize=None)
def _ret_tables():
    half = RET_HD // 2
    pos = np.arange(SEQ, dtype=np.float64)
    freqs = ROPE_BASE ** (-np.arange(half, dtype=np.float64) / half)
    d = np.arange(RET_HD)
    perm, sign = _ret_perm()
    ang = pos[:, None] * freqs[d % half][None, :]
    cosr = np.ones((LCAT, RET_HD))
    sinr = np.zeros((LCAT, RET_HD))
    cosr[CTX_LEN:] = np.cos(ang)
    sinr[CTX_LEN:] = np.sin(ang) * sign[None, :]
    cosr = np.tile(cosr, (1, RET_HEADS))
    sinr = np.tile(sinr, (1, RET_HEADS))
    pm = np.zeros((BRANCH_W, BRANCH_W))
    for h in range(RET_HEADS):
        pm[h * RET_HD + perm, h * RET_HD + d] = 1.0
    return cosr.astype(np.float32), sinr.astype(np.float32), pm.astype(np.float32)


def _dft(n, scale):
    k = np.arange(n)
    kt = (k[:, None] * k[None, :]) % n
    ang = 2.0 * np.pi * kt / n
    return np.cos(ang) * scale, np.sin(ang) * scale


@functools.lru_cache(maxsize=None)
def _fnet_tables():
    cw, sw = _dft(FNET_GW, 1.0)
    t = np.zeros((BRANCH_W, 2 * BRANCH_W))
    for g in range(FNET_GROUPS):
        s = slice(g * FNET_GW, (g + 1) * FNET_GW)
        t[s, s] = cw
        t[s, BRANCH_W + g * FNET_GW:BRANCH_W + (g + 1) * FNET_GW] = -sw
    cl, sl = _dft(SEQ, 1.0 / math.sqrt(SEQ * FNET_GW))
    clc, slc = _dft(CTX_LEN, 1.0 / math.sqrt(CTX_LEN * FNET_GW))
    return tuple(a.astype(np.float32) for a in (t, cl, sl, clc, slc))


def _bf16_const(a):
    return jnp.asarray(a).astype(BF16)


@functools.lru_cache(maxsize=None)
def _head_avg():
    p = np.zeros((BRANCH_W, BRANCH_W))
    for h in range(RET_HEADS):
        p[h * RET_HD:(h + 1) * RET_HD, h * RET_HD:(h + 1) * RET_HD] = 1.0 / RET_HD
    return p.astype(np.float32)


@functools.lru_cache(maxsize=None)
def _tile_perm():
    p = np.zeros((TR, TR), np.float32)
    for b in range(BATCH):
        for t in range(TT):
            p[t * BATCH + b, b * TT + t] = 1.0
    return p


@functools.lru_cache(maxsize=None)
def _kv_place():
    perm, _ = _mla_rope_perm()
    place = np.zeros((128, 3 * HEADS_W), np.float32)
    for h in range(MLA_HEADS):
        for r in range(MLA_ROPE):
            place[r, h * HEAD_PAD + MLA_NOPE + r] = 1.0
            place[perm[r], HEADS_W + h * HEAD_PAD + MLA_NOPE + r] = 1.0
    return place


@functools.lru_cache(maxsize=None)
def _v_ones():
    v = np.zeros((1, HEADS_W), np.float32)
    v[0, np.arange(MLA_HEADS) * HEAD_PAD + MLA_V] = 1.0
    return v


ADA_TN = 1536


def _ada_kernel(c_ref, w_ref, b_ref, o_ref):
    c = c_ref[...]
    s = (c * _sigmoid(c)).astype(BF16)
    o_ref[0] = _dot(s, w_ref[0].astype(BF16)) + b_ref[0]


def _ada_mod(c16, ada_w, ada_b):
    n = ADA_CHUNKS * D_MODEL
    return pl.pallas_call(
        _ada_kernel,
        grid=(DEPTH, n // ADA_TN),
        in_specs=[pl.BlockSpec((16, D_MODEL), lambda l, j: (0, 0)),
                  pl.BlockSpec((1, D_MODEL, ADA_TN), lambda l, j: (l, 0, j)),
                  pl.BlockSpec((1, 1, ADA_TN), lambda l, j: (l, 0, j))],
        out_specs=pl.BlockSpec((1, 16, ADA_TN), lambda l, j: (l, 0, j)),
        out_shape=jax.ShapeDtypeStruct((DEPTH, 16, n), F32),
        compiler_params=_cparams(2),
        name="ada_mod",
    )(c16, ada_w, ada_b.reshape(DEPTH, 1, n))


def _side_casts(jobs):
    specs_in, specs_out, shapes, args = [], [], [], []
    for a, l, n in jobs:
        _, r, c = a.shape
        blk = (1, r // n, c)
        specs_in.append(pl.BlockSpec(blk, lambda j, l=l, n=n: (l, jnp.minimum(j, n - 1), 0)))
        specs_out.append(pl.BlockSpec(blk, lambda j, n=n: (0, jnp.minimum(j, n - 1), 0)))
        shapes.append(jax.ShapeDtypeStruct((1, r, c), BF16))
        args.append(a)
    return specs_in, specs_out, shapes, args


def _with_side_casts(body, n_in, n_out, n_jobs):
    def wrapped(*refs):
        body(*refs[:n_in], *refs[n_in + n_jobs:n_in + n_jobs + n_out])
        for src, dst in zip(refs[n_in:n_in + n_jobs], refs[n_in + n_jobs + n_out:]):
            dst[...] = src[...].astype(BF16)
    return wrapped if n_jobs else body


def _mod_chunks(mod_ref, is_ctx, idxs):
    out = []
    for i in idxs:
        sl = slice(i * D_MODEL, (i + 1) * D_MODEL)
        m = mod_ref[0, 0:BATCH, sl]
        if is_ctx is not None:
            m = jnp.where(is_ctx, mod_ref[0, BATCH:BATCH + 1, sl], m)
        out.append(m[:, None, :])
    return out


def _tile_specs(separate, skip_ctx, width=D_MODEL):
    base = 0 if separate else CTX_TILES
    shape = (BATCH, TT, width)
    if skip_ctx:
        return (pl.BlockSpec(shape, lambda j: (0, 0, 0)),
                pl.BlockSpec(shape, lambda j: (0, j + base, 0)))
    return (pl.BlockSpec(shape, lambda j: (0, jnp.minimum(j, CTX_TILES - 1), 0)),
            pl.BlockSpec(shape, lambda j: (0, jnp.maximum(j - CTX_TILES, 0) + base, 0)))


def _head_norm_rot(xf, xp, a, b, scale):
    outs = []
    for h in range(MLA_HEADS):
        f = xf[:, h * HEAD_PAD:(h + 1) * HEAD_PAD]
        p = xp[:, h * HEAD_PAD:(h + 1) * HEAD_PAD]
        n = lax.rsqrt(jnp.sum(f * f, axis=-1, keepdims=True) * (1.0 / MLA_QK) + NORM_EPS) * scale
        outs.append(n * (f * a + p * b))
    return jnp.concatenate(outs, axis=-1)


def _rows(table):
    w = table.shape[-1]
    return jnp.broadcast_to(table[None], (BATCH, TT, w)).reshape(TR, w)


def _tile3(x):
    return x.reshape(BATCH, TT, x.shape[-1])


def _inproj_kernel(xc_ref, xl_ref, mod_ref, nw_ref, w_ref, kvw_ref, wkv_ref, qnw_ref, wq_ref,
                   cosf_ref, sinf_ref, hw_ref, pm_ref, cosr_ref, sinr_ref, t_ref, ptb_ref, vone_ref,
                   q_out, k_out, v_out, ut_out, uc_out, us_out, rq_out, rk_out, rv_out, rg_out):
    is_ctx = pl.program_id(0) < CTX_TILES
    x3 = jnp.where(is_ctx, xc_ref[...], xl_ref[...])
    sh, sc = _mod_chunks(mod_ref, is_ctx, (0, 1))
    h = (_rms(x3, nw_ref[0]) * (1.0 + sc) + sh).reshape(TR, D_MODEL).astype(BF16)
    za = _dot_nt(h, w_ref[0, 0:256, :])
    z = _dot_nt(h, w_ref[0, _O_S5:_O_GATE, :])

    cosf, sinf = _rows(cosf_ref[...]), _rows(sinf_ref[...])
    hw = hw_ref[0]
    kvn = _rms(za[:, 0:128], kvw_ref[0]).astype(BF16)
    lhs = jnp.concatenate([kvn, za[:, 128:256].astype(BF16)], axis=-1)
    kv = _dot(lhs, wkv_ref[0])
    k = _head_norm_rot(kv[:, 0:HEADS_W], kv[:, HEADS_W:2 * HEADS_W],
                       cosf * hw[0:1], sinf * hw[1:2], 1.0)
    k_out[...] = _tile3(k).astype(BF16)
    v_out[...] = _tile3(kv[:, 2 * HEADS_W:3 * HEADS_W] + vone_ref[...]).astype(BF16)
    qn = _rms(z[:, 768:1024], qnw_ref[0]).astype(BF16)
    qq = _dot(qn, wq_ref[0])
    q = _head_norm_rot(qq[:, 0:HEADS_W], qq[:, HEADS_W:2 * HEADS_W],
                       cosf * hw[2:3], sinf * hw[3:4], MLA_QK ** -0.5)
    q_out[...] = _tile3(q).astype(BF16)
    cosr, sinr = _rows(cosr_ref[...]), _rows(sinr_ref[...])
    rq = z[:, 1280:1536]
    rk = z[:, 256:512]
    rq = rq * cosr + _dot(rq.astype(BF16), pm_ref[...]) * sinr
    rk = rk * cosr + _dot(rk.astype(BF16), pm_ref[...]) * sinr
    rq_out[...] = _tile3(rq).astype(BF16)
    rk_out[...] = _tile3(rk * (RET_HD ** -0.5)).astype(BF16)
    rv_out[...] = _tile3(z[:, 512:768]).astype(BF16)
    rg_out[...] = _tile3(z[:, 1536:1792]).astype(BF16)
    ucs = _dot(z[:, 1024:1280].astype(BF16), t_ref[...])
    uc_out[...] = _tile3(ucs[:, 0:BRANCH_W]).astype(BF16)
    us_out[...] = _tile3(ucs[:, BRANCH_W:2 * BRANCH_W]).astype(BF16)
    ut_out[...] = _dot(ptb_ref[...], z[:, 0:256].astype(BF16)).astype(BF16)


def _in_proj(l, x_ctx, x_lat, mod, sw, w_in_t, jobs):
    ctx_spec, lat_spec = _tile_specs(l == 0, False)
    tok = lambda w: pl.BlockSpec((BATCH, TT, w), lambda j: (0, j, 0))
    tab = lambda w: pl.BlockSpec((TT, w), lambda j: (j, 0))
    cosf, sinf = _mla_tables()
    cosr, sinr, pm = _ret_tables()
    bshape = lambda w: jax.ShapeDtypeStruct((BATCH, LCAT, w), BF16)
    job_in, job_out, job_shapes, job_args = _side_casts(jobs)
    return pl.pallas_call(
        _with_side_casts(_inproj_kernel, 18, 10, len(jobs)),
        grid=(ALL_TILES,),
        in_specs=[ctx_spec, lat_spec,
                  _layer(l, (16, ADA_CHUNKS * D_MODEL)),
                  _layer(l, (1, D_MODEL)),
                  _layer(0, (_O_GATE, D_MODEL)),
                  _layer(l, (1, KV_LORA)),
                  _layer(l, (256, 3 * HEADS_W)),
                  _layer(l, (1, Q_LORA)),
                  _layer(l, (Q_LORA, 2 * HEADS_W)),
                  tab(HEAD_PAD), tab(HEAD_PAD),
                  _layer(l, (4, HEAD_PAD)),
                  _full((BRANCH_W, BRANCH_W)),
                  tab(BRANCH_W), tab(BRANCH_W),
                  _full((BRANCH_W, 2 * BRANCH_W)),
                  _full((TR, TR)),
                  _full((1, HEADS_W))] + job_in,
        out_specs=[tok(HEADS_W), tok(HEADS_W), tok(HEADS_W),
                   pl.BlockSpec((TR, BRANCH_W), lambda j: (j, 0)),
                   tok(256), tok(256), tok(256), tok(256), tok(256), tok(256)] + job_out,
        out_shape=[bshape(HEADS_W), bshape(HEADS_W), bshape(HEADS_W),
                   jax.ShapeDtypeStruct((LCAT * BATCH, BRANCH_W), BF16),
                   bshape(256), bshape(256), bshape(256), bshape(256), bshape(256), bshape(256)] + job_shapes,
        compiler_params=_cparams(1),
        name="in_proj",
    )(x_ctx, x_lat, mod, sw["norm_mix"], w_in_t, sw["kv_norm"], sw["wkv"], sw["q_norm"], sw["wq"],
      jnp.asarray(cosf), jnp.asarray(sinf), sw["head_w"], jnp.asarray(pm, dtype=BF16),
      jnp.asarray(cosr), jnp.asarray(sinr), _bf16_const(_fnet_tables()[0]),
      jnp.asarray(_tile_perm(), dtype=BF16), jnp.asarray(_v_ones()), *job_args)


def _attn_kernel(*refs):
    q_refs, (k_ref, v_ref, o_ref) = refs[:-3], refs[-3:]
    per = min(ATTN_UNIT_QB, len(q_refs))
    units = [(b, g, h, slice(h * HEAD_PAD, (h + 1) * HEAD_PAD))
             for b in range(k_ref.shape[0]) for g in range(len(q_refs) // per) for h in range(MLA_HEADS)]

    def scores(u):
        b, g, _, sl = u
        q = jnp.concatenate([q_refs[g * per + i][b, :, sl] for i in range(per)], axis=0)
        return _dot_nt(q, k_ref[b, :, sl])

    lane = lax.broadcasted_iota(jnp.int32, (1, HEAD_PAD), 1)
    pending = [scores(u) for u in units[:ATTN_AHEAD]]
    for i, (b, g, h, sl) in enumerate(units):
        if i + ATTN_AHEAD < len(units):
            pending.append(scores(units[i + ATTN_AHEAD]))
        s = pending.pop(0)
        p = jnp.exp((s - jnp.max(s, axis=-1, keepdims=True)).astype(BF16))
        oh = _dot(p, v_ref[b, :, sl])
        oh = oh * (1.0 / oh[:, MLA_V:MLA_V + 1])
        if h % 2 == 0:
            even = oh
        else:
            pair = jnp.where(lane < MLA_V, even, pltpu.roll(oh, MLA_V, axis=1))
            rows = slice(g * per * TQ, (g + 1) * per * TQ)
            o_ref[b, rows, (h // 2) * HEAD_PAD:(h // 2 + 1) * HEAD_PAD] = pair.astype(BF16)


def _attention_latent(q, k, v):
    first = CTX_LEN // TQ
    qspec = lambda r: pl.BlockSpec((1, TQ, HEADS_W), lambda b, j: (b, first + ATTN_QB * j + r, 0))
    kv = pl.BlockSpec((1, LCAT, HEADS_W), lambda b, j: (b, 0, 0))
    return pl.pallas_call(
        _attn_kernel,
        grid=(BATCH, SEQ // (ATTN_QB * TQ)),
        in_specs=[qspec(r) for r in range(ATTN_QB)] + [kv, kv],
        out_specs=pl.BlockSpec((1, ATTN_QB * TQ, BRANCH_W), lambda b, j: (b, j, 0)),
        out_shape=jax.ShapeDtypeStruct((BATCH, SEQ, BRANCH_W), BF16),
        compiler_params=_cparams(2),
        name="mla_attention",
    )(*([q] * ATTN_QB), k, v)


def _attention_context(q, k, v):
    blk = pl.BlockSpec((1, CTX_LEN, HEADS_W), lambda b: (b, 0, 0))
    return pl.pallas_call(
        _attn_kernel,
        grid=(BATCH,),
        in_specs=[blk, blk, blk],
        out_specs=pl.BlockSpec((1, CTX_LEN, BRANCH_W), lambda b: (b, 0, 0)),
        out_shape=jax.ShapeDtypeStruct((BATCH, CTX_LEN, BRANCH_W), BF16),
        compiler_params=_cparams(1),
        name="mla_attention_ctx",
    )(q, k, v)


def _fnet_kernel(uc_ref, us_ref, cl_ref, sl_ref, clc_ref, slc_ref, o_ref):
    o_ref[0, 0:CTX_LEN, :] = (_dot(clc_ref[...], uc_ref[0, 0:CTX_LEN, :])
                              + _dot(slc_ref[...], us_ref[0, 0:CTX_LEN, :])).astype(BF16)
    o_ref[0, CTX_LEN:LCAT, :] = (_dot(cl_ref[...], uc_ref[0, CTX_LEN:LCAT, :])
                                 + _dot(sl_ref[...], us_ref[0, CTX_LEN:LCAT, :])).astype(BF16)


def _fnet(uc, us):
    cl, sl, clc, slc = (_bf16_const(a) for a in _fnet_tables()[1:])
    tok = pl.BlockSpec((1, LCAT, BRANCH_W), lambda b: (b, 0, 0))
    return pl.pallas_call(
        _fnet_kernel,
        grid=(BATCH,),
        in_specs=[tok, tok, _full((SEQ, SEQ)), _full((SEQ, SEQ)),
                  _full((CTX_LEN, CTX_LEN)), _full((CTX_LEN, CTX_LEN))],
        out_specs=tok,
        out_shape=jax.ShapeDtypeStruct((BATCH, LCAT, BRANCH_W), BF16),
        compiler_params=_cparams(1),
        name="fnet_dft",
    )(uc, us, cl, sl, clc, slc)


def _s5_param_kernel(lr_ref, li_ref, ls_ref, bre_ref, bim_ref, cre_ref, cim_ref, lam_out, b_out, c_out):
    lr, li = lr_ref[0, 0], li_ref[0, 0]
    step = jnp.exp(ls_ref[0, 0])
    mag = jnp.exp(lr * step)
    lbr = mag * jnp.cos(li * step)
    lbi = mag * jnp.sin(li * step)
    den = 1.0 / (lr * lr + li * li)
    cr = ((lbr - 1.0) * lr + lbi * li) * den
    ci = (lbi * lr - (lbr - 1.0) * li) * den
    lam_out[0, 0, 0] = jnp.broadcast_to(lbr, (BATCH, S5_LANES))
    lam_out[0, 0, 1] = jnp.broadcast_to(lbi, (BATCH, S5_LANES))
    grp_b = (lax.broadcasted_iota(jnp.int32, (BRANCH_W, S5_LANES), 0) // S5_GROUP_CH
             == lax.broadcasted_iota(jnp.int32, (BRANCH_W, S5_LANES), 1) // S5_STATE)
    wide = lambda r: jnp.concatenate([r[0, 0]] * (S5_LANES // 128), axis=-1)
    bre = jnp.where(grp_b, wide(bre_ref), 0.0)
    bim = jnp.where(grp_b, wide(bim_ref), 0.0)
    b_out[0, 0, :, 0:S5_LANES] = (cr * bre - ci * bim).astype(BF16)
    b_out[0, 0, :, S5_LANES:2 * S5_LANES] = (cr * bim + ci * bre).astype(BF16)
    grp_c = (lax.broadcasted_iota(jnp.int32, (S5_LANES, BRANCH_W), 0) // S5_STATE
             == lax.broadcasted_iota(jnp.int32, (S5_LANES, BRANCH_W), 1) // S5_GROUP_CH)
    tall = lambda r: jnp.concatenate([r[0, 0]] * S5_GROUPS, axis=0)
    c_out[0, 0, 0:S5_LANES, :] = jnp.where(grp_c, tall(cre_ref), 0.0).astype(BF16)
    c_out[0, 0, S5_LANES:2 * S5_LANES, :] = jnp.where(grp_c, -tall(cim_ref), 0.0).astype(BF16)


def _s5_params(sw):
    spec = lambda *s: pl.BlockSpec((1, 1) + s, lambda l, d: (l, d) + (0,) * len(s))
    return pl.pallas_call(
        _s5_param_kernel,
        grid=(DEPTH, 2),
        in_specs=[spec(1, S5_LANES), spec(1, S5_LANES), spec(1, S5_LANES),
                  spec(BRANCH_W, 128), spec(BRANCH_W, 128), spec(S5_STATE, BRANCH_W), spec(S5_STATE, BRANCH_W)],
        out_specs=[spec(2, BATCH, S5_LANES), spec(BRANCH_W, 2 * S5_LANES), spec(2 * S5_LANES, BRANCH_W)],
        out_shape=[jax.ShapeDtypeStruct((DEPTH, 2, 2, BATCH, S5_LANES), F32),
                   jax.ShapeDtypeStruct((DEPTH, 2, BRANCH_W, 2 * S5_LANES), BF16),
                   jax.ShapeDtypeStruct((DEPTH, 2, 2 * S5_LANES, BRANCH_W), BF16)],
        compiler_params=_cparams(2),
        name="s5_discretise",
    )(sw["s5_lam_re"], sw["s5_lam_im"], sw["s5_log_step"], sw["s5_bre"], sw["s5_bim"], sw["s5_cre"], sw["s5_cim"])


def _s5_bwd_block(i):
    return jnp.where(i < S5_CTX_STEPS, S5_CTX_STEPS - 1 - i, S5_STEPS + S5_CTX_STEPS - 1 - i)


def _s5_kernel(uf_ref, ub_ref, lam_ref, b_ref, c_ref, yf_ref, yb_ref, xf_scr, xb_scr, st_scr):
    @pl.when(pl.program_id(0) == 0)
    def _():
        st_scr[...] = jnp.zeros_like(st_scr)

    sub_t = S5_TC // S5_SUB
    sub_rows = lambda k: pl.ds(k * sub_t * BATCH, sub_t * BATCH)
    order = (range(S5_SUB), range(S5_SUB - 1, -1, -1))
    dirs = ((uf_ref, xf_scr, yf_ref), (ub_ref, xb_scr, yb_ref))
    for d, (u_ref, x_scr, _) in enumerate(dirs):
        for k in order[d]:
            x_scr[sub_rows(k), :] = _dot(u_ref[sub_rows(k), :], b_ref[0, d])
    re, im = pl.ds(0, S5_LANES), pl.ds(S5_LANES, S5_LANES)
    for d, (_, x_scr, y_ref) in enumerate(dirs):
        xr, xi = st_scr[2 * d], st_scr[2 * d + 1]
        ar, ai = lam_ref[0, d, 0], lam_ref[0, d, 1]
        for k in order[d]:
            steps = range(k * sub_t, (k + 1) * sub_t)
            for t in (steps if d == 0 else reversed(steps)):
                rows = pl.ds(t * BATCH, BATCH)
                xr, xi = (ar * xr - ai * xi + x_scr[rows, re], ar * xi + ai * xr + x_scr[rows, im])
                x_scr[rows, re] = xr
                x_scr[rows, im] = xi
            y_ref[sub_rows(k), :] = _dot(x_scr[sub_rows(k), :].astype(BF16), c_ref[0, d]).astype(BF16)
        st_scr[2 * d], st_scr[2 * d + 1] = xr, xi


def _s5(l, u_t, lam, bblk, cblk):
    fwd = pl.BlockSpec((S5_ROWS, BRANCH_W), lambda i: (i, 0))
    bwd = pl.BlockSpec((S5_ROWS, BRANCH_W), lambda i: (_s5_bwd_block(i), 0))
    return pl.pallas_call(
        _s5_kernel,
        grid=(S5_STEPS,),
        in_specs=[fwd, bwd, _layer(l, (2, 2, BATCH, S5_LANES)),
                  _layer(l, (2, BRANCH_W, 2 * S5_LANES)), _layer(l, (2, 2 * S5_LANES, BRANCH_W))],
        out_specs=[fwd, bwd],
        out_shape=[jax.ShapeDtypeStruct((LCAT * BATCH, BRANCH_W), BF16)] * 2,
        scratch_shapes=[pltpu.VMEM((S5_ROWS, 2 * S5_LANES), F32),
                        pltpu.VMEM((S5_ROWS, 2 * S5_LANES), F32),
                        pltpu.VMEM((4, BATCH, S5_LANES), F32)],
        compiler_params=_cparams(1),
        name="s5_scan",
    )(u_t, u_t, lam, bblk, cblk)


def _log_sigmoid(x):
    return jnp.minimum(x, 0.0) - jnp.log(1.0 + jnp.exp(-jnp.abs(x)))


def _ret_kernel(q_ref, k_ref, v_ref, lgl_ref, lgh_ref, o_ref, dec_scr, sb_scr, sf_scr, sbc_scr):
    c_len = RET_CHUNK
    lgl = _log_sigmoid(lgl_ref[0])
    lgf, lgb = lgl[0], lgl[1]
    ti = lax.broadcasted_iota(jnp.int32, (c_len, BRANCH_W), 0).astype(F32)
    qdf = jnp.exp(lgf * (ti + 1.0))
    kdf = jnp.exp(lgf * (c_len - 1.0 - ti))
    qdb = jnp.exp(lgb * (c_len - ti))
    kdb = jnp.exp(lgb * ti)
    cdf = jnp.exp(lgf * float(c_len))
    cdb = jnp.exp(lgb * float(c_len))
    lane = lax.broadcasted_iota(jnp.int32, (1, BRANCH_W), 1)
    same_head = (lax.broadcasted_iota(jnp.int32, (BRANCH_W, BRANCH_W), 0) // RET_HD
                 == lax.broadcasted_iota(jnp.int32, (BRANCH_W, BRANCH_W), 1) // RET_HD)

    diff = (lax.broadcasted_iota(jnp.int32, (c_len, c_len), 0)
            - lax.broadcasted_iota(jnp.int32, (c_len, c_len), 1)).astype(F32)
    for h in range(RET_HEADS):
        gf = jnp.concatenate([_log_sigmoid(lgh_ref[0, 0, h])] * (c_len // 128), axis=-1)
        gb = jnp.concatenate([_log_sigmoid(lgh_ref[0, 1, h])] * (c_len // 128), axis=-1)
        dec_scr[h] = (jnp.where(diff >= 0, jnp.exp(gf * jnp.maximum(diff, 0.0)), 0.0)
                      + jnp.where(diff <= 0, jnp.exp(gb * jnp.maximum(-diff, 0.0)), 0.0))

    def chunk(ref, s, c):
        return ref[s, pl.ds(pl.multiple_of(c * c_len, c_len), c_len), :]

    def kv_outer(kd, v):
        s = lax.dot_general(kd.astype(BF16), v, (((0,), (0,)), ((), ())), preferred_element_type=F32)
        return jnp.where(same_head, s, 0.0)

    sbc_scr[...] = jnp.zeros_like(sbc_scr)

    def bwd(i, _):
        c = jnp.where(i < RET_CTX_CHUNKS, RET_CTX_CHUNKS - 1 - i, RET_NCHUNK + RET_CTX_CHUNKS - 1 - i)
        for s in range(RET_SPB):
            sb_scr[s, c] = sbc_scr[s].astype(BF16)
            k = chunk(k_ref, s, c).astype(F32)
            sbc_scr[s] = sbc_scr[s] * cdb + kv_outer(k * kdb, chunk(v_ref, s, c))
        return 0

    lax.fori_loop(0, RET_NCHUNK, bwd, 0)

    sf_scr[...] = jnp.zeros_like(sf_scr)

    def fwd(c, _):
        for s in range(RET_SPB):
            qb, kb, vb = chunk(q_ref, s, c), chunk(k_ref, s, c), chunk(v_ref, s, c)
            q = qb.astype(F32)
            o = (_dot((q * qdf).astype(BF16), sf_scr[s].astype(BF16))
                 + _dot((q * qdb).astype(BF16), sb_scr[s, c]))
            for h in range(RET_HEADS):
                hm = lane // RET_HD == h
                att = _dot_nt(jnp.where(hm, qb, jnp.zeros_like(qb)), kb)
                oh = _dot((att * dec_scr[h]).astype(BF16), vb)
                o = o + jnp.where(hm, oh, 0.0)
            o_ref[s, pl.ds(pl.multiple_of(c * c_len, c_len), c_len), :] = o.astype(BF16)
            sf_scr[s] = sf_scr[s] * cdf + kv_outer(kb.astype(F32) * kdf, vb)
        return 0

    lax.fori_loop(0, RET_NCHUNK, fwd, 0)


def _retention(l, rq, rk, rv, sw):
    tok = pl.BlockSpec((RET_SPB, LCAT, BRANCH_W), lambda b: (b, 0, 0))
    return pl.pallas_call(
        _ret_kernel,
        grid=(BATCH // RET_SPB,),
        in_specs=[tok, tok, tok, _layer(l, (2, 1, BRANCH_W)), _layer(l, (2, RET_HEADS, 1, 128))],
        out_specs=tok,
        out_shape=jax.ShapeDtypeStruct((BATCH, LCAT, BRANCH_W), BF16),
        scratch_shapes=[pltpu.VMEM((RET_HEADS, RET_CHUNK, RET_CHUNK), F32),
                        pltpu.VMEM((RET_SPB, RET_NCHUNK, BRANCH_W, BRANCH_W), BF16),
                        pltpu.VMEM((RET_SPB, BRANCH_W, BRANCH_W), F32),
                        pltpu.VMEM((RET_SPB, BRANCH_W, BRANCH_W), F32)],
        compiler_params=_cparams(1),
        name="retention",
    )(rq, rk, rv, sw["lgl"], sw["lgh"])


def _merge_kernel(xc_ref, xl_ref, mod_ref, nw_ref, wg_ref, oac_ref, oal_ref, ob_ref, yf_ref, yb_ref, ut_ref,
                  d_ref, wglu_ref, oret_ref, rg_ref, gnw_ref, pavg_ref, pbt_ref, wb_ref, wout_ref,
                  o_ref, *, skip_ctx):
    if skip_ctx:
        is_ctx = None
        x3 = xl_ref[...]
        oa = oal_ref[...]
    else:
        is_ctx = pl.program_id(0) < CTX_TILES
        x3 = jnp.where(is_ctx, xc_ref[...], xl_ref[...])
        oa = jnp.where(is_ctx, oac_ref[...], oal_ref[...])
    sh, sc, gate_res = _mod_chunks(mod_ref, is_ctx, (0, 1, 2))
    h = (_rms(x3, nw_ref[0]) * (1.0 + sc) + sh).reshape(TR, D_MODEL).astype(BF16)
    yt = yf_ref[...].astype(F32) + yb_ref[...].astype(F32) + d_ref[0] * ut_ref[...].astype(F32)
    y = _dot(pbt_ref[...], yt.astype(BF16))
    vg = _dot(_gelu_tanh(y).astype(BF16), wglu_ref[0])
    oc = vg[:, 0:BRANCH_W] * _sigmoid(vg[:, BRANCH_W:2 * BRANCH_W])
    o = oret_ref[...].reshape(TR, BRANCH_W)
    dl = o.astype(F32) - _dot(o, pavg_ref[...])
    var = _dot((dl * dl).astype(BF16), pavg_ref[...])
    g = rg_ref[...].reshape(TR, BRANCH_W).astype(F32)
    od = g * _sigmoid(g) * (dl * lax.rsqrt(var + NORM_EPS) * gnw_ref[0])
    branches = (oa.reshape(TR, BRANCH_W), ob_ref[...].reshape(TR, BRANCH_W), oc.astype(BF16), od.astype(BF16))
    acc = None
    for n, branch in enumerate(branches):
        gate = _dot_nt(h, wg_ref[0, _O_GATE + n * D_MODEL:_O_GATE + (n + 1) * D_MODEL, :])
        term = _sigmoid(gate) * _dot(branch, wb_ref[0, n])
        acc = term if acc is None else acc + term
    m = _dot(acc.astype(BF16), wout_ref[0])
    o_ref[...] = x3 + gate_res * m.reshape(BATCH, TT, D_MODEL)


def _merge(l, x_ctx, x_lat, mod, sw, w_in_t, oa_ctx, oa_lat, ob, yf, yb, u_t, oret, rg, skip_ctx, jobs):
    off = CTX_TILES if skip_ctx else 0
    nt = ALL_TILES - off
    ctx_spec, lat_spec = _tile_specs(l == 0, skip_ctx)
    oac_spec, oal_spec = _tile_specs(True, skip_ctx, BRANCH_W)
    tok = lambda w: pl.BlockSpec((BATCH, TT, w), lambda j: (0, j + off, 0))
    tmaj = pl.BlockSpec((TR, BRANCH_W), lambda j: (j + off, 0))
    job_in, job_out, job_shapes, job_args = _side_casts(jobs)
    return pl.pallas_call(
        _with_side_casts(functools.partial(_merge_kernel, skip_ctx=skip_ctx), 20, 1, len(jobs)),
        grid=(nt,),
        in_specs=[ctx_spec, lat_spec,
                  _layer(l, (16, ADA_CHUNKS * D_MODEL)),
                  _layer(l, (1, D_MODEL)),
                  _layer(0, (IN_COLS, D_MODEL), single=True),
                  oac_spec, oal_spec,
                  tok(BRANCH_W), tmaj, tmaj, tmaj,
                  _layer(l, (1, BRANCH_W)),
                  _layer(l, (BRANCH_W, 2 * BRANCH_W)),
                  tok(BRANCH_W), tok(BRANCH_W),
                  _layer(l, (1, BRANCH_W)),
                  _full((BRANCH_W, BRANCH_W)),
                  _full((TR, TR)),
                  _layer(l, (N_BRANCH, BRANCH_W, D_MODEL), single=True),
                  _layer(l, (D_MODEL, D_MODEL), single=True)] + job_in,
        out_specs=[pl.BlockSpec((BATCH, TT, D_MODEL), lambda j: (0, j, 0))] + job_out,
        out_shape=[jax.ShapeDtypeStruct((BATCH, nt * TT, D_MODEL), F32)] + job_shapes,
        compiler_params=_cparams(1),
        name="merge",
    )(x_ctx, x_lat, mod, sw["norm_mix"], w_in_t, oa_ctx, oa_lat, ob, yf, yb, u_t, sw["s5_d"], sw["w_glu"],
      oret, rg, sw["gn_w"], jnp.asarray(_head_avg(), dtype=BF16), jnp.asarray(_tile_perm().T, dtype=BF16),
      sw["w_branch"], sw["w_out"], *job_args)


def _ffn_kernel(x_ref, mod_ref, nw_ref, w1_ref, w2_ref, o_ref, *, skip_ctx):
    is_ctx = None if skip_ctx else pl.program_id(0) < CTX_LEN // FFN_TT
    sh, sc, gate_res = _mod_chunks(mod_ref, is_ctx, (3, 4, 5))
    x3 = x_ref[...]
    h = (_rms(x3, nw_ref[0]) * (1.0 + sc) + sh).reshape(BATCH * FFN_TT, D_MODEL).astype(BF16)
    f = None
    for c in range(0, D_FF, FFN_PIECE):
        a = jnp.maximum(_dot(h, w1_ref[0, :, c:c + FFN_PIECE]), 0.0)
        part = _dot((a * a).astype(BF16), w2_ref[0, c:c + FFN_PIECE, :])
        f = part if f is None else f + part
    o_ref[...] = x3 + gate_res * f.reshape(BATCH, FFN_TT, D_MODEL)


def _ffn(l, xm, mod, sw, w1, w2, skip_ctx):
    nt = xm.shape[1] // FFN_TT
    tok = pl.BlockSpec((BATCH, FFN_TT, D_MODEL), lambda j: (0, j, 0))
    return pl.pallas_call(
        functools.partial(_ffn_kernel, skip_ctx=skip_ctx),
        grid=(nt,),
        in_specs=[tok,
                  _layer(l, (16, ADA_CHUNKS * D_MODEL)),
                  _layer(l, (1, D_MODEL)),
                  _layer(0, (D_MODEL, D_FF), single=True),
                  _layer(0, (D_FF, D_MODEL), single=True)],
        out_specs=tok,
        out_shape=jax.ShapeDtypeStruct(xm.shape, F32),
        compiler_params=_cparams(1),
        name="ffn",
    )(xm, mod, sw["norm_ffn"], w1, w2)


def _stacked_weights(p):
    perm, _ = _mla_rope_perm()
    zeros = lambda *s: jnp.zeros((DEPTH,) + s, F32)
    wu = p["mla_w_ukv"].reshape(DEPTH, KV_LORA, MLA_HEADS, MLA_NOPE + MLA_V)
    pad_heads = lambda t: jnp.concatenate(
        [t, zeros(t.shape[1], MLA_HEADS, HEAD_PAD - t.shape[3])], -1).reshape(DEPTH, t.shape[1], HEADS_W)
    top = jnp.concatenate([pad_heads(wu[..., :MLA_NOPE]), zeros(KV_LORA, HEADS_W),
                           pad_heads(wu[..., MLA_NOPE:])], axis=2)
    place = jnp.broadcast_to(jnp.asarray(_kv_place())[None], (DEPTH, 128, 3 * HEADS_W))
    wkv = jnp.concatenate([top, place], axis=1).astype(BF16)

    wuq = p["mla_w_uq"].reshape(DEPTH, Q_LORA, MLA_HEADS, MLA_QK)
    qp = jnp.concatenate([zeros(Q_LORA, MLA_HEADS, MLA_NOPE), wuq[..., MLA_NOPE:][..., perm],
                          zeros(Q_LORA, MLA_HEADS, 32)], -1).reshape(DEPTH, Q_LORA, HEADS_W)
    wq = jnp.concatenate([pad_heads(wuq), qp], axis=2).astype(BF16)

    def head_w(v):
        wf = jnp.concatenate([v, zeros(32)], -1)[:, None, :]
        wp = jnp.concatenate([zeros(MLA_NOPE), v[:, MLA_NOPE:][:, perm], zeros(32)], -1)[:, None, :]
        return wf, wp

    head_ws = jnp.concatenate(head_w(p["mla_qk_norm_k"]) + head_w(p["mla_qk_norm_q"]), axis=1)

    def b_compact(b):
        t = b.transpose(0, 1, 2, 4, 3).reshape(DEPTH, 2, BRANCH_W, S5_STATE)
        return jnp.concatenate([t, t], axis=-1)

    c_compact = lambda c: c.transpose(0, 1, 4, 2, 3).reshape(DEPTH, 2, S5_STATE, BRANCH_W)
    vec = lambda a: a.reshape(DEPTH, 2, 1, S5_LANES)

    logit = p["ret_decay_logit"]
    row = lambda a: a[:, None, :]
    return dict(
        norm_mix=row(p["norm_mix_w"]), norm_ffn=row(p["norm_ffn_w"]),
        kv_norm=row(p["mla_kv_norm"]), q_norm=row(p["mla_q_norm"]),
        wkv=wkv, wq=wq, head_w=head_ws,
        s5_lam_re=vec(p["s5_lam_re"]), s5_lam_im=vec(p["s5_lam_im"]),
        s5_log_step=vec(jnp.repeat(p["s5_log_step"], S5_STATE, axis=-1)),
        s5_bre=b_compact(p["s5_b_re"]), s5_bim=b_compact(p["s5_b_im"]),
        s5_cre=c_compact(p["s5_c_re"]), s5_cim=c_compact(p["s5_c_im"]),
        s5_d=row(p["s5_d"]), w_glu=p["s5_w_glu"].astype(BF16),
        lgl=jnp.repeat(logit, RET_HD, axis=-1).reshape(DEPTH, 2, 1, BRANCH_W),
        lgh=jnp.broadcast_to(logit[:, :, :, None, None], (DEPTH, 2, RET_HEADS, 1, 128)),
        gn_w=row(p["ret_gn_w"]),
        w_branch=p["w_branch"].astype(BF16), w_out=p["w_out"].astype(BF16))


def kernel(x, c, ctx, c_ctx, ada_w, ada_b, norm_mix_w, norm_ffn_w, w_in, mla_q_norm, mla_w_uq, mla_kv_norm,
           mla_w_ukv, mla_qk_norm_q, mla_qk_norm_k, s5_lam_re, s5_lam_im, s5_log_step, s5_b_re, s5_b_im,
           s5_c_re, s5_c_im, s5_d, s5_w_glu, ret_decay_logit, ret_gn_w, w_branch, w_out, ffn_w1, ffn_w2):
    p = dict(norm_mix_w=norm_mix_w, norm_ffn_w=norm_ffn_w, w_in=w_in, mla_q_norm=mla_q_norm,
             mla_w_uq=mla_w_uq, mla_kv_norm=mla_kv_norm, mla_w_ukv=mla_w_ukv, mla_qk_norm_q=mla_qk_norm_q,
             mla_qk_norm_k=mla_qk_norm_k, s5_lam_re=s5_lam_re, s5_lam_im=s5_lam_im, s5_log_step=s5_log_step,
             s5_b_re=s5_b_re, s5_b_im=s5_b_im, s5_c_re=s5_c_re, s5_c_im=s5_c_im, s5_d=s5_d,
             s5_w_glu=s5_w_glu, ret_decay_logit=ret_decay_logit, ret_gn_w=ret_gn_w, w_branch=w_branch,
             w_out=w_out, ffn_w1=ffn_w1, ffn_w2=ffn_w2)
    sw = _stacked_weights(p)
    c16 = jnp.concatenate([c, c_ctx[None, :], jnp.zeros((16 - BATCH - 1, D_MODEL), F32)], axis=0)
    mod = _ada_mod(c16, ada_w, ada_b)
    lam, bblk, cblk = _s5_params(sw)
    w_in_t32 = jnp.swapaxes(w_in, 1, 2)
    w_in_t = w_in_t32[0:1].astype(BF16)
    x_ctx, x_lat = ctx, x
    for l in range(DEPTH):
        last = l == DEPTH - 1
        q, k, v, u_t, uc, us, rq, rk, rv, rg, w1, w2 = _in_proj(
            l, x_ctx, x_lat, mod, sw, w_in_t, [(ffn_w1, l, 32), (ffn_w2, l, 32)])
        oa_lat = _attention_latent(q, k, v)
        oa_ctx = oa_lat if last else _attention_context(q, k, v)
        ob = _fnet(uc, us)
        yf, yb = _s5(l, u_t, lam, bblk, cblk)
        oret = _retention(l, rq, rk, rv, sw)
        xm, *w_in_next = _merge(l, x_ctx, x_lat, mod, sw, w_in_t, oa_ctx, oa_lat, ob, yf, yb, u_t, oret, rg, last,
                                [] if last else [(w_in_t32, l + 1, 18)])
        if not last:
            w_in_t, = w_in_next
        x_ctx = x_lat = _ffn(l, xm, mod, sw, w1, w2, last)
    return x_lat
```

```python
import functools
import math

import numpy as np
import jax
import jax.numpy as jnp
from jax import lax
from jax.experimental import pallas as pl
from jax.experimental.pallas import tpu as pltpu

F32 = jnp.float32
BF16 = jnp.bfloat16

D_MODEL = 1024
BATCH = 8
SEQ = 2048
DEPTH = 2
GRID_W = 64
CTX_LEN = 256
LCAT = CTX_LEN + SEQ
N_BRANCH = 4
BRANCH_W = 256
NORM_EPS = 1e-6
ROPE_BASE = 10000.0
ADA_CHUNKS = 6

MLA_HEADS = 4
MLA_NOPE = 64
MLA_ROPE = 32
MLA_QK = MLA_NOPE + MLA_ROPE
MLA_V = 64
Q_LORA = 256
KV_LORA = 128
HEAD_PAD = 128
HEADS_W = MLA_HEADS * HEAD_PAD

FNET_GROUPS = 4
FNET_GW = BRANCH_W // FNET_GROUPS

S5_GROUP_CH = 16
S5_GROUPS = BRANCH_W // S5_GROUP_CH
S5_STATE = 64
S5_LANES = S5_GROUPS * S5_STATE

RET_HEADS = 4
RET_HD = BRANCH_W // RET_HEADS
RET_CHUNK = 256
RET_NCHUNK = LCAT // RET_CHUNK
RET_CTX_CHUNKS = CTX_LEN // RET_CHUNK
RET_SPB = 4

D_FF = 4 * D_MODEL

_O_KV, _O_KR, _O_S5, _O_RK, _O_RV = 0, 128, 160, 416, 672
_O_Q, _O_FN, _O_RQ, _O_RG, _O_GATE = 928, 1184, 1440, 1696, 1952
IN_COLS = _O_GATE + N_BRANCH * D_MODEL

TT = 64
TR = TT * BATCH
FFN_TT = 128
FFN_PIECE = 2048
CTX_TILES = CTX_LEN // TT
ALL_TILES = LCAT // TT
TQ = 256
ATTN_QB = 4
ATTN_UNIT_QB = 2
ATTN_AHEAD = 1
S5_TC = 128
S5_ROWS = S5_TC * BATCH
S5_STEPS = LCAT // S5_TC
S5_CTX_STEPS = CTX_LEN // S5_TC
S5_SUB = 1

VMEM_LIMIT = 56 * 1024 * 1024


def _cparams(n_grid):
    return pltpu.CompilerParams(dimension_semantics=("arbitrary",) * n_grid,
                                vmem_limit_bytes=VMEM_LIMIT)


def _dot(a, b):
    return jnp.dot(a, b, preferred_element_type=F32)


def _dot_nt(a, b):
    return lax.dot_general(a, b, (((1,), (1,)), ((), ())), preferred_element_type=F32)


def _sigmoid(x):
    return 0.5 * (jnp.tanh(0.5 * x) + 1.0)


def _gelu_tanh(y):
    return 0.5 * y * (1.0 + jnp.tanh(math.sqrt(2.0 / math.pi) * (y + 0.044715 * (y * y * y))))


def _rms(x, w):
    return x * lax.rsqrt(jnp.mean(x * x, axis=-1, keepdims=True) + NORM_EPS) * w


def _full(shape):
    n = len(shape)
    return pl.BlockSpec(shape, lambda *_: (0,) * n)


def _layer(l, shape, single=False):
    n = len(shape)
    mode = dict(pipeline_mode=pl.Buffered(1)) if single else {}
    return pl.BlockSpec((1,) + tuple(shape), lambda *_: (l,) + (0,) * n, **mode)


def _mla_rope_perm():
    r = np.arange(MLA_ROPE)
    first = (r % 16) < 8
    return np.where(first, r + 8, r - 8), np.where(first, -1.0, 1.0)


@functools.lru_cache(maxsize=None)
def _mla_tables():
    pos = np.arange(SEQ)
    rows, cols = pos // GRID_W, pos % GRID_W
    freqs = ROPE_BASE ** (-np.arange(8, dtype=np.float64) / 8)
    r = np.arange(MLA_ROPE)
    _, sign = _mla_rope_perm()
    p = np.where((r // 16 == 0)[None, :], rows[:, None], cols[:, None]).astype(np.float64)
    ang = p * freqs[(r % 16) % 8][None, :]
    cosf = np.zeros((LCAT, HEAD_PAD))
    sinf = np.zeros((LCAT, HEAD_PAD))
    cosf[:, :MLA_QK] = 1.0
    cosf[CTX_LEN:, MLA_NOPE:MLA_QK] = np.cos(ang)
    sinf[CTX_LEN:, MLA_NOPE:MLA_QK] = np.sin(ang) * sign[None, :]
    return cosf.astype(np.float32), sinf.astype(np.float32)


def _ret_perm():
    d = np.arange(RET_HD)
    first = d < RET_HD // 2
    return np.where(first, d + RET_HD // 2, d - RET_HD // 2), np.where(first, -1.0, 1.0)


@functools.lru_cache(maxsize=None)
def _ret_tables():
    half = RET_HD // 2
    pos = np.arange(SEQ, dtype=np.float64)
    freqs = ROPE_BASE ** (-np.arange(half, dtype=np.float64) / half)
    d = np.arange(RET_HD)
    perm, sign = _ret_perm()
    ang = pos[:, None] * freqs[d % half][None, :]
    cosr = np.ones((LCAT, RET_HD))
    sinr = np.zeros((LCAT, RET_HD))
    cosr[CTX_LEN:] = np.cos(ang)
    sinr[CTX_LEN:] = np.sin(ang) * sign[None, :]
    cosr = np.tile(cosr, (1, RET_HEADS))
    sinr = np.tile(sinr, (1, RET_HEADS))
    pm = np.zeros((BRANCH_W, BRANCH_W))
    for h in range(RET_HEADS):
        pm[h * RET_HD + perm, h * RET_HD + d] = 1.0
    return cosr.astype(np.float32), sinr.astype(np.float32), pm.astype(np.float32)


def _dft(n, scale):
    k = np.arange(n)
    kt = (k[:, None] * k[None, :]) % n
    ang = 2.0 * np.pi * kt / n
    return np.cos(ang) * scale, np.sin(ang) * scale


@functools.lru_cache(maxsize=None)
def _fnet_tables():
    cw, sw = _dft(FNET_GW, 1.0)
    t = np.zeros((BRANCH_W, 2 * BRANCH_W))
    for g in range(FNET_GROUPS):
        s = slice(g * FNET_GW, (g + 1) * FNET_GW)
        t[s, s] = cw
        t[s, BRANCH_W + g * FNET_GW:BRANCH_W + (g + 1) * FNET_GW] = -sw
    cl, sl = _dft(SEQ, 1.0 / math.sqrt(SEQ * FNET_GW))
    clc, slc = _dft(CTX_LEN, 1.0 / math.sqrt(CTX_LEN * FNET_GW))
    return tuple(a.astype(np.float32) for a in (t, cl, sl, clc, slc))


def _bf16_const(a):
    return jnp.asarray(a).astype(BF16)


@functools.lru_cache(maxsize=None)
def _head_avg():
    p = np.zeros((BRANCH_W, BRANCH_W))
    for h in range(RET_HEADS):
        p[h * RET_HD:(h + 1) * RET_HD, h * RET_HD:(h + 1) * RET_HD] = 1.0 / RET_HD
    return p.astype(np.float32)


@functools.lru_cache(maxsize=None)
def _tile_perm():
    p = np.zeros((TR, TR), np.float32)
    for b in range(BATCH):
        for t in range(TT):
            p[t * BATCH + b, b * TT + t] = 1.0
    return p


@functools.lru_cache(maxsize=None)
def _kv_place():
    perm, _ = _mla_rope_perm()
    place = np.zeros((128, 3 * HEADS_W), np.float32)
    for h in range(MLA_HEADS):
        for r in range(MLA_ROPE):
            place[r, h * HEAD_PAD + MLA_NOPE + r] = 1.0
            place[perm[r], HEADS_W + h * HEAD_PAD + MLA_NOPE + r] = 1.0
    return place


@functools.lru_cache(maxsize=None)
def _v_ones():
    v = np.zeros((1, HEADS_W), np.float32)
    v[0, np.arange(MLA_HEADS) * HEAD_PAD + MLA_V] = 1.0
    return v


ADA_TN = 1536


def _ada_kernel(c_ref, w_ref, b_ref, o_ref):
    c = c_ref[...]
    s = (c * _sigmoid(c)).astype(BF16)
    o_ref[0] = _dot(s, w_ref[0].astype(BF16)) + b_ref[0]


def _ada_mod(c16, ada_w, ada_b):
    n = ADA_CHUNKS * D_MODEL
    return pl.pallas_call(
        _ada_kernel,
        grid=(DEPTH, n // ADA_TN),
        in_specs=[pl.BlockSpec((16, D_MODEL), lambda l, j: (0, 0)),
                  pl.BlockSpec((1, D_MODEL, ADA_TN), lambda l, j: (l, 0, j)),
                  pl.BlockSpec((1, 1, ADA_TN), lambda l, j: (l, 0, j))],
        out_specs=pl.BlockSpec((1, 16, ADA_TN), lambda l, j: (l, 0, j)),
        out_shape=jax.ShapeDtypeStruct((DEPTH, 16, n), F32),
        compiler_params=_cparams(2),
        name="ada_mod",
    )(c16, ada_w, ada_b.reshape(DEPTH, 1, n))


def _side_casts(jobs):
    specs_in, specs_out, shapes, args = [], [], [], []
    for a, l, n in jobs:
        _, r, c = a.shape
        blk = (1, r // n, c)
        specs_in.append(pl.BlockSpec(blk, lambda j, l=l, n=n: (l, jnp.minimum(j, n - 1), 0)))
        specs_out.append(pl.BlockSpec(blk, lambda j, n=n: (0, jnp.minimum(j, n - 1), 0)))
        shapes.append(jax.ShapeDtypeStruct((1, r, c), BF16))
        args.append(a)
    return specs_in, specs_out, shapes, args


def _with_side_casts(body, n_in, n_out, n_jobs):
    def wrapped(*refs):
        body(*refs[:n_in], *refs[n_in + n_jobs:n_in + n_jobs + n_out])
        for src, dst in zip(refs[n_in:n_in + n_jobs], refs[n_in + n_jobs + n_out:]):
            dst[...] = src[...].astype(BF16)
    return wrapped if n_jobs else body


def _mod_chunks(mod_ref, is_ctx, idxs):
    out = []
    for i in idxs:
        sl = slice(i * D_MODEL, (i + 1) * D_MODEL)
        m = mod_ref[0, 0:BATCH, sl]
        if is_ctx is not None:
            m = jnp.where(is_ctx, mod_ref[0, BATCH:BATCH + 1, sl], m)
        out.append(m[:, None, :])
    return out


def _tile_specs(separate, skip_ctx, width=D_MODEL):
    base = 0 if separate else CTX_TILES
    shape = (BATCH, TT, width)
    if skip_ctx:
        return (pl.BlockSpec(shape, lambda j: (0, 0, 0)),
                pl.BlockSpec(shape, lambda j: (0, j + base, 0)))
    return (pl.BlockSpec(shape, lambda j: (0, jnp.minimum(j, CTX_TILES - 1), 0)),
            pl.BlockSpec(shape, lambda j: (0, jnp.maximum(j - CTX_TILES, 0) + base, 0)))


def _head_norm_rot(xf, xp, a, b, scale):
    outs = []
    for h in range(MLA_HEADS):
        f = xf[:, h * HEAD_PAD:(h + 1) * HEAD_PAD]
        p = xp[:, h * HEAD_PAD:(h + 1) * HEAD_PAD]
        n = lax.rsqrt(jnp.sum(f * f, axis=-1, keepdims=True) * (1.0 / MLA_QK) + NORM_EPS) * scale
        outs.append(n * (f * a + p * b))
    return jnp.concatenate(outs, axis=-1)


def _rows(table):
    w = table.shape[-1]
    return jnp.broadcast_to(table[None], (BATCH, TT, w)).reshape(TR, w)


def _tile3(x):
    return x.reshape(BATCH, TT, x.shape[-1])


def _inproj_kernel(xc_ref, xl_ref, mod_ref, nw_ref, w_ref, kvw_ref, wkv_ref, qnw_ref, wq_ref,
                   cosf_ref, sinf_ref, hw_ref, pm_ref, cosr_ref, sinr_ref, t_ref, ptb_ref, vone_ref,
                   q_out, k_out, v_out, ut_out, uc_out, us_out, rq_out, rk_out, rv_out, rg_out):
    is_ctx = pl.program_id(0) < CTX_TILES
    x3 = jnp.where(is_ctx, xc_ref[...], xl_ref[...])
    sh, sc = _mod_chunks(mod_ref, is_ctx, (0, 1))
    zas, zs = [], []
    for half in (slice(0, BATCH // 2), slice(BATCH // 2, BATCH)):
        hh = (_rms(x3[half], nw_ref[0]) * (1.0 + sc[half]) + sh[half]).reshape(TR // 2, D_MODEL).astype(BF16)
        zas.append(_dot_nt(hh, w_ref[0, 0:256, :]))
        zs.append(_dot_nt(hh, w_ref[0, _O_S5:_O_GATE, :]))
    za, z = jnp.concatenate(zas, axis=0), jnp.concatenate(zs, axis=0)

    cosf, sinf = _rows(cosf_ref[...]), _rows(sinf_ref[...])
    hw = hw_ref[0]
    kvn = _rms(za[:, 0:128], kvw_ref[0]).astype(BF16)
    lhs = jnp.concatenate([kvn, za[:, 128:256].astype(BF16)], axis=-1)
    kv = _dot(lhs, wkv_ref[0])
    k = _head_norm_rot(kv[:, 0:HEADS_W], kv[:, HEADS_W:2 * HEADS_W],
                       cosf * hw[0:1], sinf * hw[1:2], 1.0)
    k_out[...] = _tile3(k).astype(BF16)
    v_out[...] = _tile3(kv[:, 2 * HEADS_W:3 * HEADS_W] + vone_ref[...]).astype(BF16)
    qn = _rms(z[:, 768:1024], qnw_ref[0]).astype(BF16)
    qq = _dot(qn, wq_ref[0])
    q = _head_norm_rot(qq[:, 0:HEADS_W], qq[:, HEADS_W:2 * HEADS_W],
                       cosf * hw[2:3], sinf * hw[3:4], MLA_QK ** -0.5)
    q_out[...] = _tile3(q).astype(BF16)
    cosr, sinr = _rows(cosr_ref[...]), _rows(sinr_ref[...])
    rq = z[:, 1280:1536]
    rk = z[:, 256:512]
    rq = rq * cosr + _dot(rq.astype(BF16), pm_ref[...]) * sinr
    rk = rk * cosr + _dot(rk.astype(BF16), pm_ref[...]) * sinr
    rq_out[...] = _tile3(rq).astype(BF16)
    rk_out[...] = _tile3(rk * (RET_HD ** -0.5)).astype(BF16)
    rv_out[...] = _tile3(z[:, 512:768]).astype(BF16)
    rg_out[...] = _tile3(z[:, 1536:1792]).astype(BF16)
    ucs = _dot(z[:, 1024:1280].astype(BF16), t_ref[...])
    uc_out[...] = _tile3(ucs[:, 0:BRANCH_W]).astype(BF16)
    us_out[...] = _tile3(ucs[:, BRANCH_W:2 * BRANCH_W]).astype(BF16)
    ut_out[...] = _dot(ptb_ref[...], z[:, 0:256].astype(BF16)).astype(BF16)


def _in_proj(l, x_ctx, x_lat, mod, sw, w_in_t, jobs):
    ctx_spec, lat_spec = _tile_specs(l == 0, False)
    tok = lambda w: pl.BlockSpec((BATCH, TT, w), lambda j: (0, j, 0))
    tab = lambda w: pl.BlockSpec((TT, w), lambda j: (j, 0))
    cosf, sinf = _mla_tables()
    cosr, sinr, pm = _ret_tables()
    bshape = lambda w: jax.ShapeDtypeStruct((BATCH, LCAT, w), BF16)
    job_in, job_out, job_shapes, job_args = _side_casts(jobs)
    return pl.pallas_call(
        _with_side_casts(_inproj_kernel, 18, 10, len(jobs)),
        grid=(ALL_TILES,),
        in_specs=[ctx_spec, lat_spec,
                  _layer(l, (16, ADA_CHUNKS * D_MODEL)),
                  _layer(l, (1, D_MODEL)),
                  _layer(0, (_O_GATE, D_MODEL)),
                  _layer(l, (1, KV_LORA)),
                  _layer(l, (256, 3 * HEADS_W)),
                  _layer(l, (1, Q_LORA)),
                  _layer(l, (Q_LORA, 2 * HEADS_W)),
                  tab(HEAD_PAD), tab(HEAD_PAD),
                  _layer(l, (4, HEAD_PAD)),
                  _full((BRANCH_W, BRANCH_W)),
                  tab(BRANCH_W), tab(BRANCH_W),
                  _full((BRANCH_W, 2 * BRANCH_W)),
                  _full((TR, TR)),
                  _full((1, HEADS_W))] + job_in,
        out_specs=[tok(HEADS_W), tok(HEADS_W), tok(HEADS_W),
                   pl.BlockSpec((TR, BRANCH_W), lambda j: (j, 0)),
                   tok(256), tok(256), tok(256), tok(256), tok(256), tok(256)] + job_out,
        out_shape=[bshape(HEADS_W), bshape(HEADS_W), bshape(HEADS_W),
                   jax.ShapeDtypeStruct((LCAT * BATCH, BRANCH_W), BF16),
                   bshape(256), bshape(256), bshape(256), bshape(256), bshape(256), bshape(256)] + job_shapes,
        compiler_params=_cparams(1),
        name="in_proj",
    )(x_ctx, x_lat, mod, sw["norm_mix"], w_in_t, sw["kv_norm"], sw["wkv"], sw["q_norm"], sw["wq"],
      jnp.asarray(cosf), jnp.asarray(sinf), sw["head_w"], jnp.asarray(pm, dtype=BF16),
      jnp.asarray(cosr), jnp.asarray(sinr), _bf16_const(_fnet_tables()[0]),
      jnp.asarray(_tile_perm(), dtype=BF16), jnp.asarray(_v_ones()), *job_args)


def _attn_kernel(*refs):
    q_refs, (k_ref, v_ref, o_ref) = refs[:-3], refs[-3:]
    per = min(ATTN_UNIT_QB, len(q_refs))
    units = [(b, g, h, slice(h * HEAD_PAD, (h + 1) * HEAD_PAD))
             for b in range(k_ref.shape[0]) for g in range(len(q_refs) // per) for h in range(MLA_HEADS)]

    def scores(u):
        b, g, _, sl = u
        q = jnp.concatenate([q_refs[g * per + i][b, :, sl] for i in range(per)], axis=0)
        return _dot_nt(q, k_ref[b, :, sl])

    lane = lax.broadcasted_iota(jnp.int32, (1, HEAD_PAD), 1)
    pending = [scores(u) for u in units[:ATTN_AHEAD]]
    for i, (b, g, h, sl) in enumerate(units):
        if i + ATTN_AHEAD < len(units):
            pending.append(scores(units[i + ATTN_AHEAD]))
        s = pending.pop(0)
        p = jnp.exp((s - jnp.max(s, axis=-1, keepdims=True)).astype(BF16))
        oh = _dot(p, v_ref[b, :, sl])
        oh = oh * (1.0 / oh[:, MLA_V:MLA_V + 1])
        if h % 2 == 0:
            even = oh
        else:
            pair = jnp.where(lane < MLA_V, even, pltpu.roll(oh, MLA_V, axis=1))
            rows = slice(g * per * TQ, (g + 1) * per * TQ)
            o_ref[b, rows, (h // 2) * HEAD_PAD:(h // 2 + 1) * HEAD_PAD] = pair.astype(BF16)


def _attention_latent(q, k, v):
    first = CTX_LEN // TQ
    qspec = lambda r: pl.BlockSpec((1, TQ, HEADS_W), lambda b, j: (b, first + ATTN_QB * j + r, 0))
    kv = pl.BlockSpec((1, LCAT, HEADS_W), lambda b, j: (b, 0, 0))
    return pl.pallas_call(
        _attn_kernel,
        grid=(BATCH, SEQ // (ATTN_QB * TQ)),
        in_specs=[qspec(r) for r in range(ATTN_QB)] + [kv, kv],
        out_specs=pl.BlockSpec((1, ATTN_QB * TQ, BRANCH_W), lambda b, j: (b, j, 0)),
        out_shape=jax.ShapeDtypeStruct((BATCH, SEQ, BRANCH_W), BF16),
        compiler_params=_cparams(2),
        name="mla_attention",
    )(*([q] * ATTN_QB), k, v)


def _attention_context(q, k, v):
    blk = pl.BlockSpec((1, CTX_LEN, HEADS_W), lambda b: (b, 0, 0))
    return pl.pallas_call(
        _attn_kernel,
        grid=(BATCH,),
        in_specs=[blk, blk, blk],
        out_specs=pl.BlockSpec((1, CTX_LEN, BRANCH_W), lambda b: (b, 0, 0)),
        out_shape=jax.ShapeDtypeStruct((BATCH, CTX_LEN, BRANCH_W), BF16),
        compiler_params=_cparams(1),
        name="mla_attention_ctx",
    )(q, k, v)


def _fnet_kernel(uc_ref, us_ref, cl_ref, sl_ref, clc_ref, slc_ref, o_ref):
    o_ref[0, 0:CTX_LEN, :] = (_dot(clc_ref[...], uc_ref[0, 0:CTX_LEN, :])
                              + _dot(slc_ref[...], us_ref[0, 0:CTX_LEN, :])).astype(BF16)
    o_ref[0, CTX_LEN:LCAT, :] = (_dot(cl_ref[...], uc_ref[0, CTX_LEN:LCAT, :])
                                 + _dot(sl_ref[...], us_ref[0, CTX_LEN:LCAT, :])).astype(BF16)


def _fnet(uc, us):
    cl, sl, clc, slc = (_bf16_const(a) for a in _fnet_tables()[1:])
    tok = pl.BlockSpec((1, LCAT, BRANCH_W), lambda b: (b, 0, 0))
    return pl.pallas_call(
        _fnet_kernel,
        grid=(BATCH,),
        in_specs=[tok, tok, _full((SEQ, SEQ)), _full((SEQ, SEQ)),
                  _full((CTX_LEN, CTX_LEN)), _full((CTX_LEN, CTX_LEN))],
        out_specs=tok,
        out_shape=jax.ShapeDtypeStruct((BATCH, LCAT, BRANCH_W), BF16),
        compiler_params=_cparams(1),
        name="fnet_dft",
    )(uc, us, cl, sl, clc, slc)


def _s5_param_kernel(lr_ref, li_ref, ls_ref, bre_ref, bim_ref, cre_ref, cim_ref, lam_out, b_out, c_out):
    lr, li = lr_ref[0, 0], li_ref[0, 0]
    step = jnp.exp(ls_ref[0, 0])
    mag = jnp.exp(lr * step)
    lbr = mag * jnp.cos(li * step)
    lbi = mag * jnp.sin(li * step)
    den = 1.0 / (lr * lr + li * li)
    cr = ((lbr - 1.0) * lr + lbi * li) * den
    ci = (lbi * lr - (lbr - 1.0) * li) * den
    lam_out[0, 0, 0] = jnp.broadcast_to(lbr, (BATCH, S5_LANES))
    lam_out[0, 0, 1] = jnp.broadcast_to(lbi, (BATCH, S5_LANES))
    grp_b = (lax.broadcasted_iota(jnp.int32, (BRANCH_W, S5_LANES), 0) // S5_GROUP_CH
             == lax.broadcasted_iota(jnp.int32, (BRANCH_W, S5_LANES), 1) // S5_STATE)
    wide = lambda r: jnp.concatenate([r[0, 0]] * (S5_LANES // 128), axis=-1)
    bre = jnp.where(grp_b, wide(bre_ref), 0.0)
    bim = jnp.where(grp_b, wide(bim_ref), 0.0)
    b_out[0, 0, :, 0:S5_LANES] = (cr * bre - ci * bim).astype(BF16)
    b_out[0, 0, :, S5_LANES:2 * S5_LANES] = (cr * bim + ci * bre).astype(BF16)
    grp_c = (lax.broadcasted_iota(jnp.int32, (S5_LANES, BRANCH_W), 0) // S5_STATE
             == lax.broadcasted_iota(jnp.int32, (S5_LANES, BRANCH_W), 1) // S5_GROUP_CH)
    tall = lambda r: jnp.concatenate([r[0, 0]] * S5_GROUPS, axis=0)
    c_out[0, 0, 0:S5_LANES, :] = jnp.where(grp_c, tall(cre_ref), 0.0).astype(BF16)
    c_out[0, 0, S5_LANES:2 * S5_LANES, :] = jnp.where(grp_c, -tall(cim_ref), 0.0).astype(BF16)


def _s5_params(sw):
    spec = lambda *s: pl.BlockSpec((1, 1) + s, lambda l, d: (l, d) + (0,) * len(s))
    return pl.pallas_call(
        _s5_param_kernel,
        grid=(DEPTH, 2),
        in_specs=[spec(1, S5_LANES), spec(1, S5_LANES), spec(1, S5_LANES),
                  spec(BRANCH_W, 128), spec(BRANCH_W, 128), spec(S5_STATE, BRANCH_W), spec(S5_STATE, BRANCH_W)],
        out_specs=[spec(2, BATCH, S5_LANES), spec(BRANCH_W, 2 * S5_LANES), spec(2 * S5_LANES, BRANCH_W)],
        out_shape=[jax.ShapeDtypeStruct((DEPTH, 2, 2, BATCH, S5_LANES), F32),
                   jax.ShapeDtypeStruct((DEPTH, 2, BRANCH_W, 2 * S5_LANES), BF16),
                   jax.ShapeDtypeStruct((DEPTH, 2, 2 * S5_LANES, BRANCH_W), BF16)],
        compiler_params=_cparams(2),
        name="s5_discretise",
    )(sw["s5_lam_re"], sw["s5_lam_im"], sw["s5_log_step"], sw["s5_bre"], sw["s5_bim"], sw["s5_cre"], sw["s5_cim"])


def _s5_bwd_block(i):
    return jnp.where(i < S5_CTX_STEPS, S5_CTX_STEPS - 1 - i, S5_STEPS + S5_CTX_STEPS - 1 - i)


def _s5_kernel(uf_ref, ub_ref, lam_ref, b_ref, c_ref, yf_ref, yb_ref, xf_scr, xb_scr, st_scr):
    @pl.when(pl.program_id(0) == 0)
    def _():
        st_scr[...] = jnp.zeros_like(st_scr)

    sub_t = S5_TC // S5_SUB
    sub_rows = lambda k: pl.ds(k * sub_t * BATCH, sub_t * BATCH)
    order = (range(S5_SUB), range(S5_SUB - 1, -1, -1))
    dirs = ((uf_ref, xf_scr, yf_ref), (ub_ref, xb_scr, yb_ref))
    for d, (u_ref, x_scr, _) in enumerate(dirs):
        for k in order[d]:
            x_scr[sub_rows(k), :] = _dot(u_ref[sub_rows(k), :], b_ref[0, d])
    re, im = pl.ds(0, S5_LANES), pl.ds(S5_LANES, S5_LANES)
    for d, (_, x_scr, y_ref) in enumerate(dirs):
        xr, xi = st_scr[2 * d], st_scr[2 * d + 1]
        ar, ai = lam_ref[0, d, 0], lam_ref[0, d, 1]
        for k in order[d]:
            steps = range(k * sub_t, (k + 1) * sub_t)
            for t in (steps if d == 0 else reversed(steps)):
                rows = pl.ds(t * BATCH, BATCH)
                xr, xi = (ar * xr - ai * xi + x_scr[rows, re], ar * xi + ai * xr + x_scr[rows, im])
                x_scr[rows, re] = xr
                x_scr[rows, im] = xi
            y_ref[sub_rows(k), :] = _dot(x_scr[sub_rows(k), :].astype(BF16), c_ref[0, d]).astype(BF16)
        st_scr[2 * d], st_scr[2 * d + 1] = xr, xi


def _s5(l, u_t, lam, bblk, cblk):
    fwd = pl.BlockSpec((S5_ROWS, BRANCH_W), lambda i: (i, 0))
    bwd = pl.BlockSpec((S5_ROWS, BRANCH_W), lambda i: (_s5_bwd_block(i), 0))
    return pl.pallas_call(
        _s5_kernel,
        grid=(S5_STEPS,),
        in_specs=[fwd, bwd, _layer(l, (2, 2, BATCH, S5_LANES)),
                  _layer(l, (2, BRANCH_W, 2 * S5_LANES)), _layer(l, (2, 2 * S5_LANES, BRANCH_W))],
        out_specs=[fwd, bwd],
        out_shape=[jax.ShapeDtypeStruct((LCAT * BATCH, BRANCH_W), BF16)] * 2,
        scratch_shapes=[pltpu.VMEM((S5_ROWS, 2 * S5_LANES), F32),
                        pltpu.VMEM((S5_ROWS, 2 * S5_LANES), F32),
                        pltpu.VMEM((4, BATCH, S5_LANES), F32)],
        compiler_params=_cparams(1),
        name="s5_scan",
    )(u_t, u_t, lam, bblk, cblk)


def _log_sigmoid(x):
    return jnp.minimum(x, 0.0) - jnp.log(1.0 + jnp.exp(-jnp.abs(x)))


def _ret_kernel(q_ref, k_ref, v_ref, lgl_ref, lgh_ref, o_ref, dec_scr, sb_scr, sf_scr, sbc_scr):
    c_len = RET_CHUNK
    lgl = _log_sigmoid(lgl_ref[0])
    lgf, lgb = lgl[0], lgl[1]
    ti = lax.broadcasted_iota(jnp.int32, (c_len, BRANCH_W), 0).astype(F32)
    qdf = jnp.exp(lgf * (ti + 1.0))
    kdf = jnp.exp(lgf * (c_len - 1.0 - ti))
    qdb = jnp.exp(lgb * (c_len - ti))
    kdb = jnp.exp(lgb * ti)
    cdf = jnp.exp(lgf * float(c_len))
    cdb = jnp.exp(lgb * float(c_len))
    lane = lax.broadcasted_iota(jnp.int32, (1, BRANCH_W), 1)
    same_head = (lax.broadcasted_iota(jnp.int32, (BRANCH_W, BRANCH_W), 0) // RET_HD
                 == lax.broadcasted_iota(jnp.int32, (BRANCH_W, BRANCH_W), 1) // RET_HD)

    diff = (lax.broadcasted_iota(jnp.int32, (c_len, c_len), 0)
            - lax.broadcasted_iota(jnp.int32, (c_len, c_len), 1)).astype(F32)
    for h in range(RET_HEADS):
        gf = jnp.concatenate([_log_sigmoid(lgh_ref[0, 0, h])] * (c_len // 128), axis=-1)
        gb = jnp.concatenate([_log_sigmoid(lgh_ref[0, 1, h])] * (c_len // 128), axis=-1)
        dec_scr[h] = (jnp.where(diff >= 0, jnp.exp(gf * jnp.maximum(diff, 0.0)), 0.0)
                      + jnp.where(diff <= 0, jnp.exp(gb * jnp.maximum(-diff, 0.0)), 0.0))

    def chunk(ref, s, c):
        return ref[s, pl.ds(pl.multiple_of(c * c_len, c_len), c_len), :]

    def kv_outer(kd, v):
        s = lax.dot_general(kd.astype(BF16), v, (((0,), (0,)), ((), ())), preferred_element_type=F32)
        return jnp.where(same_head, s, 0.0)

    sbc_scr[...] = jnp.zeros_like(sbc_scr)

    def bwd(i, _):
        c = jnp.where(i < RET_CTX_CHUNKS, RET_CTX_CHUNKS - 1 - i, RET_NCHUNK + RET_CTX_CHUNKS - 1 - i)
        for s in range(RET_SPB):
            sb_scr[s, c] = sbc_scr[s].astype(BF16)
            k = chunk(k_ref, s, c).astype(F32)
            sbc_scr[s] = sbc_scr[s] * cdb + kv_outer(k * kdb, chunk(v_ref, s, c))
        return 0

    lax.fori_loop(0, RET_NCHUNK, bwd, 0)

    sf_scr[...] = jnp.zeros_like(sf_scr)

    def fwd(c, _):
        for s in range(RET_SPB):
            qb, kb, vb = chunk(q_ref, s, c), chunk(k_ref, s, c), chunk(v_ref, s, c)
            q = qb.astype(F32)
            o = (_dot((q * qdf).astype(BF16), sf_scr[s].astype(BF16))
                 + _dot((q * qdb).astype(BF16), sb_scr[s, c]))
            for h in range(RET_HEADS):
                hm = lane // RET_HD == h
                att = _dot_nt(jnp.where(hm, qb, jnp.zeros_like(qb)), kb)
                oh = _dot((att * dec_scr[h]).astype(BF16), vb)
                o = o + jnp.where(hm, oh, 0.0)
            o_ref[s, pl.ds(pl.multiple_of(c * c_len, c_len), c_len), :] = o.astype(BF16)
            sf_scr[s] = sf_scr[s] * cdf + kv_outer(kb.astype(F32) * kdf, vb)
        return 0

    lax.fori_loop(0, RET_NCHUNK, fwd, 0)


def _retention(l, rq, rk, rv, sw):
    tok = pl.BlockSpec((RET_SPB, LCAT, BRANCH_W), lambda b: (b, 0, 0))
    return pl.pallas_call(
        _ret_kernel,
        grid=(BATCH // RET_SPB,),
        in_specs=[tok, tok, tok, _layer(l, (2, 1, BRANCH_W)), _layer(l, (2, RET_HEADS, 1, 128))],
        out_specs=tok,
        out_shape=jax.ShapeDtypeStruct((BATCH, LCAT, BRANCH_W), BF16),
        scratch_shapes=[pltpu.VMEM((RET_HEADS, RET_CHUNK, RET_CHUNK), F32),
                        pltpu.VMEM((RET_SPB, RET_NCHUNK, BRANCH_W, BRANCH_W), BF16),
                        pltpu.VMEM((RET_SPB, BRANCH_W, BRANCH_W), F32),
                        pltpu.VMEM((RET_SPB, BRANCH_W, BRANCH_W), F32)],
        compiler_params=_cparams(1),
        name="retention",
    )(rq, rk, rv, sw["lgl"], sw["lgh"])


def _merge_kernel(xc_ref, xl_ref, mod_ref, nw_ref, wg_ref, oac_ref, oal_ref, ob_ref, yf_ref, yb_ref, ut_ref,
                  d_ref, wglu_ref, oret_ref, rg_ref, gnw_ref, pavg_ref, pbt_ref, wb_ref, wout_ref,
                  o_ref, *, skip_ctx):
    if skip_ctx:
        is_ctx = None
        x3 = xl_ref[...]
        oa = oal_ref[...]
    else:
        is_ctx = pl.program_id(0) < CTX_TILES
        x3 = jnp.where(is_ctx, xc_ref[...], xl_ref[...])
        oa = jnp.where(is_ctx, oac_ref[...], oal_ref[...])
    sh, sc, gate_res = _mod_chunks(mod_ref, is_ctx, (0, 1, 2))
    gate_w = lambda n: wg_ref[0, _O_GATE + n * D_MODEL:_O_GATE + (n + 1) * D_MODEL, :]
    hs = [(_rms(x3[half], nw_ref[0]) * (1.0 + sc[half]) + sh[half]).reshape(TR // 2, D_MODEL).astype(BF16)
          for half in (slice(0, BATCH // 2), slice(BATCH // 2, BATCH))]
    gate0 = jnp.concatenate([_dot_nt(hh, gate_w(0)) for hh in hs], axis=0)
    h = jnp.concatenate(hs, axis=0)
    yt = yf_ref[...].astype(F32) + yb_ref[...].astype(F32) + d_ref[0] * ut_ref[...].astype(F32)
    y = _dot(pbt_ref[...], yt.astype(BF16))
    vg = _dot(_gelu_tanh(y).astype(BF16), wglu_ref[0])
    oc = vg[:, 0:BRANCH_W] * _sigmoid(vg[:, BRANCH_W:2 * BRANCH_W])
    o = oret_ref[...].reshape(TR, BRANCH_W)
    dl = o.astype(F32) - _dot(o, pavg_ref[...])
    var = _dot((dl * dl).astype(BF16), pavg_ref[...])
    g = rg_ref[...].reshape(TR, BRANCH_W).astype(F32)
    od = g * _sigmoid(g) * (dl * lax.rsqrt(var + NORM_EPS) * gnw_ref[0])
    branches = (oa.reshape(TR, BRANCH_W), ob_ref[...].reshape(TR, BRANCH_W), oc.astype(BF16), od.astype(BF16))
    acc = None
    for n, branch in enumerate(branches):
        gate = gate0 if n == 0 else _dot_nt(h, gate_w(n))
        term = _sigmoid(gate) * _dot(branch, wb_ref[0, n])
        acc = term if acc is None else acc + term
    m = _dot(acc.astype(BF16), wout_ref[0])
    o_ref[...] = x3 + gate_res * m.reshape(BATCH, TT, D_MODEL)


def _merge(l, x_ctx, x_lat, mod, sw, w_in_t, oa_ctx, oa_lat, ob, yf, yb, u_t, oret, rg, skip_ctx, jobs):
    off = CTX_TILES if skip_ctx else 0
    nt = ALL_TILES - off
    ctx_spec, lat_spec = _tile_specs(l == 0, skip_ctx)
    oac_spec, oal_spec = _tile_specs(True, skip_ctx, BRANCH_W)
    tok = lambda w: pl.BlockSpec((BATCH, TT, w), lambda j: (0, j + off, 0))
    tmaj = pl.BlockSpec((TR, BRANCH_W), lambda j: (j + off, 0))
    job_in, job_out, job_shapes, job_args = _side_casts(jobs)
    return pl.pallas_call(
        _with_side_casts(functools.partial(_merge_kernel, skip_ctx=skip_ctx), 20, 1, len(jobs)),
        grid=(nt,),
        in_specs=[ctx_spec, lat_spec,
                  _layer(l, (16, ADA_CHUNKS * D_MODEL)),
                  _layer(l, (1, D_MODEL)),
                  _layer(0, (IN_COLS, D_MODEL), single=True),
                  oac_spec, oal_spec,
                  tok(BRANCH_W), tmaj, tmaj, tmaj,
                  _layer(l, (1, BRANCH_W)),
                  _layer(l, (BRANCH_W, 2 * BRANCH_W)),
                  tok(BRANCH_W), tok(BRANCH_W),
                  _layer(l, (1, BRANCH_W)),
                  _full((BRANCH_W, BRANCH_W)),
                  _full((TR, TR)),
                  _layer(l, (N_BRANCH, BRANCH_W, D_MODEL), single=True),
                  _layer(l, (D_MODEL, D_MODEL), single=True)] + job_in,
        out_specs=[pl.BlockSpec((BATCH, TT, D_MODEL), lambda j: (0, j, 0))] + job_out,
        out_shape=[jax.ShapeDtypeStruct((BATCH, nt * TT, D_MODEL), F32)] + job_shapes,
        compiler_params=_cparams(1),
        name="merge",
    )(x_ctx, x_lat, mod, sw["norm_mix"], w_in_t, oa_ctx, oa_lat, ob, yf, yb, u_t, sw["s5_d"], sw["w_glu"],
      oret, rg, sw["gn_w"], jnp.asarray(_head_avg(), dtype=BF16), jnp.asarray(_tile_perm().T, dtype=BF16),
      sw["w_branch"], sw["w_out"], *job_args)


def _ffn_kernel(x_ref, mod_ref, nw_ref, w1_ref, w2_ref, o_ref, *, skip_ctx):
    is_ctx = None if skip_ctx else pl.program_id(0) < CTX_LEN // FFN_TT
    sh, sc, gate_res = _mod_chunks(mod_ref, is_ctx, (3, 4, 5))
    x3 = x_ref[...]
    h = (_rms(x3, nw_ref[0]) * (1.0 + sc) + sh).reshape(BATCH * FFN_TT, D_MODEL).astype(BF16)
    f = None
    for c in range(0, D_FF, FFN_PIECE):
        a = jnp.maximum(_dot(h, w1_ref[0, :, c:c + FFN_PIECE]), 0.0)
        part = _dot((a * a).astype(BF16), w2_ref[0, c:c + FFN_PIECE, :])
        f = part if f is None else f + part
    o_ref[...] = x3 + gate_res * f.reshape(BATCH, FFN_TT, D_MODEL)


def _ffn(l, xm, mod, sw, w1, w2, skip_ctx):
    nt = xm.shape[1] // FFN_TT
    tok = pl.BlockSpec((BATCH, FFN_TT, D_MODEL), lambda j: (0, j, 0))
    return pl.pallas_call(
        functools.partial(_ffn_kernel, skip_ctx=skip_ctx),
        grid=(nt,),
        in_specs=[tok,
                  _layer(l, (16, ADA_CHUNKS * D_MODEL)),
                  _layer(l, (1, D_MODEL)),
                  _layer(0, (D_MODEL, D_FF), single=True),
                  _layer(0, (D_FF, D_MODEL), single=True)],
        out_specs=tok,
        out_shape=jax.ShapeDtypeStruct(xm.shape, F32),
        compiler_params=_cparams(1),
        name="ffn",
    )(xm, mod, sw["norm_ffn"], w1, w2)


def _stacked_weights(p):
    perm, _ = _mla_rope_perm()
    zeros = lambda *s: jnp.zeros((DEPTH,) + s, F32)
    wu = p["mla_w_ukv"].reshape(DEPTH, KV_LORA, MLA_HEADS, MLA_NOPE + MLA_V)
    pad_heads = lambda t: jnp.concatenate(
        [t, zeros(t.shape[1], MLA_HEADS, HEAD_PAD - t.shape[3])], -1).reshape(DEPTH, t.shape[1], HEADS_W)
    top = jnp.concatenate([pad_heads(wu[..., :MLA_NOPE]), zeros(KV_LORA, HEADS_W),
                           pad_heads(wu[..., MLA_NOPE:])], axis=2)
    place = jnp.broadcast_to(jnp.asarray(_kv_place())[None], (DEPTH, 128, 3 * HEADS_W))
    wkv = jnp.concatenate([top, place], axis=1).astype(BF16)

    wuq = p["mla_w_uq"].reshape(DEPTH, Q_LORA, MLA_HEADS, MLA_QK)
    qp = jnp.concatenate([zeros(Q_LORA, MLA_HEADS, MLA_NOPE), wuq[..., MLA_NOPE:][..., perm],
                          zeros(Q_LORA, MLA_HEADS, 32)], -1).reshape(DEPTH, Q_LORA, HEADS_W)
    wq = jnp.concatenate([pad_heads(wuq), qp], axis=2).astype(BF16)

    def head_w(v):
        wf = jnp.concatenate([v, zeros(32)], -1)[:, None, :]
        wp = jnp.concatenate([zeros(MLA_NOPE), v[:, MLA_NOPE:][:, perm], zeros(32)], -1)[:, None, :]
        return wf, wp

    head_ws = jnp.concatenate(head_w(p["mla_qk_norm_k"]) + head_w(p["mla_qk_norm_q"]), axis=1)

    def b_compact(b):
        t = b.transpose(0, 1, 2, 4, 3).reshape(DEPTH, 2, BRANCH_W, S5_STATE)
        return jnp.concatenate([t, t], axis=-1)

    c_compact = lambda c: c.transpose(0, 1, 4, 2, 3).reshape(DEPTH, 2, S5_STATE, BRANCH_W)
    vec = lambda a: a.reshape(DEPTH, 2, 1, S5_LANES)

    logit = p["ret_decay_logit"]
    row = lambda a: a[:, None, :]
    return dict(
        norm_mix=row(p["norm_mix_w"]), norm_ffn=row(p["norm_ffn_w"]),
        kv_norm=row(p["mla_kv_norm"]), q_norm=row(p["mla_q_norm"]),
        wkv=wkv, wq=wq, head_w=head_ws,
        s5_lam_re=vec(p["s5_lam_re"]), s5_lam_im=vec(p["s5_lam_im"]),
        s5_log_step=vec(jnp.repeat(p["s5_log_step"], S5_STATE, axis=-1)),
        s5_bre=b_compact(p["s5_b_re"]), s5_bim=b_compact(p["s5_b_im"]),
        s5_cre=c_compact(p["s5_c_re"]), s5_cim=c_compact(p["s5_c_im"]),
        s5_d=row(p["s5_d"]), w_glu=p["s5_w_glu"].astype(BF16),
        lgl=jnp.repeat(logit, RET_HD, axis=-1).reshape(DEPTH, 2, 1, BRANCH_W),
        lgh=jnp.broadcast_to(logit[:, :, :, None, None], (DEPTH, 2, RET_HEADS, 1, 128)),
        gn_w=row(p["ret_gn_w"]),
        w_branch=p["w_branch"].astype(BF16), w_out=p["w_out"].astype(BF16))


def kernel(x, c, ctx, c_ctx, ada_w, ada_b, norm_mix_w, norm_ffn_w, w_in, mla_q_norm, mla_w_uq, mla_kv_norm,
           mla_w_ukv, mla_qk_norm_q, mla_qk_norm_k, s5_lam_re, s5_lam_im, s5_log_step, s5_b_re, s5_b_im,
           s5_c_re, s5_c_im, s5_d, s5_w_glu, ret_decay_logit, ret_gn_w, w_branch, w_out, ffn_w1, ffn_w2):
    p = dict(norm_mix_w=norm_mix_w, norm_ffn_w=norm_ffn_w, w_in=w_in, mla_q_norm=mla_q_norm,
             mla_w_uq=mla_w_uq, mla_kv_norm=mla_kv_norm, mla_w_ukv=mla_w_ukv, mla_qk_norm_q=mla_qk_norm_q,
             mla_qk_norm_k=mla_qk_norm_k, s5_lam_re=s5_lam_re, s5_lam_im=s5_lam_im, s5_log_step=s5_log_step,
             s5_b_re=s5_b_re, s5_b_im=s5_b_im, s5_c_re=s5_c_re, s5_c_im=s5_c_im, s5_d=s5_d,
             s5_w_glu=s5_w_glu, ret_decay_logit=ret_decay_logit, ret_gn_w=ret_gn_w, w_branch=w_branch,
             w_out=w_out, ffn_w1=ffn_w1, ffn_w2=ffn_w2)
    sw = _stacked_weights(p)
    c16 = jnp.concatenate([c, c_ctx[None, :], jnp.zeros((16 - BATCH - 1, D_MODEL), F32)], axis=0)
    mod = _ada_mod(c16, ada_w, ada_b)
    lam, bblk, cblk = _s5_params(sw)
    w_in_t32 = jnp.swapaxes(w_in, 1, 2)
    w_in_t = w_in_t32[0:1].astype(BF16)
    x_ctx, x_lat = ctx, x
    for l in range(DEPTH):
        last = l == DEPTH - 1
        q, k, v, u_t, uc, us, rq, rk, rv, rg, w1, w2 = _in_proj(
            l, x_ctx, x_lat, mod, sw, w_in_t, [(ffn_w1, l, 32), (ffn_w2, l, 32)])
        oa_lat = _attention_latent(q, k, v)
        oa_ctx = oa_lat if last else _attention_context(q, k, v)
        ob = _fnet(uc, us)
        yf, yb = _s5(l, u_t, lam, bblk, cblk)
        oret = _retention(l, rq, rk, rv, sw)
        xm, *w_in_next = _merge(l, x_ctx, x_lat, mod, sw, w_in_t, oa_ctx, oa_lat, ob, yf, yb, u_t, oret, rg, last,
                                [] if last else [(w_in_t32, l + 1, 18)])
        if not last:
            w_in_t, = w_in_next
        x_ctx = x_lat = _ffn(l, xm, mod, sw, w1, w2, last)
    return x_lat
```

```python
import functools
import math

import numpy as np
import jax
import jax.numpy as jnp
from jax import lax
from jax.experimental import pallas as pl
from jax.experimental.pallas import tpu as pltpu

F32 = jnp.float32
BF16 = jnp.bfloat16

D_MODEL = 1024
BATCH = 8
SEQ = 2048
DEPTH = 2
GRID_W = 64
CTX_LEN = 256
LCAT = CTX_LEN + SEQ
N_BRANCH = 4
BRANCH_W = 256
NORM_EPS = 1e-6
ROPE_BASE = 10000.0
ADA_CHUNKS = 6

MLA_HEADS = 4
MLA_NOPE = 64
MLA_ROPE = 32
MLA_QK = MLA_NOPE + MLA_ROPE
MLA_V = 64
Q_LORA = 256
KV_LORA = 128
HEAD_PAD = 128
HEADS_W = MLA_HEADS * HEAD_PAD

FNET_GROUPS = 4
FNET_GW = BRANCH_W // FNET_GROUPS

S5_GROUP_CH = 16
S5_GROUPS = BRANCH_W // S5_GROUP_CH
S5_STATE = 64
S5_LANES = S5_GROUPS * S5_STATE

RET_HEADS = 4
RET_HD = BRANCH_W // RET_HEADS
RET_CHUNK = 256
RET_NCHUNK = LCAT // RET_CHUNK
RET_CTX_CHUNKS = CTX_LEN // RET_CHUNK
RET_SPB = 4

D_FF = 4 * D_MODEL

_O_KV, _O_KR, _O_S5, _O_RK, _O_RV = 0, 128, 160, 416, 672
_O_Q, _O_FN, _O_RQ, _O_RG, _O_GATE = 928, 1184, 1440, 1696, 1952
IN_COLS = _O_GATE + N_BRANCH * D_MODEL

TT = 64
TR = TT * BATCH
FFN_TT = 128
FFN_PIECE = 2048
CTX_TILES = CTX_LEN // TT
ALL_TILES = LCAT // TT
TQ = 256
ATTN_QB = 4
ATTN_UNIT_QB = 2
ATTN_AHEAD = 1
S5_TC = 128
S5_ROWS = S5_TC * BATCH
S5_STEPS = LCAT // S5_TC
S5_CTX_STEPS = CTX_LEN // S5_TC

VMEM_LIMIT = 56 * 1024 * 1024


def _cparams(n_grid):
    return pltpu.CompilerParams(dimension_semantics=("arbitrary",) * n_grid,
                                vmem_limit_bytes=VMEM_LIMIT)


def _dot(a, b):
    return jnp.dot(a, b, preferred_element_type=F32)


def _dot_nt(a, b):
    return lax.dot_general(a, b, (((1,), (1,)), ((), ())), preferred_element_type=F32)


def _sigmoid(x):
    return 0.5 * (jnp.tanh(0.5 * x) + 1.0)


def _gelu_tanh(y):
    return 0.5 * y * (1.0 + jnp.tanh(math.sqrt(2.0 / math.pi) * (y + 0.044715 * (y * y * y))))


def _rms(x, w):
    return x * lax.rsqrt(jnp.mean(x * x, axis=-1, keepdims=True) + NORM_EPS) * w


def _full(shape):
    n = len(shape)
    return pl.BlockSpec(shape, lambda *_: (0,) * n)


def _layer(l, shape, single=False):
    n = len(shape)
    mode = dict(pipeline_mode=pl.Buffered(1)) if single else {}
    return pl.BlockSpec((1,) + tuple(shape), lambda *_: (l,) + (0,) * n, **mode)


def _mla_rope_perm():
    r = np.arange(MLA_ROPE)
    first = (r % 16) < 8
    return np.where(first, r + 8, r - 8), np.where(first, -1.0, 1.0)


@functools.lru_cache(maxsize=None)
def _mla_tables():
    pos = np.arange(SEQ)
    rows, cols = pos // GRID_W, pos % GRID_W
    freqs = ROPE_BASE ** (-np.arange(8, dtype=np.float64) / 8)
    r = np.arange(MLA_ROPE)
    _, sign = _mla_rope_perm()
    p = np.where((r // 16 == 0)[None, :], rows[:, None], cols[:, None]).astype(np.float64)
    ang = p * freqs[(r % 16) % 8][None, :]
    cosf = np.zeros((LCAT, HEAD_PAD))
    sinf = np.zeros((LCAT, HEAD_PAD))
    cosf[:, :MLA_QK] = 1.0
    cosf[CTX_LEN:, MLA_NOPE:MLA_QK] = np.cos(ang)
    sinf[CTX_LEN:, MLA_NOPE:MLA_QK] = np.sin(ang) * sign[None, :]
    return cosf.astype(np.float32), sinf.astype(np.float32)


def _ret_perm():
    d = np.arange(RET_HD)
    first = d < RET_HD // 2
    return np.where(first, d + RET_HD // 2, d - RET_HD // 2), np.where(first, -1.0, 1.0)


@functools.lru_cache(maxsize=None)
def _ret_tables():
    half = RET_HD // 2
    pos = np.arange(SEQ, dtype=np.float64)
    freqs = ROPE_BASE ** (-np.arange(half, dtype=np.float64) / half)
    d = np.arange(RET_HD)
    perm, sign = _ret_perm()
    ang = pos[:, None] * freqs[d % half][None, :]
    cosr = np.ones((LCAT, RET_HD))
    sinr = np.zeros((LCAT, RET_HD))
    cosr[CTX_LEN:] = np.cos(ang)
    sinr[CTX_LEN:] = np.sin(ang) * sign[None, :]
    cosr = np.tile(cosr, (1, RET_HEADS))
    sinr = np.tile(sinr, (1, RET_HEADS))
    pm = np.zeros((BRANCH_W, BRANCH_W))
    for h in range(RET_HEADS):
        pm[h * RET_HD + perm, h * RET_HD + d] = 1.0
    return cosr.astype(np.float32), sinr.astype(np.float32), pm.astype(np.float32)


def _dft(n, scale):
    k = np.arange(n)
    kt = (k[:, None] * k[None, :]) % n
    ang = 2.0 * np.pi * kt / n
    return np.cos(ang) * scale, np.sin(ang) * scale


@functools.lru_cache(maxsize=None)
def _fnet_tables():
    cw, sw = _dft(FNET_GW, 1.0)
    t = np.zeros((BRANCH_W, 2 * BRANCH_W))
    for g in range(FNET_GROUPS):
        s = slice(g * FNET_GW, (g + 1) * FNET_GW)
        t[s, s] = cw
        t[s, BRANCH_W + g * FNET_GW:BRANCH_W + (g + 1) * FNET_GW] = -sw
    cl, sl = _dft(SEQ, 1.0 / math.sqrt(SEQ * FNET_GW))
    clc, slc = _dft(CTX_LEN, 1.0 / math.sqrt(CTX_LEN * FNET_GW))
    return tuple(a.astype(np.float32) for a in (t, cl, sl, clc, slc))


def _bf16_const(a):
    return jnp.asarray(a).astype(BF16)


@functools.lru_cache(maxsize=None)
def _head_avg():
    p = np.zeros((BRANCH_W, BRANCH_W))
    for h in range(RET_HEADS):
        p[h * RET_HD:(h + 1) * RET_HD, h * RET_HD:(h + 1) * RET_HD] = 1.0 / RET_HD
    return p.astype(np.float32)


@functools.lru_cache(maxsize=None)
def _tile_perm():
    p = np.zeros((TR, TR), np.float32)
    for b in range(BATCH):
        for t in range(TT):
            p[t * BATCH + b, b * TT + t] = 1.0
    return p


@functools.lru_cache(maxsize=None)
def _kv_place():
    perm, _ = _mla_rope_perm()
    place = np.zeros((128, 3 * HEADS_W), np.float32)
    for h in range(MLA_HEADS):
        for r in range(MLA_ROPE):
            place[r, h * HEAD_PAD + MLA_NOPE + r] = 1.0
            place[perm[r], HEADS_W + h * HEAD_PAD + MLA_NOPE + r] = 1.0
    return place


@functools.lru_cache(maxsize=None)
def _v_ones():
    v = np.zeros((1, HEADS_W), np.float32)
    v[0, np.arange(MLA_HEADS) * HEAD_PAD + MLA_V] = 1.0
    return v


ADA_TN = 1536


def _ada_kernel(c_ref, w_ref, b_ref, o_ref):
    c = c_ref[...]
    s = (c * _sigmoid(c)).astype(BF16)
    o_ref[0] = _dot(s, w_ref[0].astype(BF16)) + b_ref[0]


def _ada_mod(c16, ada_w, ada_b):
    n = ADA_CHUNKS * D_MODEL
    return pl.pallas_call(
        _ada_kernel,
        grid=(DEPTH, n // ADA_TN),
        in_specs=[pl.BlockSpec((16, D_MODEL), lambda l, j: (0, 0)),
                  pl.BlockSpec((1, D_MODEL, ADA_TN), lambda l, j: (l, 0, j)),
                  pl.BlockSpec((1, 1, ADA_TN), lambda l, j: (l, 0, j))],
        out_specs=pl.BlockSpec((1, 16, ADA_TN), lambda l, j: (l, 0, j)),
        out_shape=jax.ShapeDtypeStruct((DEPTH, 16, n), F32),
        compiler_params=_cparams(2),
        name="ada_mod",
    )(c16, ada_w, ada_b.reshape(DEPTH, 1, n))


def _side_casts(jobs):
    specs_in, specs_out, shapes, args = [], [], [], []
    for a, l, n in jobs:
        _, r, c = a.shape
        blk = (1, r // n, c)
        specs_in.append(pl.BlockSpec(blk, lambda j, l=l, n=n: (l, jnp.minimum(j, n - 1), 0)))
        specs_out.append(pl.BlockSpec(blk, lambda j, n=n: (0, jnp.minimum(j, n - 1), 0)))
        shapes.append(jax.ShapeDtypeStruct((1, r, c), BF16))
        args.append(a)
    return specs_in, specs_out, shapes, args


def _with_side_casts(body, n_in, n_out, n_jobs):
    def wrapped(*refs):
        body(*refs[:n_in], *refs[n_in + n_jobs:n_in + n_jobs + n_out])
        for src, dst in zip(refs[n_in:n_in + n_jobs], refs[n_in + n_jobs + n_out:]):
            dst[...] = src[...].astype(BF16)
    return wrapped if n_jobs else body


def _mod_chunks(mod_ref, is_ctx, idxs):
    out = []
    for i in idxs:
        sl = slice(i * D_MODEL, (i + 1) * D_MODEL)
        m = mod_ref[0, 0:BATCH, sl]
        if is_ctx is not None:
            m = jnp.where(is_ctx, mod_ref[0, BATCH:BATCH + 1, sl], m)
        out.append(m[:, None, :])
    return out


def _tile_specs(separate, skip_ctx, width=D_MODEL):
    base = 0 if separate else CTX_TILES
    shape = (BATCH, TT, width)
    if skip_ctx:
        return (pl.BlockSpec(shape, lambda j: (0, 0, 0)),
                pl.BlockSpec(shape, lambda j: (0, j + base, 0)))
    return (pl.BlockSpec(shape, lambda j: (0, jnp.minimum(j, CTX_TILES - 1), 0)),
            pl.BlockSpec(shape, lambda j: (0, jnp.maximum(j - CTX_TILES, 0) + base, 0)))


def _head_norm_rot(xf, xp, a, b, scale):
    outs = []
    for h in range(MLA_HEADS):
        f = xf[:, h * HEAD_PAD:(h + 1) * HEAD_PAD]
        p = xp[:, h * HEAD_PAD:(h + 1) * HEAD_PAD]
        n = lax.rsqrt(jnp.sum(f * f, axis=-1, keepdims=True) * (1.0 / MLA_QK) + NORM_EPS) * scale
        outs.append(n * (f * a + p * b))
    return jnp.concatenate(outs, axis=-1)


def _rows(table):
    w = table.shape[-1]
    return jnp.broadcast_to(table[None], (BATCH, TT, w)).reshape(TR, w)


def _tile3(x):
    return x.reshape(BATCH, TT, x.shape[-1])


def _inproj_kernel(xc_ref, xl_ref, mod_ref, nw_ref, w_ref, kvw_ref, wkv_ref, qnw_ref, wq_ref,
                   cosf_ref, sinf_ref, hw_ref, pm_ref, cosr_ref, sinr_ref, t_ref, ptb_ref, vone_ref,
                   q_out, k_out, v_out, ut_out, uc_out, us_out, rq_out, rk_out, rv_out, rg_out):
    is_ctx = pl.program_id(0) < CTX_TILES
    x3 = jnp.where(is_ctx, xc_ref[...], xl_ref[...])
    sh, sc = _mod_chunks(mod_ref, is_ctx, (0, 1))
    zas, zs = [], []
    for half in (slice(0, BATCH // 2), slice(BATCH // 2, BATCH)):
        hh = (_rms(x3[half], nw_ref[0]) * (1.0 + sc[half]) + sh[half]).reshape(TR // 2, D_MODEL).astype(BF16)
        zas.append(_dot_nt(hh, w_ref[0, 0:256, :]))
        zs.append(_dot_nt(hh, w_ref[0, _O_S5:_O_GATE, :]))
    za, z = jnp.concatenate(zas, axis=0), jnp.concatenate(zs, axis=0)

    cosf, sinf = _rows(cosf_ref[...]), _rows(sinf_ref[...])
    hw = hw_ref[0]
    kvn = _rms(za[:, 0:128], kvw_ref[0]).astype(BF16)
    lhs = jnp.concatenate([kvn, za[:, 128:256].astype(BF16)], axis=-1)
    kv = _dot(lhs, wkv_ref[0])
    k = _head_norm_rot(kv[:, 0:HEADS_W], kv[:, HEADS_W:2 * HEADS_W],
                       cosf * hw[0:1], sinf * hw[1:2], 1.0)
    k_out[...] = _tile3(k).astype(BF16)
    v_out[...] = _tile3(kv[:, 2 * HEADS_W:3 * HEADS_W] + vone_ref[...]).astype(BF16)
    qn = _rms(z[:, 768:1024], qnw_ref[0]).astype(BF16)
    qq = _dot(qn, wq_ref[0])
    q = _head_norm_rot(qq[:, 0:HEADS_W], qq[:, HEADS_W:2 * HEADS_W],
                       cosf * hw[2:3], sinf * hw[3:4], MLA_QK ** -0.5)
    q_out[...] = _tile3(q).astype(BF16)
    cosr, sinr = _rows(cosr_ref[...]), _rows(sinr_ref[...])
    rq = z[:, 1280:1536]
    rk = z[:, 256:512]
    rq = rq * cosr + _dot(rq.astype(BF16), pm_ref[...]) * sinr
    rk = rk * cosr + _dot(rk.astype(BF16), pm_ref[...]) * sinr
    rq_out[...] = _tile3(rq).astype(BF16)
    rk_out[...] = _tile3(rk * (RET_HD ** -0.5)).astype(BF16)
    rv_out[...] = _tile3(z[:, 512:768]).astype(BF16)
    rg_out[...] = _tile3(z[:, 1536:1792]).astype(BF16)
    ucs = _dot(z[:, 1024:1280].astype(BF16), t_ref[...])
    uc_out[...] = _tile3(ucs[:, 0:BRANCH_W]).astype(BF16)
    us_out[...] = _tile3(ucs[:, BRANCH_W:2 * BRANCH_W]).astype(BF16)
    ut_out[...] = _dot(ptb_ref[...], z[:, 0:256].astype(BF16)).astype(BF16)


def _in_proj(l, x_ctx, x_lat, mod, sw, w_in_t, jobs):
    ctx_spec, lat_spec = _tile_specs(l == 0, False)
    tok = lambda w: pl.BlockSpec((BATCH, TT, w), lambda j: (0, j, 0))
    tab = lambda w: pl.BlockSpec((TT, w), lambda j: (j, 0))
    cosf, sinf = _mla_tables()
    cosr, sinr, pm = _ret_tables()
    bshape = lambda w: jax.ShapeDtypeStruct((BATCH, LCAT, w), BF16)
    job_in, job_out, job_shapes, job_args = _side_casts(jobs)
    return pl.pallas_call(
        _with_side_casts(_inproj_kernel, 18, 10, len(jobs)),
        grid=(ALL_TILES,),
        in_specs=[ctx_spec, lat_spec,
                  _layer(l, (16, ADA_CHUNKS * D_MODEL)),
                  _layer(l, (1, D_MODEL)),
                  _layer(0, (_O_GATE, D_MODEL)),
                  _layer(l, (1, KV_LORA)),
                  _layer(l, (256, 3 * HEADS_W)),
                  _layer(l, (1, Q_LORA)),
                  _layer(l, (Q_LORA, 2 * HEADS_W)),
                  tab(HEAD_PAD), tab(HEAD_PAD),
                  _layer(l, (4, HEAD_PAD)),
                  _full((BRANCH_W, BRANCH_W)),
                  tab(BRANCH_W), tab(BRANCH_W),
                  _full((BRANCH_W, 2 * BRANCH_W)),
                  _full((TR, TR)),
                  _full((1, HEADS_W))] + job_in,
        out_specs=[tok(HEADS_W), tok(HEADS_W), tok(HEADS_W),
                   pl.BlockSpec((TR, BRANCH_W), lambda j: (j, 0)),
                   tok(256), tok(256), tok(256), tok(256), tok(256), tok(256)] + job_out,
        out_shape=[bshape(HEADS_W), bshape(HEADS_W), bshape(HEADS_W),
                   jax.ShapeDtypeStruct((LCAT * BATCH, BRANCH_W), BF16),
                   bshape(256), bshape(256), bshape(256), bshape(256), bshape(256), bshape(256)] + job_shapes,
        compiler_params=_cparams(1),
        name="in_proj",
    )(x_ctx, x_lat, mod, sw["norm_mix"], w_in_t, sw["kv_norm"], sw["wkv"], sw["q_norm"], sw["wq"],
      jnp.asarray(cosf), jnp.asarray(sinf), sw["head_w"], jnp.asarray(pm, dtype=BF16),
      jnp.asarray(cosr), jnp.asarray(sinr), _bf16_const(_fnet_tables()[0]),
      jnp.asarray(_tile_perm(), dtype=BF16), jnp.asarray(_v_ones()), *job_args)


def _attn_kernel(*refs):
    q_refs, (k_ref, v_ref, o_ref) = refs[:-3], refs[-3:]
    per = min(ATTN_UNIT_QB, len(q_refs))
    units = [(b, g, h, slice(h * HEAD_PAD, (h + 1) * HEAD_PAD))
             for b in range(k_ref.shape[0]) for g in range(len(q_refs) // per) for h in range(MLA_HEADS)]

    def scores(u):
        b, g, _, sl = u
        q = jnp.concatenate([q_refs[g * per + i][b, :, sl] for i in range(per)], axis=0)
        return _dot_nt(q, k_ref[b, :, sl])

    lane = lax.broadcasted_iota(jnp.int32, (1, HEAD_PAD), 1)
    pending = [scores(u) for u in units[:ATTN_AHEAD]]
    for i, (b, g, h, sl) in enumerate(units):
        if i + ATTN_AHEAD < len(units):
            pending.append(scores(units[i + ATTN_AHEAD]))
        s = pending.pop(0)
        p = jnp.exp((s - jnp.max(s, axis=-1, keepdims=True)).astype(BF16))
        oh = _dot(p, v_ref[b, :, sl])
        oh = oh * (1.0 / oh[:, MLA_V:MLA_V + 1])
        if h % 2 == 0:
            even = oh
        else:
            pair = jnp.where(lane < MLA_V, even, pltpu.roll(oh, MLA_V, axis=1))
            rows = slice(g * per * TQ, (g + 1) * per * TQ)
            o_ref[b, rows, (h // 2) * HEAD_PAD:(h // 2 + 1) * HEAD_PAD] = pair.astype(BF16)


def _attention_latent(q, k, v):
    first = CTX_LEN // TQ
    qspec = lambda r: pl.BlockSpec((1, TQ, HEADS_W), lambda b, j: (b, first + ATTN_QB * j + r, 0))
    kv = pl.BlockSpec((1, LCAT, HEADS_W), lambda b, j: (b, 0, 0))
    return pl.pallas_call(
        _attn_kernel,
        grid=(BATCH, SEQ // (ATTN_QB * TQ)),
        in_specs=[qspec(r) for r in range(ATTN_QB)] + [kv, kv],
        out_specs=pl.BlockSpec((1, ATTN_QB * TQ, BRANCH_W), lambda b, j: (b, j, 0)),
        out_shape=jax.ShapeDtypeStruct((BATCH, SEQ, BRANCH_W), BF16),
        compiler_params=_cparams(2),
        name="mla_attention",
    )(*([q] * ATTN_QB), k, v)


def _attention_context(q, k, v):
    blk = pl.BlockSpec((1, CTX_LEN, HEADS_W), lambda b: (b, 0, 0))
    return pl.pallas_call(
        _attn_kernel,
        grid=(BATCH,),
        in_specs=[blk, blk, blk],
        out_specs=pl.BlockSpec((1, CTX_LEN, BRANCH_W), lambda b: (b, 0, 0)),
        out_shape=jax.ShapeDtypeStruct((BATCH, CTX_LEN, BRANCH_W), BF16),
        compiler_params=_cparams(1),
        name="mla_attention_ctx",
    )(q, k, v)


def _fnet_kernel(uc_ref, us_ref, cl_ref, sl_ref, clc_ref, slc_ref, o_ref):
    o_ref[0, 0:CTX_LEN, :] = (_dot(clc_ref[...], uc_ref[0, 0:CTX_LEN, :])
                              + _dot(slc_ref[...], us_ref[0, 0:CTX_LEN, :])).astype(BF16)
    o_ref[0, CTX_LEN:LCAT, :] = (_dot(cl_ref[...], uc_ref[0, CTX_LEN:LCAT, :])
                                 + _dot(sl_ref[...], us_ref[0, CTX_LEN:LCAT, :])).astype(BF16)


def _fnet(uc, us):
    cl, sl, clc, slc = (_bf16_const(a) for a in _fnet_tables()[1:])
    tok = pl.BlockSpec((1, LCAT, BRANCH_W), lambda b: (b, 0, 0))
    return pl.pallas_call(
        _fnet_kernel,
        grid=(BATCH,),
        in_specs=[tok, tok, _full((SEQ, SEQ)), _full((SEQ, SEQ)),
                  _full((CTX_LEN, CTX_LEN)), _full((CTX_LEN, CTX_LEN))],
        out_specs=tok,
        out_shape=jax.ShapeDtypeStruct((BATCH, LCAT, BRANCH_W), BF16),
        compiler_params=_cparams(1),
        name="fnet_dft",
    )(uc, us, cl, sl, clc, slc)


def _s5_param_kernel(lr_ref, li_ref, ls_ref, bre_ref, bim_ref, cre_ref, cim_ref, lam_out, b_out, c_out):
    lr, li = lr_ref[0, 0], li_ref[0, 0]
    step = jnp.exp(ls_ref[0, 0])
    mag = jnp.exp(lr * step)
    lbr = mag * jnp.cos(li * step)
    lbi = mag * jnp.sin(li * step)
    den = 1.0 / (lr * lr + li * li)
    cr = ((lbr - 1.0) * lr + lbi * li) * den
    ci = (lbi * lr - (lbr - 1.0) * li) * den
    lam_out[0, 0, 0] = jnp.broadcast_to(lbr, (BATCH, S5_LANES))
    lam_out[0, 0, 1] = jnp.broadcast_to(lbi, (BATCH, S5_LANES))
    grp_b = (lax.broadcasted_iota(jnp.int32, (BRANCH_W, S5_LANES), 0) // S5_GROUP_CH
             == lax.broadcasted_iota(jnp.int32, (BRANCH_W, S5_LANES), 1) // S5_STATE)
    wide = lambda r: jnp.concatenate([r[0, 0]] * (S5_LANES // 128), axis=-1)
    bre = jnp.where(grp_b, wide(bre_ref), 0.0)
    bim = jnp.where(grp_b, wide(bim_ref), 0.0)
    b_out[0, 0, :, 0:S5_LANES] = (cr * bre - ci * bim).astype(BF16)
    b_out[0, 0, :, S5_LANES:2 * S5_LANES] = (cr * bim + ci * bre).astype(BF16)
    grp_c = (lax.broadcasted_iota(jnp.int32, (S5_LANES, BRANCH_W), 0) // S5_STATE
             == lax.broadcasted_iota(jnp.int32, (S5_LANES, BRANCH_W), 1) // S5_GROUP_CH)
    tall = lambda r: jnp.concatenate([r[0, 0]] * S5_GROUPS, axis=0)
    c_out[0, 0, 0:S5_LANES, :] = jnp.where(grp_c, tall(cre_ref), 0.0).astype(BF16)
    c_out[0, 0, S5_LANES:2 * S5_LANES, :] = jnp.where(grp_c, -tall(cim_ref), 0.0).astype(BF16)


def _s5_params(sw):
    spec = lambda *s: pl.BlockSpec((1, 1) + s, lambda l, d: (l, d) + (0,) * len(s))
    return pl.pallas_call(
        _s5_param_kernel,
        grid=(DEPTH, 2),
        in_specs=[spec(1, S5_LANES), spec(1, S5_LANES), spec(1, S5_LANES),
                  spec(BRANCH_W, 128), spec(BRANCH_W, 128), spec(S5_STATE, BRANCH_W), spec(S5_STATE, BRANCH_W)],
        out_specs=[spec(2, BATCH, S5_LANES), spec(BRANCH_W, 2 * S5_LANES), spec(2 * S5_LANES, BRANCH_W)],
        out_shape=[jax.ShapeDtypeStruct((DEPTH, 2, 2, BATCH, S5_LANES), F32),
                   jax.ShapeDtypeStruct((DEPTH, 2, BRANCH_W, 2 * S5_LANES), BF16),
                   jax.ShapeDtypeStruct((DEPTH, 2, 2 * S5_LANES, BRANCH_W), BF16)],
        compiler_params=_cparams(2),
        name="s5_discretise",
    )(sw["s5_lam_re"], sw["s5_lam_im"], sw["s5_log_step"], sw["s5_bre"], sw["s5_bim"], sw["s5_cre"], sw["s5_cim"])


def _s5_bwd_block(i):
    return jnp.where(i < S5_CTX_STEPS, S5_CTX_STEPS - 1 - i, S5_STEPS + S5_CTX_STEPS - 1 - i)


def _s5_kernel(uf_ref, ub_ref, lam_ref, b_ref, c_ref, yf_ref, yb_ref, xf_scr, xb_scr, st_scr):
    @pl.when(pl.program_id(0) == 0)
    def _():
        st_scr[...] = jnp.zeros_like(st_scr)

    dirs = ((uf_ref, xf_scr, yf_ref), (ub_ref, xb_scr, yb_ref))
    for d, (u_ref, x_scr, _) in enumerate(dirs):
        x_scr[...] = _dot(u_ref[...], b_ref[0, d])
    re, im = pl.ds(0, S5_LANES), pl.ds(S5_LANES, S5_LANES)
    for d, (_, x_scr, y_ref) in enumerate(dirs):
        xr, xi = st_scr[2 * d], st_scr[2 * d + 1]
        ar, ai = lam_ref[0, d, 0], lam_ref[0, d, 1]
        for t in (range(S5_TC) if d == 0 else range(S5_TC - 1, -1, -1)):
            rows = pl.ds(t * BATCH, BATCH)
            xr, xi = (ar * xr - ai * xi + x_scr[rows, re], ar * xi + ai * xr + x_scr[rows, im])
            x_scr[rows, re] = xr
            x_scr[rows, im] = xi
        st_scr[2 * d], st_scr[2 * d + 1] = xr, xi
        y_ref[...] = _dot(x_scr[...].astype(BF16), c_ref[0, d]).astype(BF16)


def _s5(l, u_t, lam, bblk, cblk):
    fwd = pl.BlockSpec((S5_ROWS, BRANCH_W), lambda i: (i, 0))
    bwd = pl.BlockSpec((S5_ROWS, BRANCH_W), lambda i: (_s5_bwd_block(i), 0))
    return pl.pallas_call(
        _s5_kernel,
        grid=(S5_STEPS,),
        in_specs=[fwd, bwd, _layer(l, (2, 2, BATCH, S5_LANES)),
                  _layer(l, (2, BRANCH_W, 2 * S5_LANES)), _layer(l, (2, 2 * S5_LANES, BRANCH_W))],
        out_specs=[fwd, bwd],
        out_shape=[jax.ShapeDtypeStruct((LCAT * BATCH, BRANCH_W), BF16)] * 2,
        scratch_shapes=[pltpu.VMEM((S5_ROWS, 2 * S5_LANES), F32),
                        pltpu.VMEM((S5_ROWS, 2 * S5_LANES), F32),
                        pltpu.VMEM((4, BATCH, S5_LANES), F32)],
        compiler_params=_cparams(1),
        name="s5_scan",
    )(u_t, u_t, lam, bblk, cblk)


def _log_sigmoid(x):
    return jnp.minimum(x, 0.0) - jnp.log(1.0 + jnp.exp(-jnp.abs(x)))


def _ret_kernel(q_ref, k_ref, v_ref, lgl_ref, lgh_ref, o_ref, dec_scr, sb_scr, sf_scr, sbc_scr):
    c_len = RET_CHUNK
    lgl = _log_sigmoid(lgl_ref[0])
    lgf, lgb = lgl[0], lgl[1]
    ti = lax.broadcasted_iota(jnp.int32, (c_len, BRANCH_W), 0).astype(F32)
    qdf = jnp.exp(lgf * (ti + 1.0))
    kdf = jnp.exp(lgf * (c_len - 1.0 - ti))
    qdb = jnp.exp(lgb * (c_len - ti))
    kdb = jnp.exp(lgb * ti)
    cdf = jnp.exp(lgf * float(c_len))
    cdb = jnp.exp(lgb * float(c_len))
    lane = lax.broadcasted_iota(jnp.int32, (1, BRANCH_W), 1)
    same_head = (lax.broadcasted_iota(jnp.int32, (BRANCH_W, BRANCH_W), 0) // RET_HD
                 == lax.broadcasted_iota(jnp.int32, (BRANCH_W, BRANCH_W), 1) // RET_HD)

    diff = (lax.broadcasted_iota(jnp.int32, (c_len, c_len), 0)
            - lax.broadcasted_iota(jnp.int32, (c_len, c_len), 1)).astype(F32)
    for h in range(RET_HEADS):
        gf = jnp.concatenate([_log_sigmoid(lgh_ref[0, 0, h])] * (c_len // 128), axis=-1)
        gb = jnp.concatenate([_log_sigmoid(lgh_ref[0, 1, h])] * (c_len // 128), axis=-1)
        dec_scr[h] = (jnp.where(diff >= 0, jnp.exp(gf * jnp.maximum(diff, 0.0)), 0.0)
                      + jnp.where(diff <= 0, jnp.exp(gb * jnp.maximum(-diff, 0.0)), 0.0))

    def chunk(ref, s, c):
        return ref[s, pl.ds(pl.multiple_of(c * c_len, c_len), c_len), :]

    def kv_outer(kd, v):
        s = lax.dot_general(kd.astype(BF16), v, (((0,), (0,)), ((), ())), preferred_element_type=F32)
        return jnp.where(same_head, s, 0.0)

    sbc_scr[...] = jnp.zeros_like(sbc_scr)

    def bwd(i, _):
        c = jnp.where(i < RET_CTX_CHUNKS, RET_CTX_CHUNKS - 1 - i, RET_NCHUNK + RET_CTX_CHUNKS - 1 - i)
        for s in range(RET_SPB):
            sb_scr[s, c] = sbc_scr[s].astype(BF16)
            k = chunk(k_ref, s, c).astype(F32)
            sbc_scr[s] = sbc_scr[s] * cdb + kv_outer(k * kdb, chunk(v_ref, s, c))
        return 0

    lax.fori_loop(0, RET_NCHUNK, bwd, 0)

    sf_scr[...] = jnp.zeros_like(sf_scr)

    def fwd(c, _):
        for s in range(RET_SPB):
            qb, kb, vb = chunk(q_ref, s, c), chunk(k_ref, s, c), chunk(v_ref, s, c)
            q = qb.astype(F32)
            o = (_dot((q * qdf).astype(BF16), sf_scr[s].astype(BF16))
                 + _dot((q * qdb).astype(BF16), sb_scr[s, c]))
            for h in range(RET_HEADS):
                hm = lane // RET_HD == h
                att = _dot_nt(jnp.where(hm, qb, jnp.zeros_like(qb)), kb)
                oh = _dot((att * dec_scr[h]).astype(BF16), vb)
                o = o + jnp.where(hm, oh, 0.0)
            o_ref[s, pl.ds(pl.multiple_of(c * c_len, c_len), c_len), :] = o.astype(BF16)
            sf_scr[s] = sf_scr[s] * cdf + kv_outer(kb.astype(F32) * kdf, vb)
        return 0

    lax.fori_loop(0, RET_NCHUNK, fwd, 0)


def _retention(l, rq, rk, rv, sw):
    tok = pl.BlockSpec((RET_SPB, LCAT, BRANCH_W), lambda b: (b, 0, 0))
    return pl.pallas_call(
        _ret_kernel,
        grid=(BATCH // RET_SPB,),
        in_specs=[tok, tok, tok, _layer(l, (2, 1, BRANCH_W)), _layer(l, (2, RET_HEADS, 1, 128))],
        out_specs=tok,
        out_shape=jax.ShapeDtypeStruct((BATCH, LCAT, BRANCH_W), BF16),
        scratch_shapes=[pltpu.VMEM((RET_HEADS, RET_CHUNK, RET_CHUNK), F32),
                        pltpu.VMEM((RET_SPB, RET_NCHUNK, BRANCH_W, BRANCH_W), BF16),
                        pltpu.VMEM((RET_SPB, BRANCH_W, BRANCH_W), F32),
                        pltpu.VMEM((RET_SPB, BRANCH_W, BRANCH_W), F32)],
        compiler_params=_cparams(1),
        name="retention",
    )(rq, rk, rv, sw["lgl"], sw["lgh"])


def _merge_kernel(xc_ref, xl_ref, mod_ref, nw_ref, wg_ref, oac_ref, oal_ref, ob_ref, yf_ref, yb_ref, ut_ref,
                  d_ref, wglu_ref, oret_ref, rg_ref, gnw_ref, pavg_ref, pbt_ref, wb_ref, wout_ref,
                  o_ref, *, skip_ctx):
    if skip_ctx:
        is_ctx = None
        x3 = xl_ref[...]
        oa = oal_ref[...]
    else:
        is_ctx = pl.program_id(0) < CTX_TILES
        x3 = jnp.where(is_ctx, xc_ref[...], xl_ref[...])
        oa = jnp.where(is_ctx, oac_ref[...], oal_ref[...])
    sh, sc, gate_res = _mod_chunks(mod_ref, is_ctx, (0, 1, 2))
    gate_w = lambda n: wg_ref[0, _O_GATE + n * D_MODEL:_O_GATE + (n + 1) * D_MODEL, :]
    hs = [(_rms(x3[half], nw_ref[0]) * (1.0 + sc[half]) + sh[half]).reshape(TR // 2, D_MODEL).astype(BF16)
          for half in (slice(0, BATCH // 2), slice(BATCH // 2, BATCH))]
    gate0 = jnp.concatenate([_dot_nt(hh, gate_w(0)) for hh in hs], axis=0)
    h = jnp.concatenate(hs, axis=0)
    yt = yf_ref[...].astype(F32) + yb_ref[...].astype(F32) + d_ref[0] * ut_ref[...].astype(F32)
    y = _dot(pbt_ref[...], yt.astype(BF16))
    vg = _dot(_gelu_tanh(y).astype(BF16), wglu_ref[0])
    oc = vg[:, 0:BRANCH_W] * _sigmoid(vg[:, BRANCH_W:2 * BRANCH_W])
    o = oret_ref[...].reshape(TR, BRANCH_W)
    dl = o.astype(F32) - _dot(o, pavg_ref[...])
    var = _dot((dl * dl).astype(BF16), pavg_ref[...])
    g = rg_ref[...].reshape(TR, BRANCH_W).astype(F32)
    od = g * _sigmoid(g) * (dl * lax.rsqrt(var + NORM_EPS) * gnw_ref[0])
    branches = (oa.reshape(TR, BRANCH_W), ob_ref[...].reshape(TR, BRANCH_W), oc.astype(BF16), od.astype(BF16))
    acc = None
    for n, branch in enumerate(branches):
        gate = gate0 if n == 0 else _dot_nt(h, gate_w(n))
        term = _sigmoid(gate) * _dot(branch, wb_ref[0, n])
        acc = term if acc is None else acc + term
    m = _dot(acc.astype(BF16), wout_ref[0])
    o_ref[...] = x3 + gate_res * m.reshape(BATCH, TT, D_MODEL)


def _merge(l, x_ctx, x_lat, mod, sw, w_in_t, oa_ctx, oa_lat, ob, yf, yb, u_t, oret, rg, skip_ctx, jobs):
    off = CTX_TILES if skip_ctx else 0
    nt = ALL_TILES - off
    ctx_spec, lat_spec = _tile_specs(l == 0, skip_ctx)
    oac_spec, oal_spec = _tile_specs(True, skip_ctx, BRANCH_W)
    tok = lambda w: pl.BlockSpec((BATCH, TT, w), lambda j: (0, j + off, 0))
    tmaj = pl.BlockSpec((TR, BRANCH_W), lambda j: (j + off, 0))
    job_in, job_out, job_shapes, job_args = _side_casts(jobs)
    return pl.pallas_call(
        _with_side_casts(functools.partial(_merge_kernel, skip_ctx=skip_ctx), 20, 1, len(jobs)),
        grid=(nt,),
        in_specs=[ctx_spec, lat_spec,
                  _layer(l, (16, ADA_CHUNKS * D_MODEL)),
                  _layer(l, (1, D_MODEL)),
                  _layer(0, (IN_COLS, D_MODEL), single=True),
                  oac_spec, oal_spec,
                  tok(BRANCH_W), tmaj, tmaj, tmaj,
                  _layer(l, (1, BRANCH_W)),
                  _layer(l, (BRANCH_W, 2 * BRANCH_W)),
                  tok(BRANCH_W), tok(BRANCH_W),
                  _layer(l, (1, BRANCH_W)),
                  _full((BRANCH_W, BRANCH_W)),
                  _full((TR, TR)),
                  _layer(l, (N_BRANCH, BRANCH_W, D_MODEL), single=True),
                  _layer(l, (D_MODEL, D_MODEL), single=True)] + job_in,
        out_specs=[pl.BlockSpec((BATCH, TT, D_MODEL), lambda j: (0, j, 0))] + job_out,
        out_shape=[jax.ShapeDtypeStruct((BATCH, nt * TT, D_MODEL), F32)] + job_shapes,
        compiler_params=_cparams(1),
        name="merge",
    )(x_ctx, x_lat, mod, sw["norm_mix"], w_in_t, oa_ctx, oa_lat, ob, yf, yb, u_t, sw["s5_d"], sw["w_glu"],
      oret, rg, sw["gn_w"], jnp.asarray(_head_avg(), dtype=BF16), jnp.asarray(_tile_perm().T, dtype=BF16),
      sw["w_branch"], sw["w_out"], *job_args)


def _ffn_kernel(x_ref, mod_ref, nw_ref, w1_ref, w2_ref, o_ref, *, skip_ctx):
    is_ctx = None if skip_ctx else pl.program_id(0) < CTX_LEN // FFN_TT
    sh, sc, gate_res = _mod_chunks(mod_ref, is_ctx, (3, 4, 5))
    x3 = x_ref[...]
    h = (_rms(x3, nw_ref[0]) * (1.0 + sc) + sh).reshape(BATCH * FFN_TT, D_MODEL).astype(BF16)
    f = None
    for c in range(0, D_FF, FFN_PIECE):
        a = jnp.maximum(_dot(h, w1_ref[0, :, c:c + FFN_PIECE]), 0.0)
        part = _dot((a * a).astype(BF16), w2_ref[0, c:c + FFN_PIECE, :])
        f = part if f is None else f + part
    o_ref[...] = x3 + gate_res * f.reshape(BATCH, FFN_TT, D_MODEL)


def _ffn(l, xm, mod, sw, w1, w2, skip_ctx):
    nt = xm.shape[1] // FFN_TT
    tok = pl.BlockSpec((BATCH, FFN_TT, D_MODEL), lambda j: (0, j, 0))
    return pl.pallas_call(
        functools.partial(_ffn_kernel, skip_ctx=skip_ctx),
        grid=(nt,),
        in_specs=[tok,
                  _layer(l, (16, ADA_CHUNKS * D_MODEL)),
                  _layer(l, (1, D_MODEL)),
                  _layer(0, (D_MODEL, D_FF), single=True),
                  _layer(0, (D_FF, D_MODEL), single=True)],
        out_specs=tok,
        out_shape=jax.ShapeDtypeStruct(xm.shape, F32),
        compiler_params=_cparams(1),
        name="ffn",
    )(xm, mod, sw["norm_ffn"], w1, w2)


def _stacked_weights(p):
    perm, _ = _mla_rope_perm()
    zeros = lambda *s: jnp.zeros((DEPTH,) + s, F32)
    wu = p["mla_w_ukv"].reshape(DEPTH, KV_LORA, MLA_HEADS, MLA_NOPE + MLA_V)
    pad_heads = lambda t: jnp.concatenate(
        [t, zeros(t.shape[1], MLA_HEADS, HEAD_PAD - t.shape[3])], -1).reshape(DEPTH, t.shape[1], HEADS_W)
    top = jnp.concatenate([pad_heads(wu[..., :MLA_NOPE]), zeros(KV_LORA, HEADS_W),
                           pad_heads(wu[..., MLA_NOPE:])], axis=2)
    place = jnp.broadcast_to(jnp.asarray(_kv_place())[None], (DEPTH, 128, 3 * HEADS_W))
    wkv = jnp.concatenate([top, place], axis=1).astype(BF16)

    wuq = p["mla_w_uq"].reshape(DEPTH, Q_LORA, MLA_HEADS, MLA_QK)
    qp = jnp.concatenate([zeros(Q_LORA, MLA_HEADS, MLA_NOPE), wuq[..., MLA_NOPE:][..., perm],
                          zeros(Q_LORA, MLA_HEADS, 32)], -1).reshape(DEPTH, Q_LORA, HEADS_W)
    wq = jnp.concatenate([pad_heads(wuq), qp], axis=2).astype(BF16)

    def head_w(v):
        wf = jnp.concatenate([v, zeros(32)], -1)[:, None, :]
        wp = jnp.concatenate([zeros(MLA_NOPE), v[:, MLA_NOPE:][:, perm], zeros(32)], -1)[:, None, :]
        return wf, wp

    head_ws = jnp.concatenate(head_w(p["mla_qk_norm_k"]) + head_w(p["mla_qk_norm_q"]), axis=1)

    def b_compact(b):
        t = b.transpose(0, 1, 2, 4, 3).reshape(DEPTH, 2, BRANCH_W, S5_STATE)
        return jnp.concatenate([t, t], axis=-1)

    c_compact = lambda c: c.transpose(0, 1, 4, 2, 3).reshape(DEPTH, 2, S5_STATE, BRANCH_W)
    vec = lambda a: a.reshape(DEPTH, 2, 1, S5_LANES)

    logit = p["ret_decay_logit"]
    row = lambda a: a[:, None, :]
    return dict(
        norm_mix=row(p["norm_mix_w"]), norm_ffn=row(p["norm_ffn_w"]),
        kv_norm=row(p["mla_kv_norm"]), q_norm=row(p["mla_q_norm"]),
        wkv=wkv, wq=wq, head_w=head_ws,
        s5_lam_re=vec(p["s5_lam_re"]), s5_lam_im=vec(p["s5_lam_im"]),
        s5_log_step=vec(jnp.repeat(p["s5_log_step"], S5_STATE, axis=-1)),
        s5_bre=b_compact(p["s5_b_re"]), s5_bim=b_compact(p["s5_b_im"]),
        s5_cre=c_compact(p["s5_c_re"]), s5_cim=c_compact(p["s5_c_im"]),
        s5_d=row(p["s5_d"]), w_glu=p["s5_w_glu"].astype(BF16),
        lgl=jnp.repeat(logit, RET_HD, axis=-1).reshape(DEPTH, 2, 1, BRANCH_W),
        lgh=jnp.broadcast_to(logit[:, :, :, None, None], (DEPTH, 2, RET_HEADS, 1, 128)),
        gn_w=row(p["ret_gn_w"]),
        w_branch=p["w_branch"].astype(BF16), w_out=p["w_out"].astype(BF16))


def kernel(x, c, ctx, c_ctx, ada_w, ada_b, norm_mix_w, norm_ffn_w, w_in, mla_q_norm, mla_w_uq, mla_kv_norm,
           mla_w_ukv, mla_qk_norm_q, mla_qk_norm_k, s5_lam_re, s5_lam_im, s5_log_step, s5_b_re, s5_b_im,
           s5_c_re, s5_c_im, s5_d, s5_w_glu, ret_decay_logit, ret_gn_w, w_branch, w_out, ffn_w1, ffn_w2):
    p = dict(norm_mix_w=norm_mix_w, norm_ffn_w=norm_ffn_w, w_in=w_in, mla_q_norm=mla_q_norm,
             mla_w_uq=mla_w_uq, mla_kv_norm=mla_kv_norm, mla_w_ukv=mla_w_ukv, mla_qk_norm_q=mla_qk_norm_q,
             mla_qk_norm_k=mla_qk_norm_k, s5_lam_re=s5_lam_re, s5_lam_im=s5_lam_im, s5_log_step=s5_log_step,
             s5_b_re=s5_b_re, s5_b_im=s5_b_im, s5_c_re=s5_c_re, s5_c_im=s5_c_im, s5_d=s5_d,
             s5_w_glu=s5_w_glu, ret_decay_logit=ret_decay_logit, ret_gn_w=ret_gn_w, w_branch=w_branch,
             w_out=w_out, ffn_w1=ffn_w1, ffn_w2=ffn_w2)
    sw = _stacked_weights(p)
    c16 = jnp.concatenate([c, c_ctx[None, :], jnp.zeros((16 - BATCH - 1, D_MODEL), F32)], axis=0)
    mod = _ada_mod(c16, ada_w, ada_b)
    lam, bblk, cblk = _s5_params(sw)
    w_in_t32 = jnp.swapaxes(w_in, 1, 2)
    w_in_t = w_in_t32[0:1].astype(BF16)
    x_ctx, x_lat = ctx, x
    for l in range(DEPTH):
        last = l == DEPTH - 1
        q, k, v, u_t, uc, us, rq, rk, rv, rg, w1, w2 = _in_proj(
            l, x_ctx, x_lat, mod, sw, w_in_t, [(ffn_w1, l, 32), (ffn_w2, l, 32)])
        oa_lat = _attention_latent(q, k, v)
        oa_ctx = oa_lat if last else _attention_context(q, k, v)
        ob = _fnet(uc, us)
        yf, yb = _s5(l, u_t, lam, bblk, cblk)
        oret = _retention(l, rq, rk, rv, sw)
        xm, *w_in_next = _merge(l, x_ctx, x_lat, mod, sw, w_in_t, oa_ctx, oa_lat, ob, yf, yb, u_t, oret, rg, last,
                                [] if last else [(w_in_t32, l + 1, 18)])
        if not last:
            w_in_t, = w_in_next
        x_ctx = x_lat = _ffn(l, xm, mod, sw, w1, w2, last)
    return x_lat
```

```python
import functools
import math

import numpy as np
import jax
import jax.numpy as jnp
from jax import lax
from jax.experimental import pallas as pl
from jax.experimental.pallas import tpu as pltpu

F32 = jnp.float32
BF16 = jnp.bfloat16

D_MODEL = 1024
BATCH = 8
SEQ = 2048
DEPTH = 2
GRID_W = 64
CTX_LEN = 256
LCAT = CTX_LEN + SEQ
N_BRANCH = 4
BRANCH_W = 256
NORM_EPS = 1e-6
ROPE_BASE = 10000.0
ADA_CHUNKS = 6

MLA_HEADS = 4
MLA_NOPE = 64
MLA_ROPE = 32
MLA_QK = MLA_NOPE + MLA_ROPE
MLA_V = 64
Q_LORA = 256
KV_LORA = 128
HEAD_PAD = 128
HEADS_W = MLA_HEADS * HEAD_PAD

FNET_GROUPS = 4
FNET_GW = BRANCH_W // FNET_GROUPS

S5_GROUP_CH = 16
S5_GROUPS = BRANCH_W // S5_GROUP_CH
S5_STATE = 64
S5_LANES = S5_GROUPS * S5_STATE

RET_HEADS = 4
RET_HD = BRANCH_W // RET_HEADS
RET_CHUNK = 256
RET_NCHUNK = LCAT // RET_CHUNK
RET_CTX_CHUNKS = CTX_LEN // RET_CHUNK
RET_SPB = 4

D_FF = 4 * D_MODEL

_O_KV, _O_KR, _O_S5, _O_RK, _O_RV = 0, 128, 160, 416, 672
_O_Q, _O_FN, _O_RQ, _O_RG, _O_GATE = 928, 1184, 1440, 1696, 1952
IN_COLS = _O_GATE + N_BRANCH * D_MODEL

TT = 64
TR = TT * BATCH
FFN_TT = 128
FFN_PIECE = 2048
CTX_TILES = CTX_LEN // TT
ALL_TILES = LCAT // TT
TQ = 256
ATTN_QB = 8
ATTN_UNIT_QB = 2
ATTN_AHEAD = 1
S5_TC = 128
S5_ROWS = S5_TC * BATCH
S5_STEPS = LCAT // S5_TC
S5_CTX_STEPS = CTX_LEN // S5_TC

VMEM_LIMIT = 56 * 1024 * 1024


def _cparams(n_grid):
    return pltpu.CompilerParams(dimension_semantics=("arbitrary",) * n_grid,
                                vmem_limit_bytes=VMEM_LIMIT)


def _dot(a, b):
    return jnp.dot(a, b, preferred_element_type=F32)


def _dot_nt(a, b):
    return lax.dot_general(a, b, (((1,), (1,)), ((), ())), preferred_element_type=F32)


def _sigmoid(x):
    return 0.5 * (jnp.tanh(0.5 * x) + 1.0)


def _gelu_tanh(y):
    return 0.5 * y * (1.0 + jnp.tanh(math.sqrt(2.0 / math.pi) * (y + 0.044715 * (y * y * y))))


def _rms(x, w):
    return x * lax.rsqrt(jnp.mean(x * x, axis=-1, keepdims=True) + NORM_EPS) * w


def _full(shape):
    n = len(shape)
    return pl.BlockSpec(shape, lambda *_: (0,) * n)


def _layer(l, shape, single=False):
    n = len(shape)
    mode = dict(pipeline_mode=pl.Buffered(1)) if single else {}
    return pl.BlockSpec((1,) + tuple(shape), lambda *_: (l,) + (0,) * n, **mode)


def _mla_rope_perm():
    r = np.arange(MLA_ROPE)
    first = (r % 16) < 8
    return np.where(first, r + 8, r - 8), np.where(first, -1.0, 1.0)


@functools.lru_cache(maxsize=None)
def _mla_tables():
    pos = np.arange(SEQ)
    rows, cols = pos // GRID_W, pos % GRID_W
    freqs = ROPE_BASE ** (-np.arange(8, dtype=np.float64) / 8)
    r = np.arange(MLA_ROPE)
    _, sign = _mla_rope_perm()
    p = np.where((r // 16 == 0)[None, :], rows[:, None], cols[:, None]).astype(np.float64)
    ang = p * freqs[(r % 16) % 8][None, :]
    cosf = np.zeros((LCAT, HEAD_PAD))
    sinf = np.zeros((LCAT, HEAD_PAD))
    cosf[:, :MLA_QK] = 1.0
    cosf[CTX_LEN:, MLA_NOPE:MLA_QK] = np.cos(ang)
    sinf[CTX_LEN:, MLA_NOPE:MLA_QK] = np.sin(ang) * sign[None, :]
    return cosf.astype(np.float32), sinf.astype(np.float32)


def _ret_perm():
    d = np.arange(RET_HD)
    first = d < RET_HD // 2
    return np.where(first, d + RET_HD // 2, d - RET_HD // 2), np.where(first, -1.0, 1.0)


@functools.lru_cache(maxsize=None)
def _ret_tables():
    half = RET_HD // 2
    pos = np.arange(SEQ, dtype=np.float64)
    freqs = ROPE_BASE ** (-np.arange(half, dtype=np.float64) / half)
    d = np.arange(RET_HD)
    perm, sign = _ret_perm()
    ang = pos[:, None] * freqs[d % half][None, :]
    cosr = np.ones((LCAT, RET_HD))
    sinr = np.zeros((LCAT, RET_HD))
    cosr[CTX_LEN:] = np.cos(ang)
    sinr[CTX_LEN:] = np.sin(ang) * sign[None, :]
    cosr = np.tile(cosr, (1, RET_HEADS))
    sinr = np.tile(sinr, (1, RET_HEADS))
    pm = np.zeros((BRANCH_W, BRANCH_W))
    for h in range(RET_HEADS):
        pm[h * RET_HD + perm, h * RET_HD + d] = 1.0
    return cosr.astype(np.float32), sinr.astype(np.float32), pm.astype(np.float32)


def _dft(n, scale):
    k = np.arange(n)
    kt = (k[:, None] * k[None, :]) % n
    ang = 2.0 * np.pi * kt / n
    return np.cos(ang) * scale, np.sin(ang) * scale


@functools.lru_cache(maxsize=None)
def _fnet_tables():
    cw, sw = _dft(FNET_GW, 1.0)
    t = np.zeros((BRANCH_W, 2 * BRANCH_W))
    for g in range(FNET_GROUPS):
        s = slice(g * FNET_GW, (g + 1) * FNET_GW)
        t[s, s] = cw
        t[s, BRANCH_W + g * FNET_GW:BRANCH_W + (g + 1) * FNET_GW] = -sw
    cl, sl = _dft(SEQ, 1.0 / math.sqrt(SEQ * FNET_GW))
    clc, slc = _dft(CTX_LEN, 1.0 / math.sqrt(CTX_LEN * FNET_GW))
    return tuple(a.astype(np.float32) for a in (t, cl, sl, clc, slc))


def _bf16_const(a):
    return jnp.asarray(a).astype(BF16)


@functools.lru_cache(maxsize=None)
def _head_avg():
    p = np.zeros((BRANCH_W, BRANCH_W))
    for h in range(RET_HEADS):
        p[h * RET_HD:(h + 1) * RET_HD, h * RET_HD:(h + 1) * RET_HD] = 1.0 / RET_HD
    return p.astype(np.float32)


@functools.lru_cache(maxsize=None)
def _tile_perm():
    p = np.zeros((TR, TR), np.float32)
    for b in range(BATCH):
        for t in range(TT):
            p[t * BATCH + b, b * TT + t] = 1.0
    return p


@functools.lru_cache(maxsize=None)
def _kv_place():
    perm, _ = _mla_rope_perm()
    place = np.zeros((128, 3 * HEADS_W), np.float32)
    for h in range(MLA_HEADS):
        for r in range(MLA_ROPE):
            place[r, h * HEAD_PAD + MLA_NOPE + r] = 1.0
            place[perm[r], HEADS_W + h * HEAD_PAD + MLA_NOPE + r] = 1.0
    return place


@functools.lru_cache(maxsize=None)
def _v_ones():
    v = np.zeros((1, HEADS_W), np.float32)
    v[0, np.arange(MLA_HEADS) * HEAD_PAD + MLA_V] = 1.0
    return v


ADA_TN = 1536


def _ada_kernel(c_ref, w_ref, b_ref, o_ref):
    c = c_ref[...]
    s = (c * _sigmoid(c)).astype(BF16)
    o_ref[0] = _dot(s, w_ref[0].astype(BF16)) + b_ref[0]


def _ada_mod(c16, ada_w, ada_b):
    n = ADA_CHUNKS * D_MODEL
    return pl.pallas_call(
        _ada_kernel,
        grid=(DEPTH, n // ADA_TN),
        in_specs=[pl.BlockSpec((16, D_MODEL), lambda l, j: (0, 0)),
                  pl.BlockSpec((1, D_MODEL, ADA_TN), lambda l, j: (l, 0, j)),
                  pl.BlockSpec((1, 1, ADA_TN), lambda l, j: (l, 0, j))],
        out_specs=pl.BlockSpec((1, 16, ADA_TN), lambda l, j: (l, 0, j)),
        out_shape=jax.ShapeDtypeStruct((DEPTH, 16, n), F32),
        compiler_params=_cparams(2),
        name="ada_mod",
    )(c16, ada_w, ada_b.reshape(DEPTH, 1, n))


def _side_casts(jobs):
    specs_in, specs_out, shapes, args = [], [], [], []
    for a, l, n in jobs:
        _, r, c = a.shape
        blk = (1, r // n, c)
        specs_in.append(pl.BlockSpec(blk, lambda j, l=l, n=n: (l, jnp.minimum(j, n - 1), 0)))
        specs_out.append(pl.BlockSpec(blk, lambda j, n=n: (0, jnp.minimum(j, n - 1), 0)))
        shapes.append(jax.ShapeDtypeStruct((1, r, c), BF16))
        args.append(a)
    return specs_in, specs_out, shapes, args


def _with_side_casts(body, n_in, n_out, n_jobs):
    def wrapped(*refs):
        body(*refs[:n_in], *refs[n_in + n_jobs:n_in + n_jobs + n_out])
        for src, dst in zip(refs[n_in:n_in + n_jobs], refs[n_in + n_jobs + n_out:]):
            dst[...] = src[...].astype(BF16)
    return wrapped if n_jobs else body


def _mod_chunks(mod_ref, is_ctx, idxs):
    out = []
    for i in idxs:
        sl = slice(i * D_MODEL, (i + 1) * D_MODEL)
        m = mod_ref[0, 0:BATCH, sl]
        if is_ctx is not None:
            m = jnp.where(is_ctx, mod_ref[0, BATCH:BATCH + 1, sl], m)
        out.append(m[:, None, :])
    return out


def _tile_specs(separate, skip_ctx, width=D_MODEL):
    base = 0 if separate else CTX_TILES
    shape = (BATCH, TT, width)
    if skip_ctx:
        return (pl.BlockSpec(shape, lambda j: (0, 0, 0)),
                pl.BlockSpec(shape, lambda j: (0, j + base, 0)))
    return (pl.BlockSpec(shape, lambda j: (0, jnp.minimum(j, CTX_TILES - 1), 0)),
            pl.BlockSpec(shape, lambda j: (0, jnp.maximum(j - CTX_TILES, 0) + base, 0)))


def _head_norm_rot(xf, xp, a, b, scale):
    outs = []
    for h in range(MLA_HEADS):
        f = xf[:, h * HEAD_PAD:(h + 1) * HEAD_PAD]
        p = xp[:, h * HEAD_PAD:(h + 1) * HEAD_PAD]
        n = lax.rsqrt(jnp.sum(f * f, axis=-1, keepdims=True) * (1.0 / MLA_QK) + NORM_EPS) * scale
        outs.append(n * (f * a + p * b))
    return jnp.concatenate(outs, axis=-1)


def _rows(table):
    w = table.shape[-1]
    return jnp.broadcast_to(table[None], (BATCH, TT, w)).reshape(TR, w)


def _tile3(x):
    return x.reshape(BATCH, TT, x.shape[-1])


def _inproj_kernel(xc_ref, xl_ref, mod_ref, nw_ref, w_ref, kvw_ref, wkv_ref, qnw_ref, wq_ref,
                   cosf_ref, sinf_ref, hw_ref, pm_ref, cosr_ref, sinr_ref, t_ref, ptb_ref, vone_ref,
                   q_out, k_out, v_out, ut_out, uc_out, us_out, rq_out, rk_out, rv_out, rg_out):
    is_ctx = pl.program_id(0) < CTX_TILES
    x3 = jnp.where(is_ctx, xc_ref[...], xl_ref[...])
    sh, sc = _mod_chunks(mod_ref, is_ctx, (0, 1))
    zas, zs = [], []
    for half in (slice(0, BATCH // 2), slice(BATCH // 2, BATCH)):
        hh = (_rms(x3[half], nw_ref[0]) * (1.0 + sc[half]) + sh[half]).reshape(TR // 2, D_MODEL).astype(BF16)
        zas.append(_dot_nt(hh, w_ref[0, 0:256, :]))
        zs.append(_dot_nt(hh, w_ref[0, _O_S5:_O_GATE, :]))
    za, z = jnp.concatenate(zas, axis=0), jnp.concatenate(zs, axis=0)

    cosf, sinf = _rows(cosf_ref[...]), _rows(sinf_ref[...])
    hw = hw_ref[0]
    kvn = _rms(za[:, 0:128], kvw_ref[0]).astype(BF16)
    lhs = jnp.concatenate([kvn, za[:, 128:256].astype(BF16)], axis=-1)
    kv = _dot(lhs, wkv_ref[0])
    k = _head_norm_rot(kv[:, 0:HEADS_W], kv[:, HEADS_W:2 * HEADS_W],
                       cosf * hw[0:1], sinf * hw[1:2], 1.0)
    k_out[...] = _tile3(k).astype(BF16)
    v_out[...] = _tile3(kv[:, 2 * HEADS_W:3 * HEADS_W] + vone_ref[...]).astype(BF16)
    qn = _rms(z[:, 768:1024], qnw_ref[0]).astype(BF16)
    qq = _dot(qn, wq_ref[0])
    q = _head_norm_rot(qq[:, 0:HEADS_W], qq[:, HEADS_W:2 * HEADS_W],
                       cosf * hw[2:3], sinf * hw[3:4], MLA_QK ** -0.5)
    q_out[...] = _tile3(q).astype(BF16)
    cosr, sinr = _rows(cosr_ref[...]), _rows(sinr_ref[...])
    rq = z[:, 1280:1536]
    rk = z[:, 256:512]
    rq = rq * cosr + _dot(rq.astype(BF16), pm_ref[...]) * sinr
    rk = rk * cosr + _dot(rk.astype(BF16), pm_ref[...]) * sinr
    rq_out[...] = _tile3(rq).astype(BF16)
    rk_out[...] = _tile3(rk * (RET_HD ** -0.5)).astype(BF16)
    rv_out[...] = _tile3(z[:, 512:768]).astype(BF16)
    rg_out[...] = _tile3(z[:, 1536:1792]).astype(BF16)
    ucs = _dot(z[:, 1024:1280].astype(BF16), t_ref[...])
    uc_out[...] = _tile3(ucs[:, 0:BRANCH_W]).astype(BF16)
    us_out[...] = _tile3(ucs[:, BRANCH_W:2 * BRANCH_W]).astype(BF16)
    ut_out[...] = _dot(ptb_ref[...], z[:, 0:256].astype(BF16)).astype(BF16)


def _in_proj(l, x_ctx, x_lat, mod, sw, w_in_t, jobs):
    ctx_spec, lat_spec = _tile_specs(l == 0, False)
    tok = lambda w: pl.BlockSpec((BATCH, TT, w), lambda j: (0, j, 0))
    tab = lambda w: pl.BlockSpec((TT, w), lambda j: (j, 0))
    cosf, sinf = _mla_tables()
    cosr, sinr, pm = _ret_tables()
    bshape = lambda w: jax.ShapeDtypeStruct((BATCH, LCAT, w), BF16)
    job_in, job_out, job_shapes, job_args = _side_casts(jobs)
    return pl.pallas_call(
        _with_side_casts(_inproj_kernel, 18, 10, len(jobs)),
        grid=(ALL_TILES,),
        in_specs=[ctx_spec, lat_spec,
                  _layer(l, (16, ADA_CHUNKS * D_MODEL)),
                  _layer(l, (1, D_MODEL)),
                  _layer(0, (_O_GATE, D_MODEL)),
                  _layer(l, (1, KV_LORA)),
                  _layer(l, (256, 3 * HEADS_W)),
                  _layer(l, (1, Q_LORA)),
                  _layer(l, (Q_LORA, 2 * HEADS_W)),
                  tab(HEAD_PAD), tab(HEAD_PAD),
                  _layer(l, (4, HEAD_PAD)),
                  _full((BRANCH_W, BRANCH_W)),
                  tab(BRANCH_W), tab(BRANCH_W),
                  _full((BRANCH_W, 2 * BRANCH_W)),
                  _full((TR, TR)),
                  _full((1, HEADS_W))] + job_in,
        out_specs=[tok(HEADS_W), tok(HEADS_W), tok(HEADS_W),
                   pl.BlockSpec((TR, BRANCH_W), lambda j: (j, 0)),
                   tok(256), tok(256), tok(256), tok(256), tok(256), tok(256)] + job_out,
        out_shape=[bshape(HEADS_W), bshape(HEADS_W), bshape(HEADS_W),
                   jax.ShapeDtypeStruct((LCAT * BATCH, BRANCH_W), BF16),
                   bshape(256), bshape(256), bshape(256), bshape(256), bshape(256), bshape(256)] + job_shapes,
        compiler_params=_cparams(1),
        name="in_proj",
    )(x_ctx, x_lat, mod, sw["norm_mix"], w_in_t, sw["kv_norm"], sw["wkv"], sw["q_norm"], sw["wq"],
      jnp.asarray(cosf), jnp.asarray(sinf), sw["head_w"], jnp.asarray(pm, dtype=BF16),
      jnp.asarray(cosr), jnp.asarray(sinr), _bf16_const(_fnet_tables()[0]),
      jnp.asarray(_tile_perm(), dtype=BF16), jnp.asarray(_v_ones()), *job_args)


def _attn_kernel(*refs):
    q_refs, (k_ref, v_ref, o_ref) = refs[:-3], refs[-3:]
    per = min(ATTN_UNIT_QB, len(q_refs))
    units = [(b, g, h, slice(h * HEAD_PAD, (h + 1) * HEAD_PAD))
             for b in range(k_ref.shape[0]) for g in range(len(q_refs) // per) for h in range(MLA_HEADS)]

    def scores(u):
        b, g, _, sl = u
        q = jnp.concatenate([q_refs[g * per + i][b, :, sl] for i in range(per)], axis=0)
        return _dot_nt(q, k_ref[b, :, sl])

    lane = lax.broadcasted_iota(jnp.int32, (1, HEAD_PAD), 1)
    pending = [scores(u) for u in units[:ATTN_AHEAD]]
    for i, (b, g, h, sl) in enumerate(units):
        if i + ATTN_AHEAD < len(units):
            pending.append(scores(units[i + ATTN_AHEAD]))
        s = pending.pop(0)
        p = jnp.exp((s - jnp.max(s, axis=-1, keepdims=True)).astype(BF16))
        oh = _dot(p, v_ref[b, :, sl])
        oh = oh * (1.0 / oh[:, MLA_V:MLA_V + 1])
        if h % 2 == 0:
            even = oh
        else:
            pair = jnp.where(lane < MLA_V, even, pltpu.roll(oh, MLA_V, axis=1))
            rows = slice(g * per * TQ, (g + 1) * per * TQ)
            o_ref[b, rows, (h // 2) * HEAD_PAD:(h // 2 + 1) * HEAD_PAD] = pair.astype(BF16)


def _attention_latent(q, k, v):
    first = CTX_LEN // TQ
    qspec = lambda r: pl.BlockSpec((1, TQ, HEADS_W), lambda b, j: (b, first + ATTN_QB * j + r, 0))
    kv = pl.BlockSpec((1, LCAT, HEADS_W), lambda b, j: (b, 0, 0))
    return pl.pallas_call(
        _attn_kernel,
        grid=(BATCH, SEQ // (ATTN_QB * TQ)),
        in_specs=[qspec(r) for r in range(ATTN_QB)] + [kv, kv],
        out_specs=pl.BlockSpec((1, ATTN_QB * TQ, BRANCH_W), lambda b, j: (b, j, 0)),
        out_shape=jax.ShapeDtypeStruct((BATCH, SEQ, BRANCH_W), BF16),
        compiler_params=_cparams(2),
        name="mla_attention",
    )(*([q] * ATTN_QB), k, v)


def _attention_context(q, k, v):
    blk = pl.BlockSpec((1, CTX_LEN, HEADS_W), lambda b: (b, 0, 0))
    return pl.pallas_call(
        _attn_kernel,
        grid=(BATCH,),
        in_specs=[blk, blk, blk],
        out_specs=pl.BlockSpec((1, CTX_LEN, BRANCH_W), lambda b: (b, 0, 0)),
        out_shape=jax.ShapeDtypeStruct((BATCH, CTX_LEN, BRANCH_W), BF16),
        compiler_params=_cparams(1),
        name="mla_attention_ctx",
    )(q, k, v)


def _fnet_kernel(uc_ref, us_ref, cl_ref, sl_ref, clc_ref, slc_ref, o_ref):
    o_ref[0, 0:CTX_LEN, :] = (_dot(clc_ref[...], uc_ref[0, 0:CTX_LEN, :])
                              + _dot(slc_ref[...], us_ref[0, 0:CTX_LEN, :])).astype(BF16)
    o_ref[0, CTX_LEN:LCAT, :] = (_dot(cl_ref[...], uc_ref[0, CTX_LEN:LCAT, :])
                                 + _dot(sl_ref[...], us_ref[0, CTX_LEN:LCAT, :])).astype(BF16)


def _fnet(uc, us):
    cl, sl, clc, slc = (_bf16_const(a) for a in _fnet_tables()[1:])
    tok = pl.BlockSpec((1, LCAT, BRANCH_W), lambda b: (b, 0, 0))
    return pl.pallas_call(
        _fnet_kernel,
        grid=(BATCH,),
        in_specs=[tok, tok, _full((SEQ, SEQ)), _full((SEQ, SEQ)),
                  _full((CTX_LEN, CTX_LEN)), _full((CTX_LEN, CTX_LEN))],
        out_specs=tok,
        out_shape=jax.ShapeDtypeStruct((BATCH, LCAT, BRANCH_W), BF16),
        compiler_params=_cparams(1),
        name="fnet_dft",
    )(uc, us, cl, sl, clc, slc)


def _s5_param_kernel(lr_ref, li_ref, ls_ref, bre_ref, bim_ref, cre_ref, cim_ref, lam_out, b_out, c_out):
    lr, li = lr_ref[0, 0], li_ref[0, 0]
    step = jnp.exp(ls_ref[0, 0])
    mag = jnp.exp(lr * step)
    lbr = mag * jnp.cos(li * step)
    lbi = mag * jnp.sin(li * step)
    den = 1.0 / (lr * lr + li * li)
    cr = ((lbr - 1.0) * lr + lbi * li) * den
    ci = (lbi * lr - (lbr - 1.0) * li) * den
    lam_out[0, 0, 0] = jnp.broadcast_to(lbr, (BATCH, S5_LANES))
    lam_out[0, 0, 1] = jnp.broadcast_to(lbi, (BATCH, S5_LANES))
    grp_b = (lax.broadcasted_iota(jnp.int32, (BRANCH_W, S5_LANES), 0) // S5_GROUP_CH
             == lax.broadcasted_iota(jnp.int32, (BRANCH_W, S5_LANES), 1) // S5_STATE)
    wide = lambda r: jnp.concatenate([r[0, 0]] * (S5_LANES // 128), axis=-1)
    bre = jnp.where(grp_b, wide(bre_ref), 0.0)
    bim = jnp.where(grp_b, wide(bim_ref), 0.0)
    b_out[0, 0, :, 0:S5_LANES] = (cr * bre - ci * bim).astype(BF16)
    b_out[0, 0, :, S5_LANES:2 * S5_LANES] = (cr * bim + ci * bre).astype(BF16)
    grp_c = (lax.broadcasted_iota(jnp.int32, (S5_LANES, BRANCH_W), 0) // S5_STATE
             == lax.broadcasted_iota(jnp.int32, (S5_LANES, BRANCH_W), 1) // S5_GROUP_CH)
    tall = lambda r: jnp.concatenate([r[0, 0]] * S5_GROUPS, axis=0)
    c_out[0, 0, 0:S5_LANES, :] = jnp.where(grp_c, tall(cre_ref), 0.0).astype(BF16)
    c_out[0, 0, S5_LANES:2 * S5_LANES, :] = jnp.where(grp_c, -tall(cim_ref), 0.0).astype(BF16)


def _s5_params(sw):
    spec = lambda *s: pl.BlockSpec((1, 1) + s, lambda l, d: (l, d) + (0,) * len(s))
    return pl.pallas_call(
        _s5_param_kernel,
        grid=(DEPTH, 2),
        in_specs=[spec(1, S5_LANES), spec(1, S5_LANES), spec(1, S5_LANES),
                  spec(BRANCH_W, 128), spec(BRANCH_W, 128), spec(S5_STATE, BRANCH_W), spec(S5_STATE, BRANCH_W)],
        out_specs=[spec(2, BATCH, S5_LANES), spec(BRANCH_W, 2 * S5_LANES), spec(2 * S5_LANES, BRANCH_W)],
        out_shape=[jax.ShapeDtypeStruct((DEPTH, 2, 2, BATCH, S5_LANES), F32),
                   jax.ShapeDtypeStruct((DEPTH, 2, BRANCH_W, 2 * S5_LANES), BF16),
                   jax.ShapeDtypeStruct((DEPTH, 2, 2 * S5_LANES, BRANCH_W), BF16)],
        compiler_params=_cparams(2),
        name="s5_discretise",
    )(sw["s5_lam_re"], sw["s5_lam_im"], sw["s5_log_step"], sw["s5_bre"], sw["s5_bim"], sw["s5_cre"], sw["s5_cim"])


def _s5_bwd_block(i):
    return jnp.where(i < S5_CTX_STEPS, S5_CTX_STEPS - 1 - i, S5_STEPS + S5_CTX_STEPS - 1 - i)


def _s5_kernel(uf_ref, ub_ref, lam_ref, b_ref, c_ref, yf_ref, yb_ref, xf_scr, xb_scr, st_scr):
    @pl.when(pl.program_id(0) == 0)
    def _():
        st_scr[...] = jnp.zeros_like(st_scr)

    dirs = ((uf_ref, xf_scr, yf_ref), (ub_ref, xb_scr, yb_ref))
    for d, (u_ref, x_scr, _) in enumerate(dirs):
        x_scr[...] = _dot(u_ref[...], b_ref[0, d])
    re, im = pl.ds(0, S5_LANES), pl.ds(S5_LANES, S5_LANES)
    for d, (_, x_scr, y_ref) in enumerate(dirs):
        xr, xi = st_scr[2 * d], st_scr[2 * d + 1]
        ar, ai = lam_ref[0, d, 0], lam_ref[0, d, 1]
        for t in (range(S5_TC) if d == 0 else range(S5_TC - 1, -1, -1)):
            rows = pl.ds(t * BATCH, BATCH)
            xr, xi = (ar * xr - ai * xi + x_scr[rows, re], ar * xi + ai * xr + x_scr[rows, im])
            x_scr[rows, re] = xr
            x_scr[rows, im] = xi
        st_scr[2 * d], st_scr[2 * d + 1] = xr, xi
        y_ref[...] = _dot(x_scr[...].astype(BF16), c_ref[0, d]).astype(BF16)


def _s5(l, u_t, lam, bblk, cblk):
    fwd = pl.BlockSpec((S5_ROWS, BRANCH_W), lambda i: (i, 0))
    bwd = pl.BlockSpec((S5_ROWS, BRANCH_W), lambda i: (_s5_bwd_block(i), 0))
    return pl.pallas_call(
        _s5_kernel,
        grid=(S5_STEPS,),
        in_specs=[fwd, bwd, _layer(l, (2, 2, BATCH, S5_LANES)),
                  _layer(l, (2, BRANCH_W, 2 * S5_LANES)), _layer(l, (2, 2 * S5_LANES, BRANCH_W))],
        out_specs=[fwd, bwd],
        out_shape=[jax.ShapeDtypeStruct((LCAT * BATCH, BRANCH_W), BF16)] * 2,
        scratch_shapes=[pltpu.VMEM((S5_ROWS, 2 * S5_LANES), F32),
                        pltpu.VMEM((S5_ROWS, 2 * S5_LANES), F32),
                        pltpu.VMEM((4, BATCH, S5_LANES), F32)],
        compiler_params=_cparams(1),
        name="s5_scan",
    )(u_t, u_t, lam, bblk, cblk)


def _log_sigmoid(x):
    return jnp.minimum(x, 0.0) - jnp.log(1.0 + jnp.exp(-jnp.abs(x)))


def _ret_kernel(q_ref, k_ref, v_ref, lgl_ref, lgh_ref, o_ref, dec_scr, sb_scr, sf_scr, sbc_scr):
    c_len = RET_CHUNK
    lgl = _log_sigmoid(lgl_ref[0])
    lgf, lgb = lgl[0], lgl[1]
    ti = lax.broadcasted_iota(jnp.int32, (c_len, BRANCH_W), 0).astype(F32)
    qdf = jnp.exp(lgf * (ti + 1.0))
    kdf = jnp.exp(lgf * (c_len - 1.0 - ti))
    qdb = jnp.exp(lgb * (c_len - ti))
    kdb = jnp.exp(lgb * ti)
    cdf = jnp.exp(lgf * float(c_len))
    cdb = jnp.exp(lgb * float(c_len))
    lane = lax.broadcasted_iota(jnp.int32, (1, BRANCH_W), 1)
    same_head = (lax.broadcasted_iota(jnp.int32, (BRANCH_W, BRANCH_W), 0) // RET_HD
                 == lax.broadcasted_iota(jnp.int32, (BRANCH_W, BRANCH_W), 1) // RET_HD)

    diff = (lax.broadcasted_iota(jnp.int32, (c_len, c_len), 0)
            - lax.broadcasted_iota(jnp.int32, (c_len, c_len), 1)).astype(F32)
    for h in range(RET_HEADS):
        gf = jnp.concatenate([_log_sigmoid(lgh_ref[0, 0, h])] * (c_len // 128), axis=-1)
        gb = jnp.concatenate([_log_sigmoid(lgh_ref[0, 1, h])] * (c_len // 128), axis=-1)
        dec_scr[h] = (jnp.where(diff >= 0, jnp.exp(gf * jnp.maximum(diff, 0.0)), 0.0)
                      + jnp.where(diff <= 0, jnp.exp(gb * jnp.maximum(-diff, 0.0)), 0.0))

    def chunk(ref, s, c):
        return ref[s, pl.ds(pl.multiple_of(c * c_len, c_len), c_len), :]

    def kv_outer(kd, v):
        s = lax.dot_general(kd.astype(BF16), v, (((0,), (0,)), ((), ())), preferred_element_type=F32)
        return jnp.where(same_head, s, 0.0)

    sbc_scr[...] = jnp.zeros_like(sbc_scr)

    def bwd(i, _):
        c = jnp.where(i < RET_CTX_CHUNKS, RET_CTX_CHUNKS - 1 - i, RET_NCHUNK + RET_CTX_CHUNKS - 1 - i)
        for s in range(RET_SPB):
            sb_scr[s, c] = sbc_scr[s].astype(BF16)
            k = chunk(k_ref, s, c).astype(F32)
            sbc_scr[s] = sbc_scr[s] * cdb + kv_outer(k * kdb, chunk(v_ref, s, c))
        return 0

    lax.fori_loop(0, RET_NCHUNK, bwd, 0)

    sf_scr[...] = jnp.zeros_like(sf_scr)

    def fwd(c, _):
        for s in range(RET_SPB):
            qb, kb, vb = chunk(q_ref, s, c), chunk(k_ref, s, c), chunk(v_ref, s, c)
            q = qb.astype(F32)
            o = (_dot((q * qdf).astype(BF16), sf_scr[s].astype(BF16))
                 + _dot((q * qdb).astype(BF16), sb_scr[s, c]))
            for h in range(RET_HEADS):
                hm = lane // RET_HD == h
                att = _dot_nt(jnp.where(hm, qb, jnp.zeros_like(qb)), kb)
                oh = _dot((att * dec_scr[h]).astype(BF16), vb)
                o = o + jnp.where(hm, oh, 0.0)
            o_ref[s, pl.ds(pl.multiple_of(c * c_len, c_len), c_len), :] = o.astype(BF16)
            sf_scr[s] = sf_scr[s] * cdf + kv_outer(kb.astype(F32) * kdf, vb)
        return 0

    lax.fori_loop(0, RET_NCHUNK, fwd, 0)


def _retention(l, rq, rk, rv, sw):
    tok = pl.BlockSpec((RET_SPB, LCAT, BRANCH_W), lambda b: (b, 0, 0))
    return pl.pallas_call(
        _ret_kernel,
        grid=(BATCH // RET_SPB,),
        in_specs=[tok, tok, tok, _layer(l, (2, 1, BRANCH_W)), _layer(l, (2, RET_HEADS, 1, 128))],
        out_specs=tok,
        out_shape=jax.ShapeDtypeStruct((BATCH, LCAT, BRANCH_W), BF16),
        scratch_shapes=[pltpu.VMEM((RET_HEADS, RET_CHUNK, RET_CHUNK), F32),
                        pltpu.VMEM((RET_SPB, RET_NCHUNK, BRANCH_W, BRANCH_W), BF16),
                        pltpu.VMEM((RET_SPB, BRANCH_W, BRANCH_W), F32),
                        pltpu.VMEM((RET_SPB, BRANCH_W, BRANCH_W), F32)],
        compiler_params=_cparams(1),
        name="retention",
    )(rq, rk, rv, sw["lgl"], sw["lgh"])


def _merge_kernel(xc_ref, xl_ref, mod_ref, nw_ref, wg_ref, oac_ref, oal_ref, ob_ref, yf_ref, yb_ref, ut_ref,
                  d_ref, wglu_ref, oret_ref, rg_ref, gnw_ref, pavg_ref, pbt_ref, wb_ref, wout_ref,
                  o_ref, *, skip_ctx):
    if skip_ctx:
        is_ctx = None
        x3 = xl_ref[...]
        oa = oal_ref[...]
    else:
        is_ctx = pl.program_id(0) < CTX_TILES
        x3 = jnp.where(is_ctx, xc_ref[...], xl_ref[...])
        oa = jnp.where(is_ctx, oac_ref[...], oal_ref[...])
    sh, sc, gate_res = _mod_chunks(mod_ref, is_ctx, (0, 1, 2))
    gate_w = lambda n: wg_ref[0, _O_GATE + n * D_MODEL:_O_GATE + (n + 1) * D_MODEL, :]
    hs = [(_rms(x3[half], nw_ref[0]) * (1.0 + sc[half]) + sh[half]).reshape(TR // 2, D_MODEL).astype(BF16)
          for half in (slice(0, BATCH // 2), slice(BATCH // 2, BATCH))]
    gate0 = jnp.concatenate([_dot_nt(hh, gate_w(0)) for hh in hs], axis=0)
    h = jnp.concatenate(hs, axis=0)
    yt = yf_ref[...].astype(F32) + yb_ref[...].astype(F32) + d_ref[0] * ut_ref[...].astype(F32)
    y = _dot(pbt_ref[...], yt.astype(BF16))
    vg = _dot(_gelu_tanh(y).astype(BF16), wglu_ref[0])
    oc = vg[:, 0:BRANCH_W] * _sigmoid(vg[:, BRANCH_W:2 * BRANCH_W])
    o = oret_ref[...].reshape(TR, BRANCH_W)
    dl = o.astype(F32) - _dot(o, pavg_ref[...])
    var = _dot((dl * dl).astype(BF16), pavg_ref[...])
    g = rg_ref[...].reshape(TR, BRANCH_W).astype(F32)
    od = g * _sigmoid(g) * (dl * lax.rsqrt(var + NORM_EPS) * gnw_ref[0])
    branches = (oa.reshape(TR, BRANCH_W), ob_ref[...].reshape(TR, BRANCH_W), oc.astype(BF16), od.astype(BF16))
    acc = None
    for n, branch in enumerate(branches):
        gate = gate0 if n == 0 else _dot_nt(h, gate_w(n))
        term = _sigmoid(gate) * _dot(branch, wb_ref[0, n])
        acc = term if acc is None else acc + term
    m = _dot(acc.astype(BF16), wout_ref[0])
    o_ref[...] = x3 + gate_res * m.reshape(BATCH, TT, D_MODEL)


def _merge(l, x_ctx, x_lat, mod, sw, w_in_t, oa_ctx, oa_lat, ob, yf, yb, u_t, oret, rg, skip_ctx, jobs):
    off = CTX_TILES if skip_ctx else 0
    nt = ALL_TILES - off
    ctx_spec, lat_spec = _tile_specs(l == 0, skip_ctx)
    oac_spec, oal_spec = _tile_specs(True, skip_ctx, BRANCH_W)
    tok = lambda w: pl.BlockSpec((BATCH, TT, w), lambda j: (0, j + off, 0))
    tmaj = pl.BlockSpec((TR, BRANCH_W), lambda j: (j + off, 0))
    job_in, job_out, job_shapes, job_args = _side_casts(jobs)
    return pl.pallas_call(
        _with_side_casts(functools.partial(_merge_kernel, skip_ctx=skip_ctx), 20, 1, len(jobs)),
        grid=(nt,),
        in_specs=[ctx_spec, lat_spec,
                  _layer(l, (16, ADA_CHUNKS * D_MODEL)),
                  _layer(l, (1, D_MODEL)),
                  _layer(0, (IN_COLS, D_MODEL), single=True),
                  oac_spec, oal_spec,
                  tok(BRANCH_W), tmaj, tmaj, tmaj,
                  _layer(l, (1, BRANCH_W)),
                  _layer(l, (BRANCH_W, 2 * BRANCH_W)),
                  tok(BRANCH_W), tok(BRANCH_W),
                  _layer(l, (1, BRANCH_W)),
                  _full((BRANCH_W, BRANCH_W)),
                  _full((TR, TR)),
                  _layer(l, (N_BRANCH, BRANCH_W, D_MODEL), single=True),
                  _layer(l, (D_MODEL, D_MODEL), single=True)] + job_in,
        out_specs=[pl.BlockSpec((BATCH, TT, D_MODEL), lambda j: (0, j, 0))] + job_out,
        out_shape=[jax.ShapeDtypeStruct((BATCH, nt * TT, D_MODEL), F32)] + job_shapes,
        compiler_params=_cparams(1),
        name="merge",
    )(x_ctx, x_lat, mod, sw["norm_mix"], w_in_t, oa_ctx, oa_lat, ob, yf, yb, u_t, sw["s5_d"], sw["w_glu"],
      oret, rg, sw["gn_w"], jnp.asarray(_head_avg(), dtype=BF16), jnp.asarray(_tile_perm().T, dtype=BF16),
      sw["w_branch"], sw["w_out"], *job_args)


def _ffn_kernel(x_ref, mod_ref, nw_ref, w1_ref, w2_ref, o_ref, *, skip_ctx):
    is_ctx = None if skip_ctx else pl.program_id(0) < CTX_LEN // FFN_TT
    sh, sc, gate_res = _mod_chunks(mod_ref, is_ctx, (3, 4, 5))
    x3 = x_ref[...]
    h = (_rms(x3, nw_ref[0]) * (1.0 + sc) + sh).reshape(BATCH * FFN_TT, D_MODEL).astype(BF16)
    f = None
    for c in range(0, D_FF, FFN_PIECE):
        a = jnp.maximum(_dot(h, w1_ref[0, :, c:c + FFN_PIECE]), 0.0)
        part = _dot((a * a).astype(BF16), w2_ref[0, c:c + FFN_PIECE, :])
        f = part if f is None else f + part
    o_ref[...] = x3 + gate_res * f.reshape(BATCH, FFN_TT, D_MODEL)


def _ffn(l, xm, mod, sw, w1, w2, skip_ctx):
    nt = xm.shape[1] // FFN_TT
    tok = pl.BlockSpec((BATCH, FFN_TT, D_MODEL), lambda j: (0, j, 0))
    return pl.pallas_call(
        functools.partial(_ffn_kernel, skip_ctx=skip_ctx),
        grid=(nt,),
        in_specs=[tok,
                  _layer(l, (16, ADA_CHUNKS * D_MODEL)),
                  _layer(l, (1, D_MODEL)),
                  _layer(0, (D_MODEL, D_FF), single=True),
                  _layer(0, (D_FF, D_MODEL), single=True)],
        out_specs=tok,
        out_shape=jax.ShapeDtypeStruct(xm.shape, F32),
        compiler_params=_cparams(1),
        name="ffn",
    )(xm, mod, sw["norm_ffn"], w1, w2)


def _stacked_weights(p):
    perm, _ = _mla_rope_perm()
    zeros = lambda *s: jnp.zeros((DEPTH,) + s, F32)
    wu = p["mla_w_ukv"].reshape(DEPTH, KV_LORA, MLA_HEADS, MLA_NOPE + MLA_V)
    pad_heads = lambda t: jnp.concatenate(
        [t, zeros(t.shape[1], MLA_HEADS, HEAD_PAD - t.shape[3])], -1).reshape(DEPTH, t.shape[1], HEADS_W)
    top = jnp.concatenate([pad_heads(wu[..., :MLA_NOPE]), zeros(KV_LORA, HEADS_W),
                           pad_heads(wu[..., MLA_NOPE:])], axis=2)
    place = jnp.broadcast_to(jnp.asarray(_kv_place())[None], (DEPTH, 128, 3 * HEADS_W))
    wkv = jnp.concatenate([top, place], axis=1).astype(BF16)

    wuq = p["mla_w_uq"].reshape(DEPTH, Q_LORA, MLA_HEADS, MLA_QK)
    qp = jnp.concatenate([zeros(Q_LORA, MLA_HEADS, MLA_NOPE), wuq[..., MLA_NOPE:][..., perm],
                          zeros(Q_LORA, MLA_HEADS, 32)], -1).reshape(DEPTH, Q_LORA, HEADS_W)
    wq = jnp.concatenate([pad_heads(wuq), qp], axis=2).astype(BF16)

    def head_w(v):
        wf = jnp.concatenate([v, zeros(32)], -1)[:, None, :]
        wp = jnp.concatenate([zeros(MLA_NOPE), v[:, MLA_NOPE:][:, perm], zeros(32)], -1)[:, None, :]
        return wf, wp

    head_ws = jnp.concatenate(head_w(p["mla_qk_norm_k"]) + head_w(p["mla_qk_norm_q"]), axis=1)

    def b_compact(b):
        t = b.transpose(0, 1, 2, 4, 3).reshape(DEPTH, 2, BRANCH_W, S5_STATE)
        return jnp.concatenate([t, t], axis=-1)

    c_compact = lambda c: c.transpose(0, 1, 4, 2, 3).reshape(DEPTH, 2, S5_STATE, BRANCH_W)
    vec = lambda a: a.reshape(DEPTH, 2, 1, S5_LANES)

    logit = p["ret_decay_logit"]
    row = lambda a: a[:, None, :]
    return dict(
        norm_mix=row(p["norm_mix_w"]), norm_ffn=row(p["norm_ffn_w"]),
        kv_norm=row(p["mla_kv_norm"]), q_norm=row(p["mla_q_norm"]),
        wkv=wkv, wq=wq, head_w=head_ws,
        s5_lam_re=vec(p["s5_lam_re"]), s5_lam_im=vec(p["s5_lam_im"]),
        s5_log_step=vec(jnp.repeat(p["s5_log_step"], S5_STATE, axis=-1)),
        s5_bre=b_compact(p["s5_b_re"]), s5_bim=b_compact(p["s5_b_im"]),
        s5_cre=c_compact(p["s5_c_re"]), s5_cim=c_compact(p["s5_c_im"]),
        s5_d=row(p["s5_d"]), w_glu=p["s5_w_glu"].astype(BF16),
        lgl=jnp.repeat(logit, RET_HD, axis=-1).reshape(DEPTH, 2, 1, BRANCH_W),
        lgh=jnp.broadcast_to(logit[:, :, :, None, None], (DEPTH, 2, RET_HEADS, 1, 128)),
        gn_w=row(p["ret_gn_w"]),
        w_branch=p["w_branch"].astype(BF16), w_out=p["w_out"].astype(BF16))


def kernel(x, c, ctx, c_ctx, ada_w, ada_b, norm_mix_w, norm_ffn_w, w_in, mla_q_norm, mla_w_uq, mla_kv_norm,
           mla_w_ukv, mla_qk_norm_q, mla_qk_norm_k, s5_lam_re, s5_lam_im, s5_log_step, s5_b_re, s5_b_im,
           s5_c_re, s5_c_im, s5_d, s5_w_glu, ret_decay_logit, ret_gn_w, w_branch, w_out, ffn_w1, ffn_w2):
    p = dict(norm_mix_w=norm_mix_w, norm_ffn_w=norm_ffn_w, w_in=w_in, mla_q_norm=mla_q_norm,
             mla_w_uq=mla_w_uq, mla_kv_norm=mla_kv_norm, mla_w_ukv=mla_w_ukv, mla_qk_norm_q=mla_qk_norm_q,
             mla_qk_norm_k=mla_qk_norm_k, s5_lam_re=s5_lam_re, s5_lam_im=s5_lam_im, s5_log_step=s5_log_step,
             s5_b_re=s5_b_re, s5_b_im=s5_b_im, s5_c_re=s5_c_re, s5_c_im=s5_c_im, s5_d=s5_d,
             s5_w_glu=s5_w_glu, ret_decay_logit=ret_decay_logit, ret_gn_w=ret_gn_w, w_branch=w_branch,
             w_out=w_out, ffn_w1=ffn_w1, ffn_w2=ffn_w2)
    sw = _stacked_weights(p)
    c16 = jnp.concatenate([c, c_ctx[None, :], jnp.zeros((16 - BATCH - 1, D_MODEL), F32)], axis=0)
    mod = _ada_mod(c16, ada_w, ada_b)
    lam, bblk, cblk = _s5_params(sw)
    w_in_t32 = jnp.swapaxes(w_in, 1, 2)
    w_in_t = w_in_t32[0:1].astype(BF16)
    x_ctx, x_lat = ctx, x
    for l in range(DEPTH):
        last = l == DEPTH - 1
        q, k, v, u_t, uc, us, rq, rk, rv, rg, w1, w2 = _in_proj(
            l, x_ctx, x_lat, mod, sw, w_in_t, [(ffn_w1, l, 32), (ffn_w2, l, 32)])
        oa_lat = _attention_latent(q, k, v)
        oa_ctx = oa_lat if last else _attention_context(q, k, v)
        ob = _fnet(uc, us)
        yf, yb = _s5(l, u_t, lam, bblk, cblk)
        oret = _retention(l, rq, rk, rv, sw)
        xm, *w_in_next = _merge(l, x_ctx, x_lat, mod, sw, w_in_t, oa_ctx, oa_lat, ob, yf, yb, u_t, oret, rg, last,
                                [] if last else [(w_in_t32, l + 1, 18)])
        if not last:
            w_in_t, = w_in_next
        x_ctx = x_lat = _ffn(l, xm, mod, sw, w1, w2, last)
    return x_lat
```

```python
import functools
import math

import numpy as np
import jax
import jax.numpy as jnp
from jax import lax
from jax.experimental import pallas as pl
from jax.experimental.pallas import tpu as pltpu

F32 = jnp.float32
BF16 = jnp.bfloat16

D_MODEL = 1024
BATCH = 8
SEQ = 2048
DEPTH = 2
GRID_W = 64
CTX_LEN = 256
LCAT = CTX_LEN + SEQ
N_BRANCH = 4
BRANCH_W = 256
NORM_EPS = 1e-6
ROPE_BASE = 10000.0
ADA_CHUNKS = 6

MLA_HEADS = 4
MLA_NOPE = 64
MLA_ROPE = 32
MLA_QK = MLA_NOPE + MLA_ROPE
MLA_V = 64
Q_LORA = 256
KV_LORA = 128
HEAD_PAD = 128
HEADS_W = MLA_HEADS * HEAD_PAD

FNET_GROUPS = 4
FNET_GW = BRANCH_W // FNET_GROUPS

S5_GROUP_CH = 16
S5_GROUPS = BRANCH_W // S5_GROUP_CH
S5_STATE = 64
S5_LANES = S5_GROUPS * S5_STATE

RET_HEADS = 4
RET_HD = BRANCH_W // RET_HEADS
RET_CHUNK = 256
RET_NCHUNK = LCAT // RET_CHUNK
RET_CTX_CHUNKS = CTX_LEN // RET_CHUNK
RET_SPB = 4

D_FF = 4 * D_MODEL

_O_KV, _O_KR, _O_S5, _O_RK, _O_RV = 0, 128, 160, 416, 672
_O_Q, _O_FN, _O_RQ, _O_RG, _O_GATE = 928, 1184, 1440, 1696, 1952
IN_COLS = _O_GATE + N_BRANCH * D_MODEL

TT = 64
TR = TT * BATCH
FFN_TT = 128
FFN_PIECE = 2048
CTX_TILES = CTX_LEN // TT
ALL_TILES = LCAT // TT
TQ = 256
ATTN_QB = 8
ATTN_UNIT_QB = 2
ATTN_AHEAD = 1
S5_TC = 128
S5_ROWS = S5_TC * BATCH
S5_STEPS = LCAT // S5_TC
S5_CTX_STEPS = CTX_LEN // S5_TC

VMEM_LIMIT = 56 * 1024 * 1024


def _cparams(n_grid):
    return pltpu.CompilerParams(dimension_semantics=("arbitrary",) * n_grid,
                                vmem_limit_bytes=VMEM_LIMIT)


def _dot(a, b):
    return jnp.dot(a, b, preferred_element_type=F32)


def _dot_nt(a, b):
    return lax.dot_general(a, b, (((1,), (1,)), ((), ())), preferred_element_type=F32)


def _sigmoid(x):
    return 0.5 * (jnp.tanh(0.5 * x) + 1.0)


def _gelu_tanh(y):
    return 0.5 * y * (1.0 + jnp.tanh(math.sqrt(2.0 / math.pi) * (y + 0.044715 * (y * y * y))))


def _rms(x, w):
    return x * lax.rsqrt(jnp.mean(x * x, axis=-1, keepdims=True) + NORM_EPS) * w


def _full(shape):
    n = len(shape)
    return pl.BlockSpec(shape, lambda *_: (0,) * n)


def _layer(l, shape, single=False):
    n = len(shape)
    mode = dict(pipeline_mode=pl.Buffered(1)) if single else {}
    return pl.BlockSpec((1,) + tuple(shape), lambda *_: (l,) + (0,) * n, **mode)


def _mla_rope_perm():
    r = np.arange(MLA_ROPE)
    first = (r % 16) < 8
    return np.where(first, r + 8, r - 8), np.where(first, -1.0, 1.0)


@functools.lru_cache(maxsize=None)
def _mla_tables():
    pos = np.arange(SEQ)
    rows, cols = pos // GRID_W, pos % GRID_W
    freqs = ROPE_BASE ** (-np.arange(8, dtype=np.float64) / 8)
    r = np.arange(MLA_ROPE)
    _, sign = _mla_rope_perm()
    p = np.where((r // 16 == 0)[None, :], rows[:, None], cols[:, None]).astype(np.float64)
    ang = p * freqs[(r % 16) % 8][None, :]
    cosf = np.zeros((LCAT, HEAD_PAD))
    sinf = np.zeros((LCAT, HEAD_PAD))
    cosf[:, :MLA_QK] = 1.0
    cosf[CTX_LEN:, MLA_NOPE:MLA_QK] = np.cos(ang)
    sinf[CTX_LEN:, MLA_NOPE:MLA_QK] = np.sin(ang) * sign[None, :]
    return cosf.astype(np.float32), sinf.astype(np.float32)


def _ret_perm():
    d = np.arange(RET_HD)
    first = d < RET_HD // 2
    return np.where(first, d + RET_HD // 2, d - RET_HD // 2), np.where(first, -1.0, 1.0)


@functools.lru_cache(maxsize=None)
def _ret_tables():
    half = RET_HD // 2
    pos = np.arange(SEQ, dtype=np.float64)
    freqs = ROPE_BASE ** (-np.arange(half, dtype=np.float64) / half)
    d = np.arange(RET_HD)
    perm, sign = _ret_perm()
    ang = pos[:, None] * freqs[d % half][None, :]
    cosr = np.ones((LCAT, RET_HD))
    sinr = np.zeros((LCAT, RET_HD))
    cosr[CTX_LEN:] = np.cos(ang)
    sinr[CTX_LEN:] = np.sin(ang) * sign[None, :]
    cosr = np.tile(cosr, (1, RET_HEADS))
    sinr = np.tile(sinr, (1, RET_HEADS))
    pm = np.zeros((BRANCH_W, BRANCH_W))
    for h in range(RET_HEADS):
        pm[h * RET_HD + perm, h * RET_HD + d] = 1.0
    return cosr.astype(np.float32), sinr.astype(np.float32), pm.astype(np.float32)


def _dft(n, scale):
    k = np.arange(n)
    kt = (k[:, None] * k[None, :]) % n
    ang = 2.0 * np.pi * kt / n
    return np.cos(ang) * scale, np.sin(ang) * scale


@functools.lru_cache(maxsize=None)
def _fnet_tables():
    cw, sw = _dft(FNET_GW, 1.0)
    t = np.zeros((BRANCH_W, 2 * BRANCH_W))
    for g in range(FNET_GROUPS):
        s = slice(g * FNET_GW, (g + 1) * FNET_GW)
        t[s, s] = cw
        t[s, BRANCH_W + g * FNET_GW:BRANCH_W + (g + 1) * FNET_GW] = -sw
    cl, sl = _dft(SEQ, 1.0 / math.sqrt(SEQ * FNET_GW))
    clc, slc = _dft(CTX_LEN, 1.0 / math.sqrt(CTX_LEN * FNET_GW))
    return tuple(a.astype(np.float32) for a in (t, cl, sl, clc, slc))


def _bf16_const(a):
    return jnp.asarray(a).astype(BF16)


@functools.lru_cache(maxsize=None)
def _head_avg():
    p = np.zeros((BRANCH_W, BRANCH_W))
    for h in range(RET_HEADS):
        p[h * RET_HD:(h + 1) * RET_HD, h * RET_HD:(h + 1) * RET_HD] = 1.0 / RET_HD
    return p.astype(np.float32)


@functools.lru_cache(maxsize=None)
def _tile_perm():
    p = np.zeros((TR, TR), np.float32)
    for b in range(BATCH):
        for t in range(TT):
            p[t * BATCH + b, b * TT + t] = 1.0
    return p


@functools.lru_cache(maxsize=None)
def _kv_place():
    perm, _ = _mla_rope_perm()
    place = np.zeros((128, 3 * HEADS_W), np.float32)
    for h in range(MLA_HEADS):
        for r in range(MLA_ROPE):
            place[r, h * HEAD_PAD + MLA_NOPE + r] = 1.0
            place[perm[r], HEADS_W + h * HEAD_PAD + MLA_NOPE + r] = 1.0
    return place


@functools.lru_cache(maxsize=None)
def _v_ones():
    v = np.zeros((1, HEADS_W), np.float32)
    v[0, np.arange(MLA_HEADS) * HEAD_PAD + MLA_V] = 1.0
    return v


ADA_TN = 1536


def _ada_kernel(c_ref, w_ref, b_ref, o_ref):
    c = c_ref[...]
    s = (c * _sigmoid(c)).astype(BF16)
    o_ref[0] = _dot(s, w_ref[0].astype(BF16)) + b_ref[pl.ds(pl.program_id(0), 1), :]


def _ada_mod(c16, ada_w, ada_b):
    n = ADA_CHUNKS * D_MODEL
    return pl.pallas_call(
        _ada_kernel,
        grid=(DEPTH, n // ADA_TN),
        in_specs=[pl.BlockSpec((16, D_MODEL), lambda l, j: (0, 0)),
                  pl.BlockSpec((1, D_MODEL, ADA_TN), lambda l, j: (l, 0, j)),
                  pl.BlockSpec((DEPTH, ADA_TN), lambda l, j: (0, j))],
        out_specs=pl.BlockSpec((1, 16, ADA_TN), lambda l, j: (l, 0, j)),
        out_shape=jax.ShapeDtypeStruct((DEPTH, 16, n), F32),
        compiler_params=_cparams(2),
        name="ada_mod",
    )(c16, ada_w, ada_b)


def _side_casts(jobs):
    specs_in, specs_out, shapes, args = [], [], [], []
    for a, l, n in jobs:
        _, r, c = a.shape
        blk = (1, r // n, c)
        specs_in.append(pl.BlockSpec(blk, lambda j, l=l, n=n: (l, jnp.minimum(j, n - 1), 0)))
        specs_out.append(pl.BlockSpec(blk, lambda j, n=n: (0, jnp.minimum(j, n - 1), 0)))
        shapes.append(jax.ShapeDtypeStruct((1, r, c), BF16))
        args.append(a)
    return specs_in, specs_out, shapes, args


def _with_side_casts(body, n_in, n_out, n_jobs):
    def wrapped(*refs):
        body(*refs[:n_in], *refs[n_in + n_jobs:n_in + n_jobs + n_out])
        for src, dst in zip(refs[n_in:n_in + n_jobs], refs[n_in + n_jobs + n_out:]):
            dst[...] = src[...].astype(BF16)
    return wrapped if n_jobs else body


def _mod_chunks(mod_ref, is_ctx, idxs):
    out = []
    for i in idxs:
        sl = slice(i * D_MODEL, (i + 1) * D_MODEL)
        m = mod_ref[0, 0:BATCH, sl]
        if is_ctx is not None:
            m = jnp.where(is_ctx, mod_ref[0, BATCH:BATCH + 1, sl], m)
        out.append(m[:, None, :])
    return out


def _tile_specs(separate, skip_ctx, width=D_MODEL):
    base = 0 if separate else CTX_TILES
    shape = (BATCH, TT, width)
    if skip_ctx:
        return (pl.BlockSpec(shape, lambda j: (0, 0, 0)),
                pl.BlockSpec(shape, lambda j: (0, j + base, 0)))
    return (pl.BlockSpec(shape, lambda j: (0, jnp.minimum(j, CTX_TILES - 1), 0)),
            pl.BlockSpec(shape, lambda j: (0, jnp.maximum(j - CTX_TILES, 0) + base, 0)))


def _head_norm_rot(xf, xp, a, b, scale):
    outs = []
    for h in range(MLA_HEADS):
        f = xf[:, h * HEAD_PAD:(h + 1) * HEAD_PAD]
        p = xp[:, h * HEAD_PAD:(h + 1) * HEAD_PAD]
        n = lax.rsqrt(jnp.sum(f * f, axis=-1, keepdims=True) * (1.0 / MLA_QK) + NORM_EPS) * scale
        outs.append(n * (f * a + p * b))
    return jnp.concatenate(outs, axis=-1)


def _rows(table):
    w = table.shape[-1]
    return jnp.broadcast_to(table[None], (BATCH, TT, w)).reshape(TR, w)


def _tile3(x):
    return x.reshape(BATCH, TT, x.shape[-1])


def _inproj_kernel(xc_ref, xl_ref, mod_ref, nw_ref, w_ref, kvw_ref, wkv_ref, qnw_ref, wq_ref,
                   cosf_ref, sinf_ref, hw_ref, pm_ref, cosr_ref, sinr_ref, t_ref, ptb_ref, vone_ref,
                   q_out, k_out, v_out, ut_out, uc_out, us_out, rq_out, rk_out, rv_out, rg_out, *, layer):
    is_ctx = pl.program_id(0) < CTX_TILES
    x3 = jnp.where(is_ctx, xc_ref[...], xl_ref[...])
    sh, sc = _mod_chunks(mod_ref, is_ctx, (0, 1))
    zas, zs = [], []
    for half in (slice(0, BATCH // 2), slice(BATCH // 2, BATCH)):
        hh = (_rms(x3[half], nw_ref[layer:layer + 1, :]) * (1.0 + sc[half]) + sh[half]).reshape(TR // 2, D_MODEL).astype(BF16)
        zas.append(_dot_nt(hh, w_ref[0, 0:256, :]))
        zs.append(_dot_nt(hh, w_ref[0, _O_S5:_O_GATE, :]))
    za, z = jnp.concatenate(zas, axis=0), jnp.concatenate(zs, axis=0)

    cosf, sinf = _rows(cosf_ref[...]), _rows(sinf_ref[...])
    hw = hw_ref[0]
    kvn = _rms(za[:, 0:128], kvw_ref[layer:layer + 1, :]).astype(BF16)
    lhs = jnp.concatenate([kvn, za[:, 128:256].astype(BF16)], axis=-1)
    kv = _dot(lhs, wkv_ref[0])
    k = _head_norm_rot(kv[:, 0:HEADS_W], kv[:, HEADS_W:2 * HEADS_W],
                       cosf * hw[0:1], sinf * hw[1:2], 1.0)
    k_out[...] = _tile3(k).astype(BF16)
    v_out[...] = _tile3(kv[:, 2 * HEADS_W:3 * HEADS_W] + vone_ref[...]).astype(BF16)
    qn = _rms(z[:, 768:1024], qnw_ref[layer:layer + 1, :]).astype(BF16)
    qq = _dot(qn, wq_ref[0])
    q = _head_norm_rot(qq[:, 0:HEADS_W], qq[:, HEADS_W:2 * HEADS_W],
                       cosf * hw[2:3], sinf * hw[3:4], MLA_QK ** -0.5)
    q_out[...] = _tile3(q).astype(BF16)
    cosr, sinr = _rows(cosr_ref[...]), _rows(sinr_ref[...])
    rq = z[:, 1280:1536]
    rk = z[:, 256:512]
    rq = rq * cosr + _dot(rq.astype(BF16), pm_ref[...]) * sinr
    rk = rk * cosr + _dot(rk.astype(BF16), pm_ref[...]) * sinr
    rq_out[...] = _tile3(rq).astype(BF16)
    rk_out[...] = _tile3(rk * (RET_HD ** -0.5)).astype(BF16)
    rv_out[...] = _tile3(z[:, 512:768]).astype(BF16)
    rg_out[...] = _tile3(z[:, 1536:1792]).astype(BF16)
    ucs = _dot(z[:, 1024:1280].astype(BF16), t_ref[...])
    uc_out[...] = _tile3(ucs[:, 0:BRANCH_W]).astype(BF16)
    us_out[...] = _tile3(ucs[:, BRANCH_W:2 * BRANCH_W]).astype(BF16)
    ut_out[...] = _dot(ptb_ref[...], z[:, 0:256].astype(BF16)).astype(BF16)


def _in_proj(l, x_ctx, x_lat, mod, sw, w_in_t, jobs):
    ctx_spec, lat_spec = _tile_specs(l == 0, False)
    tok = lambda w: pl.BlockSpec((BATCH, TT, w), lambda j: (0, j, 0))
    tab = lambda w: pl.BlockSpec((TT, w), lambda j: (j, 0))
    cosf, sinf = _mla_tables()
    cosr, sinr, pm = _ret_tables()
    bshape = lambda w: jax.ShapeDtypeStruct((BATCH, LCAT, w), BF16)
    job_in, job_out, job_shapes, job_args = _side_casts(jobs)
    return pl.pallas_call(
        _with_side_casts(functools.partial(_inproj_kernel, layer=l), 18, 10, len(jobs)),
        grid=(ALL_TILES,),
        in_specs=[ctx_spec, lat_spec,
                  _layer(l, (16, ADA_CHUNKS * D_MODEL)),
                  _full((DEPTH, D_MODEL)),
                  _layer(0, (_O_GATE, D_MODEL)),
                  _full((DEPTH, KV_LORA)),
                  _layer(l, (256, 3 * HEADS_W)),
                  _full((DEPTH, Q_LORA)),
                  _layer(l, (Q_LORA, 2 * HEADS_W)),
                  tab(HEAD_PAD), tab(HEAD_PAD),
                  _layer(l, (4, HEAD_PAD)),
                  _full((BRANCH_W, BRANCH_W)),
                  tab(BRANCH_W), tab(BRANCH_W),
                  _full((BRANCH_W, 2 * BRANCH_W)),
                  _full((TR, TR)),
                  _full((1, HEADS_W))] + job_in,
        out_specs=[tok(HEADS_W), tok(HEADS_W), tok(HEADS_W),
                   pl.BlockSpec((TR, BRANCH_W), lambda j: (j, 0)),
                   tok(256), tok(256), tok(256), tok(256), tok(256), tok(256)] + job_out,
        out_shape=[bshape(HEADS_W), bshape(HEADS_W), bshape(HEADS_W),
                   jax.ShapeDtypeStruct((LCAT * BATCH, BRANCH_W), BF16),
                   bshape(256), bshape(256), bshape(256), bshape(256), bshape(256), bshape(256)] + job_shapes,
        compiler_params=_cparams(1),
        name="in_proj",
    )(x_ctx, x_lat, mod, sw["norm_mix"], w_in_t, sw["kv_norm"], sw["wkv"], sw["q_norm"], sw["wq"],
      jnp.asarray(cosf), jnp.asarray(sinf), sw["head_w"], jnp.asarray(pm, dtype=BF16),
      jnp.asarray(cosr), jnp.asarray(sinr), _bf16_const(_fnet_tables()[0]),
      jnp.asarray(_tile_perm(), dtype=BF16), jnp.asarray(_v_ones()), *job_args)


def _attn_kernel(*refs):
    q_refs, (k_ref, v_ref, o_ref) = refs[:-3], refs[-3:]
    per = min(ATTN_UNIT_QB, len(q_refs))
    units = [(b, g, h, slice(h * HEAD_PAD, (h + 1) * HEAD_PAD))
             for b in range(k_ref.shape[0]) for g in range(len(q_refs) // per) for h in range(MLA_HEADS)]

    def scores(u):
        b, g, _, sl = u
        q = jnp.concatenate([q_refs[g * per + i][b, :, sl] for i in range(per)], axis=0)
        return _dot_nt(q, k_ref[b, :, sl])

    lane = lax.broadcasted_iota(jnp.int32, (1, HEAD_PAD), 1)
    pending = [scores(u) for u in units[:ATTN_AHEAD]]
    for i, (b, g, h, sl) in enumerate(units):
        if i + ATTN_AHEAD < len(units):
            pending.append(scores(units[i + ATTN_AHEAD]))
        s = pending.pop(0)
        p = jnp.exp((s - jnp.max(s, axis=-1, keepdims=True)).astype(BF16))
        oh = _dot(p, v_ref[b, :, sl])
        oh = oh * (1.0 / oh[:, MLA_V:MLA_V + 1])
        if h % 2 == 0:
            even = oh
        else:
            pair = jnp.where(lane < MLA_V, even, pltpu.roll(oh, MLA_V, axis=1))
            rows = slice(g * per * TQ, (g + 1) * per * TQ)
            o_ref[b, rows, (h // 2) * HEAD_PAD:(h // 2 + 1) * HEAD_PAD] = pair.astype(BF16)


def _attention_latent(q, k, v):
    first = CTX_LEN // TQ
    qspec = lambda r: pl.BlockSpec((1, TQ, HEADS_W), lambda b, j: (b, first + ATTN_QB * j + r, 0))
    kv = pl.BlockSpec((1, LCAT, HEADS_W), lambda b, j: (b, 0, 0))
    return pl.pallas_call(
        _attn_kernel,
        grid=(BATCH, SEQ // (ATTN_QB * TQ)),
        in_specs=[qspec(r) for r in range(ATTN_QB)] + [kv, kv],
        out_specs=pl.BlockSpec((1, ATTN_QB * TQ, BRANCH_W), lambda b, j: (b, j, 0)),
        out_shape=jax.ShapeDtypeStruct((BATCH, SEQ, BRANCH_W), BF16),
        compiler_params=_cparams(2),
        name="mla_attention",
    )(*([q] * ATTN_QB), k, v)


def _attention_context(q, k, v):
    blk = pl.BlockSpec((1, CTX_LEN, HEADS_W), lambda b: (b, 0, 0))
    return pl.pallas_call(
        _attn_kernel,
        grid=(BATCH,),
        in_specs=[blk, blk, blk],
        out_specs=pl.BlockSpec((1, CTX_LEN, BRANCH_W), lambda b: (b, 0, 0)),
        out_shape=jax.ShapeDtypeStruct((BATCH, CTX_LEN, BRANCH_W), BF16),
        compiler_params=_cparams(1),
        name="mla_attention_ctx",
    )(q, k, v)


def _fnet_kernel(uc_ref, us_ref, cl_ref, sl_ref, clc_ref, slc_ref, o_ref):
    o_ref[0, 0:CTX_LEN, :] = (_dot(clc_ref[...], uc_ref[0, 0:CTX_LEN, :])
                              + _dot(slc_ref[...], us_ref[0, 0:CTX_LEN, :])).astype(BF16)
    o_ref[0, CTX_LEN:LCAT, :] = (_dot(cl_ref[...], uc_ref[0, CTX_LEN:LCAT, :])
                                 + _dot(sl_ref[...], us_ref[0, CTX_LEN:LCAT, :])).astype(BF16)


def _fnet(uc, us):
    cl, sl, clc, slc = (_bf16_const(a) for a in _fnet_tables()[1:])
    tok = pl.BlockSpec((1, LCAT, BRANCH_W), lambda b: (b, 0, 0))
    return pl.pallas_call(
        _fnet_kernel,
        grid=(BATCH,),
        in_specs=[tok, tok, _full((SEQ, SEQ)), _full((SEQ, SEQ)),
                  _full((CTX_LEN, CTX_LEN)), _full((CTX_LEN, CTX_LEN))],
        out_specs=tok,
        out_shape=jax.ShapeDtypeStruct((BATCH, LCAT, BRANCH_W), BF16),
        compiler_params=_cparams(1),
        name="fnet_dft",
    )(uc, us, cl, sl, clc, slc)


def _s5_param_kernel(lr_ref, li_ref, ls_ref, bre_ref, bim_ref, cre_ref, cim_ref, lam_out, b_out, c_out):
    lr, li = lr_ref[0, 0], li_ref[0, 0]
    step = jnp.exp(ls_ref[0, 0])
    mag = jnp.exp(lr * step)
    lbr = mag * jnp.cos(li * step)
    lbi = mag * jnp.sin(li * step)
    den = 1.0 / (lr * lr + li * li)
    cr = ((lbr - 1.0) * lr + lbi * li) * den
    ci = (lbi * lr - (lbr - 1.0) * li) * den
    lam_out[0, 0, 0] = jnp.broadcast_to(lbr, (BATCH, S5_LANES))
    lam_out[0, 0, 1] = jnp.broadcast_to(lbi, (BATCH, S5_LANES))
    grp_b = (lax.broadcasted_iota(jnp.int32, (BRANCH_W, S5_LANES), 0) // S5_GROUP_CH
             == lax.broadcasted_iota(jnp.int32, (BRANCH_W, S5_LANES), 1) // S5_STATE)
    wide = lambda r: jnp.concatenate([r[0, 0]] * (S5_LANES // 128), axis=-1)
    bre = jnp.where(grp_b, wide(bre_ref), 0.0)
    bim = jnp.where(grp_b, wide(bim_ref), 0.0)
    b_out[0, 0, :, 0:S5_LANES] = (cr * bre - ci * bim).astype(BF16)
    b_out[0, 0, :, S5_LANES:2 * S5_LANES] = (cr * bim + ci * bre).astype(BF16)
    grp_c = (lax.broadcasted_iota(jnp.int32, (S5_LANES, BRANCH_W), 0) // S5_STATE
             == lax.broadcasted_iota(jnp.int32, (S5_LANES, BRANCH_W), 1) // S5_GROUP_CH)
    tall = lambda r: jnp.concatenate([r[0, 0]] * S5_GROUPS, axis=0)
    c_out[0, 0, 0:S5_LANES, :] = jnp.where(grp_c, tall(cre_ref), 0.0).astype(BF16)
    c_out[0, 0, S5_LANES:2 * S5_LANES, :] = jnp.where(grp_c, -tall(cim_ref), 0.0).astype(BF16)


def _s5_params(sw):
    spec = lambda *s: pl.BlockSpec((1, 1) + s, lambda l, d: (l, d) + (0,) * len(s))
    return pl.pallas_call(
        _s5_param_kernel,
        grid=(DEPTH, 2),
        in_specs=[spec(1, S5_LANES), spec(1, S5_LANES), spec(1, S5_LANES),
                  spec(BRANCH_W, 128), spec(BRANCH_W, 128), spec(S5_STATE, BRANCH_W), spec(S5_STATE, BRANCH_W)],
        out_specs=[spec(2, BATCH, S5_LANES), spec(BRANCH_W, 2 * S5_LANES), spec(2 * S5_LANES, BRANCH_W)],
        out_shape=[jax.ShapeDtypeStruct((DEPTH, 2, 2, BATCH, S5_LANES), F32),
                   jax.ShapeDtypeStruct((DEPTH, 2, BRANCH_W, 2 * S5_LANES), BF16),
                   jax.ShapeDtypeStruct((DEPTH, 2, 2 * S5_LANES, BRANCH_W), BF16)],
        compiler_params=_cparams(2),
        name="s5_discretise",
    )(sw["s5_lam_re"], sw["s5_lam_im"], sw["s5_log_step"], sw["s5_bre"], sw["s5_bim"], sw["s5_cre"], sw["s5_cim"])


def _s5_bwd_block(i):
    return jnp.where(i < S5_CTX_STEPS, S5_CTX_STEPS - 1 - i, S5_STEPS + S5_CTX_STEPS - 1 - i)


def _s5_kernel(uf_ref, ub_ref, lam_ref, b_ref, c_ref, yf_ref, yb_ref, xf_scr, xb_scr, st_scr):
    @pl.when(pl.program_id(0) == 0)
    def _():
        st_scr[...] = jnp.zeros_like(st_scr)

    dirs = ((uf_ref, xf_scr, yf_ref), (ub_ref, xb_scr, yb_ref))
    for d, (u_ref, x_scr, _) in enumerate(dirs):
        x_scr[...] = _dot(u_ref[...], b_ref[0, d])
    re, im = pl.ds(0, S5_LANES), pl.ds(S5_LANES, S5_LANES)
    for d, (_, x_scr, y_ref) in enumerate(dirs):
        xr, xi = st_scr[2 * d], st_scr[2 * d + 1]
        ar, ai = lam_ref[0, d, 0], lam_ref[0, d, 1]
        for t in (range(S5_TC) if d == 0 else range(S5_TC - 1, -1, -1)):
            rows = pl.ds(t * BATCH, BATCH)
            xr, xi = (ar * xr - ai * xi + x_scr[rows, re], ar * xi + ai * xr + x_scr[rows, im])
            x_scr[rows, re] = xr
            x_scr[rows, im] = xi
        st_scr[2 * d], st_scr[2 * d + 1] = xr, xi
        y_ref[...] = _dot(x_scr[...].astype(BF16), c_ref[0, d]).astype(BF16)


def _s5(l, u_t, lam, bblk, cblk):
    fwd = pl.BlockSpec((S5_ROWS, BRANCH_W), lambda i: (i, 0))
    bwd = pl.BlockSpec((S5_ROWS, BRANCH_W), lambda i: (_s5_bwd_block(i), 0))
    return pl.pallas_call(
        _s5_kernel,
        grid=(S5_STEPS,),
        in_specs=[fwd, bwd, _layer(l, (2, 2, BATCH, S5_LANES)),
                  _layer(l, (2, BRANCH_W, 2 * S5_LANES)), _layer(l, (2, 2 * S5_LANES, BRANCH_W))],
        out_specs=[fwd, bwd],
        out_shape=[jax.ShapeDtypeStruct((LCAT * BATCH, BRANCH_W), BF16)] * 2,
        scratch_shapes=[pltpu.VMEM((S5_ROWS, 2 * S5_LANES), F32),
                        pltpu.VMEM((S5_ROWS, 2 * S5_LANES), F32),
                        pltpu.VMEM((4, BATCH, S5_LANES), F32)],
        compiler_params=_cparams(1),
        name="s5_scan",
    )(u_t, u_t, lam, bblk, cblk)


def _log_sigmoid(x):
    return jnp.minimum(x, 0.0) - jnp.log(1.0 + jnp.exp(-jnp.abs(x)))


def _ret_kernel(q_ref, k_ref, v_ref, lgl_ref, lgh_ref, o_ref, dec_scr, sb_scr, sf_scr, sbc_scr):
    c_len = RET_CHUNK
    lgl = _log_sigmoid(lgl_ref[0])
    lgf, lgb = lgl[0], lgl[1]
    ti = lax.broadcasted_iota(jnp.int32, (c_len, BRANCH_W), 0).astype(F32)
    qdf = jnp.exp(lgf * (ti + 1.0))
    kdf = jnp.exp(lgf * (c_len - 1.0 - ti))
    qdb = jnp.exp(lgb * (c_len - ti))
    kdb = jnp.exp(lgb * ti)
    cdf = jnp.exp(lgf * float(c_len))
    cdb = jnp.exp(lgb * float(c_len))
    lane = lax.broadcasted_iota(jnp.int32, (1, BRANCH_W), 1)
    same_head = (lax.broadcasted_iota(jnp.int32, (BRANCH_W, BRANCH_W), 0) // RET_HD
                 == lax.broadcasted_iota(jnp.int32, (BRANCH_W, BRANCH_W), 1) // RET_HD)

    diff = (lax.broadcasted_iota(jnp.int32, (c_len, c_len), 0)
            - lax.broadcasted_iota(jnp.int32, (c_len, c_len), 1)).astype(F32)
    for h in range(RET_HEADS):
        gf = jnp.concatenate([_log_sigmoid(lgh_ref[0, 0, h])] * (c_len // 128), axis=-1)
        gb = jnp.concatenate([_log_sigmoid(lgh_ref[0, 1, h])] * (c_len // 128), axis=-1)
        dec_scr[h] = (jnp.where(diff >= 0, jnp.exp(gf * jnp.maximum(diff, 0.0)), 0.0)
                      + jnp.where(diff <= 0, jnp.exp(gb * jnp.maximum(-diff, 0.0)), 0.0))

    def chunk(ref, s, c):
        return ref[s, pl.ds(pl.multiple_of(c * c_len, c_len), c_len), :]

    def kv_outer(kd, v):
        s = lax.dot_general(kd.astype(BF16), v, (((0,), (0,)), ((), ())), preferred_element_type=F32)
        return jnp.where(same_head, s, 0.0)

    sbc_scr[...] = jnp.zeros_like(sbc_scr)

    def bwd(i, _):
        c = jnp.where(i < RET_CTX_CHUNKS, RET_CTX_CHUNKS - 1 - i, RET_NCHUNK + RET_CTX_CHUNKS - 1 - i)
        for s in range(RET_SPB):
            sb_scr[s, c] = sbc_scr[s].astype(BF16)
            k = chunk(k_ref, s, c).astype(F32)
            sbc_scr[s] = sbc_scr[s] * cdb + kv_outer(k * kdb, chunk(v_ref, s, c))
        return 0

    lax.fori_loop(0, RET_NCHUNK, bwd, 0)

    sf_scr[...] = jnp.zeros_like(sf_scr)

    def fwd(c, _):
        for s in range(RET_SPB):
            qb, kb, vb = chunk(q_ref, s, c), chunk(k_ref, s, c), chunk(v_ref, s, c)
            q = qb.astype(F32)
            o = (_dot((q * qdf).astype(BF16), sf_scr[s].astype(BF16))
                 + _dot((q * qdb).astype(BF16), sb_scr[s, c]))
            for h in range(RET_HEADS):
                hm = lane // RET_HD == h
                att = _dot_nt(jnp.where(hm, qb, jnp.zeros_like(qb)), kb)
                oh = _dot((att * dec_scr[h]).astype(BF16), vb)
                o = o + jnp.where(hm, oh, 0.0)
            o_ref[s, pl.ds(pl.multiple_of(c * c_len, c_len), c_len), :] = o.astype(BF16)
            sf_scr[s] = sf_scr[s] * cdf + kv_outer(kb.astype(F32) * kdf, vb)
        return 0

    lax.fori_loop(0, RET_NCHUNK, fwd, 0)


def _retention(l, rq, rk, rv, sw):
    tok = pl.BlockSpec((RET_SPB, LCAT, BRANCH_W), lambda b: (b, 0, 0))
    return pl.pallas_call(
        _ret_kernel,
        grid=(BATCH // RET_SPB,),
        in_specs=[tok, tok, tok, _layer(l, (2, 1, BRANCH_W)), _layer(l, (2, RET_HEADS, 1, 128))],
        out_specs=tok,
        out_shape=jax.ShapeDtypeStruct((BATCH, LCAT, BRANCH_W), BF16),
        scratch_shapes=[pltpu.VMEM((RET_HEADS, RET_CHUNK, RET_CHUNK), F32),
                        pltpu.VMEM((RET_SPB, RET_NCHUNK, BRANCH_W, BRANCH_W), BF16),
                        pltpu.VMEM((RET_SPB, BRANCH_W, BRANCH_W), F32),
                        pltpu.VMEM((RET_SPB, BRANCH_W, BRANCH_W), F32)],
        compiler_params=_cparams(1),
        name="retention",
    )(rq, rk, rv, sw["lgl"], sw["lgh"])


def _merge_kernel(xc_ref, xl_ref, mod_ref, nw_ref, wg_ref, oac_ref, oal_ref, ob_ref, yf_ref, yb_ref, ut_ref,
                  d_ref, wglu_ref, oret_ref, rg_ref, gnw_ref, pavg_ref, pbt_ref, wb_ref, wout_ref,
                  o_ref, *, skip_ctx, layer):
    if skip_ctx:
        is_ctx = None
        x3 = xl_ref[...]
        oa = oal_ref[...]
    else:
        is_ctx = pl.program_id(0) < CTX_TILES
        x3 = jnp.where(is_ctx, xc_ref[...], xl_ref[...])
        oa = jnp.where(is_ctx, oac_ref[...], oal_ref[...])
    sh, sc, gate_res = _mod_chunks(mod_ref, is_ctx, (0, 1, 2))
    gate_w = lambda n: wg_ref[0, _O_GATE + n * D_MODEL:_O_GATE + (n + 1) * D_MODEL, :]
    hs = [(_rms(x3[half], nw_ref[layer:layer + 1, :]) * (1.0 + sc[half]) + sh[half]).reshape(TR // 2, D_MODEL).astype(BF16)
          for half in (slice(0, BATCH // 2), slice(BATCH // 2, BATCH))]
    gate0 = jnp.concatenate([_dot_nt(hh, gate_w(0)) for hh in hs], axis=0)
    h = jnp.concatenate(hs, axis=0)
    yt = yf_ref[...].astype(F32) + yb_ref[...].astype(F32) + d_ref[layer:layer + 1, :] * ut_ref[...].astype(F32)
    y = _dot(pbt_ref[...], yt.astype(BF16))
    vg = _dot(_gelu_tanh(y).astype(BF16), wglu_ref[0])
    oc = vg[:, 0:BRANCH_W] * _sigmoid(vg[:, BRANCH_W:2 * BRANCH_W])
    o = oret_ref[...].reshape(TR, BRANCH_W)
    dl = o.astype(F32) - _dot(o, pavg_ref[...])
    var = _dot((dl * dl).astype(BF16), pavg_ref[...])
    g = rg_ref[...].reshape(TR, BRANCH_W).astype(F32)
    od = g * _sigmoid(g) * (dl * lax.rsqrt(var + NORM_EPS) * gnw_ref[layer:layer + 1, :])
    branches = (oa.reshape(TR, BRANCH_W), ob_ref[...].reshape(TR, BRANCH_W), oc.astype(BF16), od.astype(BF16))
    acc = None
    for n, branch in enumerate(branches):
        gate = gate0 if n == 0 else _dot_nt(h, gate_w(n))
        term = _sigmoid(gate) * _dot(branch, wb_ref[0, n])
        acc = term if acc is None else acc + term
    m = _dot(acc.astype(BF16), wout_ref[0])
    o_ref[...] = x3 + gate_res * m.reshape(BATCH, TT, D_MODEL)


def _merge(l, x_ctx, x_lat, mod, sw, w_in_t, oa_ctx, oa_lat, ob, yf, yb, u_t, oret, rg, skip_ctx, jobs):
    off = CTX_TILES if skip_ctx else 0
    nt = ALL_TILES - off
    ctx_spec, lat_spec = _tile_specs(l == 0, skip_ctx)
    oac_spec, oal_spec = _tile_specs(True, skip_ctx, BRANCH_W)
    tok = lambda w: pl.BlockSpec((BATCH, TT, w), lambda j: (0, j + off, 0))
    tmaj = pl.BlockSpec((TR, BRANCH_W), lambda j: (j + off, 0))
    job_in, job_out, job_shapes, job_args = _side_casts(jobs)
    return pl.pallas_call(
        _with_side_casts(functools.partial(_merge_kernel, skip_ctx=skip_ctx, layer=l), 20, 1, len(jobs)),
        grid=(nt,),
        in_specs=[ctx_spec, lat_spec,
                  _layer(l, (16, ADA_CHUNKS * D_MODEL)),
                  _full((DEPTH, D_MODEL)),
                  _layer(0, (IN_COLS, D_MODEL), single=True),
                  oac_spec, oal_spec,
                  tok(BRANCH_W), tmaj, tmaj, tmaj,
                  _full((DEPTH, BRANCH_W)),
                  _layer(l, (BRANCH_W, 2 * BRANCH_W)),
                  tok(BRANCH_W), tok(BRANCH_W),
                  _full((DEPTH, BRANCH_W)),
                  _full((BRANCH_W, BRANCH_W)),
                  _full((TR, TR)),
                  _layer(l, (N_BRANCH, BRANCH_W, D_MODEL), single=True),
                  _layer(l, (D_MODEL, D_MODEL), single=True)] + job_in,
        out_specs=[pl.BlockSpec((BATCH, TT, D_MODEL), lambda j: (0, j, 0))] + job_out,
        out_shape=[jax.ShapeDtypeStruct((BATCH, nt * TT, D_MODEL), F32)] + job_shapes,
        compiler_params=_cparams(1),
        name="merge",
    )(x_ctx, x_lat, mod, sw["norm_mix"], w_in_t, oa_ctx, oa_lat, ob, yf, yb, u_t, sw["s5_d"], sw["w_glu"],
      oret, rg, sw["gn_w"], jnp.asarray(_head_avg(), dtype=BF16), jnp.asarray(_tile_perm().T, dtype=BF16),
      sw["w_branch"], sw["w_out"], *job_args)


def _ffn_kernel(x_ref, mod_ref, nw_ref, w1_ref, w2_ref, o_ref, *, skip_ctx, layer):
    is_ctx = None if skip_ctx else pl.program_id(0) < CTX_LEN // FFN_TT
    sh, sc, gate_res = _mod_chunks(mod_ref, is_ctx, (3, 4, 5))
    x3 = x_ref[...]
    h = (_rms(x3, nw_ref[layer:layer + 1, :]) * (1.0 + sc) + sh).reshape(BATCH * FFN_TT, D_MODEL).astype(BF16)
    f = None
    for c in range(0, D_FF, FFN_PIECE):
        a = jnp.maximum(_dot(h, w1_ref[0, :, c:c + FFN_PIECE]), 0.0)
        part = _dot((a * a).astype(BF16), w2_ref[0, c:c + FFN_PIECE, :])
        f = part if f is None else f + part
    o_ref[...] = x3 + gate_res * f.reshape(BATCH, FFN_TT, D_MODEL)


def _ffn(l, xm, mod, sw, w1, w2, skip_ctx):
    nt = xm.shape[1] // FFN_TT
    tok = pl.BlockSpec((BATCH, FFN_TT, D_MODEL), lambda j: (0, j, 0))
    return pl.pallas_call(
        functools.partial(_ffn_kernel, skip_ctx=skip_ctx, layer=l),
        grid=(nt,),
        in_specs=[tok,
                  _layer(l, (16, ADA_CHUNKS * D_MODEL)),
                  _full((DEPTH, D_MODEL)),
                  _layer(0, (D_MODEL, D_FF), single=True),
                  _layer(0, (D_FF, D_MODEL), single=True)],
        out_specs=tok,
        out_shape=jax.ShapeDtypeStruct(xm.shape, F32),
        compiler_params=_cparams(1),
        name="ffn",
    )(xm, mod, sw["norm_ffn"], w1, w2)


def _stacked_weights(p):
    perm, _ = _mla_rope_perm()
    zeros = lambda *s: jnp.zeros((DEPTH,) + s, F32)
    wu = p["mla_w_ukv"].reshape(DEPTH, KV_LORA, MLA_HEADS, MLA_NOPE + MLA_V)
    pad_heads = lambda t: jnp.concatenate(
        [t, zeros(t.shape[1], MLA_HEADS, HEAD_PAD - t.shape[3])], -1).reshape(DEPTH, t.shape[1], HEADS_W)
    top = jnp.concatenate([pad_heads(wu[..., :MLA_NOPE]), zeros(KV_LORA, HEADS_W),
                           pad_heads(wu[..., MLA_NOPE:])], axis=2)
    place = jnp.broadcast_to(jnp.asarray(_kv_place())[None], (DEPTH, 128, 3 * HEADS_W))
    wkv = jnp.concatenate([top, place], axis=1).astype(BF16)

    wuq = p["mla_w_uq"].reshape(DEPTH, Q_LORA, MLA_HEADS, MLA_QK)
    qp = jnp.concatenate([zeros(Q_LORA, MLA_HEADS, MLA_NOPE), wuq[..., MLA_NOPE:][..., perm],
                          zeros(Q_LORA, MLA_HEADS, 32)], -1).reshape(DEPTH, Q_LORA, HEADS_W)
    wq = jnp.concatenate([pad_heads(wuq), qp], axis=2).astype(BF16)

    def head_w(v):
        wf = jnp.concatenate([v, zeros(32)], -1)[:, None, :]
        wp = jnp.concatenate([zeros(MLA_NOPE), v[:, MLA_NOPE:][:, perm], zeros(32)], -1)[:, None, :]
        return wf, wp

    head_ws = jnp.concatenate(head_w(p["mla_qk_norm_k"]) + head_w(p["mla_qk_norm_q"]), axis=1)

    def b_compact(b):
        t = b.transpose(0, 1, 2, 4, 3).reshape(DEPTH, 2, BRANCH_W, S5_STATE)
        return jnp.concatenate([t, t], axis=-1)

    c_compact = lambda c: c.transpose(0, 1, 4, 2, 3).reshape(DEPTH, 2, S5_STATE, BRANCH_W)
    vec = lambda a: a.reshape(DEPTH, 2, 1, S5_LANES)

    logit = p["ret_decay_logit"]
    return dict(
        norm_mix=p["norm_mix_w"], norm_ffn=p["norm_ffn_w"],
        kv_norm=p["mla_kv_norm"], q_norm=p["mla_q_norm"],
        wkv=wkv, wq=wq, head_w=head_ws,
        s5_lam_re=vec(p["s5_lam_re"]), s5_lam_im=vec(p["s5_lam_im"]),
        s5_log_step=vec(jnp.repeat(p["s5_log_step"], S5_STATE, axis=-1)),
        s5_bre=b_compact(p["s5_b_re"]), s5_bim=b_compact(p["s5_b_im"]),
        s5_cre=c_compact(p["s5_c_re"]), s5_cim=c_compact(p["s5_c_im"]),
        s5_d=p["s5_d"], w_glu=p["s5_w_glu"].astype(BF16),
        lgl=jnp.repeat(logit, RET_HD, axis=-1).reshape(DEPTH, 2, 1, BRANCH_W),
        lgh=jnp.broadcast_to(logit[:, :, :, None, None], (DEPTH, 2, RET_HEADS, 1, 128)),
        gn_w=p["ret_gn_w"],
        w_branch=p["w_branch"].astype(BF16), w_out=p["w_out"].astype(BF16))


def kernel(x, c, ctx, c_ctx, ada_w, ada_b, norm_mix_w, norm_ffn_w, w_in, mla_q_norm, mla_w_uq, mla_kv_norm,
           mla_w_ukv, mla_qk_norm_q, mla_qk_norm_k, s5_lam_re, s5_lam_im, s5_log_step, s5_b_re, s5_b_im,
           s5_c_re, s5_c_im, s5_d, s5_w_glu, ret_decay_logit, ret_gn_w, w_branch, w_out, ffn_w1, ffn_w2):
    p = dict(norm_mix_w=norm_mix_w, norm_ffn_w=norm_ffn_w, w_in=w_in, mla_q_norm=mla_q_norm,
             mla_w_uq=mla_w_uq, mla_kv_norm=mla_kv_norm, mla_w_ukv=mla_w_ukv, mla_qk_norm_q=mla_qk_norm_q,
             mla_qk_norm_k=mla_qk_norm_k, s5_lam_re=s5_lam_re, s5_lam_im=s5_lam_im, s5_log_step=s5_log_step,
             s5_b_re=s5_b_re, s5_b_im=s5_b_im, s5_c_re=s5_c_re, s5_c_im=s5_c_im, s5_d=s5_d,
             s5_w_glu=s5_w_glu, ret_decay_logit=ret_decay_logit, ret_gn_w=ret_gn_w, w_branch=w_branch,
             w_out=w_out, ffn_w1=ffn_w1, ffn_w2=ffn_w2)
    sw = _stacked_weights(p)
    c16 = jnp.concatenate([c, c_ctx[None, :], jnp.zeros((16 - BATCH - 1, D_MODEL), F32)], axis=0)
    mod = _ada_mod(c16, ada_w, ada_b)
    lam, bblk, cblk = _s5_params(sw)
    w_in_t32 = jnp.swapaxes(w_in, 1, 2)
    w_in_t = w_in_t32[0:1].astype(BF16)
    x_ctx, x_lat = ctx, x
    for l in range(DEPTH):
        last = l == DEPTH - 1
        q, k, v, u_t, uc, us, rq, rk, rv, rg, w1, w2 = _in_proj(
            l, x_ctx, x_lat, mod, sw, w_in_t, [(ffn_w1, l, 32), (ffn_w2, l, 32)])
        oa_lat = _attention_latent(q, k, v)
        oa_ctx = oa_lat if last else _attention_context(q, k, v)
        ob = _fnet(uc, us)
        yf, yb = _s5(l, u_t, lam, bblk, cblk)
        oret = _retention(l, rq, rk, rv, sw)
        xm, *w_in_next = _merge(l, x_ctx, x_lat, mod, sw, w_in_t, oa_ctx, oa_lat, ob, yf, yb, u_t, oret, rg, last,
                                [] if last else [(w_in_t32, l + 1, 18)])
        if not last:
            w_in_t, = w_in_next
        x_ctx = x_lat = _ffn(l, xm, mod, sw, w1, w2, last)
    return x_lat
```

```python
import functools
import math

import numpy as np
import jax
import jax.numpy as jnp
from jax import lax
from jax.experimental import pallas as pl
from jax.experimental.pallas import tpu as pltpu

F32 = jnp.float32
BF16 = jnp.bfloat16

D_MODEL = 1024
BATCH = 8
SEQ = 2048
DEPTH = 2
GRID_W = 64
CTX_LEN = 256
LCAT = CTX_LEN + SEQ
N_BRANCH = 4
BRANCH_W = 256
NORM_EPS = 1e-6
ROPE_BASE = 10000.0
ADA_CHUNKS = 6

MLA_HEADS = 4
MLA_NOPE = 64
MLA_ROPE = 32
MLA_QK = MLA_NOPE + MLA_ROPE
MLA_V = 64
Q_LORA = 256
KV_LORA = 128
HEAD_PAD = 128
HEADS_W = MLA_HEADS * HEAD_PAD

FNET_GROUPS = 4
FNET_GW = BRANCH_W // FNET_GROUPS

S5_GROUP_CH = 16
S5_GROUPS = BRANCH_W // S5_GROUP_CH
S5_STATE = 64
S5_LANES = S5_GROUPS * S5_STATE

RET_HEADS = 4
RET_HD = BRANCH_W // RET_HEADS
RET_CHUNK = 256
RET_NCHUNK = LCAT // RET_CHUNK
RET_CTX_CHUNKS = CTX_LEN // RET_CHUNK
RET_SPB = 4

D_FF = 4 * D_MODEL

_O_KV, _O_KR, _O_S5, _O_RK, _O_RV = 0, 128, 160, 416, 672
_O_Q, _O_FN, _O_RQ, _O_RG, _O_GATE = 928, 1184, 1440, 1696, 1952
IN_COLS = _O_GATE + N_BRANCH * D_MODEL

TT = 64
TR = TT * BATCH
FFN_TT = 128
FFN_PIECE = 2048
CTX_TILES = CTX_LEN // TT
ALL_TILES = LCAT // TT
TQ = 256
ATTN_QB = 8
ATTN_UNIT_QB = 2
ATTN_AHEAD = 1
S5_TC = 128
S5_ROWS = S5_TC * BATCH
S5_STEPS = LCAT // S5_TC
S5_CTX_STEPS = CTX_LEN // S5_TC

VMEM_LIMIT = 56 * 1024 * 1024


def _cparams(n_grid):
    return pltpu.CompilerParams(dimension_semantics=("arbitrary",) * n_grid,
                                vmem_limit_bytes=VMEM_LIMIT)


def _dot(a, b):
    return jnp.dot(a, b, preferred_element_type=F32)


def _dot_nt(a, b):
    return lax.dot_general(a, b, (((1,), (1,)), ((), ())), preferred_element_type=F32)


def _sigmoid(x):
    return 0.5 * (jnp.tanh(0.5 * x) + 1.0)


def _gelu_tanh(y):
    return 0.5 * y * (1.0 + jnp.tanh(math.sqrt(2.0 / math.pi) * (y + 0.044715 * (y * y * y))))


def _rms(x, w):
    return x * lax.rsqrt(jnp.mean(x * x, axis=-1, keepdims=True) + NORM_EPS) * w


def _full(shape):
    n = len(shape)
    return pl.BlockSpec(shape, lambda *_: (0,) * n)


def _layer(l, shape, single=False):
    n = len(shape)
    mode = dict(pipeline_mode=pl.Buffered(1)) if single else {}
    return pl.BlockSpec((1,) + tuple(shape), lambda *_: (l,) + (0,) * n, **mode)


def _mla_rope_perm():
    r = np.arange(MLA_ROPE)
    first = (r % 16) < 8
    return np.where(first, r + 8, r - 8), np.where(first, -1.0, 1.0)


@functools.lru_cache(maxsize=None)
def _mla_tables():
    pos = np.arange(SEQ)
    rows, cols = pos // GRID_W, pos % GRID_W
    freqs = ROPE_BASE ** (-np.arange(8, dtype=np.float64) / 8)
    r = np.arange(MLA_ROPE)
    _, sign = _mla_rope_perm()
    p = np.where((r // 16 == 0)[None, :], rows[:, None], cols[:, None]).astype(np.float64)
    ang = p * freqs[(r % 16) % 8][None, :]
    cosf = np.zeros((LCAT, HEAD_PAD))
    sinf = np.zeros((LCAT, HEAD_PAD))
    cosf[:, :MLA_QK] = 1.0
    cosf[CTX_LEN:, MLA_NOPE:MLA_QK] = np.cos(ang)
    sinf[CTX_LEN:, MLA_NOPE:MLA_QK] = np.sin(ang) * sign[None, :]
    return cosf.astype(np.float32), sinf.astype(np.float32)


def _ret_perm():
    d = np.arange(RET_HD)
    first = d < RET_HD // 2
    return np.where(first, d + RET_HD // 2, d - RET_HD // 2), np.where(first, -1.0, 1.0)


@functools.lru_cache(maxsize=None)
def _ret_tables():
    half = RET_HD // 2
    pos = np.arange(SEQ, dtype=np.float64)
    freqs = ROPE_BASE ** (-np.arange(half, dtype=np.float64) / half)
    d = np.arange(RET_HD)
    perm, sign = _ret_perm()
    ang = pos[:, None] * freqs[d % half][None, :]
    cosr = np.ones((LCAT, RET_HD))
    sinr = np.zeros((LCAT, RET_HD))
    cosr[CTX_LEN:] = np.cos(ang)
    sinr[CTX_LEN:] = np.sin(ang) * sign[None, :]
    cosr = np.tile(cosr, (1, RET_HEADS))
    sinr = np.tile(sinr, (1, RET_HEADS))
    pm = np.zeros((BRANCH_W, BRANCH_W))
    for h in range(RET_HEADS):
        pm[h * RET_HD + perm, h * RET_HD + d] = 1.0
    return cosr.astype(np.float32), sinr.astype(np.float32), pm.astype(np.float32)


def _dft(n, scale):
    k = np.arange(n)
    kt = (k[:, None] * k[None, :]) % n
    ang = 2.0 * np.pi * kt / n
    return np.cos(ang) * scale, np.sin(ang) * scale


@functools.lru_cache(maxsize=None)
def _fnet_tables():
    cw, sw = _dft(FNET_GW, 1.0)
    t = np.zeros((BRANCH_W, 2 * BRANCH_W))
    for g in range(FNET_GROUPS):
        s = slice(g * FNET_GW, (g + 1) * FNET_GW)
        t[s, s] = cw
        t[s, BRANCH_W + g * FNET_GW:BRANCH_W + (g + 1) * FNET_GW] = -sw
    cl, sl = _dft(SEQ, 1.0 / math.sqrt(SEQ * FNET_GW))
    clc, slc = _dft(CTX_LEN, 1.0 / math.sqrt(CTX_LEN * FNET_GW))
    return tuple(a.astype(np.float32) for a in (t, cl, sl, clc, slc))


def _bf16_const(a):
    return jnp.asarray(a).astype(BF16)


@functools.lru_cache(maxsize=None)
def _head_avg():
    p = np.zeros((BRANCH_W, BRANCH_W))
    for h in range(RET_HEADS):
        p[h * RET_HD:(h + 1) * RET_HD, h * RET_HD:(h + 1) * RET_HD] = 1.0 / RET_HD
    return p.astype(np.float32)


@functools.lru_cache(maxsize=None)
def _tile_perm():
    p = np.zeros((TR, TR), np.float32)
    for b in range(BATCH):
        for t in range(TT):
            p[t * BATCH + b, b * TT + t] = 1.0
    return p


@functools.lru_cache(maxsize=None)
def _kv_place():
    perm, _ = _mla_rope_perm()
    place = np.zeros((128, 3 * HEADS_W), np.float32)
    for h in range(MLA_HEADS):
        for r in range(MLA_ROPE):
            place[r, h * HEAD_PAD + MLA_NOPE + r] = 1.0
            place[perm[r], HEADS_W + h * HEAD_PAD + MLA_NOPE + r] = 1.0
    return place


@functools.lru_cache(maxsize=None)
def _v_ones():
    v = np.zeros((1, HEADS_W), np.float32)
    v[0, np.arange(MLA_HEADS) * HEAD_PAD + MLA_V] = 1.0
    return v


ADA_TN = 1536
ADA_NBUF = 3
ADA_PER_LAYER = ADA_CHUNKS * D_MODEL // ADA_TN
ADA_NCHUNK = DEPTH * ADA_PER_LAYER


def _ada_kernel(c_ref, w_hbm, b_ref, o_ref, buf, sem):
    def chunk_copy(i):
        l, j = divmod(i, ADA_PER_LAYER)
        slot = i % ADA_NBUF
        return pltpu.make_async_copy(w_hbm.at[l, :, pl.ds(j * ADA_TN, ADA_TN)], buf.at[slot], sem.at[slot])

    for i in range(min(ADA_NBUF, ADA_NCHUNK)):
        chunk_copy(i).start()
    c = c_ref[...]
    s = (c * _sigmoid(c)).astype(BF16)
    for i in range(ADA_NCHUNK):
        l, j = divmod(i, ADA_PER_LAYER)
        cols = slice(j * ADA_TN, (j + 1) * ADA_TN)
        chunk_copy(i).wait()
        o_ref[l, :, cols] = _dot(s, buf[i % ADA_NBUF].astype(BF16)) + b_ref[l:l + 1, cols]
        if i + ADA_NBUF < ADA_NCHUNK:
            chunk_copy(i + ADA_NBUF).start()


def _ada_mod(c16, ada_w, ada_b):
    n = ADA_CHUNKS * D_MODEL
    return pl.pallas_call(
        _ada_kernel,
        in_specs=[pl.BlockSpec(memory_space=pltpu.VMEM),
                  pl.BlockSpec(memory_space=pl.ANY),
                  pl.BlockSpec(memory_space=pltpu.VMEM)],
        out_specs=pl.BlockSpec(memory_space=pltpu.VMEM),
        out_shape=jax.ShapeDtypeStruct((DEPTH, 16, n), F32),
        scratch_shapes=[pltpu.VMEM((ADA_NBUF, D_MODEL, ADA_TN), F32),
                        pltpu.SemaphoreType.DMA((ADA_NBUF,))],
        compiler_params=pltpu.CompilerParams(vmem_limit_bytes=VMEM_LIMIT),
        name="ada_mod",
    )(c16, ada_w, ada_b)


def _side_casts(jobs):
    specs_in, specs_out, shapes, args = [], [], [], []
    for a, l, n in jobs:
        _, r, c = a.shape
        blk = (1, r // n, c)
        specs_in.append(pl.BlockSpec(blk, lambda j, l=l, n=n: (l, jnp.minimum(j, n - 1), 0)))
        specs_out.append(pl.BlockSpec(blk, lambda j, n=n: (0, jnp.minimum(j, n - 1), 0)))
        shapes.append(jax.ShapeDtypeStruct((1, r, c), BF16))
        args.append(a)
    return specs_in, specs_out, shapes, args


def _with_side_casts(body, n_in, n_out, n_jobs):
    def wrapped(*refs):
        body(*refs[:n_in], *refs[n_in + n_jobs:n_in + n_jobs + n_out])
        for src, dst in zip(refs[n_in:n_in + n_jobs], refs[n_in + n_jobs + n_out:]):
            dst[...] = src[...].astype(BF16)
    return wrapped if n_jobs else body


def _mod_chunks(mod_ref, is_ctx, idxs):
    out = []
    for i in idxs:
        sl = slice(i * D_MODEL, (i + 1) * D_MODEL)
        m = mod_ref[0, 0:BATCH, sl]
        if is_ctx is not None:
            m = jnp.where(is_ctx, mod_ref[0, BATCH:BATCH + 1, sl], m)
        out.append(m[:, None, :])
    return out


def _tile_specs(separate, skip_ctx, width=D_MODEL):
    base = 0 if separate else CTX_TILES
    shape = (BATCH, TT, width)
    if skip_ctx:
        return (pl.BlockSpec(shape, lambda j: (0, 0, 0)),
                pl.BlockSpec(shape, lambda j: (0, j + base, 0)))
    return (pl.BlockSpec(shape, lambda j: (0, jnp.minimum(j, CTX_TILES - 1), 0)),
            pl.BlockSpec(shape, lambda j: (0, jnp.maximum(j - CTX_TILES, 0) + base, 0)))


def _head_norm_rot(xf, xp, a, b, scale):
    outs = []
    for h in range(MLA_HEADS):
        f = xf[:, h * HEAD_PAD:(h + 1) * HEAD_PAD]
        p = xp[:, h * HEAD_PAD:(h + 1) * HEAD_PAD]
        n = lax.rsqrt(jnp.sum(f * f, axis=-1, keepdims=True) * (1.0 / MLA_QK) + NORM_EPS) * scale
        outs.append(n * (f * a + p * b))
    return jnp.concatenate(outs, axis=-1)


def _rows(table):
    w = table.shape[-1]
    return jnp.broadcast_to(table[None], (BATCH, TT, w)).reshape(TR, w)


def _tile3(x):
    return x.reshape(BATCH, TT, x.shape[-1])


def _inproj_kernel(xc_ref, xl_ref, mod_ref, nw_ref, w_ref, kvw_ref, wkv_ref, qnw_ref, wq_ref,
                   cosf_ref, sinf_ref, hw_ref, pm_ref, cosr_ref, sinr_ref, t_ref, ptb_ref, vone_ref,
                   q_out, k_out, v_out, ut_out, uc_out, us_out, rq_out, rk_out, rv_out, rg_out, *, layer):
    is_ctx = pl.program_id(0) < CTX_TILES
    x3 = jnp.where(is_ctx, xc_ref[...], xl_ref[...])
    sh, sc = _mod_chunks(mod_ref, is_ctx, (0, 1))
    zas, zs = [], []
    for half in (slice(0, BATCH // 2), slice(BATCH // 2, BATCH)):
        hh = (_rms(x3[half], nw_ref[layer:layer + 1, :]) * (1.0 + sc[half]) + sh[half]).reshape(TR // 2, D_MODEL).astype(BF16)
        zas.append(_dot_nt(hh, w_ref[0, 0:256, :]))
        zs.append(_dot_nt(hh, w_ref[0, _O_S5:_O_GATE, :]))
    za, z = jnp.concatenate(zas, axis=0), jnp.concatenate(zs, axis=0)

    cosf, sinf = _rows(cosf_ref[...]), _rows(sinf_ref[...])
    hw = hw_ref[0]
    kvn = _rms(za[:, 0:128], kvw_ref[layer:layer + 1, :]).astype(BF16)
    lhs = jnp.concatenate([kvn, za[:, 128:256].astype(BF16)], axis=-1)
    kv = _dot(lhs, wkv_ref[0])
    k = _head_norm_rot(kv[:, 0:HEADS_W], kv[:, HEADS_W:2 * HEADS_W],
                       cosf * hw[0:1], sinf * hw[1:2], 1.0)
    k_out[...] = _tile3(k).astype(BF16)
    v_out[...] = _tile3(kv[:, 2 * HEADS_W:3 * HEADS_W] + vone_ref[...]).astype(BF16)
    qn = _rms(z[:, 768:1024], qnw_ref[layer:layer + 1, :]).astype(BF16)
    qq = _dot(qn, wq_ref[0])
    q = _head_norm_rot(qq[:, 0:HEADS_W], qq[:, HEADS_W:2 * HEADS_W],
                       cosf * hw[2:3], sinf * hw[3:4], MLA_QK ** -0.5)
    q_out[...] = _tile3(q).astype(BF16)
    cosr, sinr = _rows(cosr_ref[...]), _rows(sinr_ref[...])
    rq = z[:, 1280:1536]
    rk = z[:, 256:512]
    rq = rq * cosr + _dot(rq.astype(BF16), pm_ref[...]) * sinr
    rk = rk * cosr + _dot(rk.astype(BF16), pm_ref[...]) * sinr
    rq_out[...] = _tile3(rq).astype(BF16)
    rk_out[...] = _tile3(rk * (RET_HD ** -0.5)).astype(BF16)
    rv_out[...] = _tile3(z[:, 512:768]).astype(BF16)
    rg_out[...] = _tile3(z[:, 1536:1792]).astype(BF16)
    ucs = _dot(z[:, 1024:1280].astype(BF16), t_ref[...])
    uc_out[...] = _tile3(ucs[:, 0:BRANCH_W]).astype(BF16)
    us_out[...] = _tile3(ucs[:, BRANCH_W:2 * BRANCH_W]).astype(BF16)
    ut_out[...] = _dot(ptb_ref[...], z[:, 0:256].astype(BF16)).astype(BF16)


def _in_proj(l, x_ctx, x_lat, mod, sw, w_in_t, jobs):
    ctx_spec, lat_spec = _tile_specs(l == 0, False)
    tok = lambda w: pl.BlockSpec((BATCH, TT, w), lambda j: (0, j, 0))
    tab = lambda w: pl.BlockSpec((TT, w), lambda j: (j, 0))
    cosf, sinf = _mla_tables()
    cosr, sinr, pm = _ret_tables()
    bshape = lambda w: jax.ShapeDtypeStruct((BATCH, LCAT, w), BF16)
    job_in, job_out, job_shapes, job_args = _side_casts(jobs)
    return pl.pallas_call(
        _with_side_casts(functools.partial(_inproj_kernel, layer=l), 18, 10, len(jobs)),
        grid=(ALL_TILES,),
        in_specs=[ctx_spec, lat_spec,
                  _layer(l, (16, ADA_CHUNKS * D_MODEL)),
                  _full((DEPTH, D_MODEL)),
                  _layer(0, (_O_GATE, D_MODEL)),
                  _full((DEPTH, KV_LORA)),
                  _layer(l, (256, 3 * HEADS_W)),
                  _full((DEPTH, Q_LORA)),
                  _layer(l, (Q_LORA, 2 * HEADS_W)),
                  tab(HEAD_PAD), tab(HEAD_PAD),
                  _layer(l, (4, HEAD_PAD)),
                  _full((BRANCH_W, BRANCH_W)),
                  tab(BRANCH_W), tab(BRANCH_W),
                  _full((BRANCH_W, 2 * BRANCH_W)),
                  _full((TR, TR)),
                  _full((1, HEADS_W))] + job_in,
        out_specs=[tok(HEADS_W), tok(HEADS_W), tok(HEADS_W),
                   pl.BlockSpec((TR, BRANCH_W), lambda j: (j, 0)),
                   tok(256), tok(256), tok(256), tok(256), tok(256), tok(256)] + job_out,
        out_shape=[bshape(HEADS_W), bshape(HEADS_W), bshape(HEADS_W),
                   jax.ShapeDtypeStruct((LCAT * BATCH, BRANCH_W), BF16),
                   bshape(256), bshape(256), bshape(256), bshape(256), bshape(256), bshape(256)] + job_shapes,
        compiler_params=_cparams(1),
        name="in_proj",
    )(x_ctx, x_lat, mod, sw["norm_mix"], w_in_t, sw["kv_norm"], sw["wkv"], sw["q_norm"], sw["wq"],
      jnp.asarray(cosf), jnp.asarray(sinf), sw["head_w"], jnp.asarray(pm, dtype=BF16),
      jnp.asarray(cosr), jnp.asarray(sinr), _bf16_const(_fnet_tables()[0]),
      jnp.asarray(_tile_perm(), dtype=BF16), jnp.asarray(_v_ones()), *job_args)


def _attn_kernel(*refs):
    q_refs, (k_ref, v_ref, o_ref) = refs[:-3], refs[-3:]
    per = min(ATTN_UNIT_QB, len(q_refs))
    units = [(b, g, h, slice(h * HEAD_PAD, (h + 1) * HEAD_PAD))
             for b in range(k_ref.shape[0]) for g in range(len(q_refs) // per) for h in range(MLA_HEADS)]

    def scores(u):
        b, g, _, sl = u
        q = jnp.concatenate([q_refs[g * per + i][b, :, sl] for i in range(per)], axis=0)
        return _dot_nt(q, k_ref[b, :, sl])

    lane = lax.broadcasted_iota(jnp.int32, (1, HEAD_PAD), 1)
    pending = [scores(u) for u in units[:ATTN_AHEAD]]
    for i, (b, g, h, sl) in enumerate(units):
        if i + ATTN_AHEAD < len(units):
            pending.append(scores(units[i + ATTN_AHEAD]))
        s = pending.pop(0)
        p = jnp.exp((s - jnp.max(s, axis=-1, keepdims=True)).astype(BF16))
        oh = _dot(p, v_ref[b, :, sl])
        oh = oh * (1.0 / oh[:, MLA_V:MLA_V + 1])
        if h % 2 == 0:
            even = oh
        else:
            pair = jnp.where(lane < MLA_V, even, pltpu.roll(oh, MLA_V, axis=1))
            rows = slice(g * per * TQ, (g + 1) * per * TQ)
            o_ref[b, rows, (h // 2) * HEAD_PAD:(h // 2 + 1) * HEAD_PAD] = pair.astype(BF16)


def _attention_latent(q, k, v):
    first = CTX_LEN // TQ
    qspec = lambda r: pl.BlockSpec((1, TQ, HEADS_W), lambda b, j: (b, first + ATTN_QB * j + r, 0))
    kv = pl.BlockSpec((1, LCAT, HEADS_W), lambda b, j: (b, 0, 0))
    return pl.pallas_call(
        _attn_kernel,
        grid=(BATCH, SEQ // (ATTN_QB * TQ)),
        in_specs=[qspec(r) for r in range(ATTN_QB)] + [kv, kv],
        out_specs=pl.BlockSpec((1, ATTN_QB * TQ, BRANCH_W), lambda b, j: (b, j, 0)),
        out_shape=jax.ShapeDtypeStruct((BATCH, SEQ, BRANCH_W), BF16),
        compiler_params=_cparams(2),
        name="mla_attention",
    )(*([q] * ATTN_QB), k, v)


def _attention_context(q, k, v):
    blk = pl.BlockSpec((1, CTX_LEN, HEADS_W), lambda b: (b, 0, 0))
    return pl.pallas_call(
        _attn_kernel,
        grid=(BATCH,),
        in_specs=[blk, blk, blk],
        out_specs=pl.BlockSpec((1, CTX_LEN, BRANCH_W), lambda b: (b, 0, 0)),
        out_shape=jax.ShapeDtypeStruct((BATCH, CTX_LEN, BRANCH_W), BF16),
        compiler_params=_cparams(1),
        name="mla_attention_ctx",
    )(q, k, v)


def _fnet_kernel(uc_ref, us_ref, cl_ref, sl_ref, clc_ref, slc_ref, o_ref):
    o_ref[0, 0:CTX_LEN, :] = (_dot(clc_ref[...], uc_ref[0, 0:CTX_LEN, :])
                              + _dot(slc_ref[...], us_ref[0, 0:CTX_LEN, :])).astype(BF16)
    o_ref[0, CTX_LEN:LCAT, :] = (_dot(cl_ref[...], uc_ref[0, CTX_LEN:LCAT, :])
                                 + _dot(sl_ref[...], us_ref[0, CTX_LEN:LCAT, :])).astype(BF16)


def _fnet(uc, us):
    cl, sl, clc, slc = (_bf16_const(a) for a in _fnet_tables()[1:])
    tok = pl.BlockSpec((1, LCAT, BRANCH_W), lambda b: (b, 0, 0))
    return pl.pallas_call(
        _fnet_kernel,
        grid=(BATCH,),
        in_specs=[tok, tok, _full((SEQ, SEQ)), _full((SEQ, SEQ)),
                  _full((CTX_LEN, CTX_LEN)), _full((CTX_LEN, CTX_LEN))],
        out_specs=tok,
        out_shape=jax.ShapeDtypeStruct((BATCH, LCAT, BRANCH_W), BF16),
        compiler_params=_cparams(1),
        name="fnet_dft",
    )(uc, us, cl, sl, clc, slc)


def _s5_param_kernel(lr_ref, li_ref, ls_ref, bre_ref, bim_ref, cre_ref, cim_ref, lam_out, b_out, c_out):
    lr, li = lr_ref[0, 0], li_ref[0, 0]
    step = jnp.exp(ls_ref[0, 0])
    mag = jnp.exp(lr * step)
    lbr = mag * jnp.cos(li * step)
    lbi = mag * jnp.sin(li * step)
    den = 1.0 / (lr * lr + li * li)
    cr = ((lbr - 1.0) * lr + lbi * li) * den
    ci = (lbi * lr - (lbr - 1.0) * li) * den
    lam_out[0, 0, 0] = jnp.broadcast_to(lbr, (BATCH, S5_LANES))
    lam_out[0, 0, 1] = jnp.broadcast_to(lbi, (BATCH, S5_LANES))
    grp_b = (lax.broadcasted_iota(jnp.int32, (BRANCH_W, S5_LANES), 0) // S5_GROUP_CH
             == lax.broadcasted_iota(jnp.int32, (BRANCH_W, S5_LANES), 1) // S5_STATE)
    wide = lambda r: jnp.concatenate([r[0, 0]] * (S5_LANES // 128), axis=-1)
    bre = jnp.where(grp_b, wide(bre_ref), 0.0)
    bim = jnp.where(grp_b, wide(bim_ref), 0.0)
    b_out[0, 0, :, 0:S5_LANES] = (cr * bre - ci * bim).astype(BF16)
    b_out[0, 0, :, S5_LANES:2 * S5_LANES] = (cr * bim + ci * bre).astype(BF16)
    grp_c = (lax.broadcasted_iota(jnp.int32, (S5_LANES, BRANCH_W), 0) // S5_STATE
             == lax.broadcasted_iota(jnp.int32, (S5_LANES, BRANCH_W), 1) // S5_GROUP_CH)
    tall = lambda r: jnp.concatenate([r[0, 0]] * S5_GROUPS, axis=0)
    c_out[0, 0, 0:S5_LANES, :] = jnp.where(grp_c, tall(cre_ref), 0.0).astype(BF16)
    c_out[0, 0, S5_LANES:2 * S5_LANES, :] = jnp.where(grp_c, -tall(cim_ref), 0.0).astype(BF16)


def _s5_params(sw):
    spec = lambda *s: pl.BlockSpec((1, 1) + s, lambda l, d: (l, d) + (0,) * len(s))
    return pl.pallas_call(
        _s5_param_kernel,
        grid=(DEPTH, 2),
        in_specs=[spec(1, S5_LANES), spec(1, S5_LANES), spec(1, S5_LANES),
                  spec(BRANCH_W, 128), spec(BRANCH_W, 128), spec(S5_STATE, BRANCH_W), spec(S5_STATE, BRANCH_W)],
        out_specs=[spec(2, BATCH, S5_LANES), spec(BRANCH_W, 2 * S5_LANES), spec(2 * S5_LANES, BRANCH_W)],
        out_shape=[jax.ShapeDtypeStruct((DEPTH, 2, 2, BATCH, S5_LANES), F32),
                   jax.ShapeDtypeStruct((DEPTH, 2, BRANCH_W, 2 * S5_LANES), BF16),
                   jax.ShapeDtypeStruct((DEPTH, 2, 2 * S5_LANES, BRANCH_W), BF16)],
        compiler_params=_cparams(2),
        name="s5_discretise",
    )(sw["s5_lam_re"], sw["s5_lam_im"], sw["s5_log_step"], sw["s5_bre"], sw["s5_bim"], sw["s5_cre"], sw["s5_cim"])


def _s5_bwd_block(i):
    return jnp.where(i < S5_CTX_STEPS, S5_CTX_STEPS - 1 - i, S5_STEPS + S5_CTX_STEPS - 1 - i)


def _s5_kernel(uf_ref, ub_ref, lam_ref, b_ref, c_ref, yf_ref, yb_ref, xf_scr, xb_scr, st_scr):
    @pl.when(pl.program_id(0) == 0)
    def _():
        st_scr[...] = jnp.zeros_like(st_scr)

    dirs = ((uf_ref, xf_scr, yf_ref), (ub_ref, xb_scr, yb_ref))
    for d, (u_ref, x_scr, _) in enumerate(dirs):
        x_scr[...] = _dot(u_ref[...], b_ref[0, d])
    re, im = pl.ds(0, S5_LANES), pl.ds(S5_LANES, S5_LANES)
    for d, (_, x_scr, y_ref) in enumerate(dirs):
        xr, xi = st_scr[2 * d], st_scr[2 * d + 1]
        ar, ai = lam_ref[0, d, 0], lam_ref[0, d, 1]
        for t in (range(S5_TC) if d == 0 else range(S5_TC - 1, -1, -1)):
            rows = pl.ds(t * BATCH, BATCH)
            xr, xi = (ar * xr - ai * xi + x_scr[rows, re], ar * xi + ai * xr + x_scr[rows, im])
            x_scr[rows, re] = xr
            x_scr[rows, im] = xi
        st_scr[2 * d], st_scr[2 * d + 1] = xr, xi
        y_ref[...] = _dot(x_scr[...].astype(BF16), c_ref[0, d]).astype(BF16)


def _s5(l, u_t, lam, bblk, cblk):
    fwd = pl.BlockSpec((S5_ROWS, BRANCH_W), lambda i: (i, 0))
    bwd = pl.BlockSpec((S5_ROWS, BRANCH_W), lambda i: (_s5_bwd_block(i), 0))
    return pl.pallas_call(
        _s5_kernel,
        grid=(S5_STEPS,),
        in_specs=[fwd, bwd, _layer(l, (2, 2, BATCH, S5_LANES)),
                  _layer(l, (2, BRANCH_W, 2 * S5_LANES)), _layer(l, (2, 2 * S5_LANES, BRANCH_W))],
        out_specs=[fwd, bwd],
        out_shape=[jax.ShapeDtypeStruct((LCAT * BATCH, BRANCH_W), BF16)] * 2,
        scratch_shapes=[pltpu.VMEM((S5_ROWS, 2 * S5_LANES), F32),
                        pltpu.VMEM((S5_ROWS, 2 * S5_LANES), F32),
                        pltpu.VMEM((4, BATCH, S5_LANES), F32)],
        compiler_params=_cparams(1),
        name="s5_scan",
    )(u_t, u_t, lam, bblk, cblk)


def _log_sigmoid(x):
    return jnp.minimum(x, 0.0) - jnp.log(1.0 + jnp.exp(-jnp.abs(x)))


def _ret_kernel(q_ref, k_ref, v_ref, lgl_ref, lgh_ref, o_ref, dec_scr, sb_scr, sf_scr, sbc_scr):
    c_len = RET_CHUNK
    lgl = _log_sigmoid(lgl_ref[0])
    lgf, lgb = lgl[0], lgl[1]
    ti = lax.broadcasted_iota(jnp.int32, (c_len, BRANCH_W), 0).astype(F32)
    qdf = jnp.exp(lgf * (ti + 1.0))
    kdf = jnp.exp(lgf * (c_len - 1.0 - ti))
    qdb = jnp.exp(lgb * (c_len - ti))
    kdb = jnp.exp(lgb * ti)
    cdf = jnp.exp(lgf * float(c_len))
    cdb = jnp.exp(lgb * float(c_len))
    lane = lax.broadcasted_iota(jnp.int32, (1, BRANCH_W), 1)
    same_head = (lax.broadcasted_iota(jnp.int32, (BRANCH_W, BRANCH_W), 0) // RET_HD
                 == lax.broadcasted_iota(jnp.int32, (BRANCH_W, BRANCH_W), 1) // RET_HD)

    diff = (lax.broadcasted_iota(jnp.int32, (c_len, c_len), 0)
            - lax.broadcasted_iota(jnp.int32, (c_len, c_len), 1)).astype(F32)
    for h in range(RET_HEADS):
        gf = jnp.concatenate([_log_sigmoid(lgh_ref[0, 0, h])] * (c_len // 128), axis=-1)
        gb = jnp.concatenate([_log_sigmoid(lgh_ref[0, 1, h])] * (c_len // 128), axis=-1)
        dec_scr[h] = (jnp.where(diff >= 0, jnp.exp(gf * jnp.maximum(diff, 0.0)), 0.0)
                      + jnp.where(diff <= 0, jnp.exp(gb * jnp.maximum(-diff, 0.0)), 0.0))

    def chunk(ref, s, c):
        return ref[s, pl.ds(pl.multiple_of(c * c_len, c_len), c_len), :]

    def kv_outer(kd, v):
        s = lax.dot_general(kd.astype(BF16), v, (((0,), (0,)), ((), ())), preferred_element_type=F32)
        return jnp.where(same_head, s, 0.0)

    sbc_scr[...] = jnp.zeros_like(sbc_scr)

    def bwd(i, _):
        c = jnp.where(i < RET_CTX_CHUNKS, RET_CTX_CHUNKS - 1 - i, RET_NCHUNK + RET_CTX_CHUNKS - 1 - i)
        for s in range(RET_SPB):
            sb_scr[s, c] = sbc_scr[s].astype(BF16)
            k = chunk(k_ref, s, c).astype(F32)
            sbc_scr[s] = sbc_scr[s] * cdb + kv_outer(k * kdb, chunk(v_ref, s, c))
        return 0

    lax.fori_loop(0, RET_NCHUNK, bwd, 0)

    sf_scr[...] = jnp.zeros_like(sf_scr)

    def fwd(c, _):
        for s in range(RET_SPB):
            qb, kb, vb = chunk(q_ref, s, c), chunk(k_ref, s, c), chunk(v_ref, s, c)
            q = qb.astype(F32)
            o = (_dot((q * qdf).astype(BF16), sf_scr[s].astype(BF16))
                 + _dot((q * qdb).astype(BF16), sb_scr[s, c]))
            for h in range(RET_HEADS):
                hm = lane // RET_HD == h
                att = _dot_nt(jnp.where(hm, qb, jnp.zeros_like(qb)), kb)
                oh = _dot((att * dec_scr[h]).astype(BF16), vb)
                o = o + jnp.where(hm, oh, 0.0)
            o_ref[s, pl.ds(pl.multiple_of(c * c_len, c_len), c_len), :] = o.astype(BF16)
            sf_scr[s] = sf_scr[s] * cdf + kv_outer(kb.astype(F32) * kdf, vb)
        return 0

    lax.fori_loop(0, RET_NCHUNK, fwd, 0)


def _retention(l, rq, rk, rv, sw):
    tok = pl.BlockSpec((RET_SPB, LCAT, BRANCH_W), lambda b: (b, 0, 0))
    return pl.pallas_call(
        _ret_kernel,
        grid=(BATCH // RET_SPB,),
        in_specs=[tok, tok, tok, _layer(l, (2, 1, BRANCH_W)), _layer(l, (2, RET_HEADS, 1, 128))],
        out_specs=tok,
        out_shape=jax.ShapeDtypeStruct((BATCH, LCAT, BRANCH_W), BF16),
        scratch_shapes=[pltpu.VMEM((RET_HEADS, RET_CHUNK, RET_CHUNK), F32),
                        pltpu.VMEM((RET_SPB, RET_NCHUNK, BRANCH_W, BRANCH_W), BF16),
                        pltpu.VMEM((RET_SPB, BRANCH_W, BRANCH_W), F32),
                        pltpu.VMEM((RET_SPB, BRANCH_W, BRANCH_W), F32)],
        compiler_params=_cparams(1),
        name="retention",
    )(rq, rk, rv, sw["lgl"], sw["lgh"])


def _merge_kernel(xc_ref, xl_ref, mod_ref, nw_ref, wg_ref, oac_ref, oal_ref, ob_ref, yf_ref, yb_ref, ut_ref,
                  d_ref, wglu_ref, oret_ref, rg_ref, gnw_ref, pavg_ref, pbt_ref, wb_ref, wout_ref,
                  o_ref, *, skip_ctx, layer):
    if skip_ctx:
        is_ctx = None
        x3 = xl_ref[...]
        oa = oal_ref[...]
    else:
        is_ctx = pl.program_id(0) < CTX_TILES
        x3 = jnp.where(is_ctx, xc_ref[...], xl_ref[...])
        oa = jnp.where(is_ctx, oac_ref[...], oal_ref[...])
    sh, sc, gate_res = _mod_chunks(mod_ref, is_ctx, (0, 1, 2))
    gate_w = lambda n: wg_ref[0, _O_GATE + n * D_MODEL:_O_GATE + (n + 1) * D_MODEL, :]
    hs = [(_rms(x3[half], nw_ref[layer:layer + 1, :]) * (1.0 + sc[half]) + sh[half]).reshape(TR // 2, D_MODEL).astype(BF16)
          for half in (slice(0, BATCH // 2), slice(BATCH // 2, BATCH))]
    gate0 = jnp.concatenate([_dot_nt(hh, gate_w(0)) for hh in hs], axis=0)
    h = jnp.concatenate(hs, axis=0)
    yt = yf_ref[...].astype(F32) + yb_ref[...].astype(F32) + d_ref[layer:layer + 1, :] * ut_ref[...].astype(F32)
    y = _dot(pbt_ref[...], yt.astype(BF16))
    vg = _dot(_gelu_tanh(y).astype(BF16), wglu_ref[0])
    oc = vg[:, 0:BRANCH_W] * _sigmoid(vg[:, BRANCH_W:2 * BRANCH_W])
    o = oret_ref[...].reshape(TR, BRANCH_W)
    dl = o.astype(F32) - _dot(o, pavg_ref[...])
    var = _dot((dl * dl).astype(BF16), pavg_ref[...])
    g = rg_ref[...].reshape(TR, BRANCH_W).astype(F32)
    od = g * _sigmoid(g) * (dl * lax.rsqrt(var + NORM_EPS) * gnw_ref[layer:layer + 1, :])
    branches = (oa.reshape(TR, BRANCH_W), ob_ref[...].reshape(TR, BRANCH_W), oc.astype(BF16), od.astype(BF16))
    acc = None
    for n, branch in enumerate(branches):
        gate = gate0 if n == 0 else _dot_nt(h, gate_w(n))
        term = _sigmoid(gate) * _dot(branch, wb_ref[0, n])
        acc = term if acc is None else acc + term
    m = _dot(acc.astype(BF16), wout_ref[0])
    o_ref[...] = x3 + gate_res * m.reshape(BATCH, TT, D_MODEL)


def _merge(l, x_ctx, x_lat, mod, sw, w_in_t, oa_ctx, oa_lat, ob, yf, yb, u_t, oret, rg, skip_ctx, jobs):
    off = CTX_TILES if skip_ctx else 0
    nt = ALL_TILES - off
    ctx_spec, lat_spec = _tile_specs(l == 0, skip_ctx)
    oac_spec, oal_spec = _tile_specs(True, skip_ctx, BRANCH_W)
    tok = lambda w: pl.BlockSpec((BATCH, TT, w), lambda j: (0, j + off, 0))
    tmaj = pl.BlockSpec((TR, BRANCH_W), lambda j: (j + off, 0))
    job_in, job_out, job_shapes, job_args = _side_casts(jobs)
    return pl.pallas_call(
        _with_side_casts(functools.partial(_merge_kernel, skip_ctx=skip_ctx, layer=l), 20, 1, len(jobs)),
        grid=(nt,),
        in_specs=[ctx_spec, lat_spec,
                  _layer(l, (16, ADA_CHUNKS * D_MODEL)),
                  _full((DEPTH, D_MODEL)),
                  _layer(0, (IN_COLS, D_MODEL), single=True),
                  oac_spec, oal_spec,
                  tok(BRANCH_W), tmaj, tmaj, tmaj,
                  _full((DEPTH, BRANCH_W)),
                  _layer(l, (BRANCH_W, 2 * BRANCH_W)),
                  tok(BRANCH_W), tok(BRANCH_W),
                  _full((DEPTH, BRANCH_W)),
                  _full((BRANCH_W, BRANCH_W)),
                  _full((TR, TR)),
                  _layer(l, (N_BRANCH, BRANCH_W, D_MODEL), single=True),
                  _layer(l, (D_MODEL, D_MODEL), single=True)] + job_in,
        out_specs=[pl.BlockSpec((BATCH, TT, D_MODEL), lambda j: (0, j, 0))] + job_out,
        out_shape=[jax.ShapeDtypeStruct((BATCH, nt * TT, D_MODEL), F32)] + job_shapes,
        compiler_params=_cparams(1),
        name="merge",
    )(x_ctx, x_lat, mod, sw["norm_mix"], w_in_t, oa_ctx, oa_lat, ob, yf, yb, u_t, sw["s5_d"], sw["w_glu"],
      oret, rg, sw["gn_w"], jnp.asarray(_head_avg(), dtype=BF16), jnp.asarray(_tile_perm().T, dtype=BF16),
      sw["w_branch"], sw["w_out"], *job_args)


def _ffn_kernel(x_ref, mod_ref, nw_ref, w1_ref, w2_ref, o_ref, *, skip_ctx, layer):
    is_ctx = None if skip_ctx else pl.program_id(0) < CTX_LEN // FFN_TT
    sh, sc, gate_res = _mod_chunks(mod_ref, is_ctx, (3, 4, 5))
    x3 = x_ref[...]
    h = (_rms(x3, nw_ref[layer:layer + 1, :]) * (1.0 + sc) + sh).reshape(BATCH * FFN_TT, D_MODEL).astype(BF16)
    f = None
    for c in range(0, D_FF, FFN_PIECE):
        a = jnp.maximum(_dot(h, w1_ref[0, :, c:c + FFN_PIECE]), 0.0)
        part = _dot((a * a).astype(BF16), w2_ref[0, c:c + FFN_PIECE, :])
        f = part if f is None else f + part
    o_ref[...] = x3 + gate_res * f.reshape(BATCH, FFN_TT, D_MODEL)


def _ffn(l, xm, mod, sw, w1, w2, skip_ctx):
    nt = xm.shape[1] // FFN_TT
    tok = pl.BlockSpec((BATCH, FFN_TT, D_MODEL), lambda j: (0, j, 0))
    return pl.pallas_call(
        functools.partial(_ffn_kernel, skip_ctx=skip_ctx, layer=l),
        grid=(nt,),
        in_specs=[tok,
                  _layer(l, (16, ADA_CHUNKS * D_MODEL)),
                  _full((DEPTH, D_MODEL)),
                  _layer(0, (D_MODEL, D_FF), single=True),
                  _layer(0, (D_FF, D_MODEL), single=True)],
        out_specs=tok,
        out_shape=jax.ShapeDtypeStruct(xm.shape, F32),
        compiler_params=_cparams(1),
        name="ffn",
    )(xm, mod, sw["norm_ffn"], w1, w2)


def _stacked_weights(p):
    perm, _ = _mla_rope_perm()
    zeros = lambda *s: jnp.zeros((DEPTH,) + s, F32)
    wu = p["mla_w_ukv"].reshape(DEPTH, KV_LORA, MLA_HEADS, MLA_NOPE + MLA_V)
    pad_heads = lambda t: jnp.concatenate(
        [t, zeros(t.shape[1], MLA_HEADS, HEAD_PAD - t.shape[3])], -1).reshape(DEPTH, t.shape[1], HEADS_W)
    top = jnp.concatenate([pad_heads(wu[..., :MLA_NOPE]), zeros(KV_LORA, HEADS_W),
                           pad_heads(wu[..., MLA_NOPE:])], axis=2)
    place = jnp.broadcast_to(jnp.asarray(_kv_place())[None], (DEPTH, 128, 3 * HEADS_W))
    wkv = jnp.concatenate([top, place], axis=1).astype(BF16)

    wuq = p["mla_w_uq"].reshape(DEPTH, Q_LORA, MLA_HEADS, MLA_QK)
    qp = jnp.concatenate([zeros(Q_LORA, MLA_HEADS, MLA_NOPE), wuq[..., MLA_NOPE:][..., perm],
                          zeros(Q_LORA, MLA_HEADS, 32)], -1).reshape(DEPTH, Q_LORA, HEADS_W)
    wq = jnp.concatenate([pad_heads(wuq), qp], axis=2).astype(BF16)

    def head_w(v):
        wf = jnp.concatenate([v, zeros(32)], -1)[:, None, :]
        wp = jnp.concatenate([zeros(MLA_NOPE), v[:, MLA_NOPE:][:, perm], zeros(32)], -1)[:, None, :]
        return wf, wp

    head_ws = jnp.concatenate(head_w(p["mla_qk_norm_k"]) + head_w(p["mla_qk_norm_q"]), axis=1)

    def b_compact(b):
        t = b.transpose(0, 1, 2, 4, 3).reshape(DEPTH, 2, BRANCH_W, S5_STATE)
        return jnp.concatenate([t, t], axis=-1)

    c_compact = lambda c: c.transpose(0, 1, 4, 2, 3).reshape(DEPTH, 2, S5_STATE, BRANCH_W)
    vec = lambda a: a.reshape(DEPTH, 2, 1, S5_LANES)

    logit = p["ret_decay_logit"]
    return dict(
        norm_mix=p["norm_mix_w"], norm_ffn=p["norm_ffn_w"],
        kv_norm=p["mla_kv_norm"], q_norm=p["mla_q_norm"],
        wkv=wkv, wq=wq, head_w=head_ws,
        s5_lam_re=vec(p["s5_lam_re"]), s5_lam_im=vec(p["s5_lam_im"]),
        s5_log_step=vec(jnp.repeat(p["s5_log_step"], S5_STATE, axis=-1)),
        s5_bre=b_compact(p["s5_b_re"]), s5_bim=b_compact(p["s5_b_im"]),
        s5_cre=c_compact(p["s5_c_re"]), s5_cim=c_compact(p["s5_c_im"]),
        s5_d=p["s5_d"], w_glu=p["s5_w_glu"].astype(BF16),
        lgl=jnp.repeat(logit, RET_HD, axis=-1).reshape(DEPTH, 2, 1, BRANCH_W),
        lgh=jnp.broadcast_to(logit[:, :, :, None, None], (DEPTH, 2, RET_HEADS, 1, 128)),
        gn_w=p["ret_gn_w"],
        w_branch=p["w_branch"].astype(BF16), w_out=p["w_out"].astype(BF16))


def kernel(x, c, ctx, c_ctx, ada_w, ada_b, norm_mix_w, norm_ffn_w, w_in, mla_q_norm, mla_w_uq, mla_kv_norm,
           mla_w_ukv, mla_qk_norm_q, mla_qk_norm_k, s5_lam_re, s5_lam_im, s5_log_step, s5_b_re, s5_b_im,
           s5_c_re, s5_c_im, s5_d, s5_w_glu, ret_decay_logit, ret_gn_w, w_branch, w_out, ffn_w1, ffn_w2):
    p = dict(norm_mix_w=norm_mix_w, norm_ffn_w=norm_ffn_w, w_in=w_in, mla_q_norm=mla_q_norm,
             mla_w_uq=mla_w_uq, mla_kv_norm=mla_kv_norm, mla_w_ukv=mla_w_ukv, mla_qk_norm_q=mla_qk_norm_q,
             mla_qk_norm_k=mla_qk_norm_k, s5_lam_re=s5_lam_re, s5_lam_im=s5_lam_im, s5_log_step=s5_log_step,
             s5_b_re=s5_b_re, s5_b_im=s5_b_im, s5_c_re=s5_c_re, s5_c_im=s5_c_im, s5_d=s5_d,
             s5_w_glu=s5_w_glu, ret_decay_logit=ret_decay_logit, ret_gn_w=ret_gn_w, w_branch=w_branch,
             w_out=w_out, ffn_w1=ffn_w1, ffn_w2=ffn_w2)
    sw = _stacked_weights(p)
    c16 = jnp.concatenate([c, c_ctx[None, :], jnp.zeros((16 - BATCH - 1, D_MODEL), F32)], axis=0)
    mod = _ada_mod(c16, ada_w, ada_b)
    lam, bblk, cblk = _s5_params(sw)
    w_in_t32 = jnp.swapaxes(w_in, 1, 2)
    w_in_t = w_in_t32[0:1].astype(BF16)
    x_ctx, x_lat = ctx, x
    for l in range(DEPTH):
        last = l == DEPTH - 1
        q, k, v, u_t, uc, us, rq, rk, rv, rg, w1, w2 = _in_proj(
            l, x_ctx, x_lat, mod, sw, w_in_t, [(ffn_w1, l, 32), (ffn_w2, l, 32)])
        oa_lat = _attention_latent(q, k, v)
        oa_ctx = oa_lat if last else _attention_context(q, k, v)
        ob = _fnet(uc, us)
        yf, yb = _s5(l, u_t, lam, bblk, cblk)
        oret = _retention(l, rq, rk, rv, sw)
        xm, *w_in_next = _merge(l, x_ctx, x_lat, mod, sw, w_in_t, oa_ctx, oa_lat, ob, yf, yb, u_t, oret, rg, last,
                                [] if last else [(w_in_t32, l + 1, 18)])
        if not last:
            w_in_t, = w_in_next
        x_ctx = x_lat = _ffn(l, xm, mod, sw, w1, w2, last)
    return x_lat
```
